```python
import math
import jax, jax.numpy as jnp
from jax import lax
import numpy as np

D_MODEL = 1024
BATCH = 16
SEQ = 2048
DEPTH = 4

N_MIXERS = 4
EPS = 1e-6
POOL_WINDOWS = (2, 4, 8, 16)
POOL_GROUP = D_MODEL // len(POOL_WINDOWS)
S5_GROUP = 16
S5_GROUPS = D_MODEL // S5_GROUP
S5_STATE = 64
S5_DT_MIN = 1e-3
S5_DT_MAX = 1e-1
LRU_WIDTH = D_MODEL
LRU_BLOCKS = 4
LRU_BLOCK = LRU_WIDTH // LRU_BLOCKS
LRU_CONV = 4
LRU_C = 8.0
SB_HEADS = 16
SB_HEAD_DIM = D_MODEL // SB_HEADS
SB_Q_BLOCK = 128
FFN_HIDDEN = 2816
FFN_CONV = 3

kernel_name = "interleaved_pool_s5_rglru_stickbreak_trunk"


def n_layers_of(m):
    return len(range(m, DEPTH, N_MIXERS))


def rms_norm(x, g):
    xf = x.astype(jnp.float32)
    y = xf * lax.rsqrt(jnp.mean(xf * xf, axis=-1, keepdims=True) + EPS)
    return (y * g.astype(jnp.float32)).astype(x.dtype)


def causal_depthwise_conv(x, w, b):
    k_width = w.shape[0]
    seq = x.shape[1]
    xp = jnp.pad(x, ((0, 0), (k_width - 1, 0), (0, 0)))
    y = b
    for k in range(k_width):
        y = y + w[k] * xp[:, k:k + seq]
    return y


def linear_scan_combine(left, right):
    a_l, b_l = left
    a_r, b_r = right
    return a_r * a_l, a_r * b_l + b_r


def pool_mixer(x, w, b, scale):
    bsz, seq, _ = x.shape
    xf = x.astype(jnp.float32)
    cs = jnp.pad(jnp.cumsum(xf, axis=1), ((0, 0), (1, 0), (0, 0)))
    pos = jnp.arange(seq)
    groups = []
    for gi, w_len in enumerate(POOL_WINDOWS):
        c = cs[..., gi * POOL_GROUP:(gi + 1) * POOL_GROUP]
        lo = jnp.maximum(pos + 1 - w_len, 0)
        window_sum = c[:, 1:] - jnp.take(c, lo, axis=1)
        count = (pos + 1 - lo).astype(jnp.float32)[:, None]
        groups.append(window_sum / count - xf[..., gi * POOL_GROUP:(gi + 1) * POOL_GROUP])
    d = jnp.stack(groups, axis=2)
    y = jnp.einsum('bsgc,gcd->bsgd', d, w.astype(jnp.float32)).reshape(bsz, seq, D_MODEL) + b
    return (scale * y).astype(x.dtype)


def s5_mixer(x, lam_re, lam_im, log_dt, b_re, b_im, c_re, c_im, d_skip, w_out, b_out):
    f32 = jnp.float32
    bsz, seq, _ = x.shape
    xf = x.astype(f32)
    u = xf.reshape(bsz, seq, S5_GROUPS, S5_GROUP)
    lam = lax.complex(jnp.minimum(lam_re.astype(f32), -1e-4), lam_im.astype(f32))
    dt = jnp.exp(log_dt.astype(f32))[:, None]
    lam_bar = jnp.exp(lam * dt)
    b_bar = ((lam_bar - 1.0) / lam)[..., None] * lax.complex(b_re.astype(f32), b_im.astype(f32))
    bu = lax.complex(jnp.einsum('bsgh,gph->bsgp', u, jnp.real(b_bar)),
                     jnp.einsum('bsgh,gph->bsgp', u, jnp.imag(b_bar)))
    a = jnp.broadcast_to(lam_bar, (seq,) + lam_bar.shape)[None]
    _, states = lax.associative_scan(linear_scan_combine, (a, bu), axis=1)
    y = (jnp.einsum('bsgp,ghp->bsgh', jnp.real(states), c_re.astype(f32))
         - jnp.einsum('bsgp,ghp->bsgh', jnp.imag(states), c_im.astype(f32)))
    y = y.reshape(bsz, seq, D_MODEL) + d_skip.astype(f32) * xf
    y = jax.nn.gelu(y).astype(x.dtype)
    val, gate = jnp.split(y @ w_out + b_out, 2, axis=-1)
    return val * jax.nn.sigmoid(gate)


def rglru_mixer(x, w_in, conv_w, conv_b, w_a, b_a, w_x, b_x, lam, w_out):
    f32 = jnp.float32
    bsz, seq, _ = x.shape
    gate_branch, rec = jnp.split(x @ w_in, 2, axis=-1)
    rec = causal_depthwise_conv(rec, conv_w, conv_b).astype(f32)
    rb = rec.reshape(bsz, seq, LRU_BLOCKS, LRU_BLOCK)
    r = jax.nn.sigmoid(jnp.einsum('bsnc,ncd->bsnd', rb, w_a.astype(f32)).reshape(bsz, seq, LRU_WIDTH) + b_a)
    i = jax.nn.sigmoid(jnp.einsum('bsnc,ncd->bsnd', rb, w_x.astype(f32)).reshape(bsz, seq, LRU_WIDTH) + b_x)
    log_a = -LRU_C * r * jax.nn.softplus(-lam.astype(f32))
    a = jnp.exp(log_a)
    mult = jnp.sqrt(-jnp.expm1(2.0 * log_a))
    _, h = lax.associative_scan(linear_scan_combine, (a, mult * (i * rec)), axis=1)
    y = jax.nn.gelu(gate_branch.astype(f32)) * h
    return y.astype(x.dtype) @ w_out


def stick_breaking_mixer(x, w_qkv, q_g, k_g, w_o):
    bsz, seq, _ = x.shape
    q, k, v = jnp.split(x @ w_qkv, 3, axis=-1)
    to_heads = lambda t: t.reshape(bsz, seq, SB_HEADS, SB_HEAD_DIM).transpose(0, 2, 1, 3)
    q = rms_norm(to_heads(q), q_g)
    k = rms_norm(to_heads(k), k_g)
    v = to_heads(v)
    scale = 1.0 / math.sqrt(SB_HEAD_DIM)
    outs = []
    for start in range(0, seq, SB_Q_BLOCK):
        end = start + SB_Q_BLOCK
        kb, vb = k[:, :, :end], v[:, :, :end]
        z = jnp.einsum('bhtd,bhsd->bhts', q[:, :, start:end], kb).astype(jnp.float32) * scale
        t_pos = start + jnp.arange(SB_Q_BLOCK)[:, None]
        s_pos = jnp.arange(end)[None, :]
        mask = s_pos < t_pos
        log_1m_beta = jnp.where(mask, jax.nn.log_sigmoid(-z), 0.0)
        rest = lax.cumsum(log_1m_beta, axis=3, reverse=True) - log_1m_beta
        att = jnp.where(mask, jnp.exp(jax.nn.log_sigmoid(z) + rest), 0.0)
        outs.append(jnp.einsum('bhts,bhsd->bhtd', att.astype(v.dtype), vb))
    o = jnp.concatenate(outs, axis=2).transpose(0, 2, 1, 3).reshape(bsz, seq, D_MODEL)
    return o @ w_o


def conv_ffn(x, w_in, conv_w, conv_b, w_out):
    h = causal_depthwise_conv(x @ w_in, conv_w, conv_b)
    val, gate = jnp.split(h, 2, axis=-1)
    return (jax.nn.silu(gate) * val) @ w_out


def _fwd_setup_inputs(seed: int = 0) -> dict:
    key = jax.random.key(seed)
    ks = iter(jax.random.split(key, 40))
    nrm = lambda shape, std: jax.random.normal(next(ks), shape, jnp.float32) * std
    gain = lambda shape: 1.0 + nrm(shape, 0.02)
    nA, nB, nC, nD = (n_layers_of(m) for m in range(N_MIXERS))
    G, P, H = S5_GROUPS, S5_STATE, S5_GROUP
    a0 = jax.random.uniform(next(ks), (nC, LRU_WIDTH), jnp.float32, 0.9, 0.999)
    return {
        "x": nrm((BATCH, SEQ, D_MODEL), 1.0),
        "norm_mix_g": gain((DEPTH, D_MODEL)),
        "norm_ffn_g": gain((DEPTH, D_MODEL)),
        "pool_w": nrm((nA, len(POOL_WINDOWS), POOL_GROUP, POOL_GROUP), POOL_GROUP ** -0.5),
        "pool_b": nrm((nA, D_MODEL), 0.01),
        "pool_scale": 1.0 + nrm((nA, D_MODEL), 0.1),
        "s5_lam_re": -0.5 + nrm((nA * 0 + nB, G, P), 0.01),
        "s5_lam_im": math.pi * jnp.arange(P, dtype=jnp.float32) + nrm((nB, G, P), 0.01),
        "s5_log_dt": jax.random.uniform(next(ks), (nB, G), jnp.float32, math.log(S5_DT_MIN), math.log(S5_DT_MAX)),
        "s5_b_re": nrm((nB, G, P, H), (2 * H) ** -0.5),
        "s5_b_im": nrm((nB, G, P, H), (2 * H) ** -0.5),
        "s5_c_re": nrm((nB, G, H, P), P ** -0.5),
        "s5_c_im": nrm((nB, G, H, P), P ** -0.5),
        "s5_d": nrm((nB, D_MODEL), 1.0),
        "s5_w_out": nrm((nB, D_MODEL, 2 * D_MODEL), D_MODEL ** -0.5),
        "s5_b_out": nrm((nB, 2 * D_MODEL), 0.01),
        "lru_w_in": nrm((nC, D_MODEL, 2 * LRU_WIDTH), D_MODEL ** -0.5),
        "lru_conv_w": nrm((nC, LRU_CONV, LRU_WIDTH), LRU_CONV ** -0.5),
        "lru_conv_b": nrm((nC, LRU_WIDTH), 0.01),
        "lru_w_a": nrm((nC, LRU_BLOCKS, LRU_BLOCK, LRU_BLOCK), LRU_BLOCK ** -0.5),
        "lru_b_a": nrm((nC, LRU_WIDTH), 0.01),
        "lru_w_x": nrm((nC, LRU_BLOCKS, LRU_BLOCK, LRU_BLOCK), LRU_BLOCK ** -0.5),
        "lru_b_x": nrm((nC, LRU_WIDTH), 0.01),
        "lru_lam": jnp.log(a0) - jnp.log1p(-a0),
        "lru_w_out": nrm((nC, LRU_WIDTH, D_MODEL), LRU_WIDTH ** -0.5),
        "sb_w_qkv": nrm((nD, D_MODEL, 3 * D_MODEL), D_MODEL ** -0.5),
        "sb_q_g": gain((nD, SB_HEAD_DIM)),
        "sb_k_g": gain((nD, SB_HEAD_DIM)),
        "sb_w_o": nrm((nD, D_MODEL, D_MODEL), D_MODEL ** -0.5),
        "ffn_w_in": nrm((DEPTH, D_MODEL, 2 * FFN_HIDDEN), D_MODEL ** -0.5),
        "ffn_conv_w": nrm((DEPTH, FFN_CONV, 2 * FFN_HIDDEN), FFN_CONV ** -0.5),
        "ffn_conv_b": nrm((DEPTH, 2 * FFN_HIDDEN), 0.01),
        "ffn_w_out": nrm((DEPTH, FFN_HIDDEN, D_MODEL), FFN_HIDDEN ** -0.5),
    }


def _fwd_reference(x, norm_mix_g, norm_ffn_g,
              pool_w, pool_b, pool_scale,
              s5_lam_re, s5_lam_im, s5_log_dt, s5_b_re, s5_b_im, s5_c_re, s5_c_im, s5_d, s5_w_out, s5_b_out,
              lru_w_in, lru_conv_w, lru_conv_b, lru_w_a, lru_b_a, lru_w_x, lru_b_x, lru_lam, lru_w_out,
              sb_w_qkv, sb_q_g, sb_k_g, sb_w_o,
              ffn_w_in, ffn_conv_w, ffn_conv_b, ffn_w_out):
    for layer in range(DEPTH):
        m, j = layer % N_MIXERS, layer // N_MIXERS
        h = rms_norm(x, norm_mix_g[layer])
        if m == 0:
            y = pool_mixer(h, pool_w[j], pool_b[j], pool_scale[j])
        elif m == 1:
            y = s5_mixer(h, s5_lam_re[j], s5_lam_im[j], s5_log_dt[j], s5_b_re[j], s5_b_im[j],
                         s5_c_re[j], s5_c_im[j], s5_d[j], s5_w_out[j], s5_b_out[j])
        elif m == 2:
            y = rglru_mixer(h, lru_w_in[j], lru_conv_w[j], lru_conv_b[j], lru_w_a[j], lru_b_a[j],
                            lru_w_x[j], lru_b_x[j], lru_lam[j], lru_w_out[j])
        else:
            y = stick_breaking_mixer(h, sb_w_qkv[j], sb_q_g[j], sb_k_g[j], sb_w_o[j])
        x = x + y.astype(x.dtype)
        f = conv_ffn(rms_norm(x, norm_ffn_g[layer]), ffn_w_in[layer], ffn_conv_w[layer], ffn_conv_b[layer], ffn_w_out[layer])
        x = x + f.astype(x.dtype)
    return x


import jax as _jax
import jax.numpy as _jnp

TWIN_FORMAT = 'train_step'
FWD_PARAMS = ['x', 'norm_mix_g', 'norm_ffn_g', 'pool_w', 'pool_b', 'pool_scale', 's5_lam_re', 's5_lam_im', 's5_log_dt', 's5_b_re', 's5_b_im', 's5_c_re', 's5_c_im', 's5_d', 's5_w_out', 's5_b_out', 'lru_w_in', 'lru_conv_w', 'lru_conv_b', 'lru_w_a', 'lru_b_a', 'lru_w_x', 'lru_b_x', 'lru_lam', 'lru_w_out', 'sb_w_qkv', 'sb_q_g', 'sb_k_g', 'sb_w_o', 'ffn_w_in', 'ffn_conv_w', 'ffn_conv_b', 'ffn_w_out']
TWIN_WEIGHTS = ['norm_mix_g', 'norm_ffn_g', 'pool_w', 'pool_b', 'pool_scale', 's5_lam_re', 's5_lam_im', 's5_log_dt', 's5_b_re', 's5_b_im', 's5_c_re', 's5_c_im', 's5_d', 's5_w_out', 's5_b_out', 'lru_w_in', 'lru_conv_w', 'lru_conv_b', 'lru_w_a', 'lru_b_a', 'lru_w_x', 'lru_b_x', 'lru_lam', 'lru_w_out', 'sb_w_qkv', 'sb_q_g', 'sb_k_g', 'sb_w_o', 'ffn_w_in', 'ffn_conv_w', 'ffn_conv_b', 'ffn_w_out']
TWIN_DIFF_INPUT = 'x'
TWIN_INPUTS = ['x', 'norm_mix_g', 'norm_ffn_g', 'pool_w', 'pool_b', 'pool_scale', 's5_lam_re', 's5_lam_im', 's5_log_dt', 's5_b_re', 's5_b_im', 's5_c_re', 's5_c_im', 's5_d', 's5_w_out', 's5_b_out', 'lru_w_in', 'lru_conv_w', 'lru_conv_b', 'lru_w_a', 'lru_b_a', 'lru_w_x', 'lru_b_x', 'lru_lam', 'lru_w_out', 'sb_w_qkv', 'sb_q_g', 'sb_k_g', 'sb_w_o', 'ffn_w_in', 'ffn_conv_w', 'ffn_conv_b', 'ffn_w_out', 'loss_target', 'm_norm_mix_g', 'm_norm_ffn_g', 'm_pool_w', 'm_pool_b', 'm_pool_scale', 'm_s5_lam_re', 'm_s5_lam_im', 'm_s5_log_dt', 'm_s5_b_re', 'm_s5_b_im', 'm_s5_c_re', 'm_s5_c_im', 'm_s5_d', 'm_s5_w_out', 'm_s5_b_out', 'm_lru_w_in', 'm_lru_conv_w', 'm_lru_conv_b', 'm_lru_w_a', 'm_lru_b_a', 'm_lru_w_x', 'm_lru_b_x', 'm_lru_lam', 'm_lru_w_out', 'm_sb_w_qkv', 'm_sb_q_g', 'm_sb_k_g', 'm_sb_w_o', 'm_ffn_w_in', 'm_ffn_conv_w', 'm_ffn_conv_b', 'm_ffn_w_out', 'v_norm_mix_g', 'v_norm_ffn_g', 'v_pool_w', 'v_pool_b', 'v_pool_scale', 'v_s5_lam_re', 'v_s5_lam_im', 'v_s5_log_dt', 'v_s5_b_re', 'v_s5_b_im', 'v_s5_c_re', 'v_s5_c_im', 'v_s5_d', 'v_s5_w_out', 'v_s5_b_out', 'v_lru_w_in', 'v_lru_conv_w', 'v_lru_conv_b', 'v_lru_w_a', 'v_lru_b_a', 'v_lru_w_x', 'v_lru_b_x', 'v_lru_lam', 'v_lru_w_out', 'v_sb_w_qkv', 'v_sb_q_g', 'v_sb_k_g', 'v_sb_w_o', 'v_ffn_w_in', 'v_ffn_conv_w', 'v_ffn_conv_b', 'v_ffn_w_out']
TWIN_OUTPUTS = ['loss', 'grad_x', 'grad_norm_mix_g', 'grad_norm_ffn_g', 'grad_pool_w', 'grad_pool_b', 'grad_pool_scale', 'grad_s5_lam_re', 'grad_s5_lam_im', 'grad_s5_log_dt', 'grad_s5_b_re', 'grad_s5_b_im', 'grad_s5_c_re', 'grad_s5_c_im', 'grad_s5_d', 'grad_s5_w_out', 'grad_s5_b_out', 'grad_lru_w_in', 'grad_lru_conv_w', 'grad_lru_conv_b', 'grad_lru_w_a', 'grad_lru_b_a', 'grad_lru_w_x', 'grad_lru_b_x', 'grad_lru_lam', 'grad_lru_w_out', 'grad_sb_w_qkv', 'grad_sb_q_g', 'grad_sb_k_g', 'grad_sb_w_o', 'grad_ffn_w_in', 'grad_ffn_conv_w', 'grad_ffn_conv_b', 'grad_ffn_w_out', 'delta_norm_mix_g', 'delta_norm_ffn_g', 'delta_pool_w', 'delta_pool_b', 'delta_pool_scale', 'delta_s5_lam_re', 'delta_s5_lam_im', 'delta_s5_log_dt', 'delta_s5_b_re', 'delta_s5_b_im', 'delta_s5_c_re', 'delta_s5_c_im', 'delta_s5_d', 'delta_s5_w_out', 'delta_s5_b_out', 'delta_lru_w_in', 'delta_lru_conv_w', 'delta_lru_conv_b', 'delta_lru_w_a', 'delta_lru_b_a', 'delta_lru_w_x', 'delta_lru_b_x', 'delta_lru_lam', 'delta_lru_w_out', 'delta_sb_w_qkv', 'delta_sb_q_g', 'delta_sb_k_g', 'delta_sb_w_o', 'delta_ffn_w_in', 'delta_ffn_conv_w', 'delta_ffn_conv_b', 'delta_ffn_w_out', 'new_m_norm_mix_g', 'new_m_norm_ffn_g', 'new_m_pool_w', 'new_m_pool_b', 'new_m_pool_scale', 'new_m_s5_lam_re', 'new_m_s5_lam_im', 'new_m_s5_log_dt', 'new_m_s5_b_re', 'new_m_s5_b_im', 'new_m_s5_c_re', 'new_m_s5_c_im', 'new_m_s5_d', 'new_m_s5_w_out', 'new_m_s5_b_out', 'new_m_lru_w_in', 'new_m_lru_conv_w', 'new_m_lru_conv_b', 'new_m_lru_w_a', 'new_m_lru_b_a', 'new_m_lru_w_x', 'new_m_lru_b_x', 'new_m_lru_lam', 'new_m_lru_w_out', 'new_m_sb_w_qkv', 'new_m_sb_q_g', 'new_m_sb_k_g', 'new_m_sb_w_o', 'new_m_ffn_w_in', 'new_m_ffn_conv_w', 'new_m_ffn_conv_b', 'new_m_ffn_w_out', 'new_v_norm_mix_g', 'new_v_norm_ffn_g', 'new_v_pool_w', 'new_v_pool_b', 'new_v_pool_scale', 'new_v_s5_lam_re', 'new_v_s5_lam_im', 'new_v_s5_log_dt', 'new_v_s5_b_re', 'new_v_s5_b_im', 'new_v_s5_c_re', 'new_v_s5_c_im', 'new_v_s5_d', 'new_v_s5_w_out', 'new_v_s5_b_out', 'new_v_lru_w_in', 'new_v_lru_conv_w', 'new_v_lru_conv_b', 'new_v_lru_w_a', 'new_v_lru_b_a', 'new_v_lru_w_x', 'new_v_lru_b_x', 'new_v_lru_lam', 'new_v_lru_w_out', 'new_v_sb_w_qkv', 'new_v_sb_q_g', 'new_v_sb_k_g', 'new_v_sb_w_o', 'new_v_ffn_w_in', 'new_v_ffn_conv_w', 'new_v_ffn_conv_b', 'new_v_ffn_w_out']
TWIN_LEAF_KINDS = {'loss': 'loss', 'grad_x': 'grad_x', 'grad_norm_mix_g': 'grad_w', 'grad_norm_ffn_g': 'grad_w', 'grad_pool_w': 'grad_w', 'grad_pool_b': 'grad_w', 'grad_pool_scale': 'grad_w', 'grad_s5_lam_re': 'grad_w', 'grad_s5_lam_im': 'grad_w', 'grad_s5_log_dt': 'grad_w', 'grad_s5_b_re': 'grad_w', 'grad_s5_b_im': 'grad_w', 'grad_s5_c_re': 'grad_w', 'grad_s5_c_im': 'grad_w', 'grad_s5_d': 'grad_w', 'grad_s5_w_out': 'grad_w', 'grad_s5_b_out': 'grad_w', 'grad_lru_w_in': 'grad_w', 'grad_lru_conv_w': 'grad_w', 'grad_lru_conv_b': 'grad_w', 'grad_lru_w_a': 'grad_w', 'grad_lru_b_a': 'grad_w', 'grad_lru_w_x': 'grad_w', 'grad_lru_b_x': 'grad_w', 'grad_lru_lam': 'grad_w', 'grad_lru_w_out': 'grad_w', 'grad_sb_w_qkv': 'grad_w', 'grad_sb_q_g': 'grad_w', 'grad_sb_k_g': 'grad_w', 'grad_sb_w_o': 'grad_w', 'grad_ffn_w_in': 'grad_w', 'grad_ffn_conv_w': 'grad_w', 'grad_ffn_conv_b': 'grad_w', 'grad_ffn_w_out': 'grad_w', 'delta_norm_mix_g': 'delta_w', 'delta_norm_ffn_g': 'delta_w', 'delta_pool_w': 'delta_w', 'delta_pool_b': 'delta_w', 'delta_pool_scale': 'delta_w', 'delta_s5_lam_re': 'delta_w', 'delta_s5_lam_im': 'delta_w', 'delta_s5_log_dt': 'delta_w', 'delta_s5_b_re': 'delta_w', 'delta_s5_b_im': 'delta_w', 'delta_s5_c_re': 'delta_w', 'delta_s5_c_im': 'delta_w', 'delta_s5_d': 'delta_w', 'delta_s5_w_out': 'delta_w', 'delta_s5_b_out': 'delta_w', 'delta_lru_w_in': 'delta_w', 'delta_lru_conv_w': 'delta_w', 'delta_lru_conv_b': 'delta_w', 'delta_lru_w_a': 'delta_w', 'delta_lru_b_a': 'delta_w', 'delta_lru_w_x': 'delta_w', 'delta_lru_b_x': 'delta_w', 'delta_lru_lam': 'delta_w', 'delta_lru_w_out': 'delta_w', 'delta_sb_w_qkv': 'delta_w', 'delta_sb_q_g': 'delta_w', 'delta_sb_k_g': 'delta_w', 'delta_sb_w_o': 'delta_w', 'delta_ffn_w_in': 'delta_w', 'delta_ffn_conv_w': 'delta_w', 'delta_ffn_conv_b': 'delta_w', 'delta_ffn_w_out': 'delta_w', 'new_m_norm_mix_g': 'new_m', 'new_m_norm_ffn_g': 'new_m', 'new_m_pool_w': 'new_m', 'new_m_pool_b': 'new_m', 'new_m_pool_scale': 'new_m', 'new_m_s5_lam_re': 'new_m', 'new_m_s5_lam_im': 'new_m', 'new_m_s5_log_dt': 'new_m', 'new_m_s5_b_re': 'new_m', 'new_m_s5_b_im': 'new_m', 'new_m_s5_c_re': 'new_m', 'new_m_s5_c_im': 'new_m', 'new_m_s5_d': 'new_m', 'new_m_s5_w_out': 'new_m', 'new_m_s5_b_out': 'new_m', 'new_m_lru_w_in': 'new_m', 'new_m_lru_conv_w': 'new_m', 'new_m_lru_conv_b': 'new_m', 'new_m_lru_w_a': 'new_m', 'new_m_lru_b_a': 'new_m', 'new_m_lru_w_x': 'new_m', 'new_m_lru_b_x': 'new_m', 'new_m_lru_lam': 'new_m', 'new_m_lru_w_out': 'new_m', 'new_m_sb_w_qkv': 'new_m', 'new_m_sb_q_g': 'new_m', 'new_m_sb_k_g': 'new_m', 'new_m_sb_w_o': 'new_m', 'new_m_ffn_w_in': 'new_m', 'new_m_ffn_conv_w': 'new_m', 'new_m_ffn_conv_b': 'new_m', 'new_m_ffn_w_out': 'new_m', 'new_v_norm_mix_g': 'new_v', 'new_v_norm_ffn_g': 'new_v', 'new_v_pool_w': 'new_v', 'new_v_pool_b': 'new_v', 'new_v_pool_scale': 'new_v', 'new_v_s5_lam_re': 'new_v', 'new_v_s5_lam_im': 'new_v', 'new_v_s5_log_dt': 'new_v', 'new_v_s5_b_re': 'new_v', 'new_v_s5_b_im': 'new_v', 'new_v_s5_c_re': 'new_v', 'new_v_s5_c_im': 'new_v', 'new_v_s5_d': 'new_v', 'new_v_s5_w_out': 'new_v', 'new_v_s5_b_out': 'new_v', 'new_v_lru_w_in': 'new_v', 'new_v_lru_conv_w': 'new_v', 'new_v_lru_conv_b': 'new_v', 'new_v_lru_w_a': 'new_v', 'new_v_lru_b_a': 'new_v', 'new_v_lru_w_x': 'new_v', 'new_v_lru_b_x': 'new_v', 'new_v_lru_lam': 'new_v', 'new_v_lru_w_out': 'new_v', 'new_v_sb_w_qkv': 'new_v', 'new_v_sb_q_g': 'new_v', 'new_v_sb_k_g': 'new_v', 'new_v_sb_w_o': 'new_v', 'new_v_ffn_w_in': 'new_v', 'new_v_ffn_conv_w': 'new_v', 'new_v_ffn_conv_b': 'new_v', 'new_v_ffn_w_out': 'new_v'}


def _forward(args):
    return _fwd_reference(*[args[k] for k in FWD_PARAMS])


def _output_shape():
    out = _jax.eval_shape(lambda: _forward(_fwd_setup_inputs(0)))
    return out.shape, out.dtype

N_MICROBATCH = 1
ADAM_LR = 0.001
ADAM_B1 = 0.9
ADAM_B2 = 0.999
ADAM_EPS = 1e-08
ADAM_WD = 0.01
ADAM_STEP = 10
PER_EXAMPLE_BATCH_AXIS = {'x': 0, 'loss_target': 0}
SHARED_INPUTS = []
_WEIGHT_DTYPES = {'norm_mix_g': _jnp.float32, 'norm_ffn_g': _jnp.float32, 'pool_w': _jnp.float32, 'pool_b': _jnp.float32, 'pool_scale': _jnp.float32, 's5_lam_re': _jnp.float32, 's5_lam_im': _jnp.float32, 's5_log_dt': _jnp.float32, 's5_b_re': _jnp.float32, 's5_b_im': _jnp.float32, 's5_c_re': _jnp.float32, 's5_c_im': _jnp.float32, 's5_d': _jnp.float32, 's5_w_out': _jnp.float32, 's5_b_out': _jnp.float32, 'lru_w_in': _jnp.float32, 'lru_conv_w': _jnp.float32, 'lru_conv_b': _jnp.float32, 'lru_w_a': _jnp.float32, 'lru_b_a': _jnp.float32, 'lru_w_x': _jnp.float32, 'lru_b_x': _jnp.float32, 'lru_lam': _jnp.float32, 'lru_w_out': _jnp.float32, 'sb_w_qkv': _jnp.float32, 'sb_q_g': _jnp.float32, 'sb_k_g': _jnp.float32, 'sb_w_o': _jnp.float32, 'ffn_w_in': _jnp.float32, 'ffn_conv_w': _jnp.float32, 'ffn_conv_b': _jnp.float32, 'ffn_w_out': _jnp.float32}
MOMENT_SCALE = {'norm_mix_g': 1.601752e+01, 'norm_ffn_g': 2.620197e+01, 'pool_w': 1.700833e+00, 'pool_b': 2.554175e+01, 'pool_scale': 2.428938e+01, 's5_lam_re': 3.393967e-02, 's5_lam_im': 2.812418e-02, 's5_log_dt': 1.717771e+01, 's5_b_re': 2.058777e-02, 's5_b_im': 2.134838e-02, 's5_c_re': 2.997562e-02, 's5_c_im': 2.946688e-02, 's5_d': 7.271618e+00, 's5_w_out': 2.878395e+00, 's5_b_out': 8.309195e+00, 'lru_w_in': 2.124143e+00, 'lru_conv_w': 3.899812e+00, 'lru_conv_b': 1.441949e+01, 'lru_w_a': 4.590500e-01, 'lru_b_a': 5.198330e-01, 'lru_w_x': 9.539512e-01, 'lru_b_x': 2.362602e+00, 'lru_lam': 1.175763e+00, 'lru_w_out': 8.796994e-01, 'sb_w_qkv': 5.360641e-01, 'sb_q_g': 3.200669e+01, 'sb_k_g': 3.204207e+01, 'sb_w_o': 8.854679e-01, 'ffn_w_in': 4.546013e-01, 'ffn_conv_w': 3.561149e+00, 'ffn_conv_b': 3.898646e+00, 'ffn_w_out': 5.555724e-01}


def _to_microbatches(a, axis):
    t = _jnp.moveaxis(a, axis, 0)
    t = t.reshape((N_MICROBATCH, t.shape[0] // N_MICROBATCH) + t.shape[1:])
    return _jnp.moveaxis(t, 1, axis + 1)


def setup_inputs(seed: int = 0) -> dict:
    inp = _fwd_setup_inputs(seed)
    key = _jax.random.fold_in(_jax.random.key(seed), 7919)
    shape, _ = _output_shape()
    out = dict(inp)
    out["loss_target"] = _jax.random.normal(_jax.random.fold_in(key, 0), shape, _jnp.float32)
    for i, name in enumerate(TWIN_WEIGHTS):
        w = inp[name].astype(_jnp.float32)
        if MOMENT_SCALE is None:
            s = _jnp.sqrt(_jnp.mean(_jnp.square(w)) + 1e-30)
        else:
            s = MOMENT_SCALE[name]
        km, kv = _jax.random.split(_jax.random.fold_in(key, i + 1))
        out[name] = w
        out["m_" + name] = s * _jax.random.normal(km, w.shape, _jnp.float32)
        out["v_" + name] = (s * s) * _jax.random.uniform(kv, w.shape, _jnp.float32, 0.5, 1.5)
    if N_MICROBATCH > 1:
        for name, axis in PER_EXAMPLE_BATCH_AXIS.items():
            out[name] = _to_microbatches(out[name], axis)
    return {'x': out['x'], 'norm_mix_g': out['norm_mix_g'], 'norm_ffn_g': out['norm_ffn_g'], 'pool_w': out['pool_w'], 'pool_b': out['pool_b'], 'pool_scale': out['pool_scale'], 's5_lam_re': out['s5_lam_re'], 's5_lam_im': out['s5_lam_im'], 's5_log_dt': out['s5_log_dt'], 's5_b_re': out['s5_b_re'], 's5_b_im': out['s5_b_im'], 's5_c_re': out['s5_c_re'], 's5_c_im': out['s5_c_im'], 's5_d': out['s5_d'], 's5_w_out': out['s5_w_out'], 's5_b_out': out['s5_b_out'], 'lru_w_in': out['lru_w_in'], 'lru_conv_w': out['lru_conv_w'], 'lru_conv_b': out['lru_conv_b'], 'lru_w_a': out['lru_w_a'], 'lru_b_a': out['lru_b_a'], 'lru_w_x': out['lru_w_x'], 'lru_b_x': out['lru_b_x'], 'lru_lam': out['lru_lam'], 'lru_w_out': out['lru_w_out'], 'sb_w_qkv': out['sb_w_qkv'], 'sb_q_g': out['sb_q_g'], 'sb_k_g': out['sb_k_g'], 'sb_w_o': out['sb_w_o'], 'ffn_w_in': out['ffn_w_in'], 'ffn_conv_w': out['ffn_conv_w'], 'ffn_conv_b': out['ffn_conv_b'], 'ffn_w_out': out['ffn_w_out'], 'loss_target': out['loss_target'], 'm_norm_mix_g': out['m_norm_mix_g'], 'm_norm_ffn_g': out['m_norm_ffn_g'], 'm_pool_w': out['m_pool_w'], 'm_pool_b': out['m_pool_b'], 'm_pool_scale': out['m_pool_scale'], 'm_s5_lam_re': out['m_s5_lam_re'], 'm_s5_lam_im': out['m_s5_lam_im'], 'm_s5_log_dt': out['m_s5_log_dt'], 'm_s5_b_re': out['m_s5_b_re'], 'm_s5_b_im': out['m_s5_b_im'], 'm_s5_c_re': out['m_s5_c_re'], 'm_s5_c_im': out['m_s5_c_im'], 'm_s5_d': out['m_s5_d'], 'm_s5_w_out': out['m_s5_w_out'], 'm_s5_b_out': out['m_s5_b_out'], 'm_lru_w_in': out['m_lru_w_in'], 'm_lru_conv_w': out['m_lru_conv_w'], 'm_lru_conv_b': out['m_lru_conv_b'], 'm_lru_w_a': out['m_lru_w_a'], 'm_lru_b_a': out['m_lru_b_a'], 'm_lru_w_x': out['m_lru_w_x'], 'm_lru_b_x': out['m_lru_b_x'], 'm_lru_lam': out['m_lru_lam'], 'm_lru_w_out': out['m_lru_w_out'], 'm_sb_w_qkv': out['m_sb_w_qkv'], 'm_sb_q_g': out['m_sb_q_g'], 'm_sb_k_g': out['m_sb_k_g'], 'm_sb_w_o': out['m_sb_w_o'], 'm_ffn_w_in': out['m_ffn_w_in'], 'm_ffn_conv_w': out['m_ffn_conv_w'], 'm_ffn_conv_b': out['m_ffn_conv_b'], 'm_ffn_w_out': out['m_ffn_w_out'], 'v_norm_mix_g': out['v_norm_mix_g'], 'v_norm_ffn_g': out['v_norm_ffn_g'], 'v_pool_w': out['v_pool_w'], 'v_pool_b': out['v_pool_b'], 'v_pool_scale': out['v_pool_scale'], 'v_s5_lam_re': out['v_s5_lam_re'], 'v_s5_lam_im': out['v_s5_lam_im'], 'v_s5_log_dt': out['v_s5_log_dt'], 'v_s5_b_re': out['v_s5_b_re'], 'v_s5_b_im': out['v_s5_b_im'], 'v_s5_c_re': out['v_s5_c_re'], 'v_s5_c_im': out['v_s5_c_im'], 'v_s5_d': out['v_s5_d'], 'v_s5_w_out': out['v_s5_w_out'], 'v_s5_b_out': out['v_s5_b_out'], 'v_lru_w_in': out['v_lru_w_in'], 'v_lru_conv_w': out['v_lru_conv_w'], 'v_lru_conv_b': out['v_lru_conv_b'], 'v_lru_w_a': out['v_lru_w_a'], 'v_lru_b_a': out['v_lru_b_a'], 'v_lru_w_x': out['v_lru_w_x'], 'v_lru_b_x': out['v_lru_b_x'], 'v_lru_lam': out['v_lru_lam'], 'v_lru_w_out': out['v_lru_w_out'], 'v_sb_w_qkv': out['v_sb_w_qkv'], 'v_sb_q_g': out['v_sb_q_g'], 'v_sb_k_g': out['v_sb_k_g'], 'v_sb_w_o': out['v_sb_w_o'], 'v_ffn_w_in': out['v_ffn_w_in'], 'v_ffn_conv_w': out['v_ffn_conv_w'], 'v_ffn_conv_b': out['v_ffn_conv_b'], 'v_ffn_w_out': out['v_ffn_w_out']}


def _loss(weights, diff, rest, loss_target):
    with _jax.named_scope("forward"):
        args = {**rest, TWIN_DIFF_INPUT: diff, **{k: w.astype(_WEIGHT_DTYPES[k]) for k, w in weights.items()}}
        y = _forward(args)
    with _jax.named_scope("loss_head"):
        err = _jnp.square(y.astype(_jnp.float32) - loss_target)
        return 0.5 * _jnp.sum(_jnp.mean(err, axis=-1)) if err.ndim else 0.5 * err


def _adamw(w, g, m, v):
    m = ADAM_B1 * m + (1.0 - ADAM_B1) * g
    v = ADAM_B2 * v + (1.0 - ADAM_B2) * _jnp.square(g)
    m_hat = m / (1.0 - ADAM_B1 ** ADAM_STEP)
    v_hat = v / (1.0 - ADAM_B2 ** ADAM_STEP)
    delta = -ADAM_LR * (m_hat / (_jnp.sqrt(v_hat) + ADAM_EPS) + ADAM_WD * w)
    return delta, m, v


def reference(x, norm_mix_g, norm_ffn_g, pool_w, pool_b, pool_scale, s5_lam_re, s5_lam_im, s5_log_dt, s5_b_re, s5_b_im, s5_c_re, s5_c_im, s5_d, s5_w_out, s5_b_out, lru_w_in, lru_conv_w, lru_conv_b, lru_w_a, lru_b_a, lru_w_x, lru_b_x, lru_lam, lru_w_out, sb_w_qkv, sb_q_g, sb_k_g, sb_w_o, ffn_w_in, ffn_conv_w, ffn_conv_b, ffn_w_out, loss_target, m_norm_mix_g, m_norm_ffn_g, m_pool_w, m_pool_b, m_pool_scale, m_s5_lam_re, m_s5_lam_im, m_s5_log_dt, m_s5_b_re, m_s5_b_im, m_s5_c_re, m_s5_c_im, m_s5_d, m_s5_w_out, m_s5_b_out, m_lru_w_in, m_lru_conv_w, m_lru_conv_b, m_lru_w_a, m_lru_b_a, m_lru_w_x, m_lru_b_x, m_lru_lam, m_lru_w_out, m_sb_w_qkv, m_sb_q_g, m_sb_k_g, m_sb_w_o, m_ffn_w_in, m_ffn_conv_w, m_ffn_conv_b, m_ffn_w_out, v_norm_mix_g, v_norm_ffn_g, v_pool_w, v_pool_b, v_pool_scale, v_s5_lam_re, v_s5_lam_im, v_s5_log_dt, v_s5_b_re, v_s5_b_im, v_s5_c_re, v_s5_c_im, v_s5_d, v_s5_w_out, v_s5_b_out, v_lru_w_in, v_lru_conv_w, v_lru_conv_b, v_lru_w_a, v_lru_b_a, v_lru_w_x, v_lru_b_x, v_lru_lam, v_lru_w_out, v_sb_w_qkv, v_sb_q_g, v_sb_k_g, v_sb_w_o, v_ffn_w_in, v_ffn_conv_w, v_ffn_conv_b, v_ffn_w_out):
    given = dict(x=x, norm_mix_g=norm_mix_g, norm_ffn_g=norm_ffn_g, pool_w=pool_w, pool_b=pool_b, pool_scale=pool_scale, s5_lam_re=s5_lam_re, s5_lam_im=s5_lam_im, s5_log_dt=s5_log_dt, s5_b_re=s5_b_re, s5_b_im=s5_b_im, s5_c_re=s5_c_re, s5_c_im=s5_c_im, s5_d=s5_d, s5_w_out=s5_w_out, s5_b_out=s5_b_out, lru_w_in=lru_w_in, lru_conv_w=lru_conv_w, lru_conv_b=lru_conv_b, lru_w_a=lru_w_a, lru_b_a=lru_b_a, lru_w_x=lru_w_x, lru_b_x=lru_b_x, lru_lam=lru_lam, lru_w_out=lru_w_out, sb_w_qkv=sb_w_qkv, sb_q_g=sb_q_g, sb_k_g=sb_k_g, sb_w_o=sb_w_o, ffn_w_in=ffn_w_in, ffn_conv_w=ffn_conv_w, ffn_conv_b=ffn_conv_b, ffn_w_out=ffn_w_out, loss_target=loss_target, m_norm_mix_g=m_norm_mix_g, m_norm_ffn_g=m_norm_ffn_g, m_pool_w=m_pool_w, m_pool_b=m_pool_b, m_pool_scale=m_pool_scale, m_s5_lam_re=m_s5_lam_re, m_s5_lam_im=m_s5_lam_im, m_s5_log_dt=m_s5_log_dt, m_s5_b_re=m_s5_b_re, m_s5_b_im=m_s5_b_im, m_s5_c_re=m_s5_c_re, m_s5_c_im=m_s5_c_im, m_s5_d=m_s5_d, m_s5_w_out=m_s5_w_out, m_s5_b_out=m_s5_b_out, m_lru_w_in=m_lru_w_in, m_lru_conv_w=m_lru_conv_w, m_lru_conv_b=m_lru_conv_b, m_lru_w_a=m_lru_w_a, m_lru_b_a=m_lru_b_a, m_lru_w_x=m_lru_w_x, m_lru_b_x=m_lru_b_x, m_lru_lam=m_lru_lam, m_lru_w_out=m_lru_w_out, m_sb_w_qkv=m_sb_w_qkv, m_sb_q_g=m_sb_q_g, m_sb_k_g=m_sb_k_g, m_sb_w_o=m_sb_w_o, m_ffn_w_in=m_ffn_w_in, m_ffn_conv_w=m_ffn_conv_w, m_ffn_conv_b=m_ffn_conv_b, m_ffn_w_out=m_ffn_w_out, v_norm_mix_g=v_norm_mix_g, v_norm_ffn_g=v_norm_ffn_g, v_pool_w=v_pool_w, v_pool_b=v_pool_b, v_pool_scale=v_pool_scale, v_s5_lam_re=v_s5_lam_re, v_s5_lam_im=v_s5_lam_im, v_s5_log_dt=v_s5_log_dt, v_s5_b_re=v_s5_b_re, v_s5_b_im=v_s5_b_im, v_s5_c_re=v_s5_c_re, v_s5_c_im=v_s5_c_im, v_s5_d=v_s5_d, v_s5_w_out=v_s5_w_out, v_s5_b_out=v_s5_b_out, v_lru_w_in=v_lru_w_in, v_lru_conv_w=v_lru_conv_w, v_lru_conv_b=v_lru_conv_b, v_lru_w_a=v_lru_w_a, v_lru_b_a=v_lru_b_a, v_lru_w_x=v_lru_w_x, v_lru_b_x=v_lru_b_x, v_lru_lam=v_lru_lam, v_lru_w_out=v_lru_w_out, v_sb_w_qkv=v_sb_w_qkv, v_sb_q_g=v_sb_q_g, v_sb_k_g=v_sb_k_g, v_sb_w_o=v_sb_w_o, v_ffn_w_in=v_ffn_w_in, v_ffn_conv_w=v_ffn_conv_w, v_ffn_conv_b=v_ffn_conv_b, v_ffn_w_out=v_ffn_w_out)
    weights = {n: given[n] for n in TWIN_WEIGHTS}
    shared = {n: given[n] for n in SHARED_INPUTS}
    per_example = {n: given[n] for n in ['x']}
    grad_fn = _jax.value_and_grad(_loss, argnums=(0, 1))

    def one_microbatch(ex, loss_target):
        ex = dict(ex)
        diff = ex.pop(TWIN_DIFF_INPUT)
        return grad_fn(weights, diff, {**shared, **ex}, loss_target)

    if N_MICROBATCH == 1:
        loss, (grad_w, grad_x) = one_microbatch(per_example, given["loss_target"])
    else:
        def body(carry, xs):
            loss_sum, grad_sum = carry
            l_k, (gw_k, gx_k) = one_microbatch(xs[0], xs[1])
            with _jax.named_scope("update"):
                return (loss_sum + l_k, _jax.tree.map(_jnp.add, grad_sum, gw_k)), gx_k

        init = (_jnp.zeros((), _jnp.float32), _jax.tree.map(_jnp.zeros_like, weights))
        (loss, grad_w), grad_x = _jax.lax.scan(body, init, (per_example, given["loss_target"]))
    with _jax.named_scope("update"):
        delta_w, new_m, new_v = {}, {}, {}
        for n in TWIN_WEIGHTS:
            delta_w[n], new_m[n], new_v[n] = _adamw(weights[n], grad_w[n], given["m_" + n], given["v_" + n])
    return (loss, grad_x, *[grad_w[n] for n in TWIN_WEIGHTS], *[delta_w[n] for n in TWIN_WEIGHTS],
            *[new_m[n] for n in TWIN_WEIGHTS], *[new_v[n] for n in TWIN_WEIGHTS])
```

```python
import functools
import math

import jax
import jax.numpy as jnp
from jax import lax
from jax.experimental import pallas as pl
from jax.experimental.pallas import tpu as pltpu

F32 = jnp.float32
BF16 = jnp.bfloat16

EPS = 1e-6
N_CHIPS = 4
N_DEV = 8
POOL_WINDOWS = (2, 4, 8, 16)
POOL_GROUP = 256
S5_GROUP = 16
S5_STATE = 64
S5_CHUNKS = 8
S5_LANES = 512
S5_T = 256
LRU_BLOCK = 256
LRU_CONV = 4
LRU_C = 8.0
SB_HEADS = 16
SB_DIM = 64
SB_BLOCK = 128
FFN_CONV = 3
ADAM_LR, ADAM_B1, ADAM_B2, ADAM_EPS, ADAM_WD, ADAM_STEP = 0.001, 0.9, 0.999, 1e-08, 0.01, 10
VMEM_LIMIT_BYTES = 56 * 1024 * 1024

NN = (((1,), (0,)), ((), ()))
NT = (((1,), (1,)), ((), ()))
TN = (((0,), (0,)), ((), ()))


def _cp(*sem):
    return pltpu.CompilerParams(dimension_semantics=sem, vmem_limit_bytes=VMEM_LIMIT_BYTES)


def _pick(n, prefs):
    for p in prefs:
        if n % p == 0:
            return p
    return n


def _dot(a, b, dims=NN):
    return lax.dot_general(a.astype(BF16), b.astype(BF16), dims, preferred_element_type=F32)


def _split(x):
    hi = x.astype(BF16)
    return hi, (x - hi.astype(F32)).astype(BF16)


def _dot3(a, b, dims=NN):
    ah, al = _split(a)
    bh, bl = _split(b)
    d = lambda p, q: lax.dot_general(p, q, dims, preferred_element_type=F32)
    return d(ah, bh) + (d(ah, bl) + d(al, bh))


def _dot_exact_rhs(a, b01):
    ah, al = _split(a)
    d = lambda p: lax.dot_general(p, b01, NN, preferred_element_type=F32)
    return d(ah) + d(al)


def _sig(x):
    return 1.0 / (1.0 + jnp.exp(-x))


def _softplus(x):
    return jnp.maximum(x, 0.0) + jnp.log(1.0 + jnp.exp(-jnp.abs(x)))


_GELU_C = math.sqrt(2.0 / math.pi)


def _gelu(x):
    return 0.5 * x * (1.0 + jnp.tanh(_GELU_C * (x + 0.044715 * x * x * x)))


def _gelu_grad(x):
    th = jnp.tanh(_GELU_C * (x + 0.044715 * x * x * x))
    return 0.5 * (1.0 + th) + 0.5 * x * (1.0 - th * th) * _GELU_C * (1.0 + 3.0 * 0.044715 * x * x)


def _rows(shape):
    return lax.broadcasted_iota(jnp.int32, shape, 0)


def _shift_down(x, k):
    if k >= x.shape[0]:
        return jnp.zeros_like(x)
    return jnp.where(_rows(x.shape) >= k, pltpu.roll(x, k, 0), 0.0)


def _shift_up(x, k):
    t = x.shape[0]
    if k >= t:
        return jnp.zeros_like(x)
    return jnp.where(_rows(x.shape) < t - k, pltpu.roll(x, t - k, 0), 0.0)


def matmul(a, b, *, ta=False, tb=False, bias=None, res=None, out_dtype=F32, name):
    m, k = (a.shape[1], a.shape[0]) if ta else a.shape
    n = b.shape[0] if tb else b.shape[1]
    assert (b.shape[1] if tb else b.shape[0]) == k
    tm = _pick(m, (1024, 512, 1408, 256, 128))
    tn = _pick(n, (1024, 512, 1408, 256, 128))
    tk = _pick(k, (512, 1408, 256, 128))
    nk = k // tk
    dims = (((0 if ta else 1,), (1 if tb else 0,)), ((), ()))
    has_bias, has_res = bias is not None, res is not None

    def body(*refs):
        a_ref, b_ref = refs[:2]
        rest = list(refs[2:])
        bias_ref = rest.pop(0) if has_bias else None
        res_ref = rest.pop(0) if has_res else None
        o_ref, acc_ref = rest
        kk = pl.program_id(2)

        @pl.when(kk == 0)
        def _():
            acc_ref[...] = jnp.zeros_like(acc_ref)

        acc_ref[...] += lax.dot_general(a_ref[...].astype(BF16), b_ref[...].astype(BF16), dims,
                                        preferred_element_type=F32)

        @pl.when(kk == nk - 1)
        def _():
            r = acc_ref[...]
            if has_bias:
                r = r + bias_ref[...]
            if has_res:
                r = r + res_ref[...]
            o_ref[...] = r.astype(o_ref.dtype)

    in_specs = [
        pl.BlockSpec((tk, tm), lambda i, j, kk: (kk, i)) if ta else pl.BlockSpec((tm, tk), lambda i, j, kk: (i, kk)),
        pl.BlockSpec((tn, tk), lambda i, j, kk: (j, kk)) if tb else pl.BlockSpec((tk, tn), lambda i, j, kk: (kk, j)),
    ]
    args = [a, b]
    if has_bias:
        in_specs.append(pl.BlockSpec((1, tn), lambda i, j, kk: (0, j)))
        args.append(bias)
    if has_res:
        in_specs.append(pl.BlockSpec((tm, tn), lambda i, j, kk: (i, j)))
        args.append(res)
    return pl.pallas_call(
        body, grid=(m // tm, n // tn, nk), in_specs=in_specs,
        out_specs=pl.BlockSpec((tm, tn), lambda i, j, kk: (i, j)),
        out_shape=jax.ShapeDtypeStruct((m, n), out_dtype),
        scratch_shapes=[pltpu.VMEM((tm, tn), F32)],
        compiler_params=_cp("parallel", "parallel", "arbitrary"), name=name)(*args)


def rms_fwd(x, g, out_dtype, name):
    m, d = x.shape
    tr = _pick(m, (512, 256, 128))

    def body(x_ref, g_ref, o_ref):
        xv = x_ref[...]
        r = lax.rsqrt(jnp.mean(xv * xv, axis=-1, keepdims=True) + EPS)
        o_ref[...] = (xv * r * g_ref[...]).astype(o_ref.dtype)

    return pl.pallas_call(
        body, grid=(m // tr,),
        in_specs=[pl.BlockSpec((tr, d), lambda i: (i, 0)), pl.BlockSpec((1, d), lambda i: (0, 0))],
        out_specs=pl.BlockSpec((tr, d), lambda i: (i, 0)),
        out_shape=jax.ShapeDtypeStruct((m, d), out_dtype),
        compiler_params=_cp("parallel"), name=name)(x, g)


def rms_bwd(x, g, dh, dres, name):
    m, d = x.shape
    tr = _pick(m, (512, 256, 128))

    def body(x_ref, g_ref, dh_ref, dres_ref, dx_ref, dg_ref):
        xv = x_ref[...]
        r = lax.rsqrt(jnp.mean(xv * xv, axis=-1, keepdims=True) + EPS)
        xh = xv * r
        dhv = dh_ref[...].astype(F32)
        dxh = dhv * g_ref[...]
        dx_ref[...] = dres_ref[...] + r * (dxh - xh * jnp.mean(dxh * xh, axis=-1, keepdims=True))

        @pl.when(pl.program_id(0) == 0)
        def _():
            dg_ref[...] = jnp.zeros_like(dg_ref)

        dg_ref[...] += jnp.sum(dhv * xh, axis=0, keepdims=True)

    row = pl.BlockSpec((tr, d), lambda i: (i, 0))
    vec = pl.BlockSpec((1, d), lambda i: (0, 0))
    return pl.pallas_call(
        body, grid=(m // tr,), in_specs=[row, vec, row, row], out_specs=[row, vec],
        out_shape=[jax.ShapeDtypeStruct((m, d), F32), jax.ShapeDtypeStruct((1, d), F32)],
        compiler_params=_cp("arbitrary"), name=name)(x, g, dh, dres)


def _conv_fwd(u, w_ref, b_ref, kw):
    y = b_ref[...] + w_ref[kw - 1:kw, :] * u
    for k in range(kw - 1):
        y = y + w_ref[k:k + 1, :] * _shift_down(u, kw - 1 - k)
    return y


def _conv_bwd_input(dy, w_ref, kw):
    du = w_ref[kw - 1:kw, :] * dy
    for k in range(kw - 1):
        du = du + w_ref[k:k + 1, :] * _shift_up(dy, kw - 1 - k)
    return du


def _conv_bwd_weight(dy, u, kw):
    out = [jnp.sum(dy * _shift_down(u, kw - 1 - k), axis=0, keepdims=True) for k in range(kw - 1)]
    out.append(jnp.sum(dy * u, axis=0, keepdims=True))
    return out


def ffn_act_fwd(u, cw, cb, name):
    bsz, s, f2 = u.shape
    f = f2 // 2
    tc = _pick(f, (256, 128))
    nj = f // tc

    def body(uv_ref, ug_ref, wv_ref, wg_ref, bv_ref, bg_ref, a_ref):
        hv = _conv_fwd(uv_ref[0], wv_ref, bv_ref, FFN_CONV)
        hg = _conv_fwd(ug_ref[0], wg_ref, bg_ref, FFN_CONV)
        a_ref[0] = (hg * _sig(hg) * hv).astype(a_ref.dtype)

    uv = pl.BlockSpec((1, s, tc), lambda b, j: (b, 0, j))
    ug = pl.BlockSpec((1, s, tc), lambda b, j: (b, 0, j + nj))
    wv = pl.BlockSpec((FFN_CONV, tc), lambda b, j: (0, j))
    wg = pl.BlockSpec((FFN_CONV, tc), lambda b, j: (0, j + nj))
    bv = pl.BlockSpec((1, tc), lambda b, j: (0, j))
    bg = pl.BlockSpec((1, tc), lambda b, j: (0, j + nj))
    return pl.pallas_call(
        body, grid=(bsz, nj), in_specs=[uv, ug, wv, wg, bv, bg], out_specs=uv,
        out_shape=jax.ShapeDtypeStruct((bsz, s, f), BF16),
        compiler_params=_cp("parallel", "parallel"), name=name)(u, u, cw, cw, cb, cb)


def ffn_act_bwd(u, cw, cb, da, name):
    bsz, s, f2 = u.shape
    f = f2 // 2
    tc = _pick(f, (256, 128))
    nj = f // tc

    def body(uv_ref, ug_ref, wv_ref, wg_ref, bv_ref, bg_ref, da_ref,
             duv_ref, dug_ref, dwv_ref, dwg_ref, dbv_ref, dbg_ref):
        uv, ug = uv_ref[0], ug_ref[0]
        hv = _conv_fwd(uv, wv_ref, bv_ref, FFN_CONV)
        hg = _conv_fwd(ug, wg_ref, bg_ref, FFN_CONV)
        sg = _sig(hg)
        dav = da_ref[0].astype(F32)
        dhv = dav * hg * sg
        dhg = dav * hv * (sg * (1.0 + hg * (1.0 - sg)))
        duv_ref[0] = _conv_bwd_input(dhv, wv_ref, FFN_CONV).astype(duv_ref.dtype)
        dug_ref[0] = _conv_bwd_input(dhg, wg_ref, FFN_CONV).astype(dug_ref.dtype)

        @pl.when(pl.program_id(1) == 0)
        def _():
            for r in (dwv_ref, dwg_ref, dbv_ref, dbg_ref):
                r[...] = jnp.zeros_like(r)

        for k, row in enumerate(_conv_bwd_weight(dhv, uv, FFN_CONV)):
            dwv_ref[k:k + 1, :] += row
        for k, row in enumerate(_conv_bwd_weight(dhg, ug, FFN_CONV)):
            dwg_ref[k:k + 1, :] += row
        dbv_ref[...] += jnp.sum(dhv, axis=0, keepdims=True)
        dbg_ref[...] += jnp.sum(dhg, axis=0, keepdims=True)

    uv = pl.BlockSpec((1, s, tc), lambda j, b: (b, 0, j))
    ug = pl.BlockSpec((1, s, tc), lambda j, b: (b, 0, j + nj))
    wv = pl.BlockSpec((FFN_CONV, tc), lambda j, b: (0, j))
    wg = pl.BlockSpec((FFN_CONV, tc), lambda j, b: (0, j + nj))
    bv = pl.BlockSpec((1, tc), lambda j, b: (0, j))
    bg = pl.BlockSpec((1, tc), lambda j, b: (0, j + nj))
    act = jax.ShapeDtypeStruct((bsz, s, f), BF16)
    return pl.pallas_call(
        body, grid=(nj, bsz), in_specs=[uv, ug, wv, wg, bv, bg, uv],
        out_specs=[uv, uv, wv, wv, bv, bv],
        out_shape=[act, act, jax.ShapeDtypeStruct((FFN_CONV, f), F32), jax.ShapeDtypeStruct((FFN_CONV, f), F32),
                   jax.ShapeDtypeStruct((1, f), F32), jax.ShapeDtypeStruct((1, f), F32)],
        compiler_params=_cp("parallel", "arbitrary"), name=name)(u, u, cw, cw, cb, cb, da)


def ffn_fwd(x, g, w_in, cw, cb, w_out, bsz, tag):
    m, d = x.shape
    h = rms_fwd(x, g, BF16, "ffn_norm")
    u = matmul(h, w_in, name="ffn_in")
    a = ffn_act_fwd(u.reshape(bsz, m // bsz, -1), cw, cb, "ffn_act")
    a2 = a.reshape(m, -1)
    out = matmul(a2, w_out, res=x, name="ffn_out")
    return out, (x, h, u, a2)


def ffn_bwd(saved, dout, g, w_in, cw, cb, w_out, bsz):
    x, h, u, a2 = saved
    m, d = x.shape
    da = matmul(dout, w_out, tb=True, out_dtype=BF16, name="ffn_out_dx")
    dw_out = matmul(a2, dout, ta=True, name="ffn_out_dw")
    u3 = u.reshape(bsz, m // bsz, -1)
    duv, dug, dwv, dwg, dbv, dbg = ffn_act_bwd(u3, cw, cb, da.reshape(bsz, m // bsz, -1), "ffn_act_bwd")
    du = jnp.concatenate([duv, dug], axis=-1).reshape(m, -1)
    dh = matmul(du, w_in, tb=True, name="ffn_in_dx")
    dw_in = matmul(h, du, ta=True, name="ffn_in_dw")
    dx, dg = rms_bwd(x, g, dh, dout, "ffn_norm_bwd")
    grads = dict(norm_ffn_g=dg, ffn_w_in=dw_in, ffn_conv_w=jnp.concatenate([dwv, dwg], axis=-1),
                 ffn_conv_b=jnp.concatenate([dbv, dbg], axis=-1), ffn_w_out=dw_out)
    return dx, grads


def loss_head(y, target, name="loss_head"):
    m, d = y.shape
    tr = _pick(m, (512, 256, 128))

    def body(y_ref, t_ref, dy_ref, l_ref):
        e = y_ref[...] - t_ref[...]
        dy_ref[...] = e * (1.0 / d)

        @pl.when(pl.program_id(0) == 0)
        def _():
            l_ref[...] = jnp.zeros_like(l_ref)

        l_ref[...] += jnp.sum(e * e, axis=0, keepdims=True) * (0.5 / d)

    row = pl.BlockSpec((tr, d), lambda i: (i, 0))
    vec = pl.BlockSpec((1, d), lambda i: (0, 0))
    dy, part = pl.pallas_call(
        body, grid=(m // tr,), in_specs=[row, row], out_specs=[row, vec],
        out_shape=[jax.ShapeDtypeStruct((m, d), F32), jax.ShapeDtypeStruct((1, d), F32)],
        compiler_params=_cp("arbitrary"), name=name)(y, target)
    return dy, part


def _pool_windows(h, gi):
    sums, s, width = [], h, 1
    for _ in POOL_WINDOWS:
        s = s + _shift_down(s, width)
        width *= 2
        sums.append(s)
    pos = _rows(h.shape).astype(F32) + 1.0
    wsum, inv = sums[-1], 1.0 / jnp.minimum(pos, float(POOL_WINDOWS[-1]))
    for k in range(len(POOL_WINDOWS) - 2, -1, -1):
        wsum = jnp.where(gi == k, sums[k], wsum)
        inv = jnp.where(gi == k, 1.0 / jnp.minimum(pos, float(POOL_WINDOWS[k])), inv)
    return wsum * inv - h, inv


def _pool_windows_transpose(e, gi):
    sums, s, width = [], e, 1
    for _ in POOL_WINDOWS:
        s = s + _shift_up(s, width)
        width *= 2
        sums.append(s)
    out = sums[-1]
    for k in range(len(POOL_WINDOWS) - 2, -1, -1):
        out = jnp.where(gi == k, sums[k], out)
    return out


def pool_fwd(h, w, b, scale, x, name="pool_fwd"):
    bsz, s, d = h.shape
    ng = d // POOL_GROUP

    def body(h_ref, w_ref, b_ref, s_ref, x_ref, o_ref):
        dd, _ = _pool_windows(h_ref[0], pl.program_id(1))
        y = _dot(dd, w_ref[0]) + b_ref[...]
        o_ref[0] = x_ref[0] + s_ref[...] * y

    act = pl.BlockSpec((1, s, POOL_GROUP), lambda bb, gi: (bb, 0, gi))
    vec = pl.BlockSpec((1, POOL_GROUP), lambda bb, gi: (0, gi))
    return pl.pallas_call(
        body, grid=(bsz, ng),
        in_specs=[act, pl.BlockSpec((1, POOL_GROUP, POOL_GROUP), lambda bb, gi: (gi, 0, 0)), vec, vec, act],
        out_specs=act, out_shape=jax.ShapeDtypeStruct((bsz, s, d), F32),
        compiler_params=_cp("parallel", "parallel"), name=name)(h, w, b, scale, x)


def pool_bwd(h, w, b, scale, dy, name="pool_bwd"):
    bsz, s, d = h.shape
    ng = d // POOL_GROUP

    def body(h_ref, w_ref, b_ref, s_ref, dy_ref, dh_ref, dw_ref, db_ref, ds_ref):
        gi = pl.program_id(0)
        dd, inv = _pool_windows(h_ref[0], gi)
        ypre = _dot(dd, w_ref[0]) + b_ref[...]
        dyv = dy_ref[0]
        dyb = dyv * s_ref[...]

        @pl.when(pl.program_id(1) == 0)
        def _():
            for r in (dw_ref, db_ref, ds_ref):
                r[...] = jnp.zeros_like(r)

        ds_ref[...] += jnp.sum(dyv * ypre, axis=0, keepdims=True)
        db_ref[...] += jnp.sum(dyb, axis=0, keepdims=True)
        dw_ref[0] += _dot(dd, dyb, TN)
        ddd = _dot(dyb, w_ref[0], NT)
        dh_ref[0] = _pool_windows_transpose(ddd * inv, gi) - ddd

    act = pl.BlockSpec((1, s, POOL_GROUP), lambda gi, bb: (bb, 0, gi))
    vec = pl.BlockSpec((1, POOL_GROUP), lambda gi, bb: (0, gi))
    wsp = pl.BlockSpec((1, POOL_GROUP, POOL_GROUP), lambda gi, bb: (gi, 0, 0))
    return pl.pallas_call(
        body, grid=(ng, bsz), in_specs=[act, wsp, vec, vec, act], out_specs=[act, wsp, vec, vec],
        out_shape=[jax.ShapeDtypeStruct((bsz, s, d), F32), jax.ShapeDtypeStruct((ng, POOL_GROUP, POOL_GROUP), F32),
                   jax.ShapeDtypeStruct((1, d), F32), jax.ShapeDtypeStruct((1, d), F32)],
        compiler_params=_cp("parallel", "arbitrary"), name=name)(h, w, b, scale, dy)


def pool_layer_fwd(x, g, w, b, scale, bsz):
    m, d = x.shape
    h = rms_fwd(x, g, F32, "pool_norm")
    out = pool_fwd(h.reshape(bsz, m // bsz, d), w, b, scale, x.reshape(bsz, m // bsz, d))
    return out.reshape(m, d), (x, h)


def pool_layer_bwd(saved, dout, g, w, b, scale, bsz):
    x, h = saved
    m, d = x.shape
    dh, dw, db, ds = pool_bwd(h.reshape(bsz, m // bsz, d), w, b, scale, dout.reshape(bsz, m // bsz, d))
    dx, dg = rms_bwd(x, g, dh.reshape(m, d), dout, "pool_norm_bwd")
    return dx, dict(norm_mix_g=dg, pool_w=dw[None], pool_b=db, pool_scale=ds)


def _scan_fwd(a, b):
    k = 1
    while k < a.shape[0]:
        b = b + a * _shift_down(b, k)
        if 2 * k < a.shape[0]:
            a = a * _shift_down(a, k)
        k *= 2
    return b


def _scan_bwd(a, b):
    k = 1
    while k < a.shape[0]:
        b = b + a * _shift_up(b, k)
        if 2 * k < a.shape[0]:
            a = a * _shift_up(a, k)
        k *= 2
    return b


def _neg_expm1(x):
    series = -x * (1.0 + x * (0.5 + x * (1.0 / 6.0 + x * (1.0 / 24.0 + x * (1.0 / 120.0)))))
    return jnp.where(x > -0.03, series, 1.0 - jnp.exp(x))


def _lru_gates(rec, wa_ref, ba_ref, wx_ref, bx_ref, lam_ref):
    r = _sig(_dot(rec, wa_ref[0]) + ba_ref[...])
    i = _sig(_dot(rec, wx_ref[0]) + bx_ref[...])
    sp = _softplus(-lam_ref[...])
    log_a = -LRU_C * r * sp
    a = jnp.exp(log_a)
    mult = jnp.sqrt(_neg_expm1(2.0 * log_a))
    return r, i, sp, a, mult


def lru_fwd(zz, cw, cb, wa, ba, wx, bx, lam, name="lru_fwd"):
    bsz, s, r2 = zz.shape
    rw = r2 // 2
    nb = rw // LRU_BLOCK

    def body(g_ref, p_ref, cw_ref, cb_ref, wa_ref, ba_ref, wx_ref, bx_ref, lam_ref, h_ref, y_ref):
        rec = _conv_fwd(p_ref[0], cw_ref, cb_ref, LRU_CONV)
        _, i, _, a, mult = _lru_gates(rec, wa_ref, ba_ref, wx_ref, bx_ref, lam_ref)
        hst = _scan_fwd(a, mult * (i * rec))
        h_ref[0] = hst
        y_ref[0] = (_gelu(g_ref[0]) * hst).astype(y_ref.dtype)

    gsp = pl.BlockSpec((1, s, LRU_BLOCK), lambda bb, n: (bb, 0, n))
    psp = pl.BlockSpec((1, s, LRU_BLOCK), lambda bb, n: (bb, 0, n + nb))
    cws = pl.BlockSpec((LRU_CONV, LRU_BLOCK), lambda bb, n: (0, n))
    vec = pl.BlockSpec((1, LRU_BLOCK), lambda bb, n: (0, n))
    wsp = pl.BlockSpec((1, LRU_BLOCK, LRU_BLOCK), lambda bb, n: (n, 0, 0))
    return pl.pallas_call(
        body, grid=(bsz, nb), in_specs=[gsp, psp, cws, vec, wsp, vec, wsp, vec, vec], out_specs=[gsp, gsp],
        out_shape=[jax.ShapeDtypeStruct((bsz, s, rw), F32), jax.ShapeDtypeStruct((bsz, s, rw), BF16)],
        compiler_params=_cp("parallel", "parallel"), name=name)(zz, zz, cw, cb, wa, ba, wx, bx, lam)


def lru_bwd(zz, hst, dy, cw, cb, wa, ba, wx, bx, lam, name="lru_bwd"):
    bsz, s, r2 = zz.shape
    rw = r2 // 2
    nb = rw // LRU_BLOCK

    def body(g_ref, p_ref, h_ref, dy_ref, cw_ref, cb_ref, wa_ref, ba_ref, wx_ref, bx_ref, lam_ref,
             dg_ref, dp_ref, dcw_ref, dcb_ref, dwa_ref, dba_ref, dwx_ref, dbx_ref, dlam_ref):
        pre = p_ref[0]
        rec = _conv_fwd(pre, cw_ref, cb_ref, LRU_CONV)
        r, i, sp, a, mult = _lru_gates(rec, wa_ref, ba_ref, wx_ref, bx_ref, lam_ref)
        hst_v, gate, dyv = h_ref[0], g_ref[0], dy_ref[0]
        dg_ref[0] = (dyv * hst_v * _gelu_grad(gate)).astype(dg_ref.dtype)
        lmb = _scan_bwd(_shift_up(a, 1), dyv * _gelu(gate))
        da = lmb * _shift_down(hst_v, 1)
        dmult = lmb * (i * rec)
        dlog_a = da * a - dmult * (a * a) / mult
        dr = dlog_a * (-LRU_C) * sp
        dra = dr * r * (1.0 - r)
        dxa = lmb * mult * rec * i * (1.0 - i)
        drec = lmb * mult * i + _dot(dra, wa_ref[0], NT) + _dot(dxa, wx_ref[0], NT)
        dp_ref[0] = _conv_bwd_input(drec, cw_ref, LRU_CONV).astype(dp_ref.dtype)

        @pl.when(pl.program_id(1) == 0)
        def _():
            for ref in (dcw_ref, dcb_ref, dwa_ref, dba_ref, dwx_ref, dbx_ref, dlam_ref):
                ref[...] = jnp.zeros_like(ref)

        for k, row in enumerate(_conv_bwd_weight(drec, pre, LRU_CONV)):
            dcw_ref[k:k + 1, :] += row
        dcb_ref[...] += jnp.sum(drec, axis=0, keepdims=True)
        dwa_ref[0] += _dot(rec, dra, TN)
        dwx_ref[0] += _dot(rec, dxa, TN)
        dba_ref[...] += jnp.sum(dra, axis=0, keepdims=True)
        dbx_ref[...] += jnp.sum(dxa, axis=0, keepdims=True)
        dsp = jnp.sum(dlog_a * (-LRU_C) * r, axis=0, keepdims=True)
        dlam_ref[...] += dsp * (-_sig(-lam_ref[...]))

    gsp = pl.BlockSpec((1, s, LRU_BLOCK), lambda n, bb: (bb, 0, n))
    psp = pl.BlockSpec((1, s, LRU_BLOCK), lambda n, bb: (bb, 0, n + nb))
    cws = pl.BlockSpec((LRU_CONV, LRU_BLOCK), lambda n, bb: (0, n))
    vec = pl.BlockSpec((1, LRU_BLOCK), lambda n, bb: (0, n))
    wsp = pl.BlockSpec((1, LRU_BLOCK, LRU_BLOCK), lambda n, bb: (n, 0, 0))
    act = jax.ShapeDtypeStruct((bsz, s, rw), BF16)
    vsh = jax.ShapeDtypeStruct((1, rw), F32)
    wsh = jax.ShapeDtypeStruct((nb, LRU_BLOCK, LRU_BLOCK), F32)
    return pl.pallas_call(
        body, grid=(nb, bsz), in_specs=[gsp, psp, gsp, gsp, cws, vec, wsp, vec, wsp, vec, vec],
        out_specs=[gsp, gsp, cws, vec, wsp, vec, wsp, vec, vec],
        out_shape=[act, act, jax.ShapeDtypeStruct((LRU_CONV, rw), F32), vsh, wsh, vsh, wsh, vsh, vsh],
        compiler_params=_cp("parallel", "arbitrary"), name=name)(zz, zz, hst, dy, cw, cb, wa, ba, wx, bx, lam)


def lru_layer_fwd(x, g, p, bsz):
    m, d = x.shape
    h = rms_fwd(x, g, BF16, "lru_norm")
    zz = matmul(h, p["lru_w_in"], name="lru_in")
    hst, y = lru_fwd(zz.reshape(bsz, m // bsz, -1), p["lru_conv_w"], p["lru_conv_b"], p["lru_w_a"], p["lru_b_a"],
                     p["lru_w_x"], p["lru_b_x"], p["lru_lam"])
    y2 = y.reshape(m, -1)
    out = matmul(y2, p["lru_w_out"], res=x, name="lru_out")
    return out, (x, h, zz, hst, y2)


def lru_layer_bwd(saved, dout, g, p, bsz):
    x, h, zz, hst, y2 = saved
    m, d = x.shape
    dy = matmul(dout, p["lru_w_out"], tb=True, name="lru_out_dx")
    dw_out = matmul(y2, dout, ta=True, name="lru_out_dw")
    dgate, dpre, dcw, dcb, dwa, dba, dwx, dbx, dlam = lru_bwd(
        zz.reshape(bsz, m // bsz, -1), hst, dy.reshape(bsz, m // bsz, -1), p["lru_conv_w"], p["lru_conv_b"],
        p["lru_w_a"], p["lru_b_a"], p["lru_w_x"], p["lru_b_x"], p["lru_lam"])
    dzz = jnp.concatenate([dgate, dpre], axis=-1).reshape(m, -1)
    dh = matmul(dzz, p["lru_w_in"], tb=True, name="lru_in_dx")
    dw_in = matmul(h, dzz, ta=True, name="lru_in_dw")
    dx, dg = rms_bwd(x, g, dh, dout, "lru_norm_bwd")
    return dx, dict(norm_mix_g=dg, lru_w_in=dw_in, lru_conv_w=dcw[None], lru_conv_b=dcb, lru_w_a=dwa[None],
                    lru_b_a=dba, lru_w_x=dwx[None], lru_b_x=dbx, lru_lam=dlam, lru_w_out=dw_out)


def _s5_discretise(lam_re, lam_im, log_dt, b_re, b_im):
    lr = jnp.minimum(lam_re, -1e-4)
    dt = jnp.exp(log_dt)[:, None]
    mag = jnp.exp(lr * dt)
    ar, ai = mag * jnp.cos(lam_im * dt), mag * jnp.sin(lam_im * dt)
    den = lr * lr + lam_im * lam_im
    cr = ((ar - 1.0) * lr + ai * lam_im) / den
    ci = (ai * lr - (ar - 1.0) * lam_im) / den
    bbr = cr[..., None] * b_re - ci[..., None] * b_im
    bbi = cr[..., None] * b_im + ci[..., None] * b_re
    return ar, ai, bbr, bbi


def _s5_powers(lam_re, lam_im, log_dt, ns):
    lr = jnp.minimum(lam_re, -1e-4)
    dt = jnp.exp(log_dt)[:, None]
    n = jnp.asarray(ns, F32)[:, None, None]
    mag = jnp.exp(n * (lr * dt))
    ang = n * (lam_im * dt)
    to_chunks = lambda t: t.reshape(len(ns), S5_CHUNKS, S5_LANES).transpose(1, 0, 2)
    return jnp.concatenate([to_chunks(mag * jnp.cos(ang)), to_chunks(mag * jnp.sin(ang))], axis=-1)


def _s5_in_matrix(bbr, bbi):
    eye = jnp.eye(8, dtype=F32)
    blk = lambda t: jnp.einsum("qgph,gk->qghkp", t.reshape(S5_CHUNKS, 8, S5_STATE, S5_GROUP), eye).reshape(
        S5_CHUNKS, 128, S5_LANES)
    return jnp.concatenate([blk(bbr), blk(bbi)], axis=-1)


def _s5_in_matrix_diag(dmat):
    eye = jnp.eye(8, dtype=F32)[None, :, None, :, None]
    pick = lambda t: (t.reshape(S5_CHUNKS, 8, S5_GROUP, 8, S5_STATE) * eye).sum(3).transpose(0, 1, 3, 2).reshape(
        S5_CHUNKS * 8, S5_STATE, S5_GROUP)
    return pick(dmat[..., :S5_LANES]), pick(dmat[..., S5_LANES:])


def _s5_out_matrix(c_re, c_im):
    eye = jnp.eye(8, dtype=F32)
    blk = lambda t: jnp.einsum("qghp,gk->qgpkh", t.reshape(S5_CHUNKS, 8, S5_GROUP, S5_STATE), eye).reshape(
        S5_CHUNKS, S5_LANES, 128)
    return jnp.concatenate([blk(c_re), -blk(c_im)], axis=1)


def _s5_out_matrix_diag(dmat):
    eye = jnp.eye(8, dtype=F32)[None, :, None, :, None]
    pick = lambda t: (t.reshape(S5_CHUNKS, 8, S5_STATE, 8, S5_GROUP) * eye).sum(3).transpose(0, 1, 3, 2).reshape(
        S5_CHUNKS * 8, S5_GROUP, S5_STATE)
    return pick(dmat[:, :S5_LANES]), -pick(dmat[:, S5_LANES:])


def s5_fwd(h, bmat, cmat, atab, pw, dskip, name="s5_fwd"):
    bsz, s, d = h.shape
    t = min(S5_T, s)
    nt, nlev, ln = s // t, atab.shape[1], S5_LANES

    def body(h_ref, b_ref, c_ref, a_ref, pw_ref, d_ref, xs_ref, yp_ref, yg_ref, carry):
        @pl.when(pl.program_id(2) == 0)
        def _():
            carry[...] = jnp.zeros_like(carry)

        u = h_ref[0]
        bu = _dot3(u, b_ref[0])
        xr, xi = bu[:, :ln], bu[:, ln:]
        for k in range(nlev):
            ar, ai = a_ref[0, k:k + 1, :ln], a_ref[0, k:k + 1, ln:]
            sr, si = _shift_down(xr, 1 << k), _shift_down(xi, 1 << k)
            xr, xi = xr + ar * sr - ai * si, xi + ar * si + ai * sr
        cr, ci = carry[0:1, :ln], carry[0:1, ln:]
        pr, pi = pw_ref[0, :, :ln], pw_ref[0, :, ln:]
        xr, xi = xr + pr * cr - pi * ci, xi + pr * ci + pi * cr
        carry[0:1, :ln] = xr[t - 1:t, :]
        carry[0:1, ln:] = xi[t - 1:t, :]
        xs_ref[0, :, :ln] = xr
        xs_ref[0, :, ln:] = xi
        y = _dot3(xr, c_ref[0, :ln, :]) + _dot3(xi, c_ref[0, ln:, :]) + d_ref[...] * u
        yp_ref[0] = y
        yg_ref[0] = _gelu(y).astype(yg_ref.dtype)

    act = pl.BlockSpec((1, t, 128), lambda b, q, i: (b, i, q))
    par = lambda r, c: pl.BlockSpec((1, r, c), lambda b, q, i: (q, 0, 0))
    return pl.pallas_call(
        body, grid=(bsz, S5_CHUNKS, nt),
        in_specs=[act, par(128, 2 * ln), par(2 * ln, 128), par(nlev, 2 * ln), par(t, 2 * ln),
                  pl.BlockSpec((1, 128), lambda b, q, i: (0, q))],
        out_specs=[pl.BlockSpec((1, t, 2 * ln), lambda b, q, i: (b, i, q)), act, act],
        out_shape=[jax.ShapeDtypeStruct((bsz, s, S5_CHUNKS * 2 * ln), F32), jax.ShapeDtypeStruct((bsz, s, d), F32),
                   jax.ShapeDtypeStruct((bsz, s, d), BF16)],
        scratch_shapes=[pltpu.VMEM((8, 2 * ln), F32)],
        compiler_params=_cp("parallel", "parallel", "arbitrary"), name=name)(h, bmat, cmat, atab, pw, dskip)


def s5_bwd(h, ypre, xs, dyg, bmat_t, cmat_t, atab, pw_rev, dskip, name="s5_bwd"):
    bsz, s, d = h.shape
    t = min(S5_T, s)
    nt, nlev, ln = s // t, atab.shape[1], S5_LANES

    def body(h_ref, yp_ref, xs_ref, xp_ref, dy_ref, bt_ref, ct_ref, a_ref, pw_ref, d_ref,
             dh_ref, db_ref, dc_ref, da_ref, dd_ref, carry):
        b, i = pl.program_id(1), pl.program_id(2)

        @pl.when((b == 0) & (i == 0))
        def _():
            for r in (db_ref, dc_ref, da_ref, dd_ref):
                r[...] = jnp.zeros_like(r)

        @pl.when(i == 0)
        def _():
            carry[...] = jnp.zeros_like(carry)

        u = h_ref[0]
        dyp = dy_ref[0] * _gelu_grad(yp_ref[0])
        dd_ref[...] += jnp.sum(dyp * u, axis=0, keepdims=True)
        xr, xi = xs_ref[0, :, :ln], xs_ref[0, :, ln:]
        dc_ref[0, :ln, :] += _dot3(xr, dyp, TN)
        dc_ref[0, ln:, :] += _dot3(xi, dyp, TN)
        lr, li = _dot3(dyp, ct_ref[0, :, :ln]), _dot3(dyp, ct_ref[0, :, ln:])
        for k in range(nlev):
            ar, ai = a_ref[0, k:k + 1, :ln], a_ref[0, k:k + 1, ln:]
            sr, si = _shift_up(lr, 1 << k), _shift_up(li, 1 << k)
            lr, li = lr + ar * sr + ai * si, li + ar * si - ai * sr
        cr, ci = carry[0:1, :ln], carry[0:1, ln:]
        pr, pi = pw_ref[0, :, :ln], pw_ref[0, :, ln:]
        lr, li = lr + pr * cr + pi * ci, li + pr * ci - pi * cr
        carry[0:1, :ln] = lr[0:1, :]
        carry[0:1, ln:] = li[0:1, :]
        dh_ref[0] = _dot3(lr, bt_ref[0, :ln, :]) + _dot3(li, bt_ref[0, ln:, :]) + dyp * d_ref[...]
        db_ref[0, :, :ln] += _dot3(u, lr, TN)
        db_ref[0, :, ln:] += _dot3(u, li, TN)
        first = _rows(xr.shape) == 0
        keep = jnp.where(i == nt - 1, 0.0, 1.0)
        xpr = jnp.where(first, xp_ref[0, 7:8, :ln] * keep, _shift_down(xr, 1))
        xpi = jnp.where(first, xp_ref[0, 7:8, ln:] * keep, _shift_down(xi, 1))
        da_ref[0, 0:1, :ln] += jnp.sum(lr * xpr + li * xpi, axis=0, keepdims=True)
        da_ref[0, 0:1, ln:] += jnp.sum(li * xpr - lr * xpi, axis=0, keepdims=True)

    rev = lambda i: nt - 1 - i
    act = pl.BlockSpec((1, t, 128), lambda q, b, i: (b, rev(i), q))
    xsp = pl.BlockSpec((1, t, 2 * ln), lambda q, b, i: (b, rev(i), q))
    xpp = pl.BlockSpec((1, 8, 2 * ln), lambda q, b, i: (b, jnp.maximum(rev(i) * (t // 8) - 1, 0), q))
    par = lambda r, c: pl.BlockSpec((1, r, c), lambda q, b, i: (q, 0, 0))
    dsp = pl.BlockSpec((1, 128), lambda q, b, i: (0, q))
    return pl.pallas_call(
        body, grid=(S5_CHUNKS, bsz, nt),
        in_specs=[act, act, xsp, xpp, act, par(2 * ln, 128), par(128, 2 * ln), par(nlev, 2 * ln), par(t, 2 * ln), dsp],
        out_specs=[act, par(128, 2 * ln), par(2 * ln, 128), par(8, 2 * ln), dsp],
        out_shape=[jax.ShapeDtypeStruct((bsz, s, d), F32), jax.ShapeDtypeStruct((S5_CHUNKS, 128, 2 * ln), F32),
                   jax.ShapeDtypeStruct((S5_CHUNKS, 2 * ln, 128), F32), jax.ShapeDtypeStruct((S5_CHUNKS, 8, 2 * ln), F32),
                   jax.ShapeDtypeStruct((1, d), F32)],
        scratch_shapes=[pltpu.VMEM((8, 2 * ln), F32)],
        compiler_params=_cp("parallel", "arbitrary", "arbitrary"), name=name)(
            h, ypre, xs, xs, dyg, bmat_t, cmat_t, atab, pw_rev, dskip)


def glu_fwd(z, x, name="s5_glu"):
    m, d = x.shape
    tr = _pick(m, (512, 256, 128))

    def body(z_ref, x_ref, o_ref):
        o_ref[...] = x_ref[...] + z_ref[:, :d] * _sig(z_ref[:, d:])

    return pl.pallas_call(
        body, grid=(m // tr,),
        in_specs=[pl.BlockSpec((tr, 2 * d), lambda i: (i, 0)), pl.BlockSpec((tr, d), lambda i: (i, 0))],
        out_specs=pl.BlockSpec((tr, d), lambda i: (i, 0)), out_shape=jax.ShapeDtypeStruct((m, d), F32),
        compiler_params=_cp("parallel"), name=name)(z, x)


def glu_bwd(z, dout, name="s5_glu_bwd"):
    m, d = dout.shape
    tr = _pick(m, (512, 256, 128))

    def body(z_ref, do_ref, dz_ref, db_ref):
        sg = _sig(z_ref[:, d:])
        dv = do_ref[...] * sg
        dgt = do_ref[...] * z_ref[:, :d] * sg * (1.0 - sg)
        dz_ref[:, :d] = dv.astype(dz_ref.dtype)
        dz_ref[:, d:] = dgt.astype(dz_ref.dtype)

        @pl.when(pl.program_id(0) == 0)
        def _():
            db_ref[...] = jnp.zeros_like(db_ref)

        db_ref[:, :d] += jnp.sum(dv, axis=0, keepdims=True)
        db_ref[:, d:] += jnp.sum(dgt, axis=0, keepdims=True)

    wide = pl.BlockSpec((tr, 2 * d), lambda i: (i, 0))
    return pl.pallas_call(
        body, grid=(m // tr,), in_specs=[wide, pl.BlockSpec((tr, d), lambda i: (i, 0))],
        out_specs=[wide, pl.BlockSpec((1, 2 * d), lambda i: (0, 0))],
        out_shape=[jax.ShapeDtypeStruct((m, 2 * d), BF16), jax.ShapeDtypeStruct((1, 2 * d), F32)],
        compiler_params=_cp("arbitrary"), name=name)(z, dout)


def _s5_tables(p, t):
    nlev = max(1, (t - 1).bit_length())
    lam = (p["s5_lam_re"], p["s5_lam_im"], p["s5_log_dt"])
    atab = _s5_powers(*lam, [1 << k for k in range(nlev)])
    if nlev < 8:
        atab = jnp.pad(atab, ((0, 0), (0, 8 - nlev), (0, 0)))
    pw = _s5_powers(*lam, list(range(1, t + 1)))
    return nlev, atab, pw


def s5_layer_fwd(x, g, p, bsz):
    m, d = x.shape
    s = m // bsz
    t = min(S5_T, s)
    h = rms_fwd(x, g, F32, "s5_norm")
    _, _, bbr, bbi = _s5_discretise(p["s5_lam_re"], p["s5_lam_im"], p["s5_log_dt"], p["s5_b_re"], p["s5_b_im"])
    nlev, atab, pw = _s5_tables(p, t)
    bmat, cmat = _s5_in_matrix(bbr, bbi), _s5_out_matrix(p["s5_c_re"], p["s5_c_im"])
    xs, ypre, yg = s5_fwd(h.reshape(bsz, s, d), bmat, cmat, atab[:, :max(nlev, 8)], pw, p["s5_d"])
    z = matmul(yg.reshape(m, d), p["s5_w_out"], bias=p["s5_b_out"], name="s5_out")
    out = glu_fwd(z, x)
    return out, (x, h, xs, ypre, yg, z, bmat, cmat, atab, pw)


def s5_layer_bwd(saved, dout, g, p, bsz):
    x, h, xs, ypre, yg, z, bmat, cmat, atab, pw = saved
    m, d = x.shape
    s = m // bsz
    dz, db_out = glu_bwd(z, dout)
    dyg = matmul(dz, p["s5_w_out"], tb=True, name="s5_out_dx")
    dw_out = matmul(yg.reshape(m, d), dz, ta=True, name="s5_out_dw")
    dh, dbm, dcm, dlam, dd = s5_bwd(h.reshape(bsz, s, d), ypre, xs, dyg.reshape(bsz, s, d),
                                    bmat.transpose(0, 2, 1), cmat.transpose(0, 2, 1), atab, pw[:, ::-1], p["s5_d"])
    dx, dg = rms_bwd(x, g, dh.reshape(m, d), dout, "s5_norm_bwd")
    dbbr, dbbi = _s5_in_matrix_diag(dbm)
    dc_re, dc_im = _s5_out_matrix_diag(dcm)
    dar = dlam[:, 0, :S5_LANES].reshape(S5_CHUNKS * 8, S5_STATE)
    dai = dlam[:, 0, S5_LANES:].reshape(S5_CHUNKS * 8, S5_STATE)
    _, vjp = jax.vjp(_s5_discretise, p["s5_lam_re"], p["s5_lam_im"], p["s5_log_dt"], p["s5_b_re"], p["s5_b_im"])
    dl_re, dl_im, dldt, db_re, db_im = vjp((dar, dai, dbbr, dbbi))
    return dx, dict(norm_mix_g=dg, s5_lam_re=dl_re[None], s5_lam_im=dl_im[None], s5_log_dt=dldt[None],
                    s5_b_re=db_re[None], s5_b_im=db_im[None], s5_c_re=dc_re[None], s5_c_im=dc_im[None],
                    s5_d=dd, s5_w_out=dw_out, s5_b_out=db_out)


def _log_sigmoid(z):
    return jnp.minimum(z, 0.0) - jnp.log(1.0 + jnp.exp(-jnp.abs(z)))


def _head_norm(t, g_ref):
    r = lax.rsqrt(jnp.mean(t * t, axis=-1, keepdims=True) + EPS)
    th = t * r
    return th * g_ref[...], th, r


def _tri(shape, fn):
    row = lax.broadcasted_iota(jnp.int32, shape, 0)
    col = lax.broadcasted_iota(jnp.int32, shape, 1)
    return fn(row, col)


_SB_SCALE = 1.0 / math.sqrt(SB_DIM)


def sb_fwd(q, k, v, qg, kg, name="sb_fwd"):
    bsz, nh, s, dh = q.shape
    tb = SB_BLOCK
    nq = s // tb

    def body(q_ref, k_ref, v_ref, qg_ref, kg_ref, o_ref, rt_ref):
        qi = pl.program_id(2)
        qn, _, _ = _head_norm(q_ref[0, 0], qg_ref)
        later = _tri((tb, tb), lambda r, c: r > c).astype(BF16)
        causal = _tri((tb, tb), lambda r, c: c < r)

        def block(kb, run, acc, diag):
            ks = pl.ds(pl.multiple_of(kb * tb, tb), tb)
            kn, _, _ = _head_norm(k_ref[0, 0, ks, :], kg_ref)
            z = _dot(qn, kn, NT) * _SB_SCALE
            ls = _log_sigmoid(z)
            lm = ls - z
            if diag:
                lm = jnp.where(causal, lm, 0.0)
            att = jnp.exp(ls + run + _dot_exact_rhs(lm, later))
            if diag:
                att = jnp.where(causal, att, 0.0)
            return run + jnp.sum(lm, axis=1, keepdims=True), acc + _dot(att, v_ref[0, 0, ks, :])

        run, acc = block(qi, jnp.zeros((tb, 1), F32), jnp.zeros((tb, dh), F32), True)
        run, acc = lax.fori_loop(0, qi, lambda j, c: block(qi - 1 - j, c[0], c[1], False), (run, acc))
        o_ref[0, 0] = acc
        rt_ref[0, 0] = run

    qsp = pl.BlockSpec((1, 1, tb, dh), lambda b, h, i: (b, h, i, 0))
    rsp = pl.BlockSpec((1, 1, tb, 1), lambda b, h, i: (b, h, i, 0))
    ksp = pl.BlockSpec((1, 1, s, dh), lambda b, h, i: (b, h, 0, 0))
    gsp = pl.BlockSpec((1, dh), lambda b, h, i: (0, 0))
    return pl.pallas_call(
        body, grid=(bsz, nh, nq), in_specs=[qsp, ksp, ksp, gsp, gsp], out_specs=[qsp, rsp],
        out_shape=[jax.ShapeDtypeStruct((bsz, nh, s, dh), F32), jax.ShapeDtypeStruct((bsz, nh, s, 1), F32)],
        compiler_params=_cp("parallel", "parallel", "arbitrary"), name=name)(q, k, v, qg, kg)


def sb_bwd(q, k, v, rtot, do, qg, kg, name="sb_bwd"):
    bsz, nh, s, dh = q.shape
    tb = SB_BLOCK
    nq = s // tb

    def body(q_ref, k_ref, v_ref, rt_ref, do_ref, qg_ref, kg_ref, dq_ref, dk_ref, dv_ref, dqg_ref, dkg_ref,
             qn_s, kn_s, dqn_s, dkn_s, dv_s):
        qn, qh, rq = _head_norm(q_ref[0, 0], qg_ref)
        kn, kh, rk = _head_norm(k_ref[0, 0], kg_ref)
        qn_s[...] = qn
        kn_s[...] = kn
        dkn_s[...] = jnp.zeros_like(dkn_s)
        dv_s[...] = jnp.zeros_like(dv_s)
        upto = _tri((tb, tb), lambda r, c: r <= c).astype(BF16)
        earlier = _tri((tb, tb), lambda r, c: r < c).astype(BF16)
        causal = _tri((tb, tb), lambda r, c: c < r)

        def q_block(qi, _):
            qs = pl.ds(pl.multiple_of(qi * tb, tb), tb)
            qnb, dob, rtb = qn_s[qs, :], do_ref[0, 0, qs, :], rt_ref[0, 0, qs, :]

            def block(kb, left, seen, dqn, diag):
                ks = pl.ds(pl.multiple_of(kb * tb, tb), tb)
                knb, vb = kn_s[ks, :], v_ref[0, 0, ks, :]
                z = _dot(qnb, knb, NT) * _SB_SCALE
                ls = _log_sigmoid(z)
                lm = ls - z
                if diag:
                    lm = jnp.where(causal, lm, 0.0)
                att = jnp.exp(ls + (rtb - left - _dot_exact_rhs(lm, upto)))
                if diag:
                    att = jnp.where(causal, att, 0.0)
                gg = att * _dot(dob, vb, NT)
                before = seen + _dot_exact_rhs(gg, earlier)
                sg = jnp.exp(ls)
                dz = gg * (1.0 - sg) - sg * before
                if diag:
                    dz = jnp.where(causal, dz, 0.0)
                dz = dz * _SB_SCALE
                dkn_s[ks, :] += _dot(dz, qnb, TN)
                dv_s[ks, :] += _dot(att, dob, TN)
                return (left + jnp.sum(lm, axis=1, keepdims=True), seen + jnp.sum(gg, axis=1, keepdims=True),
                        dqn + _dot(dz, knb))

            zero = jnp.zeros((tb, 1), F32)
            c = lax.fori_loop(0, qi, lambda kb, c: block(kb, c[0], c[1], c[2], False),
                              (zero, zero, jnp.zeros((tb, dh), F32)))
            c = block(qi, c[0], c[1], c[2], True)
            dqn_s[qs, :] = c[2]
            return 0

        lax.fori_loop(0, nq, q_block, 0)

        @pl.when((pl.program_id(0) == 0) & (pl.program_id(1) == 0))
        def _():
            dqg_ref[...] = jnp.zeros_like(dqg_ref)
            dkg_ref[...] = jnp.zeros_like(dkg_ref)

        def norm_bwd(dn, th, r, g_ref, dt_ref, dg_ref):
            dg_ref[...] += jnp.sum(dn * th, axis=0, keepdims=True)
            dth = dn * g_ref[...]
            dt_ref[0, 0] = r * (dth - th * jnp.mean(dth * th, axis=-1, keepdims=True))

        norm_bwd(dqn_s[...], qh, rq, qg_ref, dq_ref, dqg_ref)
        norm_bwd(dkn_s[...], kh, rk, kg_ref, dk_ref, dkg_ref)
        dv_ref[0, 0] = dv_s[...]

    hsp = pl.BlockSpec((1, 1, s, dh), lambda b, h: (b, h, 0, 0))
    gsp = pl.BlockSpec((1, dh), lambda b, h: (0, 0))
    act = jax.ShapeDtypeStruct((bsz, nh, s, dh), F32)
    gsh = jax.ShapeDtypeStruct((1, dh), F32)
    rsp = pl.BlockSpec((1, 1, s, 1), lambda b, h: (b, h, 0, 0))
    return pl.pallas_call(
        body, grid=(bsz, nh), in_specs=[hsp, hsp, hsp, rsp, hsp, gsp, gsp], out_specs=[hsp, hsp, hsp, gsp, gsp],
        out_shape=[act, act, act, gsh, gsh], scratch_shapes=[pltpu.VMEM((s, dh), F32)] * 5,
        compiler_params=_cp("arbitrary", "arbitrary"), name=name)(q, k, v, rtot, do, qg, kg)


def _to_heads(t, bsz):
    m, w = t.shape
    n = w // (SB_HEADS * SB_DIM)
    t = t.reshape(bsz, m // bsz, n, SB_HEADS, SB_DIM).transpose(2, 0, 3, 1, 4)
    return [t[i] for i in range(n)]


def _from_heads(ts):
    t = jnp.stack(ts, axis=0)
    n, bsz, nh, s, dh = t.shape
    return t.transpose(1, 3, 0, 2, 4).reshape(bsz * s, n * nh * dh)


def sb_layer_fwd(x, g, p, bsz):
    m, d = x.shape
    h = rms_fwd(x, g, BF16, "sb_norm")
    qkv = matmul(h, p["sb_w_qkv"], name="sb_qkv")
    q, k, v = _to_heads(qkv, bsz)
    o, rtot = sb_fwd(q, k, v, p["sb_q_g"], p["sb_k_g"])
    o2 = _from_heads([o])
    out = matmul(o2, p["sb_w_o"], res=x, name="sb_out")
    return out, (x, h, q, k, v, rtot, o2)


def sb_layer_bwd(saved, dout, g, p, bsz):
    x, h, q, k, v, rtot, o2 = saved
    do2 = matmul(dout, p["sb_w_o"], tb=True, name="sb_out_dx")
    dw_o = matmul(o2, dout, ta=True, name="sb_out_dw")
    dq, dk, dv, dqg, dkg = sb_bwd(q, k, v, rtot, _to_heads(do2, bsz)[0], p["sb_q_g"], p["sb_k_g"])
    dqkv = _from_heads([dq, dk, dv])
    dh = matmul(dqkv, p["sb_w_qkv"], tb=True, name="sb_qkv_dx")
    dw_qkv = matmul(h, dqkv, ta=True, name="sb_qkv_dw")
    dx, dg = rms_bwd(x, g, dh, dout, "sb_norm_bwd")
    return dx, dict(norm_mix_g=dg, sb_w_qkv=dw_qkv, sb_q_g=dqg, sb_k_g=dkg, sb_w_o=dw_o)


_CHIP_FLIPS = ((1, 0), (0, 1), (1, 1))
_MESH = pl.DeviceIdType.MESH
_ANY = pl.BlockSpec(memory_space=pl.ANY)


def _flip(v, f):
    return 1 - v if f else v


def gather_weights(shards, name="gather_weights"):
    n = len(shards)

    def body(*refs):
        ins, outs = refs[:n], refs[n:2 * n]
        send, recv, loc = refs[2 * n:]
        x, y, c = lax.axis_index("x"), lax.axis_index("y"), lax.axis_index("c")
        me = 2 * x + y

        def remote(i, j, block):
            px, py = _flip(x, _CHIP_FLIPS[j][0]), _flip(y, _CHIP_FLIPS[j][1])
            return pltpu.make_async_remote_copy(
                src_ref=ins[i], dst_ref=outs[i].at[block], send_sem=send.at[3 * i + j], recv_sem=recv.at[3 * i + j],
                device_id=(px, py, c), device_id_type=_MESH)

        local = [pltpu.make_async_copy(ins[i], outs[i].at[me], loc.at[i]) for i in range(n)]
        sends = [remote(i, j, me) for i in range(n) for j in range(3)]
        for cp in local + sends:
            cp.start()
        for i in range(n):
            for j, (fx, fy) in enumerate(_CHIP_FLIPS):
                remote(i, j, 2 * _flip(x, fx) + _flip(y, fy)).wait_recv()
        for cp in sends:
            cp.wait_send()
        for cp in local:
            cp.wait()

    return pl.pallas_call(
        body, in_specs=[_ANY] * n, out_specs=[_ANY] * n,
        out_shape=[jax.ShapeDtypeStruct((N_CHIPS,) + s.shape, s.dtype) for s in shards],
        scratch_shapes=[pltpu.SemaphoreType.DMA((3 * n,)), pltpu.SemaphoreType.DMA((3 * n,)),
                        pltpu.SemaphoreType.DMA((n,))],
        name=name)(*shards)


def grad_sync(parts, packed, name="grad_sync"):
    n = len(parts)

    def body(*refs):
        ins, pk = refs[:n], refs[n]
        mine, theirs, pk_all = refs[n + 1:2 * n + 1], refs[2 * n + 1:3 * n + 1], refs[3 * n + 1]
        send, recv, loc, fsend, frecv, psend, precv, ploc = refs[3 * n + 2:]
        x, y, c = lax.axis_index("x"), lax.axis_index("y"), lax.axis_index("c")
        me = 2 * x + y
        dev = 4 * x + 2 * y + c
        sibling = (x, y, 1 - c)

        def remote(i, j, slot):
            px, py = _flip(x, _CHIP_FLIPS[j][0]), _flip(y, _CHIP_FLIPS[j][1])
            return pltpu.make_async_remote_copy(
                src_ref=ins[i].at[2 * px + py], dst_ref=mine[i].at[slot], send_sem=send.at[3 * i + j],
                recv_sem=recv.at[3 * i + j], device_id=(px, py, c), device_id_type=_MESH)

        def packed_to(r, slot):
            px, py, pc = _flip(x, r & 4), _flip(y, r & 2), _flip(c, r & 1)
            return pltpu.make_async_remote_copy(
                src_ref=pk, dst_ref=pk_all.at[slot], send_sem=psend.at[r - 1], recv_sem=precv.at[r - 1],
                device_id=(px, py, pc), device_id_type=_MESH)

        def forward(i):
            return pltpu.make_async_remote_copy(
                src_ref=mine[i], dst_ref=theirs[i], send_sem=fsend.at[i], recv_sem=frecv.at[i],
                device_id=sibling, device_id_type=_MESH)

        local = [pltpu.make_async_copy(ins[i].at[me], mine[i].at[me], loc.at[i]) for i in range(n)]
        plocal = pltpu.make_async_copy(pk, pk_all.at[dev], ploc.at[0])
        sends = [remote(i, j, me) for i in range(n) for j in range(3)]
        psends = [packed_to(r, dev) for r in range(1, N_DEV)]
        for cp in local + [plocal] + sends + psends:
            cp.start()
        fwd = [forward(i) for i in range(n)]
        for i in range(n):
            for j, (fx, fy) in enumerate(_CHIP_FLIPS):
                remote(i, j, 2 * _flip(x, fx) + _flip(y, fy)).wait_recv()
            local[i].wait()
            fwd[i].start()
        for i in range(n):
            fwd[i].wait_recv()
        for r in range(1, N_DEV):
            packed_to(r, 4 * _flip(x, r & 4) + 2 * _flip(y, r & 2) + _flip(c, r & 1)).wait_recv()
        for cp in sends + psends + fwd:
            cp.wait_send()
        plocal.wait()

    land = [jax.ShapeDtypeStruct(p.shape, p.dtype) for p in parts]
    return pl.pallas_call(
        body, in_specs=[_ANY] * (n + 1), out_specs=[_ANY] * (2 * n + 1),
        out_shape=land + land + [jax.ShapeDtypeStruct((N_DEV,) + packed.shape, packed.dtype)],
        scratch_shapes=[pltpu.SemaphoreType.DMA((3 * n,)), pltpu.SemaphoreType.DMA((3 * n,)),
                        pltpu.SemaphoreType.DMA((n,)), pltpu.SemaphoreType.DMA((n,)), pltpu.SemaphoreType.DMA((n,)),
                        pltpu.SemaphoreType.DMA((N_DEV - 1,)), pltpu.SemaphoreType.DMA((N_DEV - 1,)),
                        pltpu.SemaphoreType.DMA((1,))],
        name=name)(*parts, packed)


def adamw(w, m, v, parts, name):
    r, c = w.shape
    tr = r
    for cand in (512, 256, 128, 64, 32, 16, 8):
        if r % cand == 0 and cand * c * 4 <= 512 * 1024:
            tr = cand
            break
    np_ = len(parts)
    nslot = parts[0].shape[0]
    bc1 = 1.0 - ADAM_B1 ** ADAM_STEP
    bc2 = 1.0 - ADAM_B2 ** ADAM_STEP

    def body(*refs):
        w_ref, m_ref, v_ref = refs[:3]
        p_refs = refs[3:3 + np_]
        g_ref, d_ref, nm_ref, nv_ref = refs[3 + np_:]
        g = None
        for k in range(nslot):
            t = p_refs[0][k]
            for p_ref in p_refs[1:]:
                t = t + p_ref[k]
            g = t if g is None else g + t
        wv = w_ref[...]
        nm = ADAM_B1 * m_ref[...] + (1.0 - ADAM_B1) * g
        nv = ADAM_B2 * v_ref[...] + (1.0 - ADAM_B2) * (g * g)
        g_ref[...] = g
        nm_ref[...] = nm
        nv_ref[...] = nv
        d_ref[...] = -ADAM_LR * ((nm / bc1) / (jnp.sqrt(nv / bc2) + ADAM_EPS) + ADAM_WD * wv)

    row = pl.BlockSpec((tr, c), lambda i: (i, 0))
    slab = pl.BlockSpec((nslot, tr, c), lambda i: (0, i, 0))
    sh = jax.ShapeDtypeStruct((r, c), F32)
    return pl.pallas_call(
        body, grid=(r // tr,), in_specs=[row, row, row] + [slab] * np_, out_specs=[row] * 4,
        out_shape=[sh] * 4, compiler_params=_cp("parallel"), name=name)(w, m, v, *parts)


WEIGHTS = ["norm_mix_g", "norm_ffn_g", "pool_w", "pool_b", "pool_scale", "s5_lam_re", "s5_lam_im", "s5_log_dt",
           "s5_b_re", "s5_b_im", "s5_c_re", "s5_c_im", "s5_d", "s5_w_out", "s5_b_out", "lru_w_in", "lru_conv_w",
           "lru_conv_b", "lru_w_a", "lru_b_a", "lru_w_x", "lru_b_x", "lru_lam", "lru_w_out", "sb_w_qkv", "sb_q_g",
           "sb_k_g", "sb_w_o", "ffn_w_in", "ffn_conv_w", "ffn_conv_b", "ffn_w_out"]
SHARD_AXIS = dict(pool_w=2, s5_d=1, s5_w_out=2, s5_b_out=1, lru_w_in=2, lru_conv_w=2, lru_conv_b=1, lru_w_a=2,
                  lru_b_a=1, lru_w_x=2, lru_b_x=1, lru_lam=1, lru_w_out=1, sb_w_qkv=2, sb_w_o=1, ffn_w_in=2,
                  ffn_conv_w=2, ffn_w_out=1)
MXU_WEIGHTS = ("pool_w", "s5_w_out", "lru_w_in", "lru_w_a", "lru_w_x", "lru_w_out", "sb_w_qkv", "sb_w_o",
               "ffn_w_in", "ffn_w_out")
SHARDED = [n for n in WEIGHTS if n in SHARD_AXIS]
REPLICATED = [n for n in WEIGHTS if n not in SHARD_AXIS]
PACK_WIDTH = 1024


def _as_rows(a):
    return a.reshape(-1, a.shape[-1])


def _pack(arrays):
    rows = []
    for a in arrays:
        flat = a.reshape(-1)
        pad = (-flat.shape[0]) % PACK_WIDTH
        rows.append(jnp.pad(flat, (0, pad)).reshape(-1, PACK_WIDTH))
    out = jnp.concatenate(rows, axis=0)
    return jnp.pad(out, ((0, (-out.shape[0]) % 8), (0, 0)))


def _unpack(packed, like):
    out, r = [], 0
    for a in like:
        size = math.prod(a.shape)
        nrow = -(-size // PACK_WIDTH)
        out.append(packed[r:r + nrow].reshape(-1)[:size].reshape(a.shape))
        r += nrow
    return out


def kernel(*args):
    names = ["x"] + WEIGHTS + ["loss_target"] + ["m_" + n for n in WEIGHTS] + ["v_" + n for n in WEIGHTS]
    assert len(args) == len(names)
    given = dict(zip(names, args))
    x, target = given["x"], given["loss_target"]
    bsz, seq, d = x.shape
    m_tok = bsz * seq

    shards = [_as_rows(given[n].astype(BF16) if n in MXU_WEIGHTS else given[n]) for n in SHARDED]
    gathered = gather_weights(shards)
    p = {n: given[n] for n in REPLICATED}
    for n, g4 in zip(SHARDED, gathered):
        blocks = g4.reshape((N_CHIPS,) + given[n].shape)
        p[n] = jnp.concatenate([blocks[k] for k in range(N_CHIPS)], axis=SHARD_AXIS[n])

    mix = {k: (v[0] if v.ndim > 2 or k == "s5_log_dt" else v) for k, v in p.items()
           if not k.startswith(("norm_", "ffn_"))}
    mixers = ((pool_layer_fwd, pool_layer_bwd), (s5_layer_fwd, s5_layer_bwd), (lru_layer_fwd, lru_layer_bwd),
              (sb_layer_fwd, sb_layer_bwd))
    ffn_p = lambda l: (p["norm_ffn_g"][l:l + 1], p["ffn_w_in"][l], p["ffn_conv_w"][l], p["ffn_conv_b"][l:l + 1],
                       p["ffn_w_out"][l])

    def mixer_args(l):
        g = p["norm_mix_g"][l:l + 1]
        if l == 0:
            return (g, mix["pool_w"], mix["pool_b"], mix["pool_scale"], bsz)
        return (g, mix, bsz)

    h = x.reshape(m_tok, d)
    saved = []
    for l in range(4):
        h, s_mix = mixers[l][0](h, *mixer_args(l))
        gl, w_in, cw, cb, w_out = ffn_p(l)
        h, s_ffn = ffn_fwd(h, gl, w_in, cw, cb, w_out, bsz, l)
        saved.append((s_mix, s_ffn))
    dh, loss_part = loss_head(h, target.reshape(m_tok, d))
    loss = lax.psum(jnp.sum(loss_part), ("x", "y", "c"))

    grads = {}
    ffn_g = [None] * 4
    mix_g = [None] * 4
    for l in range(3, -1, -1):
        gl, w_in, cw, cb, w_out = ffn_p(l)
        dh, ffn_g[l] = ffn_bwd(saved[l][1], dh, gl, w_in, cw, cb, w_out, bsz)
        dh, mix_g[l] = mixers[l][1](saved[l][0], dh, *mixer_args(l))
    for k in ("norm_ffn_g", "ffn_w_in", "ffn_conv_w", "ffn_conv_b", "ffn_w_out"):
        grads[k] = jnp.stack([ffn_g[l][k] for l in range(4)]).reshape(p[k].shape)
    grads["norm_mix_g"] = jnp.concatenate([mix_g[l]["norm_mix_g"] for l in range(4)], axis=0)
    for l in range(4):
        for k, v in mix_g[l].items():
            if k != "norm_mix_g":
                grads[k] = v.reshape(p[k].shape)
    grad_x = dh.reshape(bsz, seq, d)

    parts = []
    for n in SHARDED:
        blocks = jnp.stack(jnp.split(grads[n], N_CHIPS, axis=SHARD_AXIS[n]))
        parts.append(blocks.reshape(N_CHIPS, -1, blocks.shape[-1]))
    synced = grad_sync(parts, _pack([grads[n] for n in REPLICATED]))
    ns = len(SHARDED)
    mine, theirs, packed_all = synced[:ns], synced[ns:2 * ns], synced[2 * ns]

    out = {}
    for i, n in enumerate(SHARDED):
        res = adamw(_as_rows(given[n]), _as_rows(given["m_" + n]), _as_rows(given["v_" + n]), [mine[i], theirs[i]],
                    "adamw_" + n)
        out[n] = [r.reshape(given[n].shape) for r in res]
    res = adamw(_pack([given[n] for n in REPLICATED]), _pack([given["m_" + n] for n in REPLICATED]),
                _pack([given["v_" + n] for n in REPLICATED]), [packed_all], "adamw_replicated")
    like = [given[n] for n in REPLICATED]
    for n, *vals in zip(REPLICATED, *[_unpack(r, like) for r in res]):
        out[n] = list(vals)
    return (loss, grad_x, *[out[n][0] for n in WEIGHTS], *[out[n][1] for n in WEIGHTS],
            *[out[n][2] for n in WEIGHTS], *[out[n][3] for n in WEIGHTS])
```

```python
import functools
import math

import jax
import jax.numpy as jnp
from jax import lax
from jax.experimental import pallas as pl
from jax.experimental.pallas import tpu as pltpu

F32 = jnp.float32
BF16 = jnp.bfloat16

EPS = 1e-6
N_CHIPS = 4
N_DEV = 8
POOL_WINDOWS = (2, 4, 8, 16)
POOL_GROUP = 256
S5_GROUP = 16
S5_STATE = 64
S5_CHUNKS = 8
S5_LANES = 512
S5_T = 256
LRU_BLOCK = 256
LRU_CONV = 4
LRU_C = 8.0
SB_HEADS = 16
SB_DIM = 64
SB_BLOCK = 512
SB_CHUNK = 128
FFN_CONV = 3
ADAM_LR, ADAM_B1, ADAM_B2, ADAM_EPS, ADAM_WD, ADAM_STEP = 0.001, 0.9, 0.999, 1e-08, 0.01, 10
VMEM_LIMIT_BYTES = 56 * 1024 * 1024

NN = (((1,), (0,)), ((), ()))
NT = (((1,), (1,)), ((), ()))
TN = (((0,), (0,)), ((), ()))


def _cp(*sem):
    return pltpu.CompilerParams(dimension_semantics=sem, vmem_limit_bytes=VMEM_LIMIT_BYTES)


def _pick(n, prefs):
    for p in prefs:
        if n % p == 0:
            return p
    return n


def _dot(a, b, dims=NN):
    return lax.dot_general(a.astype(BF16), b.astype(BF16), dims, preferred_element_type=F32)


def _split(x):
    hi = x.astype(BF16)
    return hi, (x - hi.astype(F32)).astype(BF16)


def _dot3(a, b, dims=NN):
    ah, al = _split(a)
    bh, bl = _split(b)
    d = lambda p, q: lax.dot_general(p, q, dims, preferred_element_type=F32)
    return d(ah, bh) + (d(ah, bl) + d(al, bh))


def _dot_exact_rhs(a, b01):
    ah, al = _split(a)
    d = lambda p: lax.dot_general(p, b01, NN, preferred_element_type=F32)
    return d(ah) + d(al)


def _sig(x):
    return 1.0 / (1.0 + jnp.exp(-x))


def _softplus(x):
    return jnp.maximum(x, 0.0) + jnp.log(1.0 + jnp.exp(-jnp.abs(x)))


_GELU_C = math.sqrt(2.0 / math.pi)


def _gelu(x):
    return 0.5 * x * (1.0 + jnp.tanh(_GELU_C * (x + 0.044715 * x * x * x)))


def _gelu_grad(x):
    th = jnp.tanh(_GELU_C * (x + 0.044715 * x * x * x))
    return 0.5 * (1.0 + th) + 0.5 * x * (1.0 - th * th) * _GELU_C * (1.0 + 3.0 * 0.044715 * x * x)


def _rows(shape):
    return lax.broadcasted_iota(jnp.int32, shape, 0)


def _shift_down(x, k):
    if k >= x.shape[0]:
        return jnp.zeros_like(x)
    return jnp.where(_rows(x.shape) >= k, pltpu.roll(x, k, 0), 0.0)


def _shift_up(x, k):
    t = x.shape[0]
    if k >= t:
        return jnp.zeros_like(x)
    return jnp.where(_rows(x.shape) < t - k, pltpu.roll(x, t - k, 0), 0.0)


def matmul(a, b, *, ta=False, tb=False, bias=None, res=None, out_dtype=F32, name):
    m, k = (a.shape[1], a.shape[0]) if ta else a.shape
    n = b.shape[0] if tb else b.shape[1]
    assert (b.shape[1] if tb else b.shape[0]) == k
    tm = _pick(m, (1024, 512, 1408, 256, 128))
    tn = _pick(n, (1024, 512, 1408, 256, 128))
    tk = _pick(k, (512, 1408, 256, 128))
    nk = k // tk
    dims = (((0 if ta else 1,), (1 if tb else 0,)), ((), ()))
    has_bias, has_res = bias is not None, res is not None

    def body(*refs):
        a_ref, b_ref = refs[:2]
        rest = list(refs[2:])
        bias_ref = rest.pop(0) if has_bias else None
        res_ref = rest.pop(0) if has_res else None
        o_ref, acc_ref = rest
        kk = pl.program_id(2)

        @pl.when(kk == 0)
        def _():
            acc_ref[...] = jnp.zeros_like(acc_ref)

        acc_ref[...] += lax.dot_general(a_ref[...].astype(BF16), b_ref[...].astype(BF16), dims,
                                        preferred_element_type=F32)

        @pl.when(kk == nk - 1)
        def _():
            r = acc_ref[...]
            if has_bias:
                r = r + bias_ref[...]
            if has_res:
                r = r + res_ref[...]
            o_ref[...] = r.astype(o_ref.dtype)

    in_specs = [
        pl.BlockSpec((tk, tm), lambda i, j, kk: (kk, i)) if ta else pl.BlockSpec((tm, tk), lambda i, j, kk: (i, kk)),
        pl.BlockSpec((tn, tk), lambda i, j, kk: (j, kk)) if tb else pl.BlockSpec((tk, tn), lambda i, j, kk: (kk, j)),
    ]
    args = [a, b]
    if has_bias:
        in_specs.append(pl.BlockSpec((1, tn), lambda i, j, kk: (0, j)))
        args.append(bias)
    if has_res:
        in_specs.append(pl.BlockSpec((tm, tn), lambda i, j, kk: (i, j)))
        args.append(res)
    return pl.pallas_call(
        body, grid=(m // tm, n // tn, nk), in_specs=in_specs,
        out_specs=pl.BlockSpec((tm, tn), lambda i, j, kk: (i, j)),
        out_shape=jax.ShapeDtypeStruct((m, n), out_dtype),
        scratch_shapes=[pltpu.VMEM((tm, tn), F32)],
        compiler_params=_cp("parallel", "parallel", "arbitrary"), name=name)(*args)


def rms_fwd(x, g, out_dtype, name):
    m, d = x.shape
    tr = _pick(m, (512, 256, 128))

    def body(x_ref, g_ref, o_ref):
        xv = x_ref[...]
        r = lax.rsqrt(jnp.mean(xv * xv, axis=-1, keepdims=True) + EPS)
        o_ref[...] = (xv * r * g_ref[...]).astype(o_ref.dtype)

    return pl.pallas_call(
        body, grid=(m // tr,),
        in_specs=[pl.BlockSpec((tr, d), lambda i: (i, 0)), pl.BlockSpec((1, d), lambda i: (0, 0))],
        out_specs=pl.BlockSpec((tr, d), lambda i: (i, 0)),
        out_shape=jax.ShapeDtypeStruct((m, d), out_dtype),
        compiler_params=_cp("parallel"), name=name)(x, g)


def rms_bwd(x, g, dh, dres, name):
    m, d = x.shape
    tr = _pick(m, (512, 256, 128))

    def body(x_ref, g_ref, dh_ref, dres_ref, dx_ref, dg_ref):
        xv = x_ref[...]
        r = lax.rsqrt(jnp.mean(xv * xv, axis=-1, keepdims=True) + EPS)
        xh = xv * r
        dhv = dh_ref[...].astype(F32)
        dxh = dhv * g_ref[...]
        dx_ref[...] = dres_ref[...] + r * (dxh - xh * jnp.mean(dxh * xh, axis=-1, keepdims=True))

        @pl.when(pl.program_id(0) == 0)
        def _():
            dg_ref[...] = jnp.zeros_like(dg_ref)

        dg_ref[...] += jnp.sum(dhv * xh, axis=0, keepdims=True)

    row = pl.BlockSpec((tr, d), lambda i: (i, 0))
    vec = pl.BlockSpec((1, d), lambda i: (0, 0))
    return pl.pallas_call(
        body, grid=(m // tr,), in_specs=[row, vec, row, row], out_specs=[row, vec],
        out_shape=[jax.ShapeDtypeStruct((m, d), F32), jax.ShapeDtypeStruct((1, d), F32)],
        compiler_params=_cp("arbitrary"), name=name)(x, g, dh, dres)


def _conv_fwd(u, w_ref, b_ref, kw):
    y = b_ref[...] + w_ref[kw - 1:kw, :] * u
    for k in range(kw - 1):
        y = y + w_ref[k:k + 1, :] * _shift_down(u, kw - 1 - k)
    return y


def _conv_bwd_input(dy, w_ref, kw):
    du = w_ref[kw - 1:kw, :] * dy
    for k in range(kw - 1):
        du = du + w_ref[k:k + 1, :] * _shift_up(dy, kw - 1 - k)
    return du


def _conv_bwd_weight(dy, u, kw):
    out = [jnp.sum(dy * _shift_down(u, kw - 1 - k), axis=0, keepdims=True) for k in range(kw - 1)]
    out.append(jnp.sum(dy * u, axis=0, keepdims=True))
    return out


def ffn_act_fwd(u, cw, cb, name):
    bsz, s, f2 = u.shape
    f = f2 // 2
    tc = _pick(f, (256, 128))
    nj = f // tc

    def body(uv_ref, ug_ref, wv_ref, wg_ref, bv_ref, bg_ref, a_ref):
        hv = _conv_fwd(uv_ref[0], wv_ref, bv_ref, FFN_CONV)
        hg = _conv_fwd(ug_ref[0], wg_ref, bg_ref, FFN_CONV)
        a_ref[0] = (hg * _sig(hg) * hv).astype(a_ref.dtype)

    uv = pl.BlockSpec((1, s, tc), lambda b, j: (b, 0, j))
    ug = pl.BlockSpec((1, s, tc), lambda b, j: (b, 0, j + nj))
    wv = pl.BlockSpec((FFN_CONV, tc), lambda b, j: (0, j))
    wg = pl.BlockSpec((FFN_CONV, tc), lambda b, j: (0, j + nj))
    bv = pl.BlockSpec((1, tc), lambda b, j: (0, j))
    bg = pl.BlockSpec((1, tc), lambda b, j: (0, j + nj))
    return pl.pallas_call(
        body, grid=(bsz, nj), in_specs=[uv, ug, wv, wg, bv, bg], out_specs=uv,
        out_shape=jax.ShapeDtypeStruct((bsz, s, f), BF16),
        compiler_params=_cp("parallel", "parallel"), name=name)(u, u, cw, cw, cb, cb)


def ffn_act_bwd(u, cw, cb, da, name):
    bsz, s, f2 = u.shape
    f = f2 // 2
    tc = _pick(f, (256, 128))
    nj = f // tc

    def body(uv_ref, ug_ref, wv_ref, wg_ref, bv_ref, bg_ref, da_ref,
             duv_ref, dug_ref, dwv_ref, dwg_ref, dbv_ref, dbg_ref):
        uv, ug = uv_ref[0], ug_ref[0]
        hv = _conv_fwd(uv, wv_ref, bv_ref, FFN_CONV)
        hg = _conv_fwd(ug, wg_ref, bg_ref, FFN_CONV)
        sg = _sig(hg)
        dav = da_ref[0].astype(F32)
        dhv = dav * hg * sg
        dhg = dav * hv * (sg * (1.0 + hg * (1.0 - sg)))
        duv_ref[0] = _conv_bwd_input(dhv, wv_ref, FFN_CONV).astype(duv_ref.dtype)
        dug_ref[0] = _conv_bwd_input(dhg, wg_ref, FFN_CONV).astype(dug_ref.dtype)

        @pl.when(pl.program_id(1) == 0)
        def _():
            for r in (dwv_ref, dwg_ref, dbv_ref, dbg_ref):
                r[...] = jnp.zeros_like(r)

        for k, row in enumerate(_conv_bwd_weight(dhv, uv, FFN_CONV)):
            dwv_ref[k:k + 1, :] += row
        for k, row in enumerate(_conv_bwd_weight(dhg, ug, FFN_CONV)):
            dwg_ref[k:k + 1, :] += row
        dbv_ref[...] += jnp.sum(dhv, axis=0, keepdims=True)
        dbg_ref[...] += jnp.sum(dhg, axis=0, keepdims=True)

    uv = pl.BlockSpec((1, s, tc), lambda j, b: (b, 0, j))
    ug = pl.BlockSpec((1, s, tc), lambda j, b: (b, 0, j + nj))
    wv = pl.BlockSpec((FFN_CONV, tc), lambda j, b: (0, j))
    wg = pl.BlockSpec((FFN_CONV, tc), lambda j, b: (0, j + nj))
    bv = pl.BlockSpec((1, tc), lambda j, b: (0, j))
    bg = pl.BlockSpec((1, tc), lambda j, b: (0, j + nj))
    act = jax.ShapeDtypeStruct((bsz, s, f), BF16)
    return pl.pallas_call(
        body, grid=(nj, bsz), in_specs=[uv, ug, wv, wg, bv, bg, uv],
        out_specs=[uv, uv, wv, wv, bv, bv],
        out_shape=[act, act, jax.ShapeDtypeStruct((FFN_CONV, f), F32), jax.ShapeDtypeStruct((FFN_CONV, f), F32),
                   jax.ShapeDtypeStruct((1, f), F32), jax.ShapeDtypeStruct((1, f), F32)],
        compiler_params=_cp("parallel", "arbitrary"), name=name)(u, u, cw, cw, cb, cb, da)


def ffn_fwd(x, g, w_in, cw, cb, w_out, bsz):
    m, d = x.shape
    h = rms_fwd(x, g, BF16, "ffn_norm")
    u = matmul(h, w_in, name="ffn_in")
    a = ffn_act_fwd(u.reshape(bsz, m // bsz, -1), cw, cb, "ffn_act")
    a2 = a.reshape(m, -1)
    out = matmul(a2, w_out, res=x, name="ffn_out")
    return out, (x, h, u, a2)


def ffn_bwd(saved, dout, g, w_in, cw, cb, w_out, bsz):
    x, h, u, a2 = saved
    m, d = x.shape
    da = matmul(dout, w_out, tb=True, out_dtype=BF16, name="ffn_out_dx")
    dw_out = matmul(a2, dout, ta=True, name="ffn_out_dw")
    u3 = u.reshape(bsz, m // bsz, -1)
    duv, dug, dwv, dwg, dbv, dbg = ffn_act_bwd(u3, cw, cb, da.reshape(bsz, m // bsz, -1), "ffn_act_bwd")
    du = jnp.concatenate([duv, dug], axis=-1).reshape(m, -1)
    dh = matmul(du, w_in, tb=True, name="ffn_in_dx")
    dw_in = matmul(h, du, ta=True, name="ffn_in_dw")
    dx, dg = rms_bwd(x, g, dh, dout, "ffn_norm_bwd")
    grads = dict(norm_ffn_g=dg, ffn_w_in=dw_in, ffn_conv_w=jnp.concatenate([dwv, dwg], axis=-1),
                 ffn_conv_b=jnp.concatenate([dbv, dbg], axis=-1), ffn_w_out=dw_out)
    return dx, grads


def loss_head(y, target, name="loss_head"):
    m, d = y.shape
    tr = _pick(m, (512, 256, 128))

    def body(y_ref, t_ref, dy_ref, l_ref):
        e = y_ref[...] - t_ref[...]
        dy_ref[...] = e * (1.0 / d)

        @pl.when(pl.program_id(0) == 0)
        def _():
            l_ref[...] = jnp.zeros_like(l_ref)

        l_ref[...] += jnp.sum(e * e, axis=0, keepdims=True) * (0.5 / d)

    row = pl.BlockSpec((tr, d), lambda i: (i, 0))
    vec = pl.BlockSpec((1, d), lambda i: (0, 0))
    dy, part = pl.pallas_call(
        body, grid=(m // tr,), in_specs=[row, row], out_specs=[row, vec],
        out_shape=[jax.ShapeDtypeStruct((m, d), F32), jax.ShapeDtypeStruct((1, d), F32)],
        compiler_params=_cp("arbitrary"), name=name)(y, target)
    return dy, part


def _pool_windows(h, gi):
    sums, s, width = [], h, 1
    for _ in POOL_WINDOWS:
        s = s + _shift_down(s, width)
        width *= 2
        sums.append(s)
    pos = _rows(h.shape).astype(F32) + 1.0
    wsum, inv = sums[-1], 1.0 / jnp.minimum(pos, float(POOL_WINDOWS[-1]))
    for k in range(len(POOL_WINDOWS) - 2, -1, -1):
        wsum = jnp.where(gi == k, sums[k], wsum)
        inv = jnp.where(gi == k, 1.0 / jnp.minimum(pos, float(POOL_WINDOWS[k])), inv)
    return wsum * inv - h, inv


def _pool_windows_transpose(e, gi):
    sums, s, width = [], e, 1
    for _ in POOL_WINDOWS:
        s = s + _shift_up(s, width)
        width *= 2
        sums.append(s)
    out = sums[-1]
    for k in range(len(POOL_WINDOWS) - 2, -1, -1):
        out = jnp.where(gi == k, sums[k], out)
    return out


def pool_fwd(h, w, b, scale, x, name="pool_fwd"):
    bsz, s, d = h.shape
    ng = d // POOL_GROUP

    def body(h_ref, w_ref, b_ref, s_ref, x_ref, o_ref):
        dd, _ = _pool_windows(h_ref[0], pl.program_id(1))
        y = _dot(dd, w_ref[0]) + b_ref[...]
        o_ref[0] = x_ref[0] + s_ref[...] * y

    act = pl.BlockSpec((1, s, POOL_GROUP), lambda bb, gi: (bb, 0, gi))
    vec = pl.BlockSpec((1, POOL_GROUP), lambda bb, gi: (0, gi))
    return pl.pallas_call(
        body, grid=(bsz, ng),
        in_specs=[act, pl.BlockSpec((1, POOL_GROUP, POOL_GROUP), lambda bb, gi: (gi, 0, 0)), vec, vec, act],
        out_specs=act, out_shape=jax.ShapeDtypeStruct((bsz, s, d), F32),
        compiler_params=_cp("parallel", "parallel"), name=name)(h, w, b, scale, x)


def pool_bwd(h, w, b, scale, dy, name="pool_bwd"):
    bsz, s, d = h.shape
    ng = d // POOL_GROUP

    def body(h_ref, w_ref, b_ref, s_ref, dy_ref, dh_ref, dw_ref, db_ref, ds_ref):
        gi = pl.program_id(0)
        dd, inv = _pool_windows(h_ref[0], gi)
        ypre = _dot(dd, w_ref[0]) + b_ref[...]
        dyv = dy_ref[0]
        dyb = dyv * s_ref[...]

        @pl.when(pl.program_id(1) == 0)
        def _():
            for r in (dw_ref, db_ref, ds_ref):
                r[...] = jnp.zeros_like(r)

        ds_ref[...] += jnp.sum(dyv * ypre, axis=0, keepdims=True)
        db_ref[...] += jnp.sum(dyb, axis=0, keepdims=True)
        dw_ref[0] += _dot(dd, dyb, TN)
        ddd = _dot(dyb, w_ref[0], NT)
        dh_ref[0] = _pool_windows_transpose(ddd * inv, gi) - ddd

    act = pl.BlockSpec((1, s, POOL_GROUP), lambda gi, bb: (bb, 0, gi))
    vec = pl.BlockSpec((1, POOL_GROUP), lambda gi, bb: (0, gi))
    wsp = pl.BlockSpec((1, POOL_GROUP, POOL_GROUP), lambda gi, bb: (gi, 0, 0))
    return pl.pallas_call(
        body, grid=(ng, bsz), in_specs=[act, wsp, vec, vec, act], out_specs=[act, wsp, vec, vec],
        out_shape=[jax.ShapeDtypeStruct((bsz, s, d), F32), jax.ShapeDtypeStruct((ng, POOL_GROUP, POOL_GROUP), F32),
                   jax.ShapeDtypeStruct((1, d), F32), jax.ShapeDtypeStruct((1, d), F32)],
        compiler_params=_cp("parallel", "arbitrary"), name=name)(h, w, b, scale, dy)


def pool_layer_fwd(x, g, w, b, scale, bsz):
    m, d = x.shape
    h = rms_fwd(x, g, F32, "pool_norm")
    out = pool_fwd(h.reshape(bsz, m // bsz, d), w, b, scale, x.reshape(bsz, m // bsz, d))
    return out.reshape(m, d), (x, h)


def pool_layer_bwd(saved, dout, g, w, b, scale, bsz):
    x, h = saved
    m, d = x.shape
    dh, dw, db, ds = pool_bwd(h.reshape(bsz, m // bsz, d), w, b, scale, dout.reshape(bsz, m // bsz, d))
    dx, dg = rms_bwd(x, g, dh.reshape(m, d), dout, "pool_norm_bwd")
    return dx, dict(norm_mix_g=dg, pool_w=dw[None], pool_b=db, pool_scale=ds)


def _scan_fwd(a, b):
    k = 1
    while k < a.shape[0]:
        b = b + a * _shift_down(b, k)
        if 2 * k < a.shape[0]:
            a = a * _shift_down(a, k)
        k *= 2
    return b


def _scan_bwd(a, b):
    k = 1
    while k < a.shape[0]:
        b = b + a * _shift_up(b, k)
        if 2 * k < a.shape[0]:
            a = a * _shift_up(a, k)
        k *= 2
    return b


def _neg_expm1(x):
    series = -x * (1.0 + x * (0.5 + x * (1.0 / 6.0 + x * (1.0 / 24.0 + x * (1.0 / 120.0)))))
    return jnp.where(x > -0.03, series, 1.0 - jnp.exp(x))


def _lru_gates(rec, wa_ref, ba_ref, wx_ref, bx_ref, lam_ref):
    r = _sig(_dot(rec, wa_ref[0]) + ba_ref[...])
    i = _sig(_dot(rec, wx_ref[0]) + bx_ref[...])
    sp = _softplus(-lam_ref[...])
    log_a = -LRU_C * r * sp
    a = jnp.exp(log_a)
    mult = jnp.sqrt(_neg_expm1(2.0 * log_a))
    return r, i, sp, a, mult


def lru_fwd(zz, cw, cb, wa, ba, wx, bx, lam, name="lru_fwd"):
    bsz, s, r2 = zz.shape
    rw = r2 // 2
    nb = rw // LRU_BLOCK

    def body(g_ref, p_ref, cw_ref, cb_ref, wa_ref, ba_ref, wx_ref, bx_ref, lam_ref, h_ref, y_ref):
        rec = _conv_fwd(p_ref[0], cw_ref, cb_ref, LRU_CONV)
        _, i, _, a, mult = _lru_gates(rec, wa_ref, ba_ref, wx_ref, bx_ref, lam_ref)
        hst = _scan_fwd(a, mult * (i * rec))
        h_ref[0] = hst
        y_ref[0] = (_gelu(g_ref[0]) * hst).astype(y_ref.dtype)

    gsp = pl.BlockSpec((1, s, LRU_BLOCK), lambda bb, n: (bb, 0, n))
    psp = pl.BlockSpec((1, s, LRU_BLOCK), lambda bb, n: (bb, 0, n + nb))
    cws = pl.BlockSpec((LRU_CONV, LRU_BLOCK), lambda bb, n: (0, n))
    vec = pl.BlockSpec((1, LRU_BLOCK), lambda bb, n: (0, n))
    wsp = pl.BlockSpec((1, LRU_BLOCK, LRU_BLOCK), lambda bb, n: (n, 0, 0))
    return pl.pallas_call(
        body, grid=(bsz, nb), in_specs=[gsp, psp, cws, vec, wsp, vec, wsp, vec, vec], out_specs=[gsp, gsp],
        out_shape=[jax.ShapeDtypeStruct((bsz, s, rw), F32), jax.ShapeDtypeStruct((bsz, s, rw), BF16)],
        compiler_params=_cp("parallel", "parallel"), name=name)(zz, zz, cw, cb, wa, ba, wx, bx, lam)


def lru_bwd(zz, hst, dy, cw, cb, wa, ba, wx, bx, lam, name="lru_bwd"):
    bsz, s, r2 = zz.shape
    rw = r2 // 2
    nb = rw // LRU_BLOCK

    def body(g_ref, p_ref, h_ref, dy_ref, cw_ref, cb_ref, wa_ref, ba_ref, wx_ref, bx_ref, lam_ref,
             dg_ref, dp_ref, dcw_ref, dcb_ref, dwa_ref, dba_ref, dwx_ref, dbx_ref, dlam_ref):
        pre = p_ref[0]
        rec = _conv_fwd(pre, cw_ref, cb_ref, LRU_CONV)
        r, i, sp, a, mult = _lru_gates(rec, wa_ref, ba_ref, wx_ref, bx_ref, lam_ref)
        hst_v, gate, dyv = h_ref[0], g_ref[0], dy_ref[0]
        dg_ref[0] = (dyv * hst_v * _gelu_grad(gate)).astype(dg_ref.dtype)
        lmb = _scan_bwd(_shift_up(a, 1), dyv * _gelu(gate))
        da = lmb * _shift_down(hst_v, 1)
        dmult = lmb * (i * rec)
        dlog_a = da * a - dmult * (a * a) / mult
        dr = dlog_a * (-LRU_C) * sp
        dra = dr * r * (1.0 - r)
        dxa = lmb * mult * rec * i * (1.0 - i)
        drec = lmb * mult * i + _dot(dra, wa_ref[0], NT) + _dot(dxa, wx_ref[0], NT)
        dp_ref[0] = _conv_bwd_input(drec, cw_ref, LRU_CONV).astype(dp_ref.dtype)

        @pl.when(pl.program_id(1) == 0)
        def _():
            for ref in (dcw_ref, dcb_ref, dwa_ref, dba_ref, dwx_ref, dbx_ref, dlam_ref):
                ref[...] = jnp.zeros_like(ref)

        for k, row in enumerate(_conv_bwd_weight(drec, pre, LRU_CONV)):
            dcw_ref[k:k + 1, :] += row
        dcb_ref[...] += jnp.sum(drec, axis=0, keepdims=True)
        dwa_ref[0] += _dot(rec, dra, TN)
        dwx_ref[0] += _dot(rec, dxa, TN)
        dba_ref[...] += jnp.sum(dra, axis=0, keepdims=True)
        dbx_ref[...] += jnp.sum(dxa, axis=0, keepdims=True)
        dsp = jnp.sum(dlog_a * (-LRU_C) * r, axis=0, keepdims=True)
        dlam_ref[...] += dsp * (-_sig(-lam_ref[...]))

    gsp = pl.BlockSpec((1, s, LRU_BLOCK), lambda n, bb: (bb, 0, n))
    psp = pl.BlockSpec((1, s, LRU_BLOCK), lambda n, bb: (bb, 0, n + nb))
    cws = pl.BlockSpec((LRU_CONV, LRU_BLOCK), lambda n, bb: (0, n))
    vec = pl.BlockSpec((1, LRU_BLOCK), lambda n, bb: (0, n))
    wsp = pl.BlockSpec((1, LRU_BLOCK, LRU_BLOCK), lambda n, bb: (n, 0, 0))
    act = jax.ShapeDtypeStruct((bsz, s, rw), BF16)
    vsh = jax.ShapeDtypeStruct((1, rw), F32)
    wsh = jax.ShapeDtypeStruct((nb, LRU_BLOCK, LRU_BLOCK), F32)
    return pl.pallas_call(
        body, grid=(nb, bsz), in_specs=[gsp, psp, gsp, gsp, cws, vec, wsp, vec, wsp, vec, vec],
        out_specs=[gsp, gsp, cws, vec, wsp, vec, wsp, vec, vec],
        out_shape=[act, act, jax.ShapeDtypeStruct((LRU_CONV, rw), F32), vsh, wsh, vsh, wsh, vsh, vsh],
        compiler_params=_cp("parallel", "arbitrary"), name=name)(zz, zz, hst, dy, cw, cb, wa, ba, wx, bx, lam)


def lru_layer_fwd(x, g, p, bsz):
    m, d = x.shape
    h = rms_fwd(x, g, BF16, "lru_norm")
    zz = matmul(h, p["lru_w_in"], name="lru_in")
    hst, y = lru_fwd(zz.reshape(bsz, m // bsz, -1), p["lru_conv_w"], p["lru_conv_b"], p["lru_w_a"], p["lru_b_a"],
                     p["lru_w_x"], p["lru_b_x"], p["lru_lam"])
    y2 = y.reshape(m, -1)
    out = matmul(y2, p["lru_w_out"], res=x, name="lru_out")
    return out, (x, h, zz, hst, y2)


def lru_layer_bwd(saved, dout, g, p, bsz):
    x, h, zz, hst, y2 = saved
    m, d = x.shape
    dy = matmul(dout, p["lru_w_out"], tb=True, name="lru_out_dx")
    dw_out = matmul(y2, dout, ta=True, name="lru_out_dw")
    dgate, dpre, dcw, dcb, dwa, dba, dwx, dbx, dlam = lru_bwd(
        zz.reshape(bsz, m // bsz, -1), hst, dy.reshape(bsz, m // bsz, -1), p["lru_conv_w"], p["lru_conv_b"],
        p["lru_w_a"], p["lru_b_a"], p["lru_w_x"], p["lru_b_x"], p["lru_lam"])
    dzz = jnp.concatenate([dgate, dpre], axis=-1).reshape(m, -1)
    dh = matmul(dzz, p["lru_w_in"], tb=True, name="lru_in_dx")
    dw_in = matmul(h, dzz, ta=True, name="lru_in_dw")
    dx, dg = rms_bwd(x, g, dh, dout, "lru_norm_bwd")
    return dx, dict(norm_mix_g=dg, lru_w_in=dw_in, lru_conv_w=dcw[None], lru_conv_b=dcb, lru_w_a=dwa[None],
                    lru_b_a=dba, lru_w_x=dwx[None], lru_b_x=dbx, lru_lam=dlam, lru_w_out=dw_out)


def _s5_discretise(lam_re, lam_im, log_dt, b_re, b_im):
    lr = jnp.minimum(lam_re, -1e-4)
    dt = jnp.exp(log_dt)[:, None]
    mag = jnp.exp(lr * dt)
    ar, ai = mag * jnp.cos(lam_im * dt), mag * jnp.sin(lam_im * dt)
    den = lr * lr + lam_im * lam_im
    cr = ((ar - 1.0) * lr + ai * lam_im) / den
    ci = (ai * lr - (ar - 1.0) * lam_im) / den
    bbr = cr[..., None] * b_re - ci[..., None] * b_im
    bbi = cr[..., None] * b_im + ci[..., None] * b_re
    return ar, ai, bbr, bbi


def _s5_powers(lam_re, lam_im, log_dt, ns):
    lr = jnp.minimum(lam_re, -1e-4)
    dt = jnp.exp(log_dt)[:, None]
    n = jnp.asarray(ns, F32)[:, None, None]
    mag = jnp.exp(n * (lr * dt))
    ang = n * (lam_im * dt)
    to_chunks = lambda t: t.reshape(len(ns), S5_CHUNKS, S5_LANES).transpose(1, 0, 2)
    return jnp.concatenate([to_chunks(mag * jnp.cos(ang)), to_chunks(mag * jnp.sin(ang))], axis=-1)


def _s5_in_matrix(bbr, bbi):
    eye = jnp.eye(8, dtype=F32)
    blk = lambda t: jnp.einsum("qgph,gk->qghkp", t.reshape(S5_CHUNKS, 8, S5_STATE, S5_GROUP), eye).reshape(
        S5_CHUNKS, 128, S5_LANES)
    return jnp.concatenate([blk(bbr), blk(bbi)], axis=-1)


def _s5_in_matrix_diag(dmat):
    eye = jnp.eye(8, dtype=F32)[None, :, None, :, None]
    pick = lambda t: (t.reshape(S5_CHUNKS, 8, S5_GROUP, 8, S5_STATE) * eye).sum(3).transpose(0, 1, 3, 2).reshape(
        S5_CHUNKS * 8, S5_STATE, S5_GROUP)
    return pick(dmat[..., :S5_LANES]), pick(dmat[..., S5_LANES:])


def _s5_out_matrix(c_re, c_im):
    eye = jnp.eye(8, dtype=F32)
    blk = lambda t: jnp.einsum("qghp,gk->qgpkh", t.reshape(S5_CHUNKS, 8, S5_GROUP, S5_STATE), eye).reshape(
        S5_CHUNKS, S5_LANES, 128)
    return jnp.concatenate([blk(c_re), -blk(c_im)], axis=1)


def _s5_out_matrix_diag(dmat):
    eye = jnp.eye(8, dtype=F32)[None, :, None, :, None]
    pick = lambda t: (t.reshape(S5_CHUNKS, 8, S5_STATE, 8, S5_GROUP) * eye).sum(3).transpose(0, 1, 3, 2).reshape(
        S5_CHUNKS * 8, S5_GROUP, S5_STATE)
    return pick(dmat[:, :S5_LANES]), -pick(dmat[:, S5_LANES:])


def s5_fwd(h, bmat, cmat, atab, pw, dskip, name="s5_fwd"):
    bsz, s, d = h.shape
    t = min(S5_T, s)
    nt, nlev, ln = s // t, atab.shape[1], S5_LANES

    def body(h_ref, b_ref, c_ref, a_ref, pw_ref, d_ref, xs_ref, yp_ref, yg_ref, carry):
        @pl.when(pl.program_id(2) == 0)
        def _():
            carry[...] = jnp.zeros_like(carry)

        u = h_ref[0]
        bu = _dot3(u, b_ref[0])
        xr, xi = bu[:, :ln], bu[:, ln:]
        for k in range(nlev):
            ar, ai = a_ref[0, k:k + 1, :ln], a_ref[0, k:k + 1, ln:]
            sr, si = _shift_down(xr, 1 << k), _shift_down(xi, 1 << k)
            xr, xi = xr + ar * sr - ai * si, xi + ar * si + ai * sr
        cr, ci = carry[0:1, :ln], carry[0:1, ln:]
        pr, pi = pw_ref[0, :, :ln], pw_ref[0, :, ln:]
        xr, xi = xr + pr * cr - pi * ci, xi + pr * ci + pi * cr
        carry[0:1, :ln] = xr[t - 1:t, :]
        carry[0:1, ln:] = xi[t - 1:t, :]
        xs_ref[0, :, :ln] = xr
        xs_ref[0, :, ln:] = xi
        y = _dot3(xr, c_ref[0, :ln, :]) + _dot3(xi, c_ref[0, ln:, :]) + d_ref[...] * u
        yp_ref[0] = y
        yg_ref[0] = _gelu(y).astype(yg_ref.dtype)

    act = pl.BlockSpec((1, t, 128), lambda b, q, i: (b, i, q))
    par = lambda r, c: pl.BlockSpec((1, r, c), lambda b, q, i: (q, 0, 0))
    return pl.pallas_call(
        body, grid=(bsz, S5_CHUNKS, nt),
        in_specs=[act, par(128, 2 * ln), par(2 * ln, 128), par(nlev, 2 * ln), par(t, 2 * ln),
                  pl.BlockSpec((1, 128), lambda b, q, i: (0, q))],
        out_specs=[pl.BlockSpec((1, t, 2 * ln), lambda b, q, i: (b, i, q)), act, act],
        out_shape=[jax.ShapeDtypeStruct((bsz, s, S5_CHUNKS * 2 * ln), F32), jax.ShapeDtypeStruct((bsz, s, d), F32),
                   jax.ShapeDtypeStruct((bsz, s, d), BF16)],
        scratch_shapes=[pltpu.VMEM((8, 2 * ln), F32)],
        compiler_params=_cp("parallel", "parallel", "arbitrary"), name=name)(h, bmat, cmat, atab, pw, dskip)


def s5_bwd(h, ypre, xs, dyg, bmat_t, cmat_t, atab, pw_rev, dskip, name="s5_bwd"):
    bsz, s, d = h.shape
    t = min(S5_T, s)
    nt, nlev, ln = s // t, atab.shape[1], S5_LANES

    def body(h_ref, yp_ref, xs_ref, xp_ref, dy_ref, bt_ref, ct_ref, a_ref, pw_ref, d_ref,
             dh_ref, db_ref, dc_ref, da_ref, dd_ref, carry):
        b, i = pl.program_id(1), pl.program_id(2)

        @pl.when((b == 0) & (i == 0))
        def _():
            for r in (db_ref, dc_ref, da_ref, dd_ref):
                r[...] = jnp.zeros_like(r)

        @pl.when(i == 0)
        def _():
            carry[...] = jnp.zeros_like(carry)

        u = h_ref[0]
        dyp = dy_ref[0] * _gelu_grad(yp_ref[0])
        dd_ref[...] += jnp.sum(dyp * u, axis=0, keepdims=True)
        xr, xi = xs_ref[0, :, :ln], xs_ref[0, :, ln:]
        dc_ref[0, :ln, :] += _dot3(xr, dyp, TN)
        dc_ref[0, ln:, :] += _dot3(xi, dyp, TN)
        lr, li = _dot3(dyp, ct_ref[0, :, :ln]), _dot3(dyp, ct_ref[0, :, ln:])
        for k in range(nlev):
            ar, ai = a_ref[0, k:k + 1, :ln], a_ref[0, k:k + 1, ln:]
            sr, si = _shift_up(lr, 1 << k), _shift_up(li, 1 << k)
            lr, li = lr + ar * sr + ai * si, li + ar * si - ai * sr
        cr, ci = carry[0:1, :ln], carry[0:1, ln:]
        pr, pi = pw_ref[0, :, :ln], pw_ref[0, :, ln:]
        lr, li = lr + pr * cr + pi * ci, li + pr * ci - pi * cr
        carry[0:1, :ln] = lr[0:1, :]
        carry[0:1, ln:] = li[0:1, :]
        dh_ref[0] = _dot3(lr, bt_ref[0, :ln, :]) + _dot3(li, bt_ref[0, ln:, :]) + dyp * d_ref[...]
        db_ref[0, :, :ln] += _dot3(u, lr, TN)
        db_ref[0, :, ln:] += _dot3(u, li, TN)
        first = _rows(xr.shape) == 0
        keep = jnp.where(i == nt - 1, 0.0, 1.0)
        xpr = jnp.where(first, xp_ref[0, 7:8, :ln] * keep, _shift_down(xr, 1))
        xpi = jnp.where(first, xp_ref[0, 7:8, ln:] * keep, _shift_down(xi, 1))
        da_ref[0, 0:1, :ln] += jnp.sum(lr * xpr + li * xpi, axis=0, keepdims=True)
        da_ref[0, 0:1, ln:] += jnp.sum(li * xpr - lr * xpi, axis=0, keepdims=True)

    rev = lambda i: nt - 1 - i
    act = pl.BlockSpec((1, t, 128), lambda q, b, i: (b, rev(i), q))
    xsp = pl.BlockSpec((1, t, 2 * ln), lambda q, b, i: (b, rev(i), q))
    xpp = pl.BlockSpec((1, 8, 2 * ln), lambda q, b, i: (b, jnp.maximum(rev(i) * (t // 8) - 1, 0), q))
    par = lambda r, c: pl.BlockSpec((1, r, c), lambda q, b, i: (q, 0, 0))
    dsp = pl.BlockSpec((1, 128), lambda q, b, i: (0, q))
    return pl.pallas_call(
        body, grid=(S5_CHUNKS, bsz, nt),
        in_specs=[act, act, xsp, xpp, act, par(2 * ln, 128), par(128, 2 * ln), par(nlev, 2 * ln), par(t, 2 * ln), dsp],
        out_specs=[act, par(128, 2 * ln), par(2 * ln, 128), par(8, 2 * ln), dsp],
        out_shape=[jax.ShapeDtypeStruct((bsz, s, d), F32), jax.ShapeDtypeStruct((S5_CHUNKS, 128, 2 * ln), F32),
                   jax.ShapeDtypeStruct((S5_CHUNKS, 2 * ln, 128), F32), jax.ShapeDtypeStruct((S5_CHUNKS, 8, 2 * ln), F32),
                   jax.ShapeDtypeStruct((1, d), F32)],
        scratch_shapes=[pltpu.VMEM((8, 2 * ln), F32)],
        compiler_params=_cp("parallel", "arbitrary", "arbitrary"), name=name)(
            h, ypre, xs, xs, dyg, bmat_t, cmat_t, atab, pw_rev, dskip)


def glu_fwd(z, x, name="s5_glu"):
    m, d = x.shape
    tr = _pick(m, (512, 256, 128))

    def body(z_ref, x_ref, o_ref):
        o_ref[...] = x_ref[...] + z_ref[:, :d] * _sig(z_ref[:, d:])

    return pl.pallas_call(
        body, grid=(m // tr,),
        in_specs=[pl.BlockSpec((tr, 2 * d), lambda i: (i, 0)), pl.BlockSpec((tr, d), lambda i: (i, 0))],
        out_specs=pl.BlockSpec((tr, d), lambda i: (i, 0)), out_shape=jax.ShapeDtypeStruct((m, d), F32),
        compiler_params=_cp("parallel"), name=name)(z, x)


def glu_bwd(z, dout, name="s5_glu_bwd"):
    m, d = dout.shape
    tr = _pick(m, (512, 256, 128))

    def body(z_ref, do_ref, dz_ref, db_ref):
        sg = _sig(z_ref[:, d:])
        dv = do_ref[...] * sg
        dgt = do_ref[...] * z_ref[:, :d] * sg * (1.0 - sg)
        dz_ref[:, :d] = dv.astype(dz_ref.dtype)
        dz_ref[:, d:] = dgt.astype(dz_ref.dtype)

        @pl.when(pl.program_id(0) == 0)
        def _():
            db_ref[...] = jnp.zeros_like(db_ref)

        db_ref[:, :d] += jnp.sum(dv, axis=0, keepdims=True)
        db_ref[:, d:] += jnp.sum(dgt, axis=0, keepdims=True)

    wide = pl.BlockSpec((tr, 2 * d), lambda i: (i, 0))
    return pl.pallas_call(
        body, grid=(m // tr,), in_specs=[wide, pl.BlockSpec((tr, d), lambda i: (i, 0))],
        out_specs=[wide, pl.BlockSpec((1, 2 * d), lambda i: (0, 0))],
        out_shape=[jax.ShapeDtypeStruct((m, 2 * d), BF16), jax.ShapeDtypeStruct((1, 2 * d), F32)],
        compiler_params=_cp("arbitrary"), name=name)(z, dout)


def _s5_tables(p, t):
    nlev = max(1, (t - 1).bit_length())
    lam = (p["s5_lam_re"], p["s5_lam_im"], p["s5_log_dt"])
    atab = _s5_powers(*lam, [1 << k for k in range(nlev)])
    if nlev < 8:
        atab = jnp.pad(atab, ((0, 0), (0, 8 - nlev), (0, 0)))
    pw = _s5_powers(*lam, list(range(1, t + 1)))
    return nlev, atab, pw


def s5_layer_fwd(x, g, p, bsz):
    m, d = x.shape
    s = m // bsz
    t = min(S5_T, s)
    h = rms_fwd(x, g, F32, "s5_norm")
    _, _, bbr, bbi = _s5_discretise(p["s5_lam_re"], p["s5_lam_im"], p["s5_log_dt"], p["s5_b_re"], p["s5_b_im"])
    nlev, atab, pw = _s5_tables(p, t)
    bmat, cmat = _s5_in_matrix(bbr, bbi), _s5_out_matrix(p["s5_c_re"], p["s5_c_im"])
    xs, ypre, yg = s5_fwd(h.reshape(bsz, s, d), bmat, cmat, atab[:, :max(nlev, 8)], pw, p["s5_d"])
    z = matmul(yg.reshape(m, d), p["s5_w_out"], bias=p["s5_b_out"], name="s5_out")
    out = glu_fwd(z, x)
    return out, (x, h, xs, ypre, yg, z, bmat, cmat, atab, pw)


def s5_layer_bwd(saved, dout, g, p, bsz):
    x, h, xs, ypre, yg, z, bmat, cmat, atab, pw = saved
    m, d = x.shape
    s = m // bsz
    dz, db_out = glu_bwd(z, dout)
    dyg = matmul(dz, p["s5_w_out"], tb=True, name="s5_out_dx")
    dw_out = matmul(yg.reshape(m, d), dz, ta=True, name="s5_out_dw")
    dh, dbm, dcm, dlam, dd = s5_bwd(h.reshape(bsz, s, d), ypre, xs, dyg.reshape(bsz, s, d),
                                    bmat.transpose(0, 2, 1), cmat.transpose(0, 2, 1), atab, pw[:, ::-1], p["s5_d"])
    dx, dg = rms_bwd(x, g, dh.reshape(m, d), dout, "s5_norm_bwd")
    dbbr, dbbi = _s5_in_matrix_diag(dbm)
    dc_re, dc_im = _s5_out_matrix_diag(dcm)
    dar = dlam[:, 0, :S5_LANES].reshape(S5_CHUNKS * 8, S5_STATE)
    dai = dlam[:, 0, S5_LANES:].reshape(S5_CHUNKS * 8, S5_STATE)
    _, vjp = jax.vjp(_s5_discretise, p["s5_lam_re"], p["s5_lam_im"], p["s5_log_dt"], p["s5_b_re"], p["s5_b_im"])
    dl_re, dl_im, dldt, db_re, db_im = vjp((dar, dai, dbbr, dbbi))
    return dx, dict(norm_mix_g=dg, s5_lam_re=dl_re[None], s5_lam_im=dl_im[None], s5_log_dt=dldt[None],
                    s5_b_re=db_re[None], s5_b_im=db_im[None], s5_c_re=dc_re[None], s5_c_im=dc_im[None],
                    s5_d=dd, s5_w_out=dw_out, s5_b_out=db_out)


def _log_sigmoid(z):
    return jnp.minimum(z, 0.0) - jnp.log(1.0 + jnp.exp(-jnp.abs(z)))


def _head_norm(t, g_ref):
    r = lax.rsqrt(jnp.mean(t * t, axis=-1, keepdims=True) + EPS)
    th = t * r
    return th * g_ref[...], th, r


def _tri(shape, fn):
    row = lax.broadcasted_iota(jnp.int32, shape, 0)
    col = lax.broadcasted_iota(jnp.int32, shape, 1)
    return fn(row, col)


_SB_SCALE = 1.0 / math.sqrt(SB_DIM)


def _suffix_sums(t, later):
    n = t.shape[1] // SB_CHUNK
    outs, carry = [None] * n, jnp.zeros((t.shape[0], 1), F32)
    for ci in range(n - 1, -1, -1):
        ch = t[:, ci * SB_CHUNK:(ci + 1) * SB_CHUNK]
        outs[ci] = _dot_exact_rhs(ch, later) + carry
        carry = carry + jnp.sum(ch, axis=1, keepdims=True)
    return (outs[0] if n == 1 else jnp.concatenate(outs, axis=1)), carry


def _prefix_sums(t, tri):
    n = t.shape[1] // SB_CHUNK
    outs, carry = [None] * n, jnp.zeros((t.shape[0], 1), F32)
    for ci in range(n):
        ch = t[:, ci * SB_CHUNK:(ci + 1) * SB_CHUNK]
        outs[ci] = _dot_exact_rhs(ch, tri) + carry
        carry = carry + jnp.sum(ch, axis=1, keepdims=True)
    return (outs[0] if n == 1 else jnp.concatenate(outs, axis=1)), carry


def sb_fwd(q, k, v, qg, kg, name="sb_fwd"):
    bsz, nh, s, dh = q.shape
    tb = min(SB_BLOCK, s)
    nq = s // tb

    def body(q_ref, k_ref, v_ref, qg_ref, kg_ref, o_ref, rt_ref):
        qi = pl.program_id(2)
        qn, _, _ = _head_norm(q_ref[0, 0], qg_ref)
        later = _tri((SB_CHUNK, SB_CHUNK), lambda r, c: r > c).astype(BF16)
        causal = _tri((tb, tb), lambda r, c: c < r)

        def block(kb, run, acc, diag):
            ks = pl.ds(pl.multiple_of(kb * tb, tb), tb)
            kn, _, _ = _head_norm(k_ref[0, 0, ks, :], kg_ref)
            z = _dot(qn, kn, NT) * _SB_SCALE
            ls = _log_sigmoid(z)
            lm = ls - z
            if diag:
                lm = jnp.where(causal, lm, 0.0)
            rest, total = _suffix_sums(lm, later)
            att = jnp.exp(ls + run + rest)
            if diag:
                att = jnp.where(causal, att, 0.0)
            return run + total, acc + _dot(att, v_ref[0, 0, ks, :])

        run, acc = block(qi, jnp.zeros((tb, 1), F32), jnp.zeros((tb, dh), F32), True)
        run, acc = lax.fori_loop(0, qi, lambda j, c: block(qi - 1 - j, c[0], c[1], False), (run, acc))
        o_ref[0, 0] = acc
        rt_ref[0, 0] = run

    qsp = pl.BlockSpec((1, 1, tb, dh), lambda b, h, i: (b, h, i, 0))
    rsp = pl.BlockSpec((1, 1, tb, 1), lambda b, h, i: (b, h, i, 0))
    ksp = pl.BlockSpec((1, 1, s, dh), lambda b, h, i: (b, h, 0, 0))
    gsp = pl.BlockSpec((1, dh), lambda b, h, i: (0, 0))
    return pl.pallas_call(
        body, grid=(bsz, nh, nq), in_specs=[qsp, ksp, ksp, gsp, gsp], out_specs=[qsp, rsp],
        out_shape=[jax.ShapeDtypeStruct((bsz, nh, s, dh), F32), jax.ShapeDtypeStruct((bsz, nh, s, 1), F32)],
        compiler_params=_cp("parallel", "parallel", "arbitrary"), name=name)(q, k, v, qg, kg)


def sb_bwd(q, k, v, rtot, do, qg, kg, name="sb_bwd"):
    bsz, nh, s, dh = q.shape
    tb = min(SB_BLOCK, s)
    nq = s // tb

    def body(q_ref, k_ref, v_ref, rt_ref, do_ref, qg_ref, kg_ref, dq_ref, dk_ref, dv_ref, dqg_ref, dkg_ref,
             qn_s, kn_s, dqn_s, dkn_s, dv_s):
        qn, qh, rq = _head_norm(q_ref[0, 0], qg_ref)
        kn, kh, rk = _head_norm(k_ref[0, 0], kg_ref)
        qn_s[...] = qn
        kn_s[...] = kn
        dkn_s[...] = jnp.zeros_like(dkn_s)
        dv_s[...] = jnp.zeros_like(dv_s)
        chunk = (SB_CHUNK, SB_CHUNK)
        upto = _tri(chunk, lambda r, c: r <= c).astype(BF16)
        earlier = _tri(chunk, lambda r, c: r < c).astype(BF16)
        causal = _tri((tb, tb), lambda r, c: c < r)

        def q_block(qi, _):
            qs = pl.ds(pl.multiple_of(qi * tb, tb), tb)
            qnb, dob, rtb = qn_s[qs, :], do_ref[0, 0, qs, :], rt_ref[0, 0, qs, :]

            def block(kb, left, seen, dqn, diag):
                ks = pl.ds(pl.multiple_of(kb * tb, tb), tb)
                knb, vb = kn_s[ks, :], v_ref[0, 0, ks, :]
                z = _dot(qnb, knb, NT) * _SB_SCALE
                ls = _log_sigmoid(z)
                lm = ls - z
                if diag:
                    lm = jnp.where(causal, lm, 0.0)
                through, lm_total = _prefix_sums(lm, upto)
                att = jnp.exp(ls + (rtb - left - through))
                if diag:
                    att = jnp.where(causal, att, 0.0)
                gg = att * _dot(dob, vb, NT)
                before, gg_total = _prefix_sums(gg, earlier)
                sg = jnp.exp(ls)
                dz = gg * (1.0 - sg) - sg * (seen + before)
                if diag:
                    dz = jnp.where(causal, dz, 0.0)
                dz = dz * _SB_SCALE
                dkn_s[ks, :] += _dot(dz, qnb, TN)
                dv_s[ks, :] += _dot(att, dob, TN)
                return left + lm_total, seen + gg_total, dqn + _dot(dz, knb)

            zero = jnp.zeros((tb, 1), F32)
            c = lax.fori_loop(0, qi, lambda kb, c: block(kb, c[0], c[1], c[2], False),
                              (zero, zero, jnp.zeros((tb, dh), F32)))
            c = block(qi, c[0], c[1], c[2], True)
            dqn_s[qs, :] = c[2]
            return 0

        lax.fori_loop(0, nq, q_block, 0)

        @pl.when((pl.program_id(0) == 0) & (pl.program_id(1) == 0))
        def _():
            dqg_ref[...] = jnp.zeros_like(dqg_ref)
            dkg_ref[...] = jnp.zeros_like(dkg_ref)

        def norm_bwd(dn, th, r, g_ref, dt_ref, dg_ref):
            dg_ref[...] += jnp.sum(dn * th, axis=0, keepdims=True)
            dth = dn * g_ref[...]
            dt_ref[0, 0] = r * (dth - th * jnp.mean(dth * th, axis=-1, keepdims=True))

        norm_bwd(dqn_s[...], qh, rq, qg_ref, dq_ref, dqg_ref)
        norm_bwd(dkn_s[...], kh, rk, kg_ref, dk_ref, dkg_ref)
        dv_ref[0, 0] = dv_s[...]

    hsp = pl.BlockSpec((1, 1, s, dh), lambda b, h: (b, h, 0, 0))
    gsp = pl.BlockSpec((1, dh), lambda b, h: (0, 0))
    act = jax.ShapeDtypeStruct((bsz, nh, s, dh), F32)
    gsh = jax.ShapeDtypeStruct((1, dh), F32)
    rsp = pl.BlockSpec((1, 1, s, 1), lambda b, h: (b, h, 0, 0))
    return pl.pallas_call(
        body, grid=(bsz, nh), in_specs=[hsp, hsp, hsp, rsp, hsp, gsp, gsp], out_specs=[hsp, hsp, hsp, gsp, gsp],
        out_shape=[act, act, act, gsh, gsh], scratch_shapes=[pltpu.VMEM((s, dh), F32)] * 5,
        compiler_params=_cp("arbitrary", "arbitrary"), name=name)(q, k, v, rtot, do, qg, kg)


def _to_heads(t, bsz):
    m, w = t.shape
    n = w // (SB_HEADS * SB_DIM)
    t = t.reshape(bsz, m // bsz, n, SB_HEADS, SB_DIM).transpose(2, 0, 3, 1, 4)
    return [t[i] for i in range(n)]


def _from_heads(ts):
    t = jnp.stack(ts, axis=0)
    n, bsz, nh, s, dh = t.shape
    return t.transpose(1, 3, 0, 2, 4).reshape(bsz * s, n * nh * dh)


def sb_layer_fwd(x, g, p, bsz):
    m, d = x.shape
    h = rms_fwd(x, g, BF16, "sb_norm")
    qkv = matmul(h, p["sb_w_qkv"], name="sb_qkv")
    q, k, v = _to_heads(qkv, bsz)
    o, rtot = sb_fwd(q, k, v, p["sb_q_g"], p["sb_k_g"])
    o2 = _from_heads([o])
    out = matmul(o2, p["sb_w_o"], res=x, name="sb_out")
    return out, (x, h, q, k, v, rtot, o2)


def sb_layer_bwd(saved, dout, g, p, bsz):
    x, h, q, k, v, rtot, o2 = saved
    do2 = matmul(dout, p["sb_w_o"], tb=True, name="sb_out_dx")
    dw_o = matmul(o2, dout, ta=True, name="sb_out_dw")
    dq, dk, dv, dqg, dkg = sb_bwd(q, k, v, rtot, _to_heads(do2, bsz)[0], p["sb_q_g"], p["sb_k_g"])
    dqkv = _from_heads([dq, dk, dv])
    dh = matmul(dqkv, p["sb_w_qkv"], tb=True, name="sb_qkv_dx")
    dw_qkv = matmul(h, dqkv, ta=True, name="sb_qkv_dw")
    dx, dg = rms_bwd(x, g, dh, dout, "sb_norm_bwd")
    return dx, dict(norm_mix_g=dg, sb_w_qkv=dw_qkv, sb_q_g=dqg, sb_k_g=dkg, sb_w_o=dw_o)


_CHIP_FLIPS = ((1, 0), (0, 1), (1, 1))
_MESH = pl.DeviceIdType.MESH
_ANY = pl.BlockSpec(memory_space=pl.ANY)


def _flip(v, f):
    return 1 - v if f else v


def _splits(shape):
    return shape[-2] % 32 == 0


def _half(ref, c, rows):
    idx = (slice(None),) * (len(ref.shape) - 2) + (pl.ds(pl.multiple_of(c * (rows // 2), 16), rows // 2),)
    return ref.at[idx]


def gather_weights(shards, name="gather_weights"):
    n = len(shards)
    split = [_splits(s.shape) for s in shards]

    def body(*refs):
        ins, outs = refs[:n], refs[n:2 * n]
        send, recv, fsend, frecv, loc = refs[2 * n:]
        x, y, c = lax.axis_index("x"), lax.axis_index("y"), lax.axis_index("c")
        me = 2 * x + y
        sibling = (x, y, 1 - c)

        def remote(i, j, block):
            px, py = _flip(x, _CHIP_FLIPS[j][0]), _flip(y, _CHIP_FLIPS[j][1])
            rows = shards[i].shape[0]
            src = _half(ins[i], c, rows) if split[i] else ins[i]
            dst = _half(outs[i].at[block], c, rows) if split[i] else outs[i].at[block]
            return pltpu.make_async_remote_copy(
                src_ref=src, dst_ref=dst, send_sem=send.at[3 * i + j], recv_sem=recv.at[3 * i + j],
                device_id=(px, py, c), device_id_type=_MESH)

        def forward(i, j, half):
            rows = _half(outs[i].at[2 * _flip(x, _CHIP_FLIPS[j][0]) + _flip(y, _CHIP_FLIPS[j][1])], half,
                         shards[i].shape[0])
            return pltpu.make_async_remote_copy(
                src_ref=rows, dst_ref=rows, send_sem=fsend.at[3 * i + j], recv_sem=frecv.at[3 * i + j],
                device_id=sibling, device_id_type=_MESH)

        local = [pltpu.make_async_copy(ins[i], outs[i].at[me], loc.at[i]) for i in range(n)]
        sends = [remote(i, j, me) for i in range(n) for j in range(3)]
        for cp in local + sends:
            cp.start()
        fwd = []
        for i in range(n):
            for j, (fx, fy) in enumerate(_CHIP_FLIPS):
                remote(i, j, 2 * _flip(x, fx) + _flip(y, fy)).wait_recv()
                if split[i]:
                    fwd.append(forward(i, j, c))
                    fwd[-1].start()
        for i in range(n):
            if split[i]:
                for j in range(3):
                    forward(i, j, 1 - c).wait_recv()
        for cp in sends + fwd:
            cp.wait_send()
        for cp in local:
            cp.wait()

    return pl.pallas_call(
        body, in_specs=[_ANY] * n, out_specs=[_ANY] * n,
        out_shape=[jax.ShapeDtypeStruct((N_CHIPS,) + s.shape, s.dtype) for s in shards],
        scratch_shapes=[pltpu.SemaphoreType.DMA((3 * n,))] * 4 + [pltpu.SemaphoreType.DMA((n,))],
        name=name)(*shards)


def grad_halves_exchange(parts, name="grad_halves_exchange"):
    n = len(parts)

    def body(*refs):
        ins, kept, got = refs[:n], refs[n:2 * n], refs[2 * n:3 * n]
        send, recv, loc = refs[3 * n:]
        x, y, c = lax.axis_index("x"), lax.axis_index("y"), lax.axis_index("c")
        local = [pltpu.make_async_copy(_half(ins[i], c, parts[i].shape[1]), kept[i], loc.at[i]) for i in range(n)]
        swap = [pltpu.make_async_remote_copy(
            src_ref=_half(ins[i], 1 - c, parts[i].shape[1]), dst_ref=got[i], send_sem=send.at[i], recv_sem=recv.at[i],
            device_id=(x, y, 1 - c), device_id_type=_MESH) for i in range(n)]
        for cp in local + swap:
            cp.start()
        for cp in swap:
            cp.wait()
        for cp in local:
            cp.wait()

    half = [jax.ShapeDtypeStruct((N_CHIPS, p.shape[1] // 2, p.shape[2]), p.dtype) for p in parts]
    out = pl.pallas_call(
        body, in_specs=[_ANY] * n, out_specs=[_ANY] * (2 * n), out_shape=half + half,
        scratch_shapes=[pltpu.SemaphoreType.DMA((n,))] * 3, name=name)(*parts)
    return out[:n], out[n:]


def pair_sum(a, b, out_dtype, name):
    shape = a.shape
    a2, b2 = a.reshape(-1, shape[-1]), b.reshape(-1, shape[-1])
    r, c = a2.shape
    tr = _pick(r, tuple(t for t in (512, 256, 128, 64, 32, 16) if t * c * 4 <= 1024 * 1024))

    def body(a_ref, b_ref, o_ref):
        o_ref[...] = (a_ref[...] + b_ref[...]).astype(o_ref.dtype)

    row = pl.BlockSpec((tr, c), lambda i: (i, 0))
    return pl.pallas_call(
        body, grid=(r // tr,), in_specs=[row, row], out_specs=row, out_shape=jax.ShapeDtypeStruct((r, c), out_dtype),
        compiler_params=_cp("parallel"), name=name)(a2, b2).reshape(shape)


def grad_sync(halves, parts, packed, name="grad_sync"):
    nh, n = len(halves), len(parts)
    nt = nh + n

    def body(*refs):
        hin, ins, pk = refs[:nh], refs[nh:nt], refs[nt]
        outs = refs[nt + 1:]
        landed, mine, theirs, pk_all = outs[:nh], outs[nh:nt], outs[nt:nt + n], outs[nt + n]
        send, recv, loc, fsend, frecv, psend, precv, ploc = outs[nt + n + 1:]
        x, y, c = lax.axis_index("x"), lax.axis_index("y"), lax.axis_index("c")
        me = 2 * x + y
        dev = 4 * x + 2 * y + c
        sibling = (x, y, 1 - c)
        rows = [h.shape[1] * 2 for h in halves]

        def remote(i, j, slot):
            px, py = _flip(x, _CHIP_FLIPS[j][0]), _flip(y, _CHIP_FLIPS[j][1])
            if i < nh:
                src, dst = hin[i].at[2 * px + py], _half(landed[i].at[slot], c, rows[i])
            else:
                src, dst = ins[i - nh].at[2 * px + py], mine[i - nh].at[slot]
            return pltpu.make_async_remote_copy(
                src_ref=src, dst_ref=dst, send_sem=send.at[3 * i + j], recv_sem=recv.at[3 * i + j],
                device_id=(px, py, c), device_id_type=_MESH)

        def packed_to(r, slot):
            px, py, pc = _flip(x, r & 4), _flip(y, r & 2), _flip(c, r & 1)
            return pltpu.make_async_remote_copy(
                src_ref=pk, dst_ref=pk_all.at[slot], send_sem=psend.at[r - 1], recv_sem=precv.at[r - 1],
                device_id=(px, py, pc), device_id_type=_MESH)

        def forward(i, half):
            if i < nh:
                src = dst = _half(landed[i], half, rows[i])
            else:
                src, dst = mine[i - nh], theirs[i - nh]
            return pltpu.make_async_remote_copy(
                src_ref=src, dst_ref=dst, send_sem=fsend.at[i], recv_sem=frecv.at[i],
                device_id=sibling, device_id_type=_MESH)

        local = [pltpu.make_async_copy(hin[i].at[me], _half(landed[i].at[me], c, rows[i]), loc.at[i])
                 for i in range(nh)]
        local += [pltpu.make_async_copy(ins[i].at[me], mine[i].at[me], loc.at[nh + i]) for i in range(n)]
        plocal = pltpu.make_async_copy(pk, pk_all.at[dev], ploc.at[0])
        sends = [remote(i, j, me) for i in range(nt) for j in range(3)]
        psends = [packed_to(r, dev) for r in range(1, N_DEV)]
        for cp in local + [plocal] + sends + psends:
            cp.start()
        fwd = [forward(i, c) for i in range(nt)]
        for i in range(nt):
            for j, (fx, fy) in enumerate(_CHIP_FLIPS):
                remote(i, j, 2 * _flip(x, fx) + _flip(y, fy)).wait_recv()
            local[i].wait()
            fwd[i].start()
        for i in range(nt):
            forward(i, 1 - c).wait_recv()
        for r in range(1, N_DEV):
            packed_to(r, 4 * _flip(x, r & 4) + 2 * _flip(y, r & 2) + _flip(c, r & 1)).wait_recv()
        for cp in sends + psends + fwd:
            cp.wait_send()
        plocal.wait()

    full = [jax.ShapeDtypeStruct((N_CHIPS, 2 * h.shape[1], h.shape[2]), h.dtype) for h in halves]
    land = [jax.ShapeDtypeStruct(p.shape, p.dtype) for p in parts]
    out = pl.pallas_call(
        body, in_specs=[_ANY] * (nt + 1), out_specs=[_ANY] * (nt + n + 1),
        out_shape=full + land + land + [jax.ShapeDtypeStruct((N_DEV,) + packed.shape, packed.dtype)],
        scratch_shapes=[pltpu.SemaphoreType.DMA((3 * nt,)), pltpu.SemaphoreType.DMA((3 * nt,)),
                        pltpu.SemaphoreType.DMA((nt,)), pltpu.SemaphoreType.DMA((nt,)), pltpu.SemaphoreType.DMA((nt,)),
                        pltpu.SemaphoreType.DMA((N_DEV - 1,)), pltpu.SemaphoreType.DMA((N_DEV - 1,)),
                        pltpu.SemaphoreType.DMA((1,))],
        name=name)(*halves, *parts, packed)
    return out[:nh], out[nh:nt], out[nt:nt + n], out[nt + n]


def adamw(w, m, v, parts, name):
    r, c = w.shape
    tr = r
    for cand in (512, 256, 128, 64, 32, 16, 8):
        if r % cand == 0 and cand * c * 4 <= 512 * 1024:
            tr = cand
            break
    np_ = len(parts)
    nslot = parts[0].shape[0]
    bc1 = 1.0 - ADAM_B1 ** ADAM_STEP
    bc2 = 1.0 - ADAM_B2 ** ADAM_STEP

    def body(*refs):
        w_ref, m_ref, v_ref = refs[:3]
        p_refs = refs[3:3 + np_]
        g_ref, d_ref, nm_ref, nv_ref = refs[3 + np_:]
        g = None
        for k in range(nslot):
            t = p_refs[0][k].astype(F32)
            for p_ref in p_refs[1:]:
                t = t + p_ref[k].astype(F32)
            g = t if g is None else g + t
        wv = w_ref[...]
        nm = ADAM_B1 * m_ref[...] + (1.0 - ADAM_B1) * g
        nv = ADAM_B2 * v_ref[...] + (1.0 - ADAM_B2) * (g * g)
        g_ref[...] = g
        nm_ref[...] = nm
        nv_ref[...] = nv
        d_ref[...] = -ADAM_LR * ((nm / bc1) / (jnp.sqrt(nv / bc2) + ADAM_EPS) + ADAM_WD * wv)

    row = pl.BlockSpec((tr, c), lambda i: (i, 0))
    slab = pl.BlockSpec((nslot, tr, c), lambda i: (0, i, 0))
    sh = jax.ShapeDtypeStruct((r, c), F32)
    return pl.pallas_call(
        body, grid=(r // tr,), in_specs=[row, row, row] + [slab] * np_, out_specs=[row] * 4,
        out_shape=[sh] * 4, compiler_params=_cp("parallel"), name=name)(w, m, v, *parts)


WEIGHTS = ["norm_mix_g", "norm_ffn_g", "pool_w", "pool_b", "pool_scale", "s5_lam_re", "s5_lam_im", "s5_log_dt",
           "s5_b_re", "s5_b_im", "s5_c_re", "s5_c_im", "s5_d", "s5_w_out", "s5_b_out", "lru_w_in", "lru_conv_w",
           "lru_conv_b", "lru_w_a", "lru_b_a", "lru_w_x", "lru_b_x", "lru_lam", "lru_w_out", "sb_w_qkv", "sb_q_g",
           "sb_k_g", "sb_w_o", "ffn_w_in", "ffn_conv_w", "ffn_conv_b", "ffn_w_out"]
SHARD_AXIS = dict(pool_w=2, s5_d=1, s5_w_out=2, s5_b_out=1, lru_w_in=2, lru_conv_w=2, lru_conv_b=1, lru_w_a=2,
                  lru_b_a=1, lru_w_x=2, lru_b_x=1, lru_lam=1, lru_w_out=1, sb_w_qkv=2, sb_w_o=1, ffn_w_in=2,
                  ffn_conv_w=2, ffn_w_out=1)
MXU_WEIGHTS = ("pool_w", "s5_w_out", "lru_w_in", "lru_w_a", "lru_w_x", "lru_w_out", "sb_w_qkv", "sb_w_o",
               "ffn_w_in", "ffn_w_out")
SHARDED = [n for n in WEIGHTS if n in SHARD_AXIS]
REPLICATED = [n for n in WEIGHTS if n not in SHARD_AXIS]
PACK_WIDTH = 1024


def _as_rows(a):
    return a.reshape(-1, a.shape[-1])


def _pack(arrays):
    rows = []
    for a in arrays:
        flat = a.reshape(-1)
        pad = (-flat.shape[0]) % PACK_WIDTH
        rows.append(jnp.pad(flat, (0, pad)).reshape(-1, PACK_WIDTH))
    out = jnp.concatenate(rows, axis=0)
    return jnp.pad(out, ((0, (-out.shape[0]) % 8), (0, 0)))


def _unpack(packed, like):
    out, r = [], 0
    for a in like:
        size = math.prod(a.shape)
        nrow = -(-size // PACK_WIDTH)
        out.append(packed[r:r + nrow].reshape(-1)[:size].reshape(a.shape))
        r += nrow
    return out


def kernel(*args):
    names = ["x"] + WEIGHTS + ["loss_target"] + ["m_" + n for n in WEIGHTS] + ["v_" + n for n in WEIGHTS]
    assert len(args) == len(names)
    given = dict(zip(names, args))
    x, target = given["x"], given["loss_target"]
    bsz, seq, d = x.shape
    m_tok = bsz * seq

    shards = [_as_rows(given[n].astype(BF16) if n in MXU_WEIGHTS else given[n]) for n in SHARDED]
    gathered = gather_weights(shards)
    p = {n: given[n] for n in REPLICATED}
    for n, g4 in zip(SHARDED, gathered):
        blocks = g4.reshape((N_CHIPS,) + given[n].shape)
        p[n] = jnp.concatenate([blocks[k] for k in range(N_CHIPS)], axis=SHARD_AXIS[n])

    mix = {k: (v[0] if v.ndim > 2 or k == "s5_log_dt" else v) for k, v in p.items()
           if not k.startswith(("norm_", "ffn_"))}
    mixers = ((pool_layer_fwd, pool_layer_bwd), (s5_layer_fwd, s5_layer_bwd), (lru_layer_fwd, lru_layer_bwd),
              (sb_layer_fwd, sb_layer_bwd))
    ffn_p = lambda l: (p["norm_ffn_g"][l:l + 1], p["ffn_w_in"][l], p["ffn_conv_w"][l], p["ffn_conv_b"][l:l + 1],
                       p["ffn_w_out"][l])

    def mixer_args(l):
        g = p["norm_mix_g"][l:l + 1]
        if l == 0:
            return (g, mix["pool_w"], mix["pool_b"], mix["pool_scale"], bsz)
        return (g, mix, bsz)

    h = x.reshape(m_tok, d)
    saved = []
    for l in range(4):
        h, s_mix = mixers[l][0](h, *mixer_args(l))
        gl, w_in, cw, cb, w_out = ffn_p(l)
        h, s_ffn = ffn_fwd(h, gl, w_in, cw, cb, w_out, bsz)
        saved.append((s_mix, s_ffn))
    dh, loss_part = loss_head(h, target.reshape(m_tok, d))
    loss = lax.psum(jnp.sum(loss_part), ("x", "y", "c"))

    grads = {}
    ffn_g = [None] * 4
    mix_g = [None] * 4
    for l in range(3, -1, -1):
        gl, w_in, cw, cb, w_out = ffn_p(l)
        dh, ffn_g[l] = ffn_bwd(saved[l][1], dh, gl, w_in, cw, cb, w_out, bsz)
        dh, mix_g[l] = mixers[l][1](saved[l][0], dh, *mixer_args(l))
    for k in ("norm_ffn_g", "ffn_w_in", "ffn_conv_w", "ffn_conv_b", "ffn_w_out"):
        grads[k] = jnp.stack([ffn_g[l][k] for l in range(4)]).reshape(p[k].shape)
    grads["norm_mix_g"] = jnp.concatenate([mix_g[l]["norm_mix_g"] for l in range(4)], axis=0)
    for l in range(4):
        for k, v in mix_g[l].items():
            if k != "norm_mix_g":
                grads[k] = v.reshape(p[k].shape)
    grad_x = dh.reshape(bsz, seq, d)

    parts = {}
    for n in SHARDED:
        blocks = jnp.stack(jnp.split(grads[n], N_CHIPS, axis=SHARD_AXIS[n]))
        parts[n] = blocks.reshape(N_CHIPS, -1, blocks.shape[-1])
    big = [n for n in SHARDED if _splits(parts[n].shape)]
    small = [n for n in SHARDED if n not in big]
    kept, got = grad_halves_exchange([parts[n] for n in big])
    halves = [pair_sum(a, b, BF16, "pair_sum_" + n) for n, a, b in zip(big, kept, got)]
    landed, mine, theirs, packed_all = grad_sync(halves, [parts[n] for n in small],
                                                 _pack([grads[n] for n in REPLICATED]))
    summed = {n: [landed[i]] for i, n in enumerate(big)}
    summed.update({n: [mine[i], theirs[i]] for i, n in enumerate(small)})

    out = {}
    for n in SHARDED:
        res = adamw(_as_rows(given[n]), _as_rows(given["m_" + n]), _as_rows(given["v_" + n]), summed[n], "adamw_" + n)
        out[n] = [r.reshape(given[n].shape) for r in res]
    res = adamw(_pack([given[n] for n in REPLICATED]), _pack([given["m_" + n] for n in REPLICATED]),
                _pack([given["v_" + n] for n in REPLICATED]), [packed_all], "adamw_replicated")
    like = [given[n] for n in REPLICATED]
    for n, *vals in zip(REPLICATED, *[_unpack(r, like) for r in res]):
        out[n] = list(vals)
    return (loss, grad_x, *[out[n][0] for n in WEIGHTS], *[out[n][1] for n in WEIGHTS],
            *[out[n][2] for n in WEIGHTS], *[out[n][3] for n in WEIGHTS])
```

```python
import functools
import math

import jax
import jax.numpy as jnp
from jax import lax
from jax.experimental import pallas as pl
from jax.experimental.pallas import tpu as pltpu

F32 = jnp.float32
BF16 = jnp.bfloat16

EPS = 1e-6
N_CHIPS = 4
N_DEV = 8
POOL_WINDOWS = (2, 4, 8, 16)
POOL_GROUP = 256
S5_GROUP = 16
S5_STATE = 64
S5_CHUNKS = 8
S5_LANES = 512
S5_T = 256
LRU_BLOCK = 256
LRU_CONV = 4
LRU_C = 8.0
SB_HEADS = 16
SB_DIM = 64
SB_BLOCK = 512
SB_CHUNK = 128
FFN_CONV = 3
ADAM_LR, ADAM_B1, ADAM_B2, ADAM_EPS, ADAM_WD, ADAM_STEP = 0.001, 0.9, 0.999, 1e-08, 0.01, 10
VMEM_LIMIT_BYTES = 56 * 1024 * 1024
MATMUL_VMEM_BUDGET = 30 * 1024 * 1024
MATMUL_WHOLE_K = 2816

NN = (((1,), (0,)), ((), ()))
NT = (((1,), (1,)), ((), ()))
TN = (((0,), (0,)), ((), ()))


def _cp(*sem):
    return pltpu.CompilerParams(dimension_semantics=sem, vmem_limit_bytes=VMEM_LIMIT_BYTES)


def _pick(n, prefs):
    for p in prefs:
        if n % p == 0:
            return p
    return n


def _dot(a, b, dims=NN):
    return lax.dot_general(a.astype(BF16), b.astype(BF16), dims, preferred_element_type=F32)


def _split(x):
    hi = x.astype(BF16)
    return hi, (x - hi.astype(F32)).astype(BF16)


def _dot3(a, b, dims=NN):
    ah, al = _split(a)
    bh, bl = _split(b)
    d = lambda p, q: lax.dot_general(p, q, dims, preferred_element_type=F32)
    return d(ah, bh) + (d(ah, bl) + d(al, bh))


def _dot_exact_rhs(a, b01):
    ah, al = _split(a)
    d = lambda p: lax.dot_general(p, b01, NN, preferred_element_type=F32)
    return d(ah) + d(al)


def _sig(x):
    return 1.0 / (1.0 + jnp.exp(-x))


def _softplus(x):
    return jnp.maximum(x, 0.0) + jnp.log(1.0 + jnp.exp(-jnp.abs(x)))


_GELU_C = math.sqrt(2.0 / math.pi)


def _gelu(x):
    return 0.5 * x * (1.0 + jnp.tanh(_GELU_C * (x + 0.044715 * x * x * x)))


def _gelu_grad(x):
    th = jnp.tanh(_GELU_C * (x + 0.044715 * x * x * x))
    return 0.5 * (1.0 + th) + 0.5 * x * (1.0 - th * th) * _GELU_C * (1.0 + 3.0 * 0.044715 * x * x)


def _rows(shape):
    return lax.broadcasted_iota(jnp.int32, shape, 0)


def _shift_down(x, k):
    if k >= x.shape[0]:
        return jnp.zeros_like(x)
    return jnp.where(_rows(x.shape) >= k, pltpu.roll(x, k, 0), 0.0)


def _shift_up(x, k):
    t = x.shape[0]
    if k >= t:
        return jnp.zeros_like(x)
    return jnp.where(_rows(x.shape) < t - k, pltpu.roll(x, t - k, 0), 0.0)


def matmul(a, b, *, ta=False, tb=False, bias=None, res=None, out_dtype=F32, name):
    m, k = (a.shape[1], a.shape[0]) if ta else a.shape
    n = b.shape[0] if tb else b.shape[1]
    assert (b.shape[1] if tb else b.shape[0]) == k
    has_bias, has_res = bias is not None, res is not None
    tk = k if k <= MATMUL_WHOLE_K else _pick(k, (1408, 1024, 512, 256, 128))
    nk = k // tk
    sa, sb, so = a.dtype.itemsize, b.dtype.itemsize, jnp.dtype(out_dtype).itemsize
    tm = tn = None
    for cm in (2048, 1024, 512, 1408, 256, 128):
        for cn in (1024, 512, 1408, 256, 128):
            if m % cm or n % cn:
                continue
            need = 2 * cm * tk * sa + 2 * tk * cn * sb + cm * cn * (2 * so + 4 + (4 if nk > 1 else 0)
                                                                  + (8 if has_res else 0))
            if need <= MATMUL_VMEM_BUDGET and (tm is None or cm * cn > tm * tn):
                tm, tn = cm, cn
    assert tm is not None, (m, n, k)
    dims = (((0 if ta else 1,), (1 if tb else 0,)), ((), ()))

    def body(*refs):
        a_ref, b_ref = refs[:2]
        rest = list(refs[2:])
        bias_ref = rest.pop(0) if has_bias else None
        res_ref = rest.pop(0) if has_res else None
        o_ref = rest[0]

        def finish(r):
            if has_bias:
                r = r + bias_ref[...]
            if has_res:
                r = r + res_ref[...]
            o_ref[...] = r.astype(o_ref.dtype)

        part = lax.dot_general(a_ref[...].astype(BF16), b_ref[...].astype(BF16), dims, preferred_element_type=F32)
        if nk == 1:
            finish(part)
            return
        acc_ref = rest[1]
        kk = pl.program_id(2)

        @pl.when(kk == 0)
        def _():
            acc_ref[...] = part

        @pl.when(kk > 0)
        def _():
            acc_ref[...] += part

        @pl.when(kk == nk - 1)
        def _():
            finish(acc_ref[...])

    in_specs = [
        pl.BlockSpec((tk, tm), lambda i, j, kk: (kk, i)) if ta else pl.BlockSpec((tm, tk), lambda i, j, kk: (i, kk)),
        pl.BlockSpec((tn, tk), lambda i, j, kk: (j, kk)) if tb else pl.BlockSpec((tk, tn), lambda i, j, kk: (kk, j)),
    ]
    args = [a, b]
    if has_bias:
        in_specs.append(pl.BlockSpec((1, tn), lambda i, j, kk: (0, j)))
        args.append(bias)
    if has_res:
        in_specs.append(pl.BlockSpec((tm, tn), lambda i, j, kk: (i, j)))
        args.append(res)
    return pl.pallas_call(
        body, grid=(m // tm, n // tn, nk), in_specs=in_specs,
        out_specs=pl.BlockSpec((tm, tn), lambda i, j, kk: (i, j)),
        out_shape=jax.ShapeDtypeStruct((m, n), out_dtype),
        scratch_shapes=[pltpu.VMEM((tm, tn), F32)] if nk > 1 else [],
        compiler_params=_cp("parallel", "parallel", "arbitrary"), name=name)(*args)


def rms_fwd(x, g, out_dtype, name):
    m, d = x.shape
    tr = _pick(m, (512, 256, 128))

    def body(x_ref, g_ref, o_ref):
        xv = x_ref[...]
        r = lax.rsqrt(jnp.mean(xv * xv, axis=-1, keepdims=True) + EPS)
        o_ref[...] = (xv * r * g_ref[...]).astype(o_ref.dtype)

    return pl.pallas_call(
        body, grid=(m // tr,),
        in_specs=[pl.BlockSpec((tr, d), lambda i: (i, 0)), pl.BlockSpec((1, d), lambda i: (0, 0))],
        out_specs=pl.BlockSpec((tr, d), lambda i: (i, 0)),
        out_shape=jax.ShapeDtypeStruct((m, d), out_dtype),
        compiler_params=_cp("parallel"), name=name)(x, g)


def rms_bwd(x, g, dh, dres, name):
    m, d = x.shape
    tr = _pick(m, (512, 256, 128))

    def body(x_ref, g_ref, dh_ref, dres_ref, dx_ref, dg_ref):
        xv = x_ref[...]
        r = lax.rsqrt(jnp.mean(xv * xv, axis=-1, keepdims=True) + EPS)
        xh = xv * r
        dhv = dh_ref[...].astype(F32)
        dxh = dhv * g_ref[...]
        dx_ref[...] = dres_ref[...] + r * (dxh - xh * jnp.mean(dxh * xh, axis=-1, keepdims=True))

        @pl.when(pl.program_id(0) == 0)
        def _():
            dg_ref[...] = jnp.zeros_like(dg_ref)

        dg_ref[...] += jnp.sum(dhv * xh, axis=0, keepdims=True)

    row = pl.BlockSpec((tr, d), lambda i: (i, 0))
    vec = pl.BlockSpec((1, d), lambda i: (0, 0))
    return pl.pallas_call(
        body, grid=(m // tr,), in_specs=[row, vec, row, row], out_specs=[row, vec],
        out_shape=[jax.ShapeDtypeStruct((m, d), F32), jax.ShapeDtypeStruct((1, d), F32)],
        compiler_params=_cp("arbitrary"), name=name)(x, g, dh, dres)


def _conv_fwd(u, w_ref, b_ref, kw):
    y = b_ref[...] + w_ref[kw - 1:kw, :] * u
    for k in range(kw - 1):
        y = y + w_ref[k:k + 1, :] * _shift_down(u, kw - 1 - k)
    return y


def _conv_bwd_input(dy, w_ref, kw):
    du = w_ref[kw - 1:kw, :] * dy
    for k in range(kw - 1):
        du = du + w_ref[k:k + 1, :] * _shift_up(dy, kw - 1 - k)
    return du


def _conv_bwd_weight(dy, u, kw):
    out = [jnp.sum(dy * _shift_down(u, kw - 1 - k), axis=0, keepdims=True) for k in range(kw - 1)]
    out.append(jnp.sum(dy * u, axis=0, keepdims=True))
    return out


def ffn_act_fwd(u, cw, cb, name):
    bsz, s, f2 = u.shape
    f = f2 // 2
    tc = _pick(f, (256, 128))
    nj = f // tc

    def body(uv_ref, ug_ref, wv_ref, wg_ref, bv_ref, bg_ref, a_ref):
        hv = _conv_fwd(uv_ref[0], wv_ref, bv_ref, FFN_CONV)
        hg = _conv_fwd(ug_ref[0], wg_ref, bg_ref, FFN_CONV)
        a_ref[0] = (hg * _sig(hg) * hv).astype(a_ref.dtype)

    uv = pl.BlockSpec((1, s, tc), lambda b, j: (b, 0, j))
    ug = pl.BlockSpec((1, s, tc), lambda b, j: (b, 0, j + nj))
    wv = pl.BlockSpec((FFN_CONV, tc), lambda b, j: (0, j))
    wg = pl.BlockSpec((FFN_CONV, tc), lambda b, j: (0, j + nj))
    bv = pl.BlockSpec((1, tc), lambda b, j: (0, j))
    bg = pl.BlockSpec((1, tc), lambda b, j: (0, j + nj))
    return pl.pallas_call(
        body, grid=(bsz, nj), in_specs=[uv, ug, wv, wg, bv, bg], out_specs=uv,
        out_shape=jax.ShapeDtypeStruct((bsz, s, f), BF16),
        compiler_params=_cp("parallel", "parallel"), name=name)(u, u, cw, cw, cb, cb)


def ffn_act_bwd(u, cw, cb, da, name):
    bsz, s, f2 = u.shape
    f = f2 // 2
    tc = _pick(f, (256, 128))
    nj = f // tc

    def body(uv_ref, ug_ref, wv_ref, wg_ref, bv_ref, bg_ref, da_ref,
             duv_ref, dug_ref, dwv_ref, dwg_ref, dbv_ref, dbg_ref):
        uv, ug = uv_ref[0], ug_ref[0]
        hv = _conv_fwd(uv, wv_ref, bv_ref, FFN_CONV)
        hg = _conv_fwd(ug, wg_ref, bg_ref, FFN_CONV)
        sg = _sig(hg)
        dav = da_ref[0].astype(F32)
        dhv = dav * hg * sg
        dhg = dav * hv * (sg * (1.0 + hg * (1.0 - sg)))
        duv_ref[0] = _conv_bwd_input(dhv, wv_ref, FFN_CONV).astype(duv_ref.dtype)
        dug_ref[0] = _conv_bwd_input(dhg, wg_ref, FFN_CONV).astype(dug_ref.dtype)

        @pl.when(pl.program_id(1) == 0)
        def _():
            for r in (dwv_ref, dwg_ref, dbv_ref, dbg_ref):
                r[...] = jnp.zeros_like(r)

        for k, row in enumerate(_conv_bwd_weight(dhv, uv, FFN_CONV)):
            dwv_ref[k:k + 1, :] += row
        for k, row in enumerate(_conv_bwd_weight(dhg, ug, FFN_CONV)):
            dwg_ref[k:k + 1, :] += row
        dbv_ref[...] += jnp.sum(dhv, axis=0, keepdims=True)
        dbg_ref[...] += jnp.sum(dhg, axis=0, keepdims=True)

    uv = pl.BlockSpec((1, s, tc), lambda j, b: (b, 0, j))
    ug = pl.BlockSpec((1, s, tc), lambda j, b: (b, 0, j + nj))
    wv = pl.BlockSpec((FFN_CONV, tc), lambda j, b: (0, j))
    wg = pl.BlockSpec((FFN_CONV, tc), lambda j, b: (0, j + nj))
    bv = pl.BlockSpec((1, tc), lambda j, b: (0, j))
    bg = pl.BlockSpec((1, tc), lambda j, b: (0, j + nj))
    act = jax.ShapeDtypeStruct((bsz, s, f), BF16)
    return pl.pallas_call(
        body, grid=(nj, bsz), in_specs=[uv, ug, wv, wg, bv, bg, uv],
        out_specs=[uv, uv, wv, wv, bv, bv],
        out_shape=[act, act, jax.ShapeDtypeStruct((FFN_CONV, f), F32), jax.ShapeDtypeStruct((FFN_CONV, f), F32),
                   jax.ShapeDtypeStruct((1, f), F32), jax.ShapeDtypeStruct((1, f), F32)],
        compiler_params=_cp("parallel", "arbitrary"), name=name)(u, u, cw, cw, cb, cb, da)


def ffn_fwd(x, g, w_in, cw, cb, w_out, bsz):
    m, d = x.shape
    h = rms_fwd(x, g, BF16, "ffn_norm")
    u = matmul(h, w_in, name="ffn_in")
    a = ffn_act_fwd(u.reshape(bsz, m // bsz, -1), cw, cb, "ffn_act")
    a2 = a.reshape(m, -1)
    out = matmul(a2, w_out, res=x, name="ffn_out")
    return out, (x, h, u, a2)


def ffn_bwd(saved, dout, g, w_in, cw, cb, w_out, bsz):
    x, h, u, a2 = saved
    m, d = x.shape
    da = matmul(dout, w_out, tb=True, out_dtype=BF16, name="ffn_out_dx")
    dw_out = matmul(a2, dout, ta=True, name="ffn_out_dw")
    u3 = u.reshape(bsz, m // bsz, -1)
    duv, dug, dwv, dwg, dbv, dbg = ffn_act_bwd(u3, cw, cb, da.reshape(bsz, m // bsz, -1), "ffn_act_bwd")
    du = jnp.concatenate([duv, dug], axis=-1).reshape(m, -1)
    dh = matmul(du, w_in, tb=True, name="ffn_in_dx")
    dw_in = matmul(h, du, ta=True, name="ffn_in_dw")
    dx, dg = rms_bwd(x, g, dh, dout, "ffn_norm_bwd")
    grads = dict(norm_ffn_g=dg, ffn_w_in=dw_in, ffn_conv_w=jnp.concatenate([dwv, dwg], axis=-1),
                 ffn_conv_b=jnp.concatenate([dbv, dbg], axis=-1), ffn_w_out=dw_out)
    return dx, grads


def loss_head(y, target, name="loss_head"):
    m, d = y.shape
    tr = _pick(m, (512, 256, 128))

    def body(y_ref, t_ref, dy_ref, l_ref):
        e = y_ref[...] - t_ref[...]
        dy_ref[...] = e * (1.0 / d)

        @pl.when(pl.program_id(0) == 0)
        def _():
            l_ref[...] = jnp.zeros_like(l_ref)

        l_ref[...] += jnp.sum(e * e, axis=0, keepdims=True) * (0.5 / d)

    row = pl.BlockSpec((tr, d), lambda i: (i, 0))
    vec = pl.BlockSpec((1, d), lambda i: (0, 0))
    dy, part = pl.pallas_call(
        body, grid=(m // tr,), in_specs=[row, row], out_specs=[row, vec],
        out_shape=[jax.ShapeDtypeStruct((m, d), F32), jax.ShapeDtypeStruct((1, d), F32)],
        compiler_params=_cp("arbitrary"), name=name)(y, target)
    return dy, part


def _pool_windows(h, gi):
    sums, s, width = [], h, 1
    for _ in POOL_WINDOWS:
        s = s + _shift_down(s, width)
        width *= 2
        sums.append(s)
    pos = _rows(h.shape).astype(F32) + 1.0
    wsum, inv = sums[-1], 1.0 / jnp.minimum(pos, float(POOL_WINDOWS[-1]))
    for k in range(len(POOL_WINDOWS) - 2, -1, -1):
        wsum = jnp.where(gi == k, sums[k], wsum)
        inv = jnp.where(gi == k, 1.0 / jnp.minimum(pos, float(POOL_WINDOWS[k])), inv)
    return wsum * inv - h, inv


def _pool_windows_transpose(e, gi):
    sums, s, width = [], e, 1
    for _ in POOL_WINDOWS:
        s = s + _shift_up(s, width)
        width *= 2
        sums.append(s)
    out = sums[-1]
    for k in range(len(POOL_WINDOWS) - 2, -1, -1):
        out = jnp.where(gi == k, sums[k], out)
    return out


def pool_fwd(h, w, b, scale, x, name="pool_fwd"):
    bsz, s, d = h.shape
    ng = d // POOL_GROUP

    def body(h_ref, w_ref, b_ref, s_ref, x_ref, o_ref):
        dd, _ = _pool_windows(h_ref[0], pl.program_id(1))
        y = _dot(dd, w_ref[0]) + b_ref[...]
        o_ref[0] = x_ref[0] + s_ref[...] * y

    act = pl.BlockSpec((1, s, POOL_GROUP), lambda bb, gi: (bb, 0, gi))
    vec = pl.BlockSpec((1, POOL_GROUP), lambda bb, gi: (0, gi))
    return pl.pallas_call(
        body, grid=(bsz, ng),
        in_specs=[act, pl.BlockSpec((1, POOL_GROUP, POOL_GROUP), lambda bb, gi: (gi, 0, 0)), vec, vec, act],
        out_specs=act, out_shape=jax.ShapeDtypeStruct((bsz, s, d), F32),
        compiler_params=_cp("parallel", "parallel"), name=name)(h, w, b, scale, x)


def pool_bwd(h, w, b, scale, dy, name="pool_bwd"):
    bsz, s, d = h.shape
    ng = d // POOL_GROUP

    def body(h_ref, w_ref, b_ref, s_ref, dy_ref, dh_ref, dw_ref, db_ref, ds_ref):
        gi = pl.program_id(0)
        dd, inv = _pool_windows(h_ref[0], gi)
        ypre = _dot(dd, w_ref[0]) + b_ref[...]
        dyv = dy_ref[0]
        dyb = dyv * s_ref[...]

        @pl.when(pl.program_id(1) == 0)
        def _():
            for r in (dw_ref, db_ref, ds_ref):
                r[...] = jnp.zeros_like(r)

        ds_ref[...] += jnp.sum(dyv * ypre, axis=0, keepdims=True)
        db_ref[...] += jnp.sum(dyb, axis=0, keepdims=True)
        dw_ref[0] += _dot(dd, dyb, TN)
        ddd = _dot(dyb, w_ref[0], NT)
        dh_ref[0] = _pool_windows_transpose(ddd * inv, gi) - ddd

    act = pl.BlockSpec((1, s, POOL_GROUP), lambda gi, bb: (bb, 0, gi))
    vec = pl.BlockSpec((1, POOL_GROUP), lambda gi, bb: (0, gi))
    wsp = pl.BlockSpec((1, POOL_GROUP, POOL_GROUP), lambda gi, bb: (gi, 0, 0))
    return pl.pallas_call(
        body, grid=(ng, bsz), in_specs=[act, wsp, vec, vec, act], out_specs=[act, wsp, vec, vec],
        out_shape=[jax.ShapeDtypeStruct((bsz, s, d), F32), jax.ShapeDtypeStruct((ng, POOL_GROUP, POOL_GROUP), F32),
                   jax.ShapeDtypeStruct((1, d), F32), jax.ShapeDtypeStruct((1, d), F32)],
        compiler_params=_cp("parallel", "arbitrary"), name=name)(h, w, b, scale, dy)


def pool_layer_fwd(x, g, w, b, scale, bsz):
    m, d = x.shape
    h = rms_fwd(x, g, F32, "pool_norm")
    out = pool_fwd(h.reshape(bsz, m // bsz, d), w, b, scale, x.reshape(bsz, m // bsz, d))
    return out.reshape(m, d), (x, h)


def pool_layer_bwd(saved, dout, g, w, b, scale, bsz):
    x, h = saved
    m, d = x.shape
    dh, dw, db, ds = pool_bwd(h.reshape(bsz, m // bsz, d), w, b, scale, dout.reshape(bsz, m // bsz, d))
    dx, dg = rms_bwd(x, g, dh.reshape(m, d), dout, "pool_norm_bwd")
    return dx, dict(norm_mix_g=dg, pool_w=dw[None], pool_b=db, pool_scale=ds)


def _scan_fwd(a, b):
    k = 1
    while k < a.shape[0]:
        b = b + a * _shift_down(b, k)
        if 2 * k < a.shape[0]:
            a = a * _shift_down(a, k)
        k *= 2
    return b


def _scan_bwd(a, b):
    k = 1
    while k < a.shape[0]:
        b = b + a * _shift_up(b, k)
        if 2 * k < a.shape[0]:
            a = a * _shift_up(a, k)
        k *= 2
    return b


def _neg_expm1(x):
    series = -x * (1.0 + x * (0.5 + x * (1.0 / 6.0 + x * (1.0 / 24.0 + x * (1.0 / 120.0)))))
    return jnp.where(x > -0.03, series, 1.0 - jnp.exp(x))


def _lru_gates(rec, wa_ref, ba_ref, wx_ref, bx_ref, lam_ref):
    r = _sig(_dot(rec, wa_ref[0]) + ba_ref[...])
    i = _sig(_dot(rec, wx_ref[0]) + bx_ref[...])
    sp = _softplus(-lam_ref[...])
    log_a = -LRU_C * r * sp
    a = jnp.exp(log_a)
    mult = jnp.sqrt(_neg_expm1(2.0 * log_a))
    return r, i, sp, a, mult


def lru_fwd(zz, cw, cb, wa, ba, wx, bx, lam, name="lru_fwd"):
    bsz, s, r2 = zz.shape
    rw = r2 // 2
    nb = rw // LRU_BLOCK

    def body(g_ref, p_ref, cw_ref, cb_ref, wa_ref, ba_ref, wx_ref, bx_ref, lam_ref, h_ref, y_ref):
        rec = _conv_fwd(p_ref[0], cw_ref, cb_ref, LRU_CONV)
        _, i, _, a, mult = _lru_gates(rec, wa_ref, ba_ref, wx_ref, bx_ref, lam_ref)
        hst = _scan_fwd(a, mult * (i * rec))
        h_ref[0] = hst
        y_ref[0] = (_gelu(g_ref[0]) * hst).astype(y_ref.dtype)

    gsp = pl.BlockSpec((1, s, LRU_BLOCK), lambda bb, n: (bb, 0, n))
    psp = pl.BlockSpec((1, s, LRU_BLOCK), lambda bb, n: (bb, 0, n + nb))
    cws = pl.BlockSpec((LRU_CONV, LRU_BLOCK), lambda bb, n: (0, n))
    vec = pl.BlockSpec((1, LRU_BLOCK), lambda bb, n: (0, n))
    wsp = pl.BlockSpec((1, LRU_BLOCK, LRU_BLOCK), lambda bb, n: (n, 0, 0))
    return pl.pallas_call(
        body, grid=(bsz, nb), in_specs=[gsp, psp, cws, vec, wsp, vec, wsp, vec, vec], out_specs=[gsp, gsp],
        out_shape=[jax.ShapeDtypeStruct((bsz, s, rw), F32), jax.ShapeDtypeStruct((bsz, s, rw), BF16)],
        compiler_params=_cp("parallel", "parallel"), name=name)(zz, zz, cw, cb, wa, ba, wx, bx, lam)


def lru_bwd(zz, hst, dy, cw, cb, wa, ba, wx, bx, lam, name="lru_bwd"):
    bsz, s, r2 = zz.shape
    rw = r2 // 2
    nb = rw // LRU_BLOCK

    def body(g_ref, p_ref, h_ref, dy_ref, cw_ref, cb_ref, wa_ref, ba_ref, wx_ref, bx_ref, lam_ref,
             dg_ref, dp_ref, dcw_ref, dcb_ref, dwa_ref, dba_ref, dwx_ref, dbx_ref, dlam_ref):
        pre = p_ref[0]
        rec = _conv_fwd(pre, cw_ref, cb_ref, LRU_CONV)
        r, i, sp, a, mult = _lru_gates(rec, wa_ref, ba_ref, wx_ref, bx_ref, lam_ref)
        hst_v, gate, dyv = h_ref[0], g_ref[0], dy_ref[0]
        dg_ref[0] = (dyv * hst_v * _gelu_grad(gate)).astype(dg_ref.dtype)
        lmb = _scan_bwd(_shift_up(a, 1), dyv * _gelu(gate))
        da = lmb * _shift_down(hst_v, 1)
        dmult = lmb * (i * rec)
        dlog_a = da * a - dmult * (a * a) / mult
        dr = dlog_a * (-LRU_C) * sp
        dra = dr * r * (1.0 - r)
        dxa = lmb * mult * rec * i * (1.0 - i)
        drec = lmb * mult * i + _dot(dra, wa_ref[0], NT) + _dot(dxa, wx_ref[0], NT)
        dp_ref[0] = _conv_bwd_input(drec, cw_ref, LRU_CONV).astype(dp_ref.dtype)

        @pl.when(pl.program_id(1) == 0)
        def _():
            for ref in (dcw_ref, dcb_ref, dwa_ref, dba_ref, dwx_ref, dbx_ref, dlam_ref):
                ref[...] = jnp.zeros_like(ref)

        for k, row in enumerate(_conv_bwd_weight(drec, pre, LRU_CONV)):
            dcw_ref[k:k + 1, :] += row
        dcb_ref[...] += jnp.sum(drec, axis=0, keepdims=True)
        dwa_ref[0] += _dot(rec, dra, TN)
        dwx_ref[0] += _dot(rec, dxa, TN)
        dba_ref[...] += jnp.sum(dra, axis=0, keepdims=True)
        dbx_ref[...] += jnp.sum(dxa, axis=0, keepdims=True)
        dsp = jnp.sum(dlog_a * (-LRU_C) * r, axis=0, keepdims=True)
        dlam_ref[...] += dsp * (-_sig(-lam_ref[...]))

    gsp = pl.BlockSpec((1, s, LRU_BLOCK), lambda n, bb: (bb, 0, n))
    psp = pl.BlockSpec((1, s, LRU_BLOCK), lambda n, bb: (bb, 0, n + nb))
    cws = pl.BlockSpec((LRU_CONV, LRU_BLOCK), lambda n, bb: (0, n))
    vec = pl.BlockSpec((1, LRU_BLOCK), lambda n, bb: (0, n))
    wsp = pl.BlockSpec((1, LRU_BLOCK, LRU_BLOCK), lambda n, bb: (n, 0, 0))
    act = jax.ShapeDtypeStruct((bsz, s, rw), BF16)
    vsh = jax.ShapeDtypeStruct((1, rw), F32)
    wsh = jax.ShapeDtypeStruct((nb, LRU_BLOCK, LRU_BLOCK), F32)
    return pl.pallas_call(
        body, grid=(nb, bsz), in_specs=[gsp, psp, gsp, gsp, cws, vec, wsp, vec, wsp, vec, vec],
        out_specs=[gsp, gsp, cws, vec, wsp, vec, wsp, vec, vec],
        out_shape=[act, act, jax.ShapeDtypeStruct((LRU_CONV, rw), F32), vsh, wsh, vsh, wsh, vsh, vsh],
        compiler_params=_cp("parallel", "arbitrary"), name=name)(zz, zz, hst, dy, cw, cb, wa, ba, wx, bx, lam)


def lru_layer_fwd(x, g, p, bsz):
    m, d = x.shape
    h = rms_fwd(x, g, BF16, "lru_norm")
    zz = matmul(h, p["lru_w_in"], name="lru_in")
    hst, y = lru_fwd(zz.reshape(bsz, m // bsz, -1), p["lru_conv_w"], p["lru_conv_b"], p["lru_w_a"], p["lru_b_a"],
                     p["lru_w_x"], p["lru_b_x"], p["lru_lam"])
    y2 = y.reshape(m, -1)
    out = matmul(y2, p["lru_w_out"], res=x, name="lru_out")
    return out, (x, h, zz, hst, y2)


def lru_layer_bwd(saved, dout, g, p, bsz):
    x, h, zz, hst, y2 = saved
    m, d = x.shape
    dy = matmul(dout, p["lru_w_out"], tb=True, name="lru_out_dx")
    dw_out = matmul(y2, dout, ta=True, name="lru_out_dw")
    dgate, dpre, dcw, dcb, dwa, dba, dwx, dbx, dlam = lru_bwd(
        zz.reshape(bsz, m // bsz, -1), hst, dy.reshape(bsz, m // bsz, -1), p["lru_conv_w"], p["lru_conv_b"],
        p["lru_w_a"], p["lru_b_a"], p["lru_w_x"], p["lru_b_x"], p["lru_lam"])
    dzz = jnp.concatenate([dgate, dpre], axis=-1).reshape(m, -1)
    dh = matmul(dzz, p["lru_w_in"], tb=True, name="lru_in_dx")
    dw_in = matmul(h, dzz, ta=True, name="lru_in_dw")
    dx, dg = rms_bwd(x, g, dh, dout, "lru_norm_bwd")
    return dx, dict(norm_mix_g=dg, lru_w_in=dw_in, lru_conv_w=dcw[None], lru_conv_b=dcb, lru_w_a=dwa[None],
                    lru_b_a=dba, lru_w_x=dwx[None], lru_b_x=dbx, lru_lam=dlam, lru_w_out=dw_out)


def _s5_discretise(lam_re, lam_im, log_dt, b_re, b_im):
    lr = jnp.minimum(lam_re, -1e-4)
    dt = jnp.exp(log_dt)[:, None]
    mag = jnp.exp(lr * dt)
    ar, ai = mag * jnp.cos(lam_im * dt), mag * jnp.sin(lam_im * dt)
    den = lr * lr + lam_im * lam_im
    cr = ((ar - 1.0) * lr + ai * lam_im) / den
    ci = (ai * lr - (ar - 1.0) * lam_im) / den
    bbr = cr[..., None] * b_re - ci[..., None] * b_im
    bbi = cr[..., None] * b_im + ci[..., None] * b_re
    return ar, ai, bbr, bbi


def _s5_powers(lam_re, lam_im, log_dt, ns):
    lr = jnp.minimum(lam_re, -1e-4)
    dt = jnp.exp(log_dt)[:, None]
    n = jnp.asarray(ns, F32)[:, None, None]
    mag = jnp.exp(n * (lr * dt))
    ang = n * (lam_im * dt)
    to_chunks = lambda t: t.reshape(len(ns), S5_CHUNKS, S5_LANES).transpose(1, 0, 2)
    return jnp.concatenate([to_chunks(mag * jnp.cos(ang)), to_chunks(mag * jnp.sin(ang))], axis=-1)


def _s5_in_matrix(bbr, bbi):
    eye = jnp.eye(8, dtype=F32)
    blk = lambda t: jnp.einsum("qgph,gk->qghkp", t.reshape(S5_CHUNKS, 8, S5_STATE, S5_GROUP), eye).reshape(
        S5_CHUNKS, 128, S5_LANES)
    return jnp.concatenate([blk(bbr), blk(bbi)], axis=-1)


def _s5_in_matrix_diag(dmat):
    eye = jnp.eye(8, dtype=F32)[None, :, None, :, None]
    pick = lambda t: (t.reshape(S5_CHUNKS, 8, S5_GROUP, 8, S5_STATE) * eye).sum(3).transpose(0, 1, 3, 2).reshape(
        S5_CHUNKS * 8, S5_STATE, S5_GROUP)
    return pick(dmat[..., :S5_LANES]), pick(dmat[..., S5_LANES:])


def _s5_out_matrix(c_re, c_im):
    eye = jnp.eye(8, dtype=F32)
    blk = lambda t: jnp.einsum("qghp,gk->qgpkh", t.reshape(S5_CHUNKS, 8, S5_GROUP, S5_STATE), eye).reshape(
        S5_CHUNKS, S5_LANES, 128)
    return jnp.concatenate([blk(c_re), -blk(c_im)], axis=1)


def _s5_out_matrix_diag(dmat):
    eye = jnp.eye(8, dtype=F32)[None, :, None, :, None]
    pick = lambda t: (t.reshape(S5_CHUNKS, 8, S5_STATE, 8, S5_GROUP) * eye).sum(3).transpose(0, 1, 3, 2).reshape(
        S5_CHUNKS * 8, S5_GROUP, S5_STATE)
    return pick(dmat[:, :S5_LANES]), -pick(dmat[:, S5_LANES:])


def s5_fwd(h, bmat, cmat, atab, pw, dskip, name="s5_fwd"):
    bsz, s, d = h.shape
    t = min(S5_T, s)
    nt, nlev, ln = s // t, atab.shape[1], S5_LANES

    def body(h_ref, b_ref, c_ref, a_ref, pw_ref, d_ref, xs_ref, yp_ref, yg_ref, carry):
        @pl.when(pl.program_id(2) == 0)
        def _():
            carry[...] = jnp.zeros_like(carry)

        u = h_ref[0]
        bu = _dot3(u, b_ref[0])
        xr, xi = bu[:, :ln], bu[:, ln:]
        for k in range(nlev):
            ar, ai = a_ref[0, k:k + 1, :ln], a_ref[0, k:k + 1, ln:]
            sr, si = _shift_down(xr, 1 << k), _shift_down(xi, 1 << k)
            xr, xi = xr + ar * sr - ai * si, xi + ar * si + ai * sr
        cr, ci = carry[0:1, :ln], carry[0:1, ln:]
        pr, pi = pw_ref[0, :, :ln], pw_ref[0, :, ln:]
        xr, xi = xr + pr * cr - pi * ci, xi + pr * ci + pi * cr
        carry[0:1, :ln] = xr[t - 1:t, :]
        carry[0:1, ln:] = xi[t - 1:t, :]
        xs_ref[0, :, :ln] = xr
        xs_ref[0, :, ln:] = xi
        y = _dot3(xr, c_ref[0, :ln, :]) + _dot3(xi, c_ref[0, ln:, :]) + d_ref[...] * u
        yp_ref[0] = y
        yg_ref[0] = _gelu(y).astype(yg_ref.dtype)

    act = pl.BlockSpec((1, t, 128), lambda b, q, i: (b, i, q))
    par = lambda r, c: pl.BlockSpec((1, r, c), lambda b, q, i: (q, 0, 0))
    return pl.pallas_call(
        body, grid=(bsz, S5_CHUNKS, nt),
        in_specs=[act, par(128, 2 * ln), par(2 * ln, 128), par(nlev, 2 * ln), par(t, 2 * ln),
                  pl.BlockSpec((1, 128), lambda b, q, i: (0, q))],
        out_specs=[pl.BlockSpec((1, t, 2 * ln), lambda b, q, i: (b, i, q)), act, act],
        out_shape=[jax.ShapeDtypeStruct((bsz, s, S5_CHUNKS * 2 * ln), F32), jax.ShapeDtypeStruct((bsz, s, d), F32),
                   jax.ShapeDtypeStruct((bsz, s, d), BF16)],
        scratch_shapes=[pltpu.VMEM((8, 2 * ln), F32)],
        compiler_params=_cp("parallel", "parallel", "arbitrary"), name=name)(h, bmat, cmat, atab, pw, dskip)


def s5_bwd(h, ypre, xs, dyg, bmat_t, cmat_t, atab, pw_rev, dskip, name="s5_bwd"):
    bsz, s, d = h.shape
    t = min(S5_T, s)
    nt, nlev, ln = s // t, atab.shape[1], S5_LANES

    def body(h_ref, yp_ref, xs_ref, xp_ref, dy_ref, bt_ref, ct_ref, a_ref, pw_ref, d_ref,
             dh_ref, db_ref, dc_ref, da_ref, dd_ref, carry):
        b, i = pl.program_id(1), pl.program_id(2)

        @pl.when((b == 0) & (i == 0))
        def _():
            for r in (db_ref, dc_ref, da_ref, dd_ref):
                r[...] = jnp.zeros_like(r)

        @pl.when(i == 0)
        def _():
            carry[...] = jnp.zeros_like(carry)

        u = h_ref[0]
        dyp = dy_ref[0] * _gelu_grad(yp_ref[0])
        dd_ref[...] += jnp.sum(dyp * u, axis=0, keepdims=True)
        xr, xi = xs_ref[0, :, :ln], xs_ref[0, :, ln:]
        dc_ref[0, :ln, :] += _dot3(xr, dyp, TN)
        dc_ref[0, ln:, :] += _dot3(xi, dyp, TN)
        lr, li = _dot3(dyp, ct_ref[0, :, :ln]), _dot3(dyp, ct_ref[0, :, ln:])
        for k in range(nlev):
            ar, ai = a_ref[0, k:k + 1, :ln], a_ref[0, k:k + 1, ln:]
            sr, si = _shift_up(lr, 1 << k), _shift_up(li, 1 << k)
            lr, li = lr + ar * sr + ai * si, li + ar * si - ai * sr
        cr, ci = carry[0:1, :ln], carry[0:1, ln:]
        pr, pi = pw_ref[0, :, :ln], pw_ref[0, :, ln:]
        lr, li = lr + pr * cr + pi * ci, li + pr * ci - pi * cr
        carry[0:1, :ln] = lr[0:1, :]
        carry[0:1, ln:] = li[0:1, :]
        dh_ref[0] = _dot3(lr, bt_ref[0, :ln, :]) + _dot3(li, bt_ref[0, ln:, :]) + dyp * d_ref[...]
        db_ref[0, :, :ln] += _dot3(u, lr, TN)
        db_ref[0, :, ln:] += _dot3(u, li, TN)
        first = _rows(xr.shape) == 0
        keep = jnp.where(i == nt - 1, 0.0, 1.0)
        xpr = jnp.where(first, xp_ref[0, 7:8, :ln] * keep, _shift_down(xr, 1))
        xpi = jnp.where(first, xp_ref[0, 7:8, ln:] * keep, _shift_down(xi, 1))
        da_ref[0, 0:1, :ln] += jnp.sum(lr * xpr + li * xpi, axis=0, keepdims=True)
        da_ref[0, 0:1, ln:] += jnp.sum(li * xpr - lr * xpi, axis=0, keepdims=True)

    rev = lambda i: nt - 1 - i
    act = pl.BlockSpec((1, t, 128), lambda q, b, i: (b, rev(i), q))
    xsp = pl.BlockSpec((1, t, 2 * ln), lambda q, b, i: (b, rev(i), q))
    xpp = pl.BlockSpec((1, 8, 2 * ln), lambda q, b, i: (b, jnp.maximum(rev(i) * (t // 8) - 1, 0), q))
    par = lambda r, c: pl.BlockSpec((1, r, c), lambda q, b, i: (q, 0, 0))
    dsp = pl.BlockSpec((1, 128), lambda q, b, i: (0, q))
    return pl.pallas_call(
        body, grid=(S5_CHUNKS, bsz, nt),
        in_specs=[act, act, xsp, xpp, act, par(2 * ln, 128), par(128, 2 * ln), par(nlev, 2 * ln), par(t, 2 * ln), dsp],
        out_specs=[act, par(128, 2 * ln), par(2 * ln, 128), par(8, 2 * ln), dsp],
        out_shape=[jax.ShapeDtypeStruct((bsz, s, d), F32), jax.ShapeDtypeStruct((S5_CHUNKS, 128, 2 * ln), F32),
                   jax.ShapeDtypeStruct((S5_CHUNKS, 2 * ln, 128), F32), jax.ShapeDtypeStruct((S5_CHUNKS, 8, 2 * ln), F32),
                   jax.ShapeDtypeStruct((1, d), F32)],
        scratch_shapes=[pltpu.VMEM((8, 2 * ln), F32)],
        compiler_params=_cp("parallel", "arbitrary", "arbitrary"), name=name)(
            h, ypre, xs, xs, dyg, bmat_t, cmat_t, atab, pw_rev, dskip)


def glu_fwd(z, x, name="s5_glu"):
    m, d = x.shape
    tr = _pick(m, (512, 256, 128))

    def body(z_ref, x_ref, o_ref):
        o_ref[...] = x_ref[...] + z_ref[:, :d] * _sig(z_ref[:, d:])

    return pl.pallas_call(
        body, grid=(m // tr,),
        in_specs=[pl.BlockSpec((tr, 2 * d), lambda i: (i, 0)), pl.BlockSpec((tr, d), lambda i: (i, 0))],
        out_specs=pl.BlockSpec((tr, d), lambda i: (i, 0)), out_shape=jax.ShapeDtypeStruct((m, d), F32),
        compiler_params=_cp("parallel"), name=name)(z, x)


def glu_bwd(z, dout, name="s5_glu_bwd"):
    m, d = dout.shape
    tr = _pick(m, (512, 256, 128))

    def body(z_ref, do_ref, dz_ref, db_ref):
        sg = _sig(z_ref[:, d:])
        dv = do_ref[...] * sg
        dgt = do_ref[...] * z_ref[:, :d] * sg * (1.0 - sg)
        dz_ref[:, :d] = dv.astype(dz_ref.dtype)
        dz_ref[:, d:] = dgt.astype(dz_ref.dtype)

        @pl.when(pl.program_id(0) == 0)
        def _():
            db_ref[...] = jnp.zeros_like(db_ref)

        db_ref[:, :d] += jnp.sum(dv, axis=0, keepdims=True)
        db_ref[:, d:] += jnp.sum(dgt, axis=0, keepdims=True)

    wide = pl.BlockSpec((tr, 2 * d), lambda i: (i, 0))
    return pl.pallas_call(
        body, grid=(m // tr,), in_specs=[wide, pl.BlockSpec((tr, d), lambda i: (i, 0))],
        out_specs=[wide, pl.BlockSpec((1, 2 * d), lambda i: (0, 0))],
        out_shape=[jax.ShapeDtypeStruct((m, 2 * d), BF16), jax.ShapeDtypeStruct((1, 2 * d), F32)],
        compiler_params=_cp("arbitrary"), name=name)(z, dout)


def _s5_tables(p, t):
    nlev = max(1, (t - 1).bit_length())
    lam = (p["s5_lam_re"], p["s5_lam_im"], p["s5_log_dt"])
    atab = _s5_powers(*lam, [1 << k for k in range(nlev)])
    if nlev < 8:
        atab = jnp.pad(atab, ((0, 0), (0, 8 - nlev), (0, 0)))
    pw = _s5_powers(*lam, list(range(1, t + 1)))
    return nlev, atab, pw


def s5_layer_fwd(x, g, p, bsz):
    m, d = x.shape
    s = m // bsz
    t = min(S5_T, s)
    h = rms_fwd(x, g, F32, "s5_norm")
    _, _, bbr, bbi = _s5_discretise(p["s5_lam_re"], p["s5_lam_im"], p["s5_log_dt"], p["s5_b_re"], p["s5_b_im"])
    nlev, atab, pw = _s5_tables(p, t)
    bmat, cmat = _s5_in_matrix(bbr, bbi), _s5_out_matrix(p["s5_c_re"], p["s5_c_im"])
    xs, ypre, yg = s5_fwd(h.reshape(bsz, s, d), bmat, cmat, atab[:, :max(nlev, 8)], pw, p["s5_d"])
    z = matmul(yg.reshape(m, d), p["s5_w_out"], bias=p["s5_b_out"], name="s5_out")
    out = glu_fwd(z, x)
    return out, (x, h, xs, ypre, yg, z, bmat, cmat, atab, pw)


def s5_layer_bwd(saved, dout, g, p, bsz):
    x, h, xs, ypre, yg, z, bmat, cmat, atab, pw = saved
    m, d = x.shape
    s = m // bsz
    dz, db_out = glu_bwd(z, dout)
    dyg = matmul(dz, p["s5_w_out"], tb=True, name="s5_out_dx")
    dw_out = matmul(yg.reshape(m, d), dz, ta=True, name="s5_out_dw")
    dh, dbm, dcm, dlam, dd = s5_bwd(h.reshape(bsz, s, d), ypre, xs, dyg.reshape(bsz, s, d),
                                    bmat.transpose(0, 2, 1), cmat.transpose(0, 2, 1), atab, pw[:, ::-1], p["s5_d"])
    dx, dg = rms_bwd(x, g, dh.reshape(m, d), dout, "s5_norm_bwd")
    dbbr, dbbi = _s5_in_matrix_diag(dbm)
    dc_re, dc_im = _s5_out_matrix_diag(dcm)
    dar = dlam[:, 0, :S5_LANES].reshape(S5_CHUNKS * 8, S5_STATE)
    dai = dlam[:, 0, S5_LANES:].reshape(S5_CHUNKS * 8, S5_STATE)
    _, vjp = jax.vjp(_s5_discretise, p["s5_lam_re"], p["s5_lam_im"], p["s5_log_dt"], p["s5_b_re"], p["s5_b_im"])
    dl_re, dl_im, dldt, db_re, db_im = vjp((dar, dai, dbbr, dbbi))
    return dx, dict(norm_mix_g=dg, s5_lam_re=dl_re[None], s5_lam_im=dl_im[None], s5_log_dt=dldt[None],
                    s5_b_re=db_re[None], s5_b_im=db_im[None], s5_c_re=dc_re[None], s5_c_im=dc_im[None],
                    s5_d=dd, s5_w_out=dw_out, s5_b_out=db_out)


def _log_sigmoid(z):
    return jnp.minimum(z, 0.0) - jnp.log(1.0 + jnp.exp(-jnp.abs(z)))


def _head_norm(t, g_ref):
    r = lax.rsqrt(jnp.mean(t * t, axis=-1, keepdims=True) + EPS)
    th = t * r
    return th * g_ref[...], th, r


def _tri(shape, fn):
    row = lax.broadcasted_iota(jnp.int32, shape, 0)
    col = lax.broadcasted_iota(jnp.int32, shape, 1)
    return fn(row, col)


_SB_SCALE = 1.0 / math.sqrt(SB_DIM)


def _suffix_sums(t, later):
    n = t.shape[1] // SB_CHUNK
    outs, carry = [None] * n, jnp.zeros((t.shape[0], 1), F32)
    for ci in range(n - 1, -1, -1):
        ch = t[:, ci * SB_CHUNK:(ci + 1) * SB_CHUNK]
        outs[ci] = _dot_exact_rhs(ch, later) + carry
        carry = carry + jnp.sum(ch, axis=1, keepdims=True)
    return (outs[0] if n == 1 else jnp.concatenate(outs, axis=1)), carry


def _prefix_sums(t, tri):
    n = t.shape[1] // SB_CHUNK
    outs, carry = [None] * n, jnp.zeros((t.shape[0], 1), F32)
    for ci in range(n):
        ch = t[:, ci * SB_CHUNK:(ci + 1) * SB_CHUNK]
        outs[ci] = _dot_exact_rhs(ch, tri) + carry
        carry = carry + jnp.sum(ch, axis=1, keepdims=True)
    return (outs[0] if n == 1 else jnp.concatenate(outs, axis=1)), carry


def sb_fwd(q, k, v, qg, kg, name="sb_fwd"):
    bsz, nh, s, dh = q.shape
    tb = min(SB_BLOCK, s)
    nq = s // tb

    def body(q_ref, k_ref, v_ref, qg_ref, kg_ref, o_ref, rt_ref):
        qi = pl.program_id(2)
        qn, _, _ = _head_norm(q_ref[0, 0], qg_ref)
        later = _tri((SB_CHUNK, SB_CHUNK), lambda r, c: r > c).astype(BF16)
        causal = _tri((tb, tb), lambda r, c: c < r)

        def block(kb, run, acc, diag):
            ks = pl.ds(pl.multiple_of(kb * tb, tb), tb)
            kn, _, _ = _head_norm(k_ref[0, 0, ks, :], kg_ref)
            z = _dot(qn, kn, NT) * _SB_SCALE
            ls = _log_sigmoid(z)
            lm = ls - z
            if diag:
                lm = jnp.where(causal, lm, 0.0)
            rest, total = _suffix_sums(lm, later)
            att = jnp.exp(ls + run + rest)
            if diag:
                att = jnp.where(causal, att, 0.0)
            return run + total, acc + _dot(att, v_ref[0, 0, ks, :])

        run, acc = block(qi, jnp.zeros((tb, 1), F32), jnp.zeros((tb, dh), F32), True)
        run, acc = lax.fori_loop(0, qi, lambda j, c: block(qi - 1 - j, c[0], c[1], False), (run, acc))
        o_ref[0, 0] = acc
        rt_ref[0, 0] = run

    qsp = pl.BlockSpec((1, 1, tb, dh), lambda b, h, i: (b, h, i, 0))
    rsp = pl.BlockSpec((1, 1, tb, 1), lambda b, h, i: (b, h, i, 0))
    ksp = pl.BlockSpec((1, 1, s, dh), lambda b, h, i: (b, h, 0, 0))
    gsp = pl.BlockSpec((1, dh), lambda b, h, i: (0, 0))
    return pl.pallas_call(
        body, grid=(bsz, nh, nq), in_specs=[qsp, ksp, ksp, gsp, gsp], out_specs=[qsp, rsp],
        out_shape=[jax.ShapeDtypeStruct((bsz, nh, s, dh), F32), jax.ShapeDtypeStruct((bsz, nh, s, 1), F32)],
        compiler_params=_cp("parallel", "parallel", "arbitrary"), name=name)(q, k, v, qg, kg)


def sb_bwd(q, k, v, rtot, do, qg, kg, name="sb_bwd"):
    bsz, nh, s, dh = q.shape
    tb = min(SB_BLOCK, s)
    nq = s // tb

    def body(q_ref, k_ref, v_ref, rt_ref, do_ref, qg_ref, kg_ref, dq_ref, dk_ref, dv_ref, dqg_ref, dkg_ref,
             qn_s, kn_s, dqn_s, dkn_s, dv_s):
        qn, qh, rq = _head_norm(q_ref[0, 0], qg_ref)
        kn, kh, rk = _head_norm(k_ref[0, 0], kg_ref)
        qn_s[...] = qn
        kn_s[...] = kn
        dkn_s[...] = jnp.zeros_like(dkn_s)
        dv_s[...] = jnp.zeros_like(dv_s)
        chunk = (SB_CHUNK, SB_CHUNK)
        upto = _tri(chunk, lambda r, c: r <= c).astype(BF16)
        earlier = _tri(chunk, lambda r, c: r < c).astype(BF16)
        causal = _tri((tb, tb), lambda r, c: c < r)

        def q_block(qi, _):
            qs = pl.ds(pl.multiple_of(qi * tb, tb), tb)
            qnb, dob, rtb = qn_s[qs, :], do_ref[0, 0, qs, :], rt_ref[0, 0, qs, :]

            def block(kb, left, seen, dqn, diag):
                ks = pl.ds(pl.multiple_of(kb * tb, tb), tb)
                knb, vb = kn_s[ks, :], v_ref[0, 0, ks, :]
                z = _dot(qnb, knb, NT) * _SB_SCALE
                ls = _log_sigmoid(z)
                lm = ls - z
                if diag:
                    lm = jnp.where(causal, lm, 0.0)
                through, lm_total = _prefix_sums(lm, upto)
                att = jnp.exp(ls + (rtb - left - through))
                if diag:
                    att = jnp.where(causal, att, 0.0)
                gg = att * _dot(dob, vb, NT)
                before, gg_total = _prefix_sums(gg, earlier)
                sg = jnp.exp(ls)
                dz = gg * (1.0 - sg) - sg * (seen + before)
                if diag:
                    dz = jnp.where(causal, dz, 0.0)
                dz = dz * _SB_SCALE
                dkn_s[ks, :] += _dot(dz, qnb, TN)
                dv_s[ks, :] += _dot(att, dob, TN)
                return left + lm_total, seen + gg_total, dqn + _dot(dz, knb)

            zero = jnp.zeros((tb, 1), F32)
            c = lax.fori_loop(0, qi, lambda kb, c: block(kb, c[0], c[1], c[2], False),
                              (zero, zero, jnp.zeros((tb, dh), F32)))
            c = block(qi, c[0], c[1], c[2], True)
            dqn_s[qs, :] = c[2]
            return 0

        lax.fori_loop(0, nq, q_block, 0)

        @pl.when((pl.program_id(0) == 0) & (pl.program_id(1) == 0))
        def _():
            dqg_ref[...] = jnp.zeros_like(dqg_ref)
            dkg_ref[...] = jnp.zeros_like(dkg_ref)

        def norm_bwd(dn, th, r, g_ref, dt_ref, dg_ref):
            dg_ref[...] += jnp.sum(dn * th, axis=0, keepdims=True)
            dth = dn * g_ref[...]
            dt_ref[0, 0] = r * (dth - th * jnp.mean(dth * th, axis=-1, keepdims=True))

        norm_bwd(dqn_s[...], qh, rq, qg_ref, dq_ref, dqg_ref)
        norm_bwd(dkn_s[...], kh, rk, kg_ref, dk_ref, dkg_ref)
        dv_ref[0, 0] = dv_s[...]

    hsp = pl.BlockSpec((1, 1, s, dh), lambda b, h: (b, h, 0, 0))
    gsp = pl.BlockSpec((1, dh), lambda b, h: (0, 0))
    act = jax.ShapeDtypeStruct((bsz, nh, s, dh), F32)
    gsh = jax.ShapeDtypeStruct((1, dh), F32)
    rsp = pl.BlockSpec((1, 1, s, 1), lambda b, h: (b, h, 0, 0))
    return pl.pallas_call(
        body, grid=(bsz, nh), in_specs=[hsp, hsp, hsp, rsp, hsp, gsp, gsp], out_specs=[hsp, hsp, hsp, gsp, gsp],
        out_shape=[act, act, act, gsh, gsh], scratch_shapes=[pltpu.VMEM((s, dh), F32)] * 5,
        compiler_params=_cp("arbitrary", "arbitrary"), name=name)(q, k, v, rtot, do, qg, kg)


def _to_heads(t, bsz):
    m, w = t.shape
    n = w // (SB_HEADS * SB_DIM)
    t = t.reshape(bsz, m // bsz, n, SB_HEADS, SB_DIM).transpose(2, 0, 3, 1, 4)
    return [t[i] for i in range(n)]


def _from_heads(ts):
    t = jnp.stack(ts, axis=0)
    n, bsz, nh, s, dh = t.shape
    return t.transpose(1, 3, 0, 2, 4).reshape(bsz * s, n * nh * dh)


def sb_layer_fwd(x, g, p, bsz):
    m, d = x.shape
    h = rms_fwd(x, g, BF16, "sb_norm")
    qkv = matmul(h, p["sb_w_qkv"], name="sb_qkv")
    q, k, v = _to_heads(qkv, bsz)
    o, rtot = sb_fwd(q, k, v, p["sb_q_g"], p["sb_k_g"])
    o2 = _from_heads([o])
    out = matmul(o2, p["sb_w_o"], res=x, name="sb_out")
    return out, (x, h, q, k, v, rtot, o2)


def sb_layer_bwd(saved, dout, g, p, bsz):
    x, h, q, k, v, rtot, o2 = saved
    do2 = matmul(dout, p["sb_w_o"], tb=True, name="sb_out_dx")
    dw_o = matmul(o2, dout, ta=True, name="sb_out_dw")
    dq, dk, dv, dqg, dkg = sb_bwd(q, k, v, rtot, _to_heads(do2, bsz)[0], p["sb_q_g"], p["sb_k_g"])
    dqkv = _from_heads([dq, dk, dv])
    dh = matmul(dqkv, p["sb_w_qkv"], tb=True, name="sb_qkv_dx")
    dw_qkv = matmul(h, dqkv, ta=True, name="sb_qkv_dw")
    dx, dg = rms_bwd(x, g, dh, dout, "sb_norm_bwd")
    return dx, dict(norm_mix_g=dg, sb_w_qkv=dw_qkv, sb_q_g=dqg, sb_k_g=dkg, sb_w_o=dw_o)


_CHIP_FLIPS = ((1, 0), (0, 1), (1, 1))
_MESH = pl.DeviceIdType.MESH
_ANY = pl.BlockSpec(memory_space=pl.ANY)


def _flip(v, f):
    return 1 - v if f else v


def _splits(shape):
    return shape[-2] % 32 == 0


def _half(ref, c, rows):
    idx = (slice(None),) * (len(ref.shape) - 2) + (pl.ds(pl.multiple_of(c * (rows // 2), 16), rows // 2),)
    return ref.at[idx]


def gather_weights(shards, name="gather_weights"):
    n = len(shards)
    split = [_splits(s.shape) for s in shards]

    def body(*refs):
        ins, outs = refs[:n], refs[n:2 * n]
        send, recv, fsend, frecv = refs[2 * n:]
        x, y, c = lax.axis_index("x"), lax.axis_index("y"), lax.axis_index("c")
        me = 2 * x + y
        sibling = (x, y, 1 - c)

        def remote(i, j, block):
            px, py = _flip(x, _CHIP_FLIPS[j][0]), _flip(y, _CHIP_FLIPS[j][1])
            rows = shards[i].shape[0]
            src = _half(ins[i], c, rows) if split[i] else ins[i]
            dst = _half(outs[i].at[block], c, rows) if split[i] else outs[i].at[block]
            return pltpu.make_async_remote_copy(
                src_ref=src, dst_ref=dst, send_sem=send.at[3 * i + j], recv_sem=recv.at[3 * i + j],
                device_id=(px, py, c), device_id_type=_MESH)

        def forward(i, j, half):
            rows = _half(outs[i].at[2 * _flip(x, _CHIP_FLIPS[j][0]) + _flip(y, _CHIP_FLIPS[j][1])], half,
                         shards[i].shape[0])
            return pltpu.make_async_remote_copy(
                src_ref=rows, dst_ref=rows, send_sem=fsend.at[3 * i + j], recv_sem=frecv.at[3 * i + j],
                device_id=sibling, device_id_type=_MESH)

        sends = [remote(i, j, me) for i in range(n) for j in range(3)]
        for cp in sends:
            cp.start()
        fwd = []
        for i in range(n):
            for j, (fx, fy) in enumerate(_CHIP_FLIPS):
                remote(i, j, 2 * _flip(x, fx) + _flip(y, fy)).wait_recv()
                if split[i]:
                    fwd.append(forward(i, j, c))
                    fwd[-1].start()
        for i in range(n):
            if split[i]:
                for j in range(3):
                    forward(i, j, 1 - c).wait_recv()
        for cp in sends + fwd:
            cp.wait_send()

    return pl.pallas_call(
        body, in_specs=[_ANY] * n, out_specs=[_ANY] * n,
        out_shape=[jax.ShapeDtypeStruct((N_CHIPS,) + s.shape, s.dtype) for s in shards],
        scratch_shapes=[pltpu.SemaphoreType.DMA((3 * n,))] * 4, name=name)(*shards)


def grad_halves_exchange(parts, name="grad_halves_exchange"):
    n = len(parts)

    def body(*refs):
        ins, got = refs[:n], refs[n:2 * n]
        send, recv = refs[2 * n:]
        x, y, c = lax.axis_index("x"), lax.axis_index("y"), lax.axis_index("c")
        swap = [pltpu.make_async_remote_copy(
            src_ref=_half(ins[i], 1 - c, parts[i].shape[1]), dst_ref=got[i], send_sem=send.at[i], recv_sem=recv.at[i],
            device_id=(x, y, 1 - c), device_id_type=_MESH) for i in range(n)]
        for cp in swap:
            cp.start()
        for cp in swap:
            cp.wait()

    half = [jax.ShapeDtypeStruct((N_CHIPS, p.shape[1] // 2, p.shape[2]), p.dtype) for p in parts]
    return pl.pallas_call(
        body, in_specs=[_ANY] * n, out_specs=[_ANY] * n, out_shape=half,
        scratch_shapes=[pltpu.SemaphoreType.DMA((n,))] * 2, name=name)(*parts)


def pair_sum(full, got, core, out_dtype, name):
    k, r, c = full.shape
    rh = r // 2
    tr = _pick(rh, tuple(t for t in (512, 256, 128, 64, 32, 16) if t * c * 4 <= 1024 * 1024))
    nb = rh // tr

    def body(core_ref, a_ref, b_ref, o_ref):
        o_ref[...] = (a_ref[...] + b_ref[...]).astype(o_ref.dtype)

    blk = pl.BlockSpec((1, tr, c), lambda kk, i, core_ref: (kk, i, 0))
    mine = pl.BlockSpec((1, tr, c), lambda kk, i, core_ref: (kk, core_ref[0] * nb + i, 0))
    return pl.pallas_call(
        body, out_shape=jax.ShapeDtypeStruct((k, rh, c), out_dtype),
        grid_spec=pltpu.PrefetchScalarGridSpec(num_scalar_prefetch=1, grid=(k, nb), in_specs=[mine, blk],
                                               out_specs=blk),
        compiler_params=_cp("parallel", "parallel"), name=name)(core, full, got)


def grad_sync(halves, parts, packed, name="grad_sync"):
    nh, n = len(halves), len(parts)
    nt = nh + n

    def body(*refs):
        hin, ins, pk = refs[:nh], refs[nh:nt], refs[nt]
        outs = refs[nt + 1:]
        landed, mine, theirs, pk_all = outs[:nh], outs[nh:nt], outs[nt:nt + n], outs[nt + n]
        send, recv, loc, fsend, frecv, psend, precv, ploc = outs[nt + n + 1:]
        x, y, c = lax.axis_index("x"), lax.axis_index("y"), lax.axis_index("c")
        me = 2 * x + y
        dev = 4 * x + 2 * y + c
        sibling = (x, y, 1 - c)
        rows = [h.shape[1] * 2 for h in halves]

        def remote(i, j, slot):
            px, py = _flip(x, _CHIP_FLIPS[j][0]), _flip(y, _CHIP_FLIPS[j][1])
            if i < nh:
                src, dst = hin[i].at[2 * px + py], _half(landed[i].at[slot], c, rows[i])
            else:
                src, dst = ins[i - nh].at[2 * px + py], mine[i - nh].at[slot]
            return pltpu.make_async_remote_copy(
                src_ref=src, dst_ref=dst, send_sem=send.at[3 * i + j], recv_sem=recv.at[3 * i + j],
                device_id=(px, py, c), device_id_type=_MESH)

        def packed_to(r, slot):
            px, py, pc = _flip(x, r & 4), _flip(y, r & 2), _flip(c, r & 1)
            return pltpu.make_async_remote_copy(
                src_ref=pk, dst_ref=pk_all.at[slot], send_sem=psend.at[r - 1], recv_sem=precv.at[r - 1],
                device_id=(px, py, pc), device_id_type=_MESH)

        def forward(i, half):
            if i < nh:
                src = dst = _half(landed[i], half, rows[i])
            else:
                src, dst = mine[i - nh], theirs[i - nh]
            return pltpu.make_async_remote_copy(
                src_ref=src, dst_ref=dst, send_sem=fsend.at[i], recv_sem=frecv.at[i],
                device_id=sibling, device_id_type=_MESH)

        local = [pltpu.make_async_copy(hin[i].at[me], _half(landed[i].at[me], c, rows[i]), loc.at[i])
                 for i in range(nh)]
        local += [pltpu.make_async_copy(ins[i].at[me], mine[i].at[me], loc.at[nh + i]) for i in range(n)]
        plocal = pltpu.make_async_copy(pk, pk_all.at[dev], ploc.at[0])
        sends = [remote(i, j, me) for i in range(nt) for j in range(3)]
        psends = [packed_to(r, dev) for r in range(1, N_DEV)]
        for cp in local + [plocal] + sends + psends:
            cp.start()
        fwd = [forward(i, c) for i in range(nt)]
        for i in range(nt):
            for j, (fx, fy) in enumerate(_CHIP_FLIPS):
                remote(i, j, 2 * _flip(x, fx) + _flip(y, fy)).wait_recv()
            local[i].wait()
            fwd[i].start()
        for i in range(nt):
            forward(i, 1 - c).wait_recv()
        for r in range(1, N_DEV):
            packed_to(r, 4 * _flip(x, r & 4) + 2 * _flip(y, r & 2) + _flip(c, r & 1)).wait_recv()
        for cp in sends + psends + fwd:
            cp.wait_send()
        plocal.wait()

    full = [jax.ShapeDtypeStruct((N_CHIPS, 2 * h.shape[1], h.shape[2]), h.dtype) for h in halves]
    land = [jax.ShapeDtypeStruct(p.shape, p.dtype) for p in parts]
    out = pl.pallas_call(
        body, in_specs=[_ANY] * (nt + 1), out_specs=[_ANY] * (nt + n + 1),
        out_shape=full + land + land + [jax.ShapeDtypeStruct((N_DEV,) + packed.shape, packed.dtype)],
        scratch_shapes=[pltpu.SemaphoreType.DMA((3 * nt,)), pltpu.SemaphoreType.DMA((3 * nt,)),
                        pltpu.SemaphoreType.DMA((nt,)), pltpu.SemaphoreType.DMA((nt,)), pltpu.SemaphoreType.DMA((nt,)),
                        pltpu.SemaphoreType.DMA((N_DEV - 1,)), pltpu.SemaphoreType.DMA((N_DEV - 1,)),
                        pltpu.SemaphoreType.DMA((1,))],
        name=name)(*halves, *parts, packed)
    return out[:nh], out[nh:nt], out[nt:nt + n], out[nt + n]


def adamw(w, m, v, parts, name):
    r, c = w.shape
    tr = r
    for cand in (512, 256, 128, 64, 32, 16, 8):
        if r % cand == 0 and cand * c * 4 <= 512 * 1024:
            tr = cand
            break
    np_ = len(parts)
    nslot = parts[0].shape[0]
    bc1 = 1.0 - ADAM_B1 ** ADAM_STEP
    bc2 = 1.0 - ADAM_B2 ** ADAM_STEP

    def body(*refs):
        w_ref, m_ref, v_ref = refs[:3]
        p_refs = refs[3:3 + np_]
        g_ref, d_ref, nm_ref, nv_ref = refs[3 + np_:]
        g = None
        for k in range(nslot):
            t = p_refs[0][k].astype(F32)
            for p_ref in p_refs[1:]:
                t = t + p_ref[k].astype(F32)
            g = t if g is None else g + t
        wv = w_ref[...]
        nm = ADAM_B1 * m_ref[...] + (1.0 - ADAM_B1) * g
        nv = ADAM_B2 * v_ref[...] + (1.0 - ADAM_B2) * (g * g)
        g_ref[...] = g
        nm_ref[...] = nm
        nv_ref[...] = nv
        d_ref[...] = -ADAM_LR * ((nm / bc1) / (jnp.sqrt(nv / bc2) + ADAM_EPS) + ADAM_WD * wv)

    row = pl.BlockSpec((tr, c), lambda i: (i, 0))
    slab = pl.BlockSpec((nslot, tr, c), lambda i: (0, i, 0))
    sh = jax.ShapeDtypeStruct((r, c), F32)
    return pl.pallas_call(
        body, grid=(r // tr,), in_specs=[row, row, row] + [slab] * np_, out_specs=[row] * 4,
        out_shape=[sh] * 4, compiler_params=_cp("parallel"), name=name)(w, m, v, *parts)


WEIGHTS = ["norm_mix_g", "norm_ffn_g", "pool_w", "pool_b", "pool_scale", "s5_lam_re", "s5_lam_im", "s5_log_dt",
           "s5_b_re", "s5_b_im", "s5_c_re", "s5_c_im", "s5_d", "s5_w_out", "s5_b_out", "lru_w_in", "lru_conv_w",
           "lru_conv_b", "lru_w_a", "lru_b_a", "lru_w_x", "lru_b_x", "lru_lam", "lru_w_out", "sb_w_qkv", "sb_q_g",
           "sb_k_g", "sb_w_o", "ffn_w_in", "ffn_conv_w", "ffn_conv_b", "ffn_w_out"]
SHARD_AXIS = dict(pool_w=2, s5_d=1, s5_w_out=2, s5_b_out=1, lru_w_in=2, lru_conv_w=2, lru_conv_b=1, lru_w_a=2,
                  lru_b_a=1, lru_w_x=2, lru_b_x=1, lru_lam=1, lru_w_out=1, sb_w_qkv=2, sb_w_o=1, ffn_w_in=2,
                  ffn_conv_w=2, ffn_w_out=1)
MXU_WEIGHTS = ("pool_w", "s5_w_out", "lru_w_in", "lru_w_a", "lru_w_x", "lru_w_out", "sb_w_qkv", "sb_w_o",
               "ffn_w_in", "ffn_w_out")
SHARDED = [n for n in WEIGHTS if n in SHARD_AXIS]
REPLICATED = [n for n in WEIGHTS if n not in SHARD_AXIS]
PACK_WIDTH = 1024


def _as_rows(a):
    return a.reshape(-1, a.shape[-1])


def _pack(arrays):
    rows = []
    for a in arrays:
        flat = a.reshape(-1)
        pad = (-flat.shape[0]) % PACK_WIDTH
        rows.append(jnp.pad(flat, (0, pad)).reshape(-1, PACK_WIDTH))
    out = jnp.concatenate(rows, axis=0)
    return jnp.pad(out, ((0, (-out.shape[0]) % 8), (0, 0)))


def _unpack(packed, like):
    out, r = [], 0
    for a in like:
        size = math.prod(a.shape)
        nrow = -(-size // PACK_WIDTH)
        out.append(packed[r:r + nrow].reshape(-1)[:size].reshape(a.shape))
        r += nrow
    return out


def kernel(*args):
    names = ["x"] + WEIGHTS + ["loss_target"] + ["m_" + n for n in WEIGHTS] + ["v_" + n for n in WEIGHTS]
    assert len(args) == len(names)
    given = dict(zip(names, args))
    x, target = given["x"], given["loss_target"]
    bsz, seq, d = x.shape
    m_tok = bsz * seq

    shards = [_as_rows(given[n].astype(BF16) if n in MXU_WEIGHTS else given[n]) for n in SHARDED]
    gathered = gather_weights(shards)
    p = {n: given[n] for n in REPLICATED}
    my_chip = 2 * lax.axis_index("x") + lax.axis_index("y")
    for n, g4, own in zip(SHARDED, gathered, shards):
        blocks = g4.reshape((N_CHIPS,) + given[n].shape)
        own = own.reshape(given[n].shape)
        p[n] = jnp.concatenate([jnp.where(my_chip == k, own, blocks[k]) for k in range(N_CHIPS)],
                               axis=SHARD_AXIS[n])

    mix = {k: (v[0] if v.ndim > 2 or k == "s5_log_dt" else v) for k, v in p.items()
           if not k.startswith(("norm_", "ffn_"))}
    mixers = ((pool_layer_fwd, pool_layer_bwd), (s5_layer_fwd, s5_layer_bwd), (lru_layer_fwd, lru_layer_bwd),
              (sb_layer_fwd, sb_layer_bwd))
    ffn_p = lambda l: (p["norm_ffn_g"][l:l + 1], p["ffn_w_in"][l], p["ffn_conv_w"][l], p["ffn_conv_b"][l:l + 1],
                       p["ffn_w_out"][l])

    def mixer_args(l):
        g = p["norm_mix_g"][l:l + 1]
        if l == 0:
            return (g, mix["pool_w"], mix["pool_b"], mix["pool_scale"], bsz)
        return (g, mix, bsz)

    h = x.reshape(m_tok, d)
    saved = []
    for l in range(4):
        h, s_mix = mixers[l][0](h, *mixer_args(l))
        gl, w_in, cw, cb, w_out = ffn_p(l)
        h, s_ffn = ffn_fwd(h, gl, w_in, cw, cb, w_out, bsz)
        saved.append((s_mix, s_ffn))
    dh, loss_part = loss_head(h, target.reshape(m_tok, d))
    loss = lax.psum(jnp.sum(loss_part), ("x", "y", "c"))

    grads = {}
    ffn_g = [None] * 4
    mix_g = [None] * 4
    for l in range(3, -1, -1):
        gl, w_in, cw, cb, w_out = ffn_p(l)
        dh, ffn_g[l] = ffn_bwd(saved[l][1], dh, gl, w_in, cw, cb, w_out, bsz)
        dh, mix_g[l] = mixers[l][1](saved[l][0], dh, *mixer_args(l))
    for k in ("norm_ffn_g", "ffn_w_in", "ffn_conv_w", "ffn_conv_b", "ffn_w_out"):
        grads[k] = jnp.stack([ffn_g[l][k] for l in range(4)]).reshape(p[k].shape)
    grads["norm_mix_g"] = jnp.concatenate([mix_g[l]["norm_mix_g"] for l in range(4)], axis=0)
    for l in range(4):
        for k, v in mix_g[l].items():
            if k != "norm_mix_g":
                grads[k] = v.reshape(p[k].shape)
    grad_x = dh.reshape(bsz, seq, d)

    parts = {}
    for n in SHARDED:
        blocks = jnp.stack(jnp.split(grads[n], N_CHIPS, axis=SHARD_AXIS[n]))
        parts[n] = blocks.reshape(N_CHIPS, -1, blocks.shape[-1])
    big = [n for n in SHARDED if _splits(parts[n].shape)]
    small = [n for n in SHARDED if n not in big]
    got = grad_halves_exchange([parts[n] for n in big])
    core = lax.axis_index("c").astype(jnp.int32).reshape(1)
    halves = [pair_sum(parts[n], b, core, BF16, "pair_sum_" + n) for n, b in zip(big, got)]
    landed, mine, theirs, packed_all = grad_sync(halves, [parts[n] for n in small],
                                                 _pack([grads[n] for n in REPLICATED]))
    summed = {n: [landed[i]] for i, n in enumerate(big)}
    summed.update({n: [mine[i], theirs[i]] for i, n in enumerate(small)})

    out = {}
    for n in SHARDED:
        res = adamw(_as_rows(given[n]), _as_rows(given["m_" + n]), _as_rows(given["v_" + n]), summed[n], "adamw_" + n)
        out[n] = [r.reshape(given[n].shape) for r in res]
    res = adamw(_pack([given[n] for n in REPLICATED]), _pack([given["m_" + n] for n in REPLICATED]),
                _pack([given["v_" + n] for n in REPLICATED]), [packed_all], "adamw_replicated")
    like = [given[n] for n in REPLICATED]
    for n, *vals in zip(REPLICATED, *[_unpack(r, like) for r in res]):
        out[n] = list(vals)
    return (loss, grad_x, *[out[n][0] for n in WEIGHTS], *[out[n][1] for n in WEIGHTS],
            *[out[n][2] for n in WEIGHTS], *[out[n][3] for n in WEIGHTS])
```

```python
import functools
import math

import jax
import jax.numpy as jnp
from jax import lax
from jax.experimental import pallas as pl
from jax.experimental.pallas import tpu as pltpu

F32 = jnp.float32
BF16 = jnp.bfloat16

EPS = 1e-6
N_CHIPS = 4
N_DEV = 8
POOL_WINDOWS = (2, 4, 8, 16)
POOL_GROUP = 256
S5_GROUP = 16
S5_STATE = 64
S5_CHUNKS = 8
S5_LANES = 512
S5_T = 256
LRU_BLOCK = 256
LRU_CONV = 4
LRU_C = 8.0
SB_HEADS = 16
SB_DIM = 64
SB_BLOCK = 512
SB_CHUNK = 128
FFN_CONV = 3
ADAM_LR, ADAM_B1, ADAM_B2, ADAM_EPS, ADAM_WD, ADAM_STEP = 0.001, 0.9, 0.999, 1e-08, 0.01, 10
VMEM_LIMIT_BYTES = 56 * 1024 * 1024
MATMUL_VMEM_BUDGET = 30 * 1024 * 1024
MATMUL_WHOLE_K = 2816

NN = (((1,), (0,)), ((), ()))
NT = (((1,), (1,)), ((), ()))
TN = (((0,), (0,)), ((), ()))


def _cp(*sem):
    return pltpu.CompilerParams(dimension_semantics=sem, vmem_limit_bytes=VMEM_LIMIT_BYTES)


def _pick(n, prefs):
    for p in prefs:
        if n % p == 0:
            return p
    return n


def _dot(a, b, dims=NN):
    return lax.dot_general(a.astype(BF16), b.astype(BF16), dims, preferred_element_type=F32)


def _split(x):
    hi = x.astype(BF16)
    return hi, (x - hi.astype(F32)).astype(BF16)


def _dot3(a, b, dims=NN):
    ah, al = _split(a)
    bh, bl = _split(b)
    d = lambda p, q: lax.dot_general(p, q, dims, preferred_element_type=F32)
    return d(ah, bh) + (d(ah, bl) + d(al, bh))


def _dot_exact_rhs(a, b01):
    ah, al = _split(a)
    d = lambda p: lax.dot_general(p, b01, NN, preferred_element_type=F32)
    return d(ah) + d(al)


def _sig(x):
    return 1.0 / (1.0 + jnp.exp(-x))


def _softplus(x):
    return jnp.maximum(x, 0.0) + jnp.log(1.0 + jnp.exp(-jnp.abs(x)))


_GELU_C = math.sqrt(2.0 / math.pi)


def _gelu(x):
    return 0.5 * x * (1.0 + jnp.tanh(_GELU_C * (x + 0.044715 * x * x * x)))


def _gelu_grad(x):
    th = jnp.tanh(_GELU_C * (x + 0.044715 * x * x * x))
    return 0.5 * (1.0 + th) + 0.5 * x * (1.0 - th * th) * _GELU_C * (1.0 + 3.0 * 0.044715 * x * x)


def _rows(shape):
    return lax.broadcasted_iota(jnp.int32, shape, 0)


SUBLANES = 8


def _shift_all(xs, k, up):
    t = xs[0].shape[0]
    if k >= t:
        return [jnp.zeros_like(x) for x in xs]
    if k % SUBLANES == 0:
        pad = jnp.zeros((k,) + xs[0].shape[1:], xs[0].dtype)
        return [jnp.concatenate([x[k:], pad] if up else [pad, x[:t - k]], axis=0) for x in xs]
    rows = _rows(xs[0].shape)
    keep = rows < t - k if up else rows >= k
    return [jnp.where(keep, pltpu.roll(x, t - k if up else k, 0), 0.0) for x in xs]


def _shift_down(x, k):
    return _shift_all([x], k, False)[0]


def _shift_up(x, k):
    return _shift_all([x], k, True)[0]


def matmul(a, b, *, ta=False, tb=False, bias=None, res=None, out_dtype=F32, name):
    m, k = (a.shape[1], a.shape[0]) if ta else a.shape
    n = b.shape[0] if tb else b.shape[1]
    assert (b.shape[1] if tb else b.shape[0]) == k
    has_bias, has_res = bias is not None, res is not None
    tk = k if k <= MATMUL_WHOLE_K else _pick(k, (1408, 1024, 512, 256, 128))
    nk = k // tk
    sa, sb, so = a.dtype.itemsize, b.dtype.itemsize, jnp.dtype(out_dtype).itemsize
    tm = tn = None
    for cm in (2048, 1024, 512, 1408, 256, 128):
        for cn in (1024, 512, 1408, 256, 128):
            if m % cm or n % cn:
                continue
            need = 2 * cm * tk * sa + 2 * tk * cn * sb + cm * cn * (2 * so + 4 + (4 if nk > 1 else 0)
                                                                  + (8 if has_res else 0))
            if need <= MATMUL_VMEM_BUDGET and (tm is None or cm * cn > tm * tn):
                tm, tn = cm, cn
    assert tm is not None, (m, n, k)
    dims = (((0 if ta else 1,), (1 if tb else 0,)), ((), ()))

    def body(*refs):
        a_ref, b_ref = refs[:2]
        rest = list(refs[2:])
        bias_ref = rest.pop(0) if has_bias else None
        res_ref = rest.pop(0) if has_res else None
        o_ref = rest[0]

        def finish(r):
            if has_bias:
                r = r + bias_ref[...]
            if has_res:
                r = r + res_ref[...]
            o_ref[...] = r.astype(o_ref.dtype)

        part = lax.dot_general(a_ref[...].astype(BF16), b_ref[...].astype(BF16), dims, preferred_element_type=F32)
        if nk == 1:
            finish(part)
            return
        acc_ref = rest[1]
        kk = pl.program_id(2)

        @pl.when(kk == 0)
        def _():
            acc_ref[...] = part

        @pl.when(kk > 0)
        def _():
            acc_ref[...] += part

        @pl.when(kk == nk - 1)
        def _():
            finish(acc_ref[...])

    in_specs = [
        pl.BlockSpec((tk, tm), lambda i, j, kk: (kk, i)) if ta else pl.BlockSpec((tm, tk), lambda i, j, kk: (i, kk)),
        pl.BlockSpec((tn, tk), lambda i, j, kk: (j, kk)) if tb else pl.BlockSpec((tk, tn), lambda i, j, kk: (kk, j)),
    ]
    args = [a, b]
    if has_bias:
        in_specs.append(pl.BlockSpec((1, tn), lambda i, j, kk: (0, j)))
        args.append(bias)
    if has_res:
        in_specs.append(pl.BlockSpec((tm, tn), lambda i, j, kk: (i, j)))
        args.append(res)
    return pl.pallas_call(
        body, grid=(m // tm, n // tn, nk), in_specs=in_specs,
        out_specs=pl.BlockSpec((tm, tn), lambda i, j, kk: (i, j)),
        out_shape=jax.ShapeDtypeStruct((m, n), out_dtype),
        scratch_shapes=[pltpu.VMEM((tm, tn), F32)] if nk > 1 else [],
        compiler_params=_cp("parallel", "parallel", "arbitrary"), name=name)(*args)


def rms_fwd(x, g, out_dtype, name):
    m, d = x.shape
    tr = _pick(m, (512, 256, 128))

    def body(x_ref, g_ref, o_ref):
        xv = x_ref[...]
        r = lax.rsqrt(jnp.mean(xv * xv, axis=-1, keepdims=True) + EPS)
        o_ref[...] = (xv * r * g_ref[...]).astype(o_ref.dtype)

    return pl.pallas_call(
        body, grid=(m // tr,),
        in_specs=[pl.BlockSpec((tr, d), lambda i: (i, 0)), pl.BlockSpec((1, d), lambda i: (0, 0))],
        out_specs=pl.BlockSpec((tr, d), lambda i: (i, 0)),
        out_shape=jax.ShapeDtypeStruct((m, d), out_dtype),
        compiler_params=_cp("parallel"), name=name)(x, g)


def rms_bwd(x, g, dh, dres, name):
    m, d = x.shape
    tr = _pick(m, (512, 256, 128))

    def body(x_ref, g_ref, dh_ref, dres_ref, dx_ref, dg_ref):
        xv = x_ref[...]
        r = lax.rsqrt(jnp.mean(xv * xv, axis=-1, keepdims=True) + EPS)
        xh = xv * r
        dhv = dh_ref[...].astype(F32)
        dxh = dhv * g_ref[...]
        dx_ref[...] = dres_ref[...] + r * (dxh - xh * jnp.mean(dxh * xh, axis=-1, keepdims=True))

        @pl.when(pl.program_id(0) == 0)
        def _():
            dg_ref[...] = jnp.zeros_like(dg_ref)

        dg_ref[...] += jnp.sum(dhv * xh, axis=0, keepdims=True)

    row = pl.BlockSpec((tr, d), lambda i: (i, 0))
    vec = pl.BlockSpec((1, d), lambda i: (0, 0))
    return pl.pallas_call(
        body, grid=(m // tr,), in_specs=[row, vec, row, row], out_specs=[row, vec],
        out_shape=[jax.ShapeDtypeStruct((m, d), F32), jax.ShapeDtypeStruct((1, d), F32)],
        compiler_params=_cp("arbitrary"), name=name)(x, g, dh, dres)


def _conv_taps(u, kw):
    return [_shift_down(u, kw - 1 - k) for k in range(kw - 1)] + [u]


def _conv_fwd(taps, w_ref, b_ref):
    y = b_ref[...] + w_ref[len(taps) - 1:len(taps), :] * taps[-1]
    for k in range(len(taps) - 1):
        y = y + w_ref[k:k + 1, :] * taps[k]
    return y


def _conv_bwd_input(dy, w_ref, kw):
    du = w_ref[kw - 1:kw, :] * dy
    for k in range(kw - 1):
        du = du + w_ref[k:k + 1, :] * _shift_up(dy, kw - 1 - k)
    return du


def _conv_bwd_weight(dy, taps):
    return [jnp.sum(dy * tap, axis=0, keepdims=True) for tap in taps]


FFN_TILE = 256


def ffn_pair_layout(t):
    f = t.shape[-1] // 2
    return t.reshape(t.shape[:-1] + (2, f // FFN_TILE, FFN_TILE)).swapaxes(-3, -2).reshape(t.shape)


def ffn_plain_layout(t):
    f = t.shape[-1] // 2
    return t.reshape(t.shape[:-1] + (f // FFN_TILE, 2, FFN_TILE)).swapaxes(-3, -2).reshape(t.shape)


def ffn_act_fwd(u, cw, cb, name):
    bsz, s, f2 = u.shape
    tc = FFN_TILE
    nj = f2 // (2 * tc)

    def body(u_ref, w_ref, b_ref, a_ref):
        h = _conv_fwd(_conv_taps(u_ref[0], FFN_CONV), w_ref, b_ref)
        hv, hg = h[:, :tc], h[:, tc:]
        a_ref[0] = (hg * _sig(hg) * hv).astype(a_ref.dtype)

    return pl.pallas_call(
        body, grid=(bsz, nj),
        in_specs=[pl.BlockSpec((1, s, 2 * tc), lambda b, j: (b, 0, j)), pl.BlockSpec((FFN_CONV, 2 * tc), lambda b, j: (0, j)),
                  pl.BlockSpec((1, 2 * tc), lambda b, j: (0, j))],
        out_specs=pl.BlockSpec((1, s, tc), lambda b, j: (b, 0, j)),
        out_shape=jax.ShapeDtypeStruct((bsz, s, f2 // 2), BF16),
        compiler_params=_cp("parallel", "parallel"), name=name)(u, cw, cb)


def ffn_act_bwd(u, cw, cb, da, name):
    bsz, s, f2 = u.shape
    tc = FFN_TILE
    nj = f2 // (2 * tc)

    def body(u_ref, w_ref, b_ref, da_ref, du_ref, dw_ref, db_ref):
        taps = _conv_taps(u_ref[0], FFN_CONV)
        h = _conv_fwd(taps, w_ref, b_ref)
        hv, hg = h[:, :tc], h[:, tc:]
        sg = _sig(hg)
        dav = da_ref[0].astype(F32)
        dh = jnp.concatenate([dav * hg * sg, dav * hv * (sg * (1.0 + hg * (1.0 - sg)))], axis=1)
        du_ref[0] = _conv_bwd_input(dh, w_ref, FFN_CONV).astype(du_ref.dtype)

        @pl.when(pl.program_id(1) == 0)
        def _():
            dw_ref[...] = jnp.zeros_like(dw_ref)
            db_ref[...] = jnp.zeros_like(db_ref)

        for k, row in enumerate(_conv_bwd_weight(dh, taps)):
            dw_ref[k:k + 1, :] += row
        db_ref[...] += jnp.sum(dh, axis=0, keepdims=True)

    usp = pl.BlockSpec((1, s, 2 * tc), lambda j, b: (b, 0, j))
    wsp = pl.BlockSpec((FFN_CONV, 2 * tc), lambda j, b: (0, j))
    bsp = pl.BlockSpec((1, 2 * tc), lambda j, b: (0, j))
    return pl.pallas_call(
        body, grid=(nj, bsz), in_specs=[usp, wsp, bsp, pl.BlockSpec((1, s, tc), lambda j, b: (b, 0, j))],
        out_specs=[usp, wsp, bsp],
        out_shape=[jax.ShapeDtypeStruct((bsz, s, f2), BF16), jax.ShapeDtypeStruct((FFN_CONV, f2), F32),
                   jax.ShapeDtypeStruct((1, f2), F32)],
        compiler_params=_cp("parallel", "arbitrary"), name=name)(u, cw, cb, da)


def ffn_fwd(x, g, w_in, cw, cb, w_out, bsz):
    m, d = x.shape
    h = rms_fwd(x, g, BF16, "ffn_norm")
    u = matmul(h, w_in, name="ffn_in")
    a = ffn_act_fwd(u.reshape(bsz, m // bsz, -1), cw, cb, "ffn_act")
    a2 = a.reshape(m, -1)
    out = matmul(a2, w_out, res=x, name="ffn_out")
    return out, (x, h, u, a2)


def ffn_bwd(saved, dout, g, w_in, cw, cb, w_out, bsz):
    x, h, u, a2 = saved
    m, d = x.shape
    da = matmul(dout, w_out, tb=True, out_dtype=BF16, name="ffn_out_dx")
    dw_out = matmul(a2, dout, ta=True, name="ffn_out_dw")
    u3 = u.reshape(bsz, m // bsz, -1)
    du, dcw, dcb = ffn_act_bwd(u3, cw, cb, da.reshape(bsz, m // bsz, -1), "ffn_act_bwd")
    du = du.reshape(m, -1)
    dh = matmul(du, w_in, tb=True, name="ffn_in_dx")
    dw_in = matmul(h, du, ta=True, name="ffn_in_dw")
    dx, dg = rms_bwd(x, g, dh, dout, "ffn_norm_bwd")
    grads = dict(norm_ffn_g=dg, ffn_w_in=ffn_plain_layout(dw_in), ffn_conv_w=ffn_plain_layout(dcw),
                 ffn_conv_b=ffn_plain_layout(dcb), ffn_w_out=dw_out)
    return dx, grads


def loss_head(y, target, name="loss_head"):
    m, d = y.shape
    tr = _pick(m, (512, 256, 128))

    def body(y_ref, t_ref, dy_ref, l_ref):
        e = y_ref[...] - t_ref[...]
        dy_ref[...] = e * (1.0 / d)

        @pl.when(pl.program_id(0) == 0)
        def _():
            l_ref[...] = jnp.zeros_like(l_ref)

        l_ref[...] += jnp.sum(e * e, axis=0, keepdims=True) * (0.5 / d)

    row = pl.BlockSpec((tr, d), lambda i: (i, 0))
    vec = pl.BlockSpec((1, d), lambda i: (0, 0))
    dy, part = pl.pallas_call(
        body, grid=(m // tr,), in_specs=[row, row], out_specs=[row, vec],
        out_shape=[jax.ShapeDtypeStruct((m, d), F32), jax.ShapeDtypeStruct((1, d), F32)],
        compiler_params=_cp("arbitrary"), name=name)(y, target)
    return dy, part


def _pool_windows(h, gi):
    sums, s, width = [], h, 1
    for _ in POOL_WINDOWS:
        s = s + _shift_down(s, width)
        width *= 2
        sums.append(s)
    pos = _rows(h.shape).astype(F32) + 1.0
    wsum, inv = sums[-1], 1.0 / jnp.minimum(pos, float(POOL_WINDOWS[-1]))
    for k in range(len(POOL_WINDOWS) - 2, -1, -1):
        wsum = jnp.where(gi == k, sums[k], wsum)
        inv = jnp.where(gi == k, 1.0 / jnp.minimum(pos, float(POOL_WINDOWS[k])), inv)
    return wsum * inv - h, inv


def _pool_windows_transpose(e, gi):
    sums, s, width = [], e, 1
    for _ in POOL_WINDOWS:
        s = s + _shift_up(s, width)
        width *= 2
        sums.append(s)
    out = sums[-1]
    for k in range(len(POOL_WINDOWS) - 2, -1, -1):
        out = jnp.where(gi == k, sums[k], out)
    return out


def pool_fwd(h, w, b, scale, x, name="pool_fwd"):
    bsz, s, d = h.shape
    ng = d // POOL_GROUP

    def body(h_ref, w_ref, b_ref, s_ref, x_ref, o_ref):
        dd, _ = _pool_windows(h_ref[0], pl.program_id(1))
        y = _dot(dd, w_ref[0]) + b_ref[...]
        o_ref[0] = x_ref[0] + s_ref[...] * y

    act = pl.BlockSpec((1, s, POOL_GROUP), lambda bb, gi: (bb, 0, gi))
    vec = pl.BlockSpec((1, POOL_GROUP), lambda bb, gi: (0, gi))
    return pl.pallas_call(
        body, grid=(bsz, ng),
        in_specs=[act, pl.BlockSpec((1, POOL_GROUP, POOL_GROUP), lambda bb, gi: (gi, 0, 0)), vec, vec, act],
        out_specs=act, out_shape=jax.ShapeDtypeStruct((bsz, s, d), F32),
        compiler_params=_cp("parallel", "parallel"), name=name)(h, w, b, scale, x)


def pool_bwd(h, w, b, scale, dy, name="pool_bwd"):
    bsz, s, d = h.shape
    ng = d // POOL_GROUP

    def body(h_ref, w_ref, b_ref, s_ref, dy_ref, dh_ref, dw_ref, db_ref, ds_ref):
        gi = pl.program_id(0)
        dd, inv = _pool_windows(h_ref[0], gi)
        ypre = _dot(dd, w_ref[0]) + b_ref[...]
        dyv = dy_ref[0]
        dyb = dyv * s_ref[...]

        @pl.when(pl.program_id(1) == 0)
        def _():
            for r in (dw_ref, db_ref, ds_ref):
                r[...] = jnp.zeros_like(r)

        ds_ref[...] += jnp.sum(dyv * ypre, axis=0, keepdims=True)
        db_ref[...] += jnp.sum(dyb, axis=0, keepdims=True)
        dw_ref[0] += _dot(dd, dyb, TN)
        ddd = _dot(dyb, w_ref[0], NT)
        dh_ref[0] = _pool_windows_transpose(ddd * inv, gi) - ddd

    act = pl.BlockSpec((1, s, POOL_GROUP), lambda gi, bb: (bb, 0, gi))
    vec = pl.BlockSpec((1, POOL_GROUP), lambda gi, bb: (0, gi))
    wsp = pl.BlockSpec((1, POOL_GROUP, POOL_GROUP), lambda gi, bb: (gi, 0, 0))
    return pl.pallas_call(
        body, grid=(ng, bsz), in_specs=[act, wsp, vec, vec, act], out_specs=[act, wsp, vec, vec],
        out_shape=[jax.ShapeDtypeStruct((bsz, s, d), F32), jax.ShapeDtypeStruct((ng, POOL_GROUP, POOL_GROUP), F32),
                   jax.ShapeDtypeStruct((1, d), F32), jax.ShapeDtypeStruct((1, d), F32)],
        compiler_params=_cp("parallel", "arbitrary"), name=name)(h, w, b, scale, dy)


def pool_layer_fwd(x, g, w, b, scale, bsz):
    m, d = x.shape
    h = rms_fwd(x, g, F32, "pool_norm")
    out = pool_fwd(h.reshape(bsz, m // bsz, d), w, b, scale, x.reshape(bsz, m // bsz, d))
    return out.reshape(m, d), (x, h)


def pool_layer_bwd(saved, dout, g, w, b, scale, bsz):
    x, h = saved
    m, d = x.shape
    dh, dw, db, ds = pool_bwd(h.reshape(bsz, m // bsz, d), w, b, scale, dout.reshape(bsz, m // bsz, d))
    dx, dg = rms_bwd(x, g, dh.reshape(m, d), dout, "pool_norm_bwd")
    return dx, dict(norm_mix_g=dg, pool_w=dw[None], pool_b=db, pool_scale=ds)


def _scan_fwd(a, b):
    return _scan(a, b, False)


def _scan(a, b, up):
    k = 1
    while k < a.shape[0]:
        if 2 * k < a.shape[0]:
            sa, sb = _shift_all([a, b], k, up)
            a, b = a * sa, b + a * sb
        else:
            b = b + a * _shift_all([b], k, up)[0]
        k *= 2
    return b


def _scan_bwd(a, b):
    return _scan(a, b, True)


def _neg_expm1(x):
    series = -x * (1.0 + x * (0.5 + x * (1.0 / 6.0 + x * (1.0 / 24.0 + x * (1.0 / 120.0)))))
    return jnp.where(x > -0.03, series, 1.0 - jnp.exp(x))


def _lru_gates(rec, wa_ref, ba_ref, wx_ref, bx_ref, lam_ref):
    r = _sig(_dot(rec, wa_ref[0]) + ba_ref[...])
    i = _sig(_dot(rec, wx_ref[0]) + bx_ref[...])
    sp = _softplus(-lam_ref[...])
    log_a = -LRU_C * r * sp
    a = jnp.exp(log_a)
    mult = jnp.sqrt(_neg_expm1(2.0 * log_a))
    return r, i, sp, a, mult


def lru_fwd(zz, cw, cb, wa, ba, wx, bx, lam, name="lru_fwd"):
    bsz, s, r2 = zz.shape
    rw = r2 // 2
    nb = rw // LRU_BLOCK

    def body(g_ref, p_ref, cw_ref, cb_ref, wa_ref, ba_ref, wx_ref, bx_ref, lam_ref, h_ref, y_ref):
        rec = _conv_fwd(_conv_taps(p_ref[0], LRU_CONV), cw_ref, cb_ref)
        _, i, _, a, mult = _lru_gates(rec, wa_ref, ba_ref, wx_ref, bx_ref, lam_ref)
        hst = _scan_fwd(a, mult * (i * rec))
        h_ref[0] = hst
        y_ref[0] = (_gelu(g_ref[0]) * hst).astype(y_ref.dtype)

    gsp = pl.BlockSpec((1, s, LRU_BLOCK), lambda bb, n: (bb, 0, n))
    psp = pl.BlockSpec((1, s, LRU_BLOCK), lambda bb, n: (bb, 0, n + nb))
    cws = pl.BlockSpec((LRU_CONV, LRU_BLOCK), lambda bb, n: (0, n))
    vec = pl.BlockSpec((1, LRU_BLOCK), lambda bb, n: (0, n))
    wsp = pl.BlockSpec((1, LRU_BLOCK, LRU_BLOCK), lambda bb, n: (n, 0, 0))
    return pl.pallas_call(
        body, grid=(bsz, nb), in_specs=[gsp, psp, cws, vec, wsp, vec, wsp, vec, vec], out_specs=[gsp, gsp],
        out_shape=[jax.ShapeDtypeStruct((bsz, s, rw), F32), jax.ShapeDtypeStruct((bsz, s, rw), BF16)],
        compiler_params=_cp("parallel", "parallel"), name=name)(zz, zz, cw, cb, wa, ba, wx, bx, lam)


def lru_bwd(zz, hst, dy, cw, cb, wa, ba, wx, bx, lam, name="lru_bwd"):
    bsz, s, r2 = zz.shape
    rw = r2 // 2
    nb = rw // LRU_BLOCK

    def body(g_ref, p_ref, h_ref, dy_ref, cw_ref, cb_ref, wa_ref, ba_ref, wx_ref, bx_ref, lam_ref,
             dg_ref, dp_ref, dcw_ref, dcb_ref, dwa_ref, dba_ref, dwx_ref, dbx_ref, dlam_ref):
        pre = p_ref[0]
        taps = _conv_taps(pre, LRU_CONV)
        rec = _conv_fwd(taps, cw_ref, cb_ref)
        r, i, sp, a, mult = _lru_gates(rec, wa_ref, ba_ref, wx_ref, bx_ref, lam_ref)
        hst_v, gate, dyv = h_ref[0], g_ref[0], dy_ref[0]
        dg_ref[0] = (dyv * hst_v * _gelu_grad(gate)).astype(dg_ref.dtype)
        lmb = _scan_bwd(_shift_up(a, 1), dyv * _gelu(gate))
        da = lmb * _shift_down(hst_v, 1)
        dmult = lmb * (i * rec)
        dlog_a = da * a - dmult * (a * a) / mult
        dr = dlog_a * (-LRU_C) * sp
        dra = dr * r * (1.0 - r)
        dxa = lmb * mult * rec * i * (1.0 - i)
        drec = lmb * mult * i + _dot(dra, wa_ref[0], NT) + _dot(dxa, wx_ref[0], NT)
        dp_ref[0] = _conv_bwd_input(drec, cw_ref, LRU_CONV).astype(dp_ref.dtype)

        @pl.when(pl.program_id(1) == 0)
        def _():
            for ref in (dcw_ref, dcb_ref, dwa_ref, dba_ref, dwx_ref, dbx_ref, dlam_ref):
                ref[...] = jnp.zeros_like(ref)

        for k, row in enumerate(_conv_bwd_weight(drec, taps)):
            dcw_ref[k:k + 1, :] += row
        dcb_ref[...] += jnp.sum(drec, axis=0, keepdims=True)
        dwa_ref[0] += _dot(rec, dra, TN)
        dwx_ref[0] += _dot(rec, dxa, TN)
        dba_ref[...] += jnp.sum(dra, axis=0, keepdims=True)
        dbx_ref[...] += jnp.sum(dxa, axis=0, keepdims=True)
        dsp = jnp.sum(dlog_a * (-LRU_C) * r, axis=0, keepdims=True)
        dlam_ref[...] += dsp * (-_sig(-lam_ref[...]))

    gsp = pl.BlockSpec((1, s, LRU_BLOCK), lambda n, bb: (bb, 0, n))
    psp = pl.BlockSpec((1, s, LRU_BLOCK), lambda n, bb: (bb, 0, n + nb))
    cws = pl.BlockSpec((LRU_CONV, LRU_BLOCK), lambda n, bb: (0, n))
    vec = pl.BlockSpec((1, LRU_BLOCK), lambda n, bb: (0, n))
    wsp = pl.BlockSpec((1, LRU_BLOCK, LRU_BLOCK), lambda n, bb: (n, 0, 0))
    act = jax.ShapeDtypeStruct((bsz, s, rw), BF16)
    vsh = jax.ShapeDtypeStruct((1, rw), F32)
    wsh = jax.ShapeDtypeStruct((nb, LRU_BLOCK, LRU_BLOCK), F32)
    return pl.pallas_call(
        body, grid=(nb, bsz), in_specs=[gsp, psp, gsp, gsp, cws, vec, wsp, vec, wsp, vec, vec],
        out_specs=[gsp, gsp, cws, vec, wsp, vec, wsp, vec, vec],
        out_shape=[act, act, jax.ShapeDtypeStruct((LRU_CONV, rw), F32), vsh, wsh, vsh, wsh, vsh, vsh],
        compiler_params=_cp("parallel", "arbitrary"), name=name)(zz, zz, hst, dy, cw, cb, wa, ba, wx, bx, lam)


def lru_layer_fwd(x, g, p, bsz):
    m, d = x.shape
    h = rms_fwd(x, g, BF16, "lru_norm")
    zz = matmul(h, p["lru_w_in"], name="lru_in")
    hst, y = lru_fwd(zz.reshape(bsz, m // bsz, -1), p["lru_conv_w"], p["lru_conv_b"], p["lru_w_a"], p["lru_b_a"],
                     p["lru_w_x"], p["lru_b_x"], p["lru_lam"])
    y2 = y.reshape(m, -1)
    out = matmul(y2, p["lru_w_out"], res=x, name="lru_out")
    return out, (x, h, zz, hst, y2)


def lru_layer_bwd(saved, dout, g, p, bsz):
    x, h, zz, hst, y2 = saved
    m, d = x.shape
    dy = matmul(dout, p["lru_w_out"], tb=True, name="lru_out_dx")
    dw_out = matmul(y2, dout, ta=True, name="lru_out_dw")
    dgate, dpre, dcw, dcb, dwa, dba, dwx, dbx, dlam = lru_bwd(
        zz.reshape(bsz, m // bsz, -1), hst, dy.reshape(bsz, m // bsz, -1), p["lru_conv_w"], p["lru_conv_b"],
        p["lru_w_a"], p["lru_b_a"], p["lru_w_x"], p["lru_b_x"], p["lru_lam"])
    dzz = jnp.concatenate([dgate, dpre], axis=-1).reshape(m, -1)
    dh = matmul(dzz, p["lru_w_in"], tb=True, name="lru_in_dx")
    dw_in = matmul(h, dzz, ta=True, name="lru_in_dw")
    dx, dg = rms_bwd(x, g, dh, dout, "lru_norm_bwd")
    return dx, dict(norm_mix_g=dg, lru_w_in=dw_in, lru_conv_w=dcw[None], lru_conv_b=dcb, lru_w_a=dwa[None],
                    lru_b_a=dba, lru_w_x=dwx[None], lru_b_x=dbx, lru_lam=dlam, lru_w_out=dw_out)


def _s5_discretise(lam_re, lam_im, log_dt, b_re, b_im):
    lr = jnp.minimum(lam_re, -1e-4)
    dt = jnp.exp(log_dt)[:, None]
    mag = jnp.exp(lr * dt)
    ar, ai = mag * jnp.cos(lam_im * dt), mag * jnp.sin(lam_im * dt)
    den = lr * lr + lam_im * lam_im
    cr = ((ar - 1.0) * lr + ai * lam_im) / den
    ci = (ai * lr - (ar - 1.0) * lam_im) / den
    bbr = cr[..., None] * b_re - ci[..., None] * b_im
    bbi = cr[..., None] * b_im + ci[..., None] * b_re
    return ar, ai, bbr, bbi


def _s5_powers(lam_re, lam_im, log_dt, ns):
    lr = jnp.minimum(lam_re, -1e-4)
    dt = jnp.exp(log_dt)[:, None]
    n = jnp.asarray(ns, F32)[:, None, None]
    mag = jnp.exp(n * (lr * dt))
    ang = n * (lam_im * dt)
    to_chunks = lambda t: t.reshape(len(ns), S5_CHUNKS, S5_LANES).transpose(1, 0, 2)
    return jnp.concatenate([to_chunks(mag * jnp.cos(ang)), to_chunks(mag * jnp.sin(ang))], axis=-1)


def _s5_in_matrix(bbr, bbi):
    eye = jnp.eye(8, dtype=F32)
    blk = lambda t: jnp.einsum("qgph,gk->qghkp", t.reshape(S5_CHUNKS, 8, S5_STATE, S5_GROUP), eye).reshape(
        S5_CHUNKS, 128, S5_LANES)
    return jnp.concatenate([blk(bbr), blk(bbi)], axis=-1)


def _s5_in_matrix_diag(dmat):
    eye = jnp.eye(8, dtype=F32)[None, :, None, :, None]
    pick = lambda t: (t.reshape(S5_CHUNKS, 8, S5_GROUP, 8, S5_STATE) * eye).sum(3).transpose(0, 1, 3, 2).reshape(
        S5_CHUNKS * 8, S5_STATE, S5_GROUP)
    return pick(dmat[..., :S5_LANES]), pick(dmat[..., S5_LANES:])


def _s5_out_matrix(c_re, c_im):
    eye = jnp.eye(8, dtype=F32)
    blk = lambda t: jnp.einsum("qghp,gk->qgpkh", t.reshape(S5_CHUNKS, 8, S5_GROUP, S5_STATE), eye).reshape(
        S5_CHUNKS, S5_LANES, 128)
    return jnp.concatenate([blk(c_re), -blk(c_im)], axis=1)


def _s5_out_matrix_diag(dmat):
    eye = jnp.eye(8, dtype=F32)[None, :, None, :, None]
    pick = lambda t: (t.reshape(S5_CHUNKS, 8, S5_STATE, 8, S5_GROUP) * eye).sum(3).transpose(0, 1, 3, 2).reshape(
        S5_CHUNKS * 8, S5_GROUP, S5_STATE)
    return pick(dmat[:, :S5_LANES]), -pick(dmat[:, S5_LANES:])


def s5_fwd(h, bmat, cmat, atab, pw, dskip, name="s5_fwd"):
    bsz, s, d = h.shape
    t = min(S5_T, s)
    nt, nlev, ln = s // t, atab.shape[1], S5_LANES

    def body(h_ref, b_ref, c_ref, a_ref, pw_ref, d_ref, xs_ref, yp_ref, yg_ref, carry):
        @pl.when(pl.program_id(2) == 0)
        def _():
            carry[...] = jnp.zeros_like(carry)

        u = h_ref[0]
        bu = _dot3(u, b_ref[0])
        xr, xi = bu[:, :ln], bu[:, ln:]
        for k in range(nlev):
            ar, ai = a_ref[0, k:k + 1, :ln], a_ref[0, k:k + 1, ln:]
            sr, si = _shift_all([xr, xi], 1 << k, False)
            xr, xi = xr + ar * sr - ai * si, xi + ar * si + ai * sr
        cr, ci = carry[0:1, :ln], carry[0:1, ln:]
        pr, pi = pw_ref[0, :, :ln], pw_ref[0, :, ln:]
        xr, xi = xr + pr * cr - pi * ci, xi + pr * ci + pi * cr
        carry[0:1, :ln] = xr[t - 1:t, :]
        carry[0:1, ln:] = xi[t - 1:t, :]
        xs_ref[0, :, :ln] = xr
        xs_ref[0, :, ln:] = xi
        y = _dot3(xr, c_ref[0, :ln, :]) + _dot3(xi, c_ref[0, ln:, :]) + d_ref[...] * u
        yp_ref[0] = y
        yg_ref[0] = _gelu(y).astype(yg_ref.dtype)

    act = pl.BlockSpec((1, t, 128), lambda b, q, i: (b, i, q))
    par = lambda r, c: pl.BlockSpec((1, r, c), lambda b, q, i: (q, 0, 0))
    return pl.pallas_call(
        body, grid=(bsz, S5_CHUNKS, nt),
        in_specs=[act, par(128, 2 * ln), par(2 * ln, 128), par(nlev, 2 * ln), par(t, 2 * ln),
                  pl.BlockSpec((1, 128), lambda b, q, i: (0, q))],
        out_specs=[pl.BlockSpec((1, t, 2 * ln), lambda b, q, i: (b, i, q)), act, act],
        out_shape=[jax.ShapeDtypeStruct((bsz, s, S5_CHUNKS * 2 * ln), F32), jax.ShapeDtypeStruct((bsz, s, d), F32),
                   jax.ShapeDtypeStruct((bsz, s, d), BF16)],
        scratch_shapes=[pltpu.VMEM((8, 2 * ln), F32)],
        compiler_params=_cp("parallel", "parallel", "arbitrary"), name=name)(h, bmat, cmat, atab, pw, dskip)


def s5_bwd(h, ypre, xs, dyg, bmat_t, cmat_t, atab, pw_rev, dskip, name="s5_bwd"):
    bsz, s, d = h.shape
    t = min(S5_T, s)
    nt, nlev, ln = s // t, atab.shape[1], S5_LANES

    def body(h_ref, yp_ref, xs_ref, xp_ref, dy_ref, bt_ref, ct_ref, a_ref, pw_ref, d_ref,
             dh_ref, db_ref, dc_ref, da_ref, dd_ref, carry):
        b, i = pl.program_id(1), pl.program_id(2)

        @pl.when((b == 0) & (i == 0))
        def _():
            for r in (db_ref, dc_ref, da_ref, dd_ref):
                r[...] = jnp.zeros_like(r)

        @pl.when(i == 0)
        def _():
            carry[...] = jnp.zeros_like(carry)

        u = h_ref[0]
        dyp = dy_ref[0] * _gelu_grad(yp_ref[0])
        dd_ref[...] += jnp.sum(dyp * u, axis=0, keepdims=True)
        xr, xi = xs_ref[0, :, :ln], xs_ref[0, :, ln:]
        dc_ref[0, :ln, :] += _dot3(xr, dyp, TN)
        dc_ref[0, ln:, :] += _dot3(xi, dyp, TN)
        lr, li = _dot3(dyp, ct_ref[0, :, :ln]), _dot3(dyp, ct_ref[0, :, ln:])
        for k in range(nlev):
            ar, ai = a_ref[0, k:k + 1, :ln], a_ref[0, k:k + 1, ln:]
            sr, si = _shift_all([lr, li], 1 << k, True)
            lr, li = lr + ar * sr + ai * si, li + ar * si - ai * sr
        cr, ci = carry[0:1, :ln], carry[0:1, ln:]
        pr, pi = pw_ref[0, :, :ln], pw_ref[0, :, ln:]
        lr, li = lr + pr * cr + pi * ci, li + pr * ci - pi * cr
        carry[0:1, :ln] = lr[0:1, :]
        carry[0:1, ln:] = li[0:1, :]
        dh_ref[0] = _dot3(lr, bt_ref[0, :ln, :]) + _dot3(li, bt_ref[0, ln:, :]) + dyp * d_ref[...]
        db_ref[0, :, :ln] += _dot3(u, lr, TN)
        db_ref[0, :, ln:] += _dot3(u, li, TN)
        first = _rows(xr.shape) == 0
        keep = jnp.where(i == nt - 1, 0.0, 1.0)
        xpr = jnp.where(first, xp_ref[0, 7:8, :ln] * keep, pltpu.roll(xr, 1, 0))
        xpi = jnp.where(first, xp_ref[0, 7:8, ln:] * keep, pltpu.roll(xi, 1, 0))
        da_ref[0, 0:1, :ln] += jnp.sum(lr * xpr + li * xpi, axis=0, keepdims=True)
        da_ref[0, 0:1, ln:] += jnp.sum(li * xpr - lr * xpi, axis=0, keepdims=True)

    rev = lambda i: nt - 1 - i
    act = pl.BlockSpec((1, t, 128), lambda q, b, i: (b, rev(i), q))
    xsp = pl.BlockSpec((1, t, 2 * ln), lambda q, b, i: (b, rev(i), q))
    xpp = pl.BlockSpec((1, 8, 2 * ln), lambda q, b, i: (b, jnp.maximum(rev(i) * (t // 8) - 1, 0), q))
    par = lambda r, c: pl.BlockSpec((1, r, c), lambda q, b, i: (q, 0, 0))
    dsp = pl.BlockSpec((1, 128), lambda q, b, i: (0, q))
    return pl.pallas_call(
        body, grid=(S5_CHUNKS, bsz, nt),
        in_specs=[act, act, xsp, xpp, act, par(2 * ln, 128), par(128, 2 * ln), par(nlev, 2 * ln), par(t, 2 * ln), dsp],
        out_specs=[act, par(128, 2 * ln), par(2 * ln, 128), par(8, 2 * ln), dsp],
        out_shape=[jax.ShapeDtypeStruct((bsz, s, d), F32), jax.ShapeDtypeStruct((S5_CHUNKS, 128, 2 * ln), F32),
                   jax.ShapeDtypeStruct((S5_CHUNKS, 2 * ln, 128), F32), jax.ShapeDtypeStruct((S5_CHUNKS, 8, 2 * ln), F32),
                   jax.ShapeDtypeStruct((1, d), F32)],
        scratch_shapes=[pltpu.VMEM((8, 2 * ln), F32)],
        compiler_params=_cp("parallel", "arbitrary", "arbitrary"), name=name)(
            h, ypre, xs, xs, dyg, bmat_t, cmat_t, atab, pw_rev, dskip)


def glu_fwd(z, x, name="s5_glu"):
    m, d = x.shape
    tr = _pick(m, (512, 256, 128))

    def body(z_ref, x_ref, o_ref):
        o_ref[...] = x_ref[...] + z_ref[:, :d] * _sig(z_ref[:, d:])

    return pl.pallas_call(
        body, grid=(m // tr,),
        in_specs=[pl.BlockSpec((tr, 2 * d), lambda i: (i, 0)), pl.BlockSpec((tr, d), lambda i: (i, 0))],
        out_specs=pl.BlockSpec((tr, d), lambda i: (i, 0)), out_shape=jax.ShapeDtypeStruct((m, d), F32),
        compiler_params=_cp("parallel"), name=name)(z, x)


def glu_bwd(z, dout, name="s5_glu_bwd"):
    m, d = dout.shape
    tr = _pick(m, (512, 256, 128))

    def body(z_ref, do_ref, dz_ref, db_ref):
        sg = _sig(z_ref[:, d:])
        dv = do_ref[...] * sg
        dgt = do_ref[...] * z_ref[:, :d] * sg * (1.0 - sg)
        dz_ref[:, :d] = dv.astype(dz_ref.dtype)
        dz_ref[:, d:] = dgt.astype(dz_ref.dtype)

        @pl.when(pl.program_id(0) == 0)
        def _():
            db_ref[...] = jnp.zeros_like(db_ref)

        db_ref[:, :d] += jnp.sum(dv, axis=0, keepdims=True)
        db_ref[:, d:] += jnp.sum(dgt, axis=0, keepdims=True)

    wide = pl.BlockSpec((tr, 2 * d), lambda i: (i, 0))
    return pl.pallas_call(
        body, grid=(m // tr,), in_specs=[wide, pl.BlockSpec((tr, d), lambda i: (i, 0))],
        out_specs=[wide, pl.BlockSpec((1, 2 * d), lambda i: (0, 0))],
        out_shape=[jax.ShapeDtypeStruct((m, 2 * d), BF16), jax.ShapeDtypeStruct((1, 2 * d), F32)],
        compiler_params=_cp("arbitrary"), name=name)(z, dout)


def _s5_tables(p, t):
    nlev = max(1, (t - 1).bit_length())
    lam = (p["s5_lam_re"], p["s5_lam_im"], p["s5_log_dt"])
    atab = _s5_powers(*lam, [1 << k for k in range(nlev)])
    if nlev < 8:
        atab = jnp.pad(atab, ((0, 0), (0, 8 - nlev), (0, 0)))
    pw = _s5_powers(*lam, list(range(1, t + 1)))
    return nlev, atab, pw


def s5_layer_fwd(x, g, p, bsz):
    m, d = x.shape
    s = m // bsz
    t = min(S5_T, s)
    h = rms_fwd(x, g, F32, "s5_norm")
    _, _, bbr, bbi = _s5_discretise(p["s5_lam_re"], p["s5_lam_im"], p["s5_log_dt"], p["s5_b_re"], p["s5_b_im"])
    nlev, atab, pw = _s5_tables(p, t)
    bmat, cmat = _s5_in_matrix(bbr, bbi), _s5_out_matrix(p["s5_c_re"], p["s5_c_im"])
    xs, ypre, yg = s5_fwd(h.reshape(bsz, s, d), bmat, cmat, atab[:, :max(nlev, 8)], pw, p["s5_d"])
    z = matmul(yg.reshape(m, d), p["s5_w_out"], bias=p["s5_b_out"], name="s5_out")
    out = glu_fwd(z, x)
    return out, (x, h, xs, ypre, yg, z, bmat, cmat, atab, pw)


def s5_layer_bwd(saved, dout, g, p, bsz):
    x, h, xs, ypre, yg, z, bmat, cmat, atab, pw = saved
    m, d = x.shape
    s = m // bsz
    pw_rev = _s5_powers(p["s5_lam_re"], p["s5_lam_im"], p["s5_log_dt"], list(range(pw.shape[1], 0, -1)))
    dz, db_out = glu_bwd(z, dout)
    dyg = matmul(dz, p["s5_w_out"], tb=True, name="s5_out_dx")
    dw_out = matmul(yg.reshape(m, d), dz, ta=True, name="s5_out_dw")
    dh, dbm, dcm, dlam, dd = s5_bwd(h.reshape(bsz, s, d), ypre, xs, dyg.reshape(bsz, s, d),
                                    bmat.transpose(0, 2, 1), cmat.transpose(0, 2, 1), atab, pw_rev, p["s5_d"])
    dx, dg = rms_bwd(x, g, dh.reshape(m, d), dout, "s5_norm_bwd")
    dbbr, dbbi = _s5_in_matrix_diag(dbm)
    dc_re, dc_im = _s5_out_matrix_diag(dcm)
    dar = dlam[:, 0, :S5_LANES].reshape(S5_CHUNKS * 8, S5_STATE)
    dai = dlam[:, 0, S5_LANES:].reshape(S5_CHUNKS * 8, S5_STATE)
    _, vjp = jax.vjp(_s5_discretise, p["s5_lam_re"], p["s5_lam_im"], p["s5_log_dt"], p["s5_b_re"], p["s5_b_im"])
    dl_re, dl_im, dldt, db_re, db_im = vjp((dar, dai, dbbr, dbbi))
    return dx, dict(norm_mix_g=dg, s5_lam_re=dl_re[None], s5_lam_im=dl_im[None], s5_log_dt=dldt[None],
                    s5_b_re=db_re[None], s5_b_im=db_im[None], s5_c_re=dc_re[None], s5_c_im=dc_im[None],
                    s5_d=dd, s5_w_out=dw_out, s5_b_out=db_out)


def _log_sigmoid(z):
    return jnp.minimum(z, 0.0) - jnp.log(1.0 + jnp.exp(-jnp.abs(z)))


def _head_norm(t, g_ref):
    r = lax.rsqrt(jnp.mean(t * t, axis=-1, keepdims=True) + EPS)
    th = t * r
    return th * g_ref[...], th, r


def _tri(shape, fn):
    row = lax.broadcasted_iota(jnp.int32, shape, 0)
    col = lax.broadcasted_iota(jnp.int32, shape, 1)
    return fn(row, col)


_SB_SCALE = 1.0 / math.sqrt(SB_DIM)


def _suffix_sums(t, later):
    n = t.shape[1] // SB_CHUNK
    outs, carry = [None] * n, jnp.zeros((t.shape[0], 1), F32)
    for ci in range(n - 1, -1, -1):
        ch = t[:, ci * SB_CHUNK:(ci + 1) * SB_CHUNK]
        outs[ci] = _dot_exact_rhs(ch, later) + carry
        carry = carry + jnp.sum(ch, axis=1, keepdims=True)
    return (outs[0] if n == 1 else jnp.concatenate(outs, axis=1)), carry


def _prefix_sums(t, tri):
    n = t.shape[1] // SB_CHUNK
    outs, carry = [None] * n, jnp.zeros((t.shape[0], 1), F32)
    for ci in range(n):
        ch = t[:, ci * SB_CHUNK:(ci + 1) * SB_CHUNK]
        outs[ci] = _dot_exact_rhs(ch, tri) + carry
        carry = carry + jnp.sum(ch, axis=1, keepdims=True)
    return (outs[0] if n == 1 else jnp.concatenate(outs, axis=1)), carry


def sb_fwd(q, k, v, qg, kg, name="sb_fwd"):
    bsz, nh, s, dh = q.shape
    tb = min(SB_BLOCK, s)
    nq = s // tb

    def body(q_ref, k_ref, v_ref, qg_ref, kg_ref, o_ref, rt_ref):
        qi = pl.program_id(2)
        qn, _, _ = _head_norm(q_ref[0, 0], qg_ref)
        later = _tri((SB_CHUNK, SB_CHUNK), lambda r, c: r > c).astype(BF16)
        causal = _tri((tb, tb), lambda r, c: c < r)

        def block(kb, run, acc, diag):
            ks = pl.ds(pl.multiple_of(kb * tb, tb), tb)
            kn, _, _ = _head_norm(k_ref[0, 0, ks, :], kg_ref)
            z = _dot(qn, kn, NT) * _SB_SCALE
            ls = _log_sigmoid(z)
            lm = ls - z
            if diag:
                lm = jnp.where(causal, lm, 0.0)
            rest, total = _suffix_sums(lm, later)
            att = jnp.exp(ls + run + rest)
            if diag:
                att = jnp.where(causal, att, 0.0)
            return run + total, acc + _dot(att, v_ref[0, 0, ks, :])

        run, acc = block(qi, jnp.zeros((tb, 1), F32), jnp.zeros((tb, dh), F32), True)
        run, acc = lax.fori_loop(0, qi, lambda j, c: block(qi - 1 - j, c[0], c[1], False), (run, acc))
        o_ref[0, 0] = acc
        rt_ref[0, 0] = run

    qsp = pl.BlockSpec((1, 1, tb, dh), lambda b, h, i: (b, h, i, 0))
    rsp = pl.BlockSpec((1, 1, tb, 1), lambda b, h, i: (b, h, i, 0))
    ksp = pl.BlockSpec((1, 1, s, dh), lambda b, h, i: (b, h, 0, 0))
    gsp = pl.BlockSpec((1, dh), lambda b, h, i: (0, 0))
    return pl.pallas_call(
        body, grid=(bsz, nh, nq), in_specs=[qsp, ksp, ksp, gsp, gsp], out_specs=[qsp, rsp],
        out_shape=[jax.ShapeDtypeStruct((bsz, nh, s, dh), F32), jax.ShapeDtypeStruct((bsz, nh, s, 1), F32)],
        compiler_params=_cp("parallel", "parallel", "arbitrary"), name=name)(q, k, v, qg, kg)


def sb_bwd(q, k, v, rtot, do, qg, kg, name="sb_bwd"):
    bsz, nh, s, dh = q.shape
    tb = min(SB_BLOCK, s)
    nq = s // tb

    def body(q_ref, k_ref, v_ref, rt_ref, do_ref, qg_ref, kg_ref, dq_ref, dk_ref, dv_ref, dqg_ref, dkg_ref,
             qn_s, kn_s, dqn_s, dkn_s, dv_s):
        qn, qh, rq = _head_norm(q_ref[0, 0], qg_ref)
        kn, kh, rk = _head_norm(k_ref[0, 0], kg_ref)
        qn_s[...] = qn
        kn_s[...] = kn
        dkn_s[...] = jnp.zeros_like(dkn_s)
        dv_s[...] = jnp.zeros_like(dv_s)
        chunk = (SB_CHUNK, SB_CHUNK)
        upto = _tri(chunk, lambda r, c: r <= c).astype(BF16)
        earlier = _tri(chunk, lambda r, c: r < c).astype(BF16)
        causal = _tri((tb, tb), lambda r, c: c < r)

        def q_block(qi, _):
            qs = pl.ds(pl.multiple_of(qi * tb, tb), tb)
            qnb, dob, rtb = qn_s[qs, :], do_ref[0, 0, qs, :], rt_ref[0, 0, qs, :]

            def block(kb, left, seen, dqn, diag):
                ks = pl.ds(pl.multiple_of(kb * tb, tb), tb)
                knb, vb = kn_s[ks, :], v_ref[0, 0, ks, :]
                z = _dot(qnb, knb, NT) * _SB_SCALE
                ls = _log_sigmoid(z)
                lm = ls - z
                if diag:
                    lm = jnp.where(causal, lm, 0.0)
                through, lm_total = _prefix_sums(lm, upto)
                att = jnp.exp(ls + (rtb - left - through))
                if diag:
                    att = jnp.where(causal, att, 0.0)
                gg = att * _dot(dob, vb, NT)
                before, gg_total = _prefix_sums(gg, earlier)
                sg = jnp.exp(ls)
                dz = gg * (1.0 - sg) - sg * (seen + before)
                if diag:
                    dz = jnp.where(causal, dz, 0.0)
                dz = dz * _SB_SCALE
                dkn_s[ks, :] += _dot(dz, qnb, TN)
                dv_s[ks, :] += _dot(att, dob, TN)
                return left + lm_total, seen + gg_total, dqn + _dot(dz, knb)

            zero = jnp.zeros((tb, 1), F32)
            c = lax.fori_loop(0, qi, lambda kb, c: block(kb, c[0], c[1], c[2], False),
                              (zero, zero, jnp.zeros((tb, dh), F32)))
            c = block(qi, c[0], c[1], c[2], True)
            dqn_s[qs, :] = c[2]
            return 0

        lax.fori_loop(0, nq, q_block, 0)

        @pl.when((pl.program_id(0) == 0) & (pl.program_id(1) == 0))
        def _():
            dqg_ref[...] = jnp.zeros_like(dqg_ref)
            dkg_ref[...] = jnp.zeros_like(dkg_ref)

        def norm_bwd(dn, th, r, g_ref, dt_ref, dg_ref):
            dg_ref[...] += jnp.sum(dn * th, axis=0, keepdims=True)
            dth = dn * g_ref[...]
            dt_ref[0, 0] = r * (dth - th * jnp.mean(dth * th, axis=-1, keepdims=True))

        norm_bwd(dqn_s[...], qh, rq, qg_ref, dq_ref, dqg_ref)
        norm_bwd(dkn_s[...], kh, rk, kg_ref, dk_ref, dkg_ref)
        dv_ref[0, 0] = dv_s[...]

    hsp = pl.BlockSpec((1, 1, s, dh), lambda b, h: (b, h, 0, 0))
    gsp = pl.BlockSpec((1, dh), lambda b, h: (0, 0))
    act = jax.ShapeDtypeStruct((bsz, nh, s, dh), F32)
    gsh = jax.ShapeDtypeStruct((1, dh), F32)
    rsp = pl.BlockSpec((1, 1, s, 1), lambda b, h: (b, h, 0, 0))
    return pl.pallas_call(
        body, grid=(bsz, nh), in_specs=[hsp, hsp, hsp, rsp, hsp, gsp, gsp], out_specs=[hsp, hsp, hsp, gsp, gsp],
        out_shape=[act, act, act, gsh, gsh], scratch_shapes=[pltpu.VMEM((s, dh), F32)] * 5,
        compiler_params=_cp("arbitrary", "arbitrary"), name=name)(q, k, v, rtot, do, qg, kg)


def _to_heads(t, bsz):
    m, w = t.shape
    n = w // (SB_HEADS * SB_DIM)
    t = t.reshape(bsz, m // bsz, n, SB_HEADS, SB_DIM).transpose(2, 0, 3, 1, 4)
    return [t[i] for i in range(n)]


def _from_heads(ts):
    t = jnp.stack(ts, axis=0)
    n, bsz, nh, s, dh = t.shape
    return t.transpose(1, 3, 0, 2, 4).reshape(bsz * s, n * nh * dh)


def sb_layer_fwd(x, g, p, bsz):
    m, d = x.shape
    h = rms_fwd(x, g, BF16, "sb_norm")
    qkv = matmul(h, p["sb_w_qkv"], name="sb_qkv")
    q, k, v = _to_heads(qkv, bsz)
    o, rtot = sb_fwd(q, k, v, p["sb_q_g"], p["sb_k_g"])
    o2 = _from_heads([o])
    out = matmul(o2, p["sb_w_o"], res=x, name="sb_out")
    return out, (x, h, q, k, v, rtot, o2)


def sb_layer_bwd(saved, dout, g, p, bsz):
    x, h, q, k, v, rtot, o2 = saved
    do2 = matmul(dout, p["sb_w_o"], tb=True, name="sb_out_dx")
    dw_o = matmul(o2, dout, ta=True, name="sb_out_dw")
    dq, dk, dv, dqg, dkg = sb_bwd(q, k, v, rtot, _to_heads(do2, bsz)[0], p["sb_q_g"], p["sb_k_g"])
    dqkv = _from_heads([dq, dk, dv])
    dh = matmul(dqkv, p["sb_w_qkv"], tb=True, name="sb_qkv_dx")
    dw_qkv = matmul(h, dqkv, ta=True, name="sb_qkv_dw")
    dx, dg = rms_bwd(x, g, dh, dout, "sb_norm_bwd")
    return dx, dict(norm_mix_g=dg, sb_w_qkv=dw_qkv, sb_q_g=dqg, sb_k_g=dkg, sb_w_o=dw_o)


_CHIP_FLIPS = ((1, 0), (0, 1), (1, 1))
_MESH = pl.DeviceIdType.MESH
_ANY = pl.BlockSpec(memory_space=pl.ANY)


def _flip(v, f):
    return 1 - v if f else v


def _splits(shape):
    return shape[-2] % 32 == 0


def _half(ref, c, rows):
    idx = (slice(None),) * (len(ref.shape) - 2) + (pl.ds(pl.multiple_of(c * (rows // 2), 16), rows // 2),)
    return ref.at[idx]


def gather_weights(shards, name="gather_weights"):
    n = len(shards)
    split = [_splits(s.shape) for s in shards]

    def body(*refs):
        ins, outs = refs[:n], refs[n:2 * n]
        send, recv, fsend, frecv = refs[2 * n:]
        x, y, c = lax.axis_index("x"), lax.axis_index("y"), lax.axis_index("c")
        me = 2 * x + y
        sibling = (x, y, 1 - c)

        def remote(i, j, block):
            px, py = _flip(x, _CHIP_FLIPS[j][0]), _flip(y, _CHIP_FLIPS[j][1])
            rows = shards[i].shape[0]
            src = _half(ins[i], c, rows) if split[i] else ins[i]
            dst = _half(outs[i].at[block], c, rows) if split[i] else outs[i].at[block]
            return pltpu.make_async_remote_copy(
                src_ref=src, dst_ref=dst, send_sem=send.at[3 * i + j], recv_sem=recv.at[3 * i + j],
                device_id=(px, py, c), device_id_type=_MESH)

        def forward(i, j, half):
            rows = _half(outs[i].at[2 * _flip(x, _CHIP_FLIPS[j][0]) + _flip(y, _CHIP_FLIPS[j][1])], half,
                         shards[i].shape[0])
            return pltpu.make_async_remote_copy(
                src_ref=rows, dst_ref=rows, send_sem=fsend.at[3 * i + j], recv_sem=frecv.at[3 * i + j],
                device_id=sibling, device_id_type=_MESH)

        sends = [remote(i, j, me) for i in range(n) for j in range(3)]
        for cp in sends:
            cp.start()
        fwd = []
        for i in range(n):
            for j, (fx, fy) in enumerate(_CHIP_FLIPS):
                remote(i, j, 2 * _flip(x, fx) + _flip(y, fy)).wait_recv()
                if split[i]:
                    fwd.append(forward(i, j, c))
                    fwd[-1].start()
        for i in range(n):
            if split[i]:
                for j in range(3):
                    forward(i, j, 1 - c).wait_recv()
        for cp in sends + fwd:
            cp.wait_send()

    return pl.pallas_call(
        body, in_specs=[_ANY] * n, out_specs=[_ANY] * n,
        out_shape=[jax.ShapeDtypeStruct((N_CHIPS,) + s.shape, s.dtype) for s in shards],
        scratch_shapes=[pltpu.SemaphoreType.DMA((3 * n,))] * 4, name=name)(*shards)


def grad_halves_exchange(parts, name="grad_halves_exchange"):
    n = len(parts)

    def body(*refs):
        ins, got = refs[:n], refs[n:2 * n]
        send, recv = refs[2 * n:]
        x, y, c = lax.axis_index("x"), lax.axis_index("y"), lax.axis_index("c")
        swap = [pltpu.make_async_remote_copy(
            src_ref=_half(ins[i], 1 - c, parts[i].shape[1]), dst_ref=got[i], send_sem=send.at[i], recv_sem=recv.at[i],
            device_id=(x, y, 1 - c), device_id_type=_MESH) for i in range(n)]
        for cp in swap:
            cp.start()
        for cp in swap:
            cp.wait()

    half = [jax.ShapeDtypeStruct((N_CHIPS, p.shape[1] // 2, p.shape[2]), p.dtype) for p in parts]
    return pl.pallas_call(
        body, in_specs=[_ANY] * n, out_specs=[_ANY] * n, out_shape=half,
        scratch_shapes=[pltpu.SemaphoreType.DMA((n,))] * 2, name=name)(*parts)


def pair_sum(full, got, core, out_dtype, name):
    k, r, c = full.shape
    rh = r // 2
    tr = _pick(rh, tuple(t for t in (512, 256, 128, 64, 32, 16) if t * c * 4 <= 1024 * 1024))
    nb = rh // tr

    def body(core_ref, a_ref, b_ref, o_ref):
        o_ref[...] = (a_ref[...] + b_ref[...]).astype(o_ref.dtype)

    blk = pl.BlockSpec((1, tr, c), lambda kk, i, core_ref: (kk, i, 0))
    mine = pl.BlockSpec((1, tr, c), lambda kk, i, core_ref: (kk, core_ref[0] * nb + i, 0))
    return pl.pallas_call(
        body, out_shape=jax.ShapeDtypeStruct((k, rh, c), out_dtype),
        grid_spec=pltpu.PrefetchScalarGridSpec(num_scalar_prefetch=1, grid=(k, nb), in_specs=[mine, blk],
                                               out_specs=blk),
        compiler_params=_cp("parallel", "parallel"), name=name)(core, full, got)


def grad_sync(halves, parts, packed, name="grad_sync"):
    nh, n = len(halves), len(parts)
    nt = nh + n

    def body(*refs):
        hin, ins, pk = refs[:nh], refs[nh:nt], refs[nt]
        outs = refs[nt + 1:]
        landed, mine, theirs, pk_all = outs[:nh], outs[nh:nt], outs[nt:nt + n], outs[nt + n]
        send, recv, loc, fsend, frecv, psend, precv, ploc = outs[nt + n + 1:]
        x, y, c = lax.axis_index("x"), lax.axis_index("y"), lax.axis_index("c")
        me = 2 * x + y
        dev = 4 * x + 2 * y + c
        sibling = (x, y, 1 - c)
        rows = [h.shape[1] * 2 for h in halves]

        def remote(i, j, slot):
            px, py = _flip(x, _CHIP_FLIPS[j][0]), _flip(y, _CHIP_FLIPS[j][1])
            if i < nh:
                src, dst = hin[i].at[2 * px + py], _half(landed[i].at[slot], c, rows[i])
            else:
                src, dst = ins[i - nh].at[2 * px + py], mine[i - nh].at[slot]
            return pltpu.make_async_remote_copy(
                src_ref=src, dst_ref=dst, send_sem=send.at[3 * i + j], recv_sem=recv.at[3 * i + j],
                device_id=(px, py, c), device_id_type=_MESH)

        def packed_to(r, slot):
            px, py, pc = _flip(x, r & 4), _flip(y, r & 2), _flip(c, r & 1)
            return pltpu.make_async_remote_copy(
                src_ref=pk, dst_ref=pk_all.at[slot], send_sem=psend.at[r - 1], recv_sem=precv.at[r - 1],
                device_id=(px, py, pc), device_id_type=_MESH)

        def forward(i, half):
            if i < nh:
                src = dst = _half(landed[i], half, rows[i])
            else:
                src, dst = mine[i - nh], theirs[i - nh]
            return pltpu.make_async_remote_copy(
                src_ref=src, dst_ref=dst, send_sem=fsend.at[i], recv_sem=frecv.at[i],
                device_id=sibling, device_id_type=_MESH)

        local = [pltpu.make_async_copy(hin[i].at[me], _half(landed[i].at[me], c, rows[i]), loc.at[i])
                 for i in range(nh)]
        local += [pltpu.make_async_copy(ins[i].at[me], mine[i].at[me], loc.at[nh + i]) for i in range(n)]
        plocal = pltpu.make_async_copy(pk, pk_all.at[dev], ploc.at[0])
        sends = [remote(i, j, me) for i in range(nt) for j in range(3)]
        psends = [packed_to(r, dev) for r in range(1, N_DEV)]
        for cp in local + [plocal] + sends + psends:
            cp.start()
        fwd = [forward(i, c) for i in range(nt)]
        for i in range(nt):
            for j, (fx, fy) in enumerate(_CHIP_FLIPS):
                remote(i, j, 2 * _flip(x, fx) + _flip(y, fy)).wait_recv()
            local[i].wait()
            fwd[i].start()
        for i in range(nt):
            forward(i, 1 - c).wait_recv()
        for r in range(1, N_DEV):
            packed_to(r, 4 * _flip(x, r & 4) + 2 * _flip(y, r & 2) + _flip(c, r & 1)).wait_recv()
        for cp in sends + psends + fwd:
            cp.wait_send()
        plocal.wait()

    full = [jax.ShapeDtypeStruct((N_CHIPS, 2 * h.shape[1], h.shape[2]), h.dtype) for h in halves]
    land = [jax.ShapeDtypeStruct(p.shape, p.dtype) for p in parts]
    out = pl.pallas_call(
        body, in_specs=[_ANY] * (nt + 1), out_specs=[_ANY] * (nt + n + 1),
        out_shape=full + land + land + [jax.ShapeDtypeStruct((N_DEV,) + packed.shape, packed.dtype)],
        scratch_shapes=[pltpu.SemaphoreType.DMA((3 * nt,)), pltpu.SemaphoreType.DMA((3 * nt,)),
                        pltpu.SemaphoreType.DMA((nt,)), pltpu.SemaphoreType.DMA((nt,)), pltpu.SemaphoreType.DMA((nt,)),
                        pltpu.SemaphoreType.DMA((N_DEV - 1,)), pltpu.SemaphoreType.DMA((N_DEV - 1,)),
                        pltpu.SemaphoreType.DMA((1,))],
        name=name)(*halves, *parts, packed)
    return out[:nh], out[nh:nt], out[nt:nt + n], out[nt + n]


def adamw(w, m, v, parts, name):
    r, c = w.shape
    tr = r
    for cand in (512, 256, 128, 64, 32, 16, 8):
        if r % cand == 0 and cand * c * 4 <= 512 * 1024:
            tr = cand
            break
    np_ = len(parts)
    nslot = parts[0].shape[0]
    bc1 = 1.0 - ADAM_B1 ** ADAM_STEP
    bc2 = 1.0 - ADAM_B2 ** ADAM_STEP

    def body(*refs):
        w_ref, m_ref, v_ref = refs[:3]
        p_refs = refs[3:3 + np_]
        g_ref, d_ref, nm_ref, nv_ref = refs[3 + np_:]
        g = None
        for k in range(nslot):
            t = p_refs[0][k].astype(F32)
            for p_ref in p_refs[1:]:
                t = t + p_ref[k].astype(F32)
            g = t if g is None else g + t
        wv = w_ref[...]
        nm = ADAM_B1 * m_ref[...] + (1.0 - ADAM_B1) * g
        nv = ADAM_B2 * v_ref[...] + (1.0 - ADAM_B2) * (g * g)
        g_ref[...] = g
        nm_ref[...] = nm
        nv_ref[...] = nv
        d_ref[...] = -ADAM_LR * ((nm / bc1) / (jnp.sqrt(nv / bc2) + ADAM_EPS) + ADAM_WD * wv)

    row = pl.BlockSpec((tr, c), lambda i: (i, 0))
    slab = pl.BlockSpec((nslot, tr, c), lambda i: (0, i, 0))
    sh = jax.ShapeDtypeStruct((r, c), F32)
    return pl.pallas_call(
        body, grid=(r // tr,), in_specs=[row, row, row] + [slab] * np_, out_specs=[row] * 4,
        out_shape=[sh] * 4, compiler_params=_cp("parallel"), name=name)(w, m, v, *parts)


WEIGHTS = ["norm_mix_g", "norm_ffn_g", "pool_w", "pool_b", "pool_scale", "s5_lam_re", "s5_lam_im", "s5_log_dt",
           "s5_b_re", "s5_b_im", "s5_c_re", "s5_c_im", "s5_d", "s5_w_out", "s5_b_out", "lru_w_in", "lru_conv_w",
           "lru_conv_b", "lru_w_a", "lru_b_a", "lru_w_x", "lru_b_x", "lru_lam", "lru_w_out", "sb_w_qkv", "sb_q_g",
           "sb_k_g", "sb_w_o", "ffn_w_in", "ffn_conv_w", "ffn_conv_b", "ffn_w_out"]
SHARD_AXIS = dict(pool_w=2, s5_d=1, s5_w_out=2, s5_b_out=1, lru_w_in=2, lru_conv_w=2, lru_conv_b=1, lru_w_a=2,
                  lru_b_a=1, lru_w_x=2, lru_b_x=1, lru_lam=1, lru_w_out=1, sb_w_qkv=2, sb_w_o=1, ffn_w_in=2,
                  ffn_conv_w=2, ffn_w_out=1)
MXU_WEIGHTS = ("pool_w", "s5_w_out", "lru_w_in", "lru_w_a", "lru_w_x", "lru_w_out", "sb_w_qkv", "sb_w_o",
               "ffn_w_in", "ffn_w_out")
SHARDED = [n for n in WEIGHTS if n in SHARD_AXIS]
REPLICATED = [n for n in WEIGHTS if n not in SHARD_AXIS]
PACK_WIDTH = 1024


def _as_rows(a):
    return a.reshape(-1, a.shape[-1])


def _pack(arrays):
    rows = []
    for a in arrays:
        flat = a.reshape(-1)
        pad = (-flat.shape[0]) % PACK_WIDTH
        rows.append(jnp.pad(flat, (0, pad)).reshape(-1, PACK_WIDTH))
    out = jnp.concatenate(rows, axis=0)
    return jnp.pad(out, ((0, (-out.shape[0]) % 8), (0, 0)))


def _unpack(packed, like):
    out, r = [], 0
    for a in like:
        size = math.prod(a.shape)
        nrow = -(-size // PACK_WIDTH)
        out.append(packed[r:r + nrow].reshape(-1)[:size].reshape(a.shape))
        r += nrow
    return out


def kernel(*args):
    names = ["x"] + WEIGHTS + ["loss_target"] + ["m_" + n for n in WEIGHTS] + ["v_" + n for n in WEIGHTS]
    assert len(args) == len(names)
    given = dict(zip(names, args))
    x, target = given["x"], given["loss_target"]
    bsz, seq, d = x.shape
    m_tok = bsz * seq

    shards = [_as_rows(given[n].astype(BF16) if n in MXU_WEIGHTS else given[n]) for n in SHARDED]
    gathered = gather_weights(shards)
    p = {n: given[n] for n in REPLICATED}
    my_chip = 2 * lax.axis_index("x") + lax.axis_index("y")
    for n, g4, own in zip(SHARDED, gathered, shards):
        blocks = g4.reshape((N_CHIPS,) + given[n].shape)
        own = own.reshape(given[n].shape)
        p[n] = jnp.concatenate([jnp.where(my_chip == k, own, blocks[k]) for k in range(N_CHIPS)],
                               axis=SHARD_AXIS[n])

    mix = {k: (v[0] if v.ndim > 2 or k == "s5_log_dt" else v) for k, v in p.items()
           if not k.startswith(("norm_", "ffn_"))}
    mixers = ((pool_layer_fwd, pool_layer_bwd), (s5_layer_fwd, s5_layer_bwd), (lru_layer_fwd, lru_layer_bwd),
              (sb_layer_fwd, sb_layer_bwd))
    ffn_in, ffn_cw, ffn_cb = (ffn_pair_layout(p[k]) for k in ("ffn_w_in", "ffn_conv_w", "ffn_conv_b"))
    ffn_p = lambda l: (p["norm_ffn_g"][l:l + 1], ffn_in[l], ffn_cw[l], ffn_cb[l:l + 1], p["ffn_w_out"][l])

    def mixer_args(l):
        g = p["norm_mix_g"][l:l + 1]
        if l == 0:
            return (g, mix["pool_w"], mix["pool_b"], mix["pool_scale"], bsz)
        return (g, mix, bsz)

    h = x.reshape(m_tok, d)
    saved = []
    for l in range(4):
        h, s_mix = mixers[l][0](h, *mixer_args(l))
        gl, w_in, cw, cb, w_out = ffn_p(l)
        h, s_ffn = ffn_fwd(h, gl, w_in, cw, cb, w_out, bsz)
        saved.append((s_mix, s_ffn))
    dh, loss_part = loss_head(h, target.reshape(m_tok, d))
    loss = lax.psum(jnp.sum(loss_part), ("x", "y", "c"))

    grads = {}
    ffn_g = [None] * 4
    mix_g = [None] * 4
    for l in range(3, -1, -1):
        gl, w_in, cw, cb, w_out = ffn_p(l)
        dh, ffn_g[l] = ffn_bwd(saved[l][1], dh, gl, w_in, cw, cb, w_out, bsz)
        dh, mix_g[l] = mixers[l][1](saved[l][0], dh, *mixer_args(l))
    for k in ("norm_ffn_g", "ffn_w_in", "ffn_conv_w", "ffn_conv_b", "ffn_w_out"):
        grads[k] = jnp.stack([ffn_g[l][k] for l in range(4)]).reshape(p[k].shape)
    grads["norm_mix_g"] = jnp.concatenate([mix_g[l]["norm_mix_g"] for l in range(4)], axis=0)
    for l in range(4):
        for k, v in mix_g[l].items():
            if k != "norm_mix_g":
                grads[k] = v.reshape(p[k].shape)
    grad_x = dh.reshape(bsz, seq, d)

    parts = {}
    for n in SHARDED:
        blocks = jnp.stack(jnp.split(grads[n], N_CHIPS, axis=SHARD_AXIS[n]))
        parts[n] = blocks.reshape(N_CHIPS, -1, blocks.shape[-1])
    big = [n for n in SHARDED if _splits(parts[n].shape)]
    small = [n for n in SHARDED if n not in big]
    got = grad_halves_exchange([parts[n] for n in big])
    core = lax.axis_index("c").astype(jnp.int32).reshape(1)
    halves = [pair_sum(parts[n], b, core, BF16, "pair_sum_" + n) for n, b in zip(big, got)]
    landed, mine, theirs, packed_all = grad_sync(halves, [parts[n] for n in small],
                                                 _pack([grads[n] for n in REPLICATED]))
    summed = {n: [landed[i]] for i, n in enumerate(big)}
    summed.update({n: [mine[i], theirs[i]] for i, n in enumerate(small)})

    out = {}
    for n in SHARDED:
        res = adamw(_as_rows(given[n]), _as_rows(given["m_" + n]), _as_rows(given["v_" + n]), summed[n], "adamw_" + n)
        out[n] = [r.reshape(given[n].shape) for r in res]
    res = adamw(_pack([given[n] for n in REPLICATED]), _pack([given["m_" + n] for n in REPLICATED]),
                _pack([given["v_" + n] for n in REPLICATED]), [packed_all], "adamw_replicated")
    like = [given[n] for n in REPLICATED]
    for n, *vals in zip(REPLICATED, *[_unpack(r, like) for r in res]):
        out[n] = list(vals)
    return (loss, grad_x, *[out[n][0] for n in WEIGHTS], *[out[n][1] for n in WEIGHTS],
            *[out[n][2] for n in WEIGHTS], *[out[n][3] for n in WEIGHTS])
```

```python
import functools
import math

import jax
import jax.numpy as jnp
from jax import lax
from jax.experimental import pallas as pl
from jax.experimental.pallas import tpu as pltpu

F32 = jnp.float32
BF16 = jnp.bfloat16

EPS = 1e-6
N_CHIPS = 4
N_DEV = 8
POOL_WINDOWS = (2, 4, 8, 16)
POOL_GROUP = 256
S5_GROUP = 16
S5_STATE = 64
S5_CHUNKS = 8
S5_LANES = 512
S5_T = 256
LRU_BLOCK = 256
LRU_CONV = 4
LRU_C = 8.0
SB_HEADS = 16
SB_DIM = 64
SB_BLOCK = 512
SB_CHUNK = 128
FFN_CONV = 3
ADAM_LR, ADAM_B1, ADAM_B2, ADAM_EPS, ADAM_WD, ADAM_STEP = 0.001, 0.9, 0.999, 1e-08, 0.01, 10
VMEM_LIMIT_BYTES = 56 * 1024 * 1024
MATMUL_VMEM_BUDGET = 30 * 1024 * 1024
MATMUL_WHOLE_K = 2816

NN = (((1,), (0,)), ((), ()))
NT = (((1,), (1,)), ((), ()))
TN = (((0,), (0,)), ((), ()))


def _cp(*sem):
    return pltpu.CompilerParams(dimension_semantics=sem, vmem_limit_bytes=VMEM_LIMIT_BYTES)


def _pick(n, prefs):
    for p in prefs:
        if n % p == 0:
            return p
    return n


def _dot(a, b, dims=NN):
    return lax.dot_general(a.astype(BF16), b.astype(BF16), dims, preferred_element_type=F32)


def _split(x):
    hi = x.astype(BF16)
    return hi, (x - hi.astype(F32)).astype(BF16)


def _dot3(a, b, dims=NN):
    ah, al = _split(a)
    bh, bl = _split(b)
    d = lambda p, q: lax.dot_general(p, q, dims, preferred_element_type=F32)
    return d(ah, bh) + (d(ah, bl) + d(al, bh))


def _dot_exact_rhs(a, b01):
    ah, al = _split(a)
    d = lambda p: lax.dot_general(p, b01, NN, preferred_element_type=F32)
    return d(ah) + d(al)


def _sig(x):
    return 1.0 / (1.0 + jnp.exp(-x))


def _softplus(x):
    return jnp.maximum(x, 0.0) + jnp.log(1.0 + jnp.exp(-jnp.abs(x)))


_GELU_C = math.sqrt(2.0 / math.pi)


def _gelu(x):
    return 0.5 * x * (1.0 + jnp.tanh(_GELU_C * (x + 0.044715 * x * x * x)))


def _gelu_grad(x):
    th = jnp.tanh(_GELU_C * (x + 0.044715 * x * x * x))
    return 0.5 * (1.0 + th) + 0.5 * x * (1.0 - th * th) * _GELU_C * (1.0 + 3.0 * 0.044715 * x * x)


def _rows(shape):
    return lax.broadcasted_iota(jnp.int32, shape, 0)


SUBLANES = 8


def _shift_all(xs, k, up):
    t = xs[0].shape[0]
    if k >= t:
        return [jnp.zeros_like(x) for x in xs]
    if k % SUBLANES == 0:
        pad = jnp.zeros((k,) + xs[0].shape[1:], xs[0].dtype)
        return [jnp.concatenate([x[k:], pad] if up else [pad, x[:t - k]], axis=0) for x in xs]
    rows = _rows(xs[0].shape)
    keep = rows < t - k if up else rows >= k
    return [jnp.where(keep, pltpu.roll(x, t - k if up else k, 0), 0.0) for x in xs]


def _shift_down(x, k):
    return _shift_all([x], k, False)[0]


def _shift_up(x, k):
    return _shift_all([x], k, True)[0]


def matmul(a, b, *, ta=False, tb=False, bias=None, res=None, out_dtype=F32, name):
    m, k = (a.shape[1], a.shape[0]) if ta else a.shape
    n = b.shape[0] if tb else b.shape[1]
    assert (b.shape[1] if tb else b.shape[0]) == k
    has_bias, has_res = bias is not None, res is not None
    tk = k if k <= MATMUL_WHOLE_K else _pick(k, (1408, 1024, 512, 256, 128))
    nk = k // tk
    sa, sb, so = a.dtype.itemsize, b.dtype.itemsize, jnp.dtype(out_dtype).itemsize
    tm = tn = None
    for cm in (2048, 1024, 512, 1408, 256, 128):
        for cn in (1024, 512, 1408, 256, 128):
            if m % cm or n % cn:
                continue
            need = 2 * cm * tk * sa + 2 * tk * cn * sb + cm * cn * (2 * so + 4 + (4 if nk > 1 else 0)
                                                                  + (8 if has_res else 0))
            if need <= MATMUL_VMEM_BUDGET and (tm is None or cm * cn > tm * tn):
                tm, tn = cm, cn
    assert tm is not None, (m, n, k)
    dims = (((0 if ta else 1,), (1 if tb else 0,)), ((), ()))

    def body(*refs):
        a_ref, b_ref = refs[:2]
        rest = list(refs[2:])
        bias_ref = rest.pop(0) if has_bias else None
        res_ref = rest.pop(0) if has_res else None
        o_ref = rest[0]

        def finish(r):
            if has_bias:
                r = r + bias_ref[...]
            if has_res:
                r = r + res_ref[...]
            o_ref[...] = r.astype(o_ref.dtype)

        part = lax.dot_general(a_ref[...].astype(BF16), b_ref[...].astype(BF16), dims, preferred_element_type=F32)
        if nk == 1:
            finish(part)
            return
        acc_ref = rest[1]
        kk = pl.program_id(2)

        @pl.when(kk == 0)
        def _():
            acc_ref[...] = part

        @pl.when(kk > 0)
        def _():
            acc_ref[...] += part

        @pl.when(kk == nk - 1)
        def _():
            finish(acc_ref[...])

    in_specs = [
        pl.BlockSpec((tk, tm), lambda i, j, kk: (kk, i)) if ta else pl.BlockSpec((tm, tk), lambda i, j, kk: (i, kk)),
        pl.BlockSpec((tn, tk), lambda i, j, kk: (j, kk)) if tb else pl.BlockSpec((tk, tn), lambda i, j, kk: (kk, j)),
    ]
    args = [a, b]
    if has_bias:
        in_specs.append(pl.BlockSpec((1, tn), lambda i, j, kk: (0, j)))
        args.append(bias)
    if has_res:
        in_specs.append(pl.BlockSpec((tm, tn), lambda i, j, kk: (i, j)))
        args.append(res)
    return pl.pallas_call(
        body, grid=(m // tm, n // tn, nk), in_specs=in_specs,
        out_specs=pl.BlockSpec((tm, tn), lambda i, j, kk: (i, j)),
        out_shape=jax.ShapeDtypeStruct((m, n), out_dtype),
        scratch_shapes=[pltpu.VMEM((tm, tn), F32)] if nk > 1 else [],
        compiler_params=_cp("parallel", "parallel", "arbitrary"), name=name)(*args)


def rms_fwd(x, g, out_dtype, name):
    m, d = x.shape
    tr = _pick(m, (512, 256, 128))

    def body(x_ref, g_ref, o_ref):
        xv = x_ref[...]
        r = lax.rsqrt(jnp.mean(xv * xv, axis=-1, keepdims=True) + EPS)
        o_ref[...] = (xv * r * g_ref[...]).astype(o_ref.dtype)

    return pl.pallas_call(
        body, grid=(m // tr,),
        in_specs=[pl.BlockSpec((tr, d), lambda i: (i, 0)), pl.BlockSpec((1, d), lambda i: (0, 0))],
        out_specs=pl.BlockSpec((tr, d), lambda i: (i, 0)),
        out_shape=jax.ShapeDtypeStruct((m, d), out_dtype),
        compiler_params=_cp("parallel"), name=name)(x, g)


def rms_bwd(x, g, dh, dres, name):
    m, d = x.shape
    tr = _pick(m, (512, 256, 128))

    def body(x_ref, g_ref, dh_ref, dres_ref, dx_ref, dg_ref):
        xv = x_ref[...]
        r = lax.rsqrt(jnp.mean(xv * xv, axis=-1, keepdims=True) + EPS)
        xh = xv * r
        dhv = dh_ref[...].astype(F32)
        dxh = dhv * g_ref[...]
        dx_ref[...] = dres_ref[...] + r * (dxh - xh * jnp.mean(dxh * xh, axis=-1, keepdims=True))

        @pl.when(pl.program_id(0) == 0)
        def _():
            dg_ref[...] = jnp.zeros_like(dg_ref)

        dg_ref[...] += jnp.sum(dhv * xh, axis=0, keepdims=True)

    row = pl.BlockSpec((tr, d), lambda i: (i, 0))
    vec = pl.BlockSpec((1, d), lambda i: (0, 0))
    return pl.pallas_call(
        body, grid=(m // tr,), in_specs=[row, vec, row, row], out_specs=[row, vec],
        out_shape=[jax.ShapeDtypeStruct((m, d), F32), jax.ShapeDtypeStruct((1, d), F32)],
        compiler_params=_cp("arbitrary"), name=name)(x, g, dh, dres)


def _conv_taps(u, kw):
    return [_shift_down(u, kw - 1 - k) for k in range(kw - 1)] + [u]


def _conv_fwd(taps, w_ref, b_ref):
    y = b_ref[...] + w_ref[len(taps) - 1:len(taps), :] * taps[-1]
    for k in range(len(taps) - 1):
        y = y + w_ref[k:k + 1, :] * taps[k]
    return y


def _conv_bwd_input(dy, w_ref, kw):
    du = w_ref[kw - 1:kw, :] * dy
    for k in range(kw - 1):
        du = du + w_ref[k:k + 1, :] * _shift_up(dy, kw - 1 - k)
    return du


def _conv_bwd_weight(dy, taps):
    return [jnp.sum(dy * tap, axis=0, keepdims=True) for tap in taps]


def ffn_act_fwd(u, cw, cb, name):
    bsz, s, f2 = u.shape
    f = f2 // 2
    tc = _pick(f, (256, 128))
    nj = f // tc

    def body(uv_ref, ug_ref, wv_ref, wg_ref, bv_ref, bg_ref, a_ref):
        hv = _conv_fwd(_conv_taps(uv_ref[0], FFN_CONV), wv_ref, bv_ref)
        hg = _conv_fwd(_conv_taps(ug_ref[0], FFN_CONV), wg_ref, bg_ref)
        a_ref[0] = (hg * _sig(hg) * hv).astype(a_ref.dtype)

    uv = pl.BlockSpec((1, s, tc), lambda b, j: (b, 0, j))
    ug = pl.BlockSpec((1, s, tc), lambda b, j: (b, 0, j + nj))
    wv = pl.BlockSpec((FFN_CONV, tc), lambda b, j: (0, j))
    wg = pl.BlockSpec((FFN_CONV, tc), lambda b, j: (0, j + nj))
    bv = pl.BlockSpec((1, tc), lambda b, j: (0, j))
    bg = pl.BlockSpec((1, tc), lambda b, j: (0, j + nj))
    return pl.pallas_call(
        body, grid=(bsz, nj), in_specs=[uv, ug, wv, wg, bv, bg], out_specs=uv,
        out_shape=jax.ShapeDtypeStruct((bsz, s, f), BF16),
        compiler_params=_cp("parallel", "parallel"), name=name)(u, u, cw, cw, cb, cb)


def ffn_act_bwd(u, cw, cb, da, name):
    bsz, s, f2 = u.shape
    f = f2 // 2
    tc = _pick(f, (256, 128))
    nj = f // tc

    def body(uv_ref, ug_ref, wv_ref, wg_ref, bv_ref, bg_ref, da_ref,
             duv_ref, dug_ref, dwv_ref, dwg_ref, dbv_ref, dbg_ref):
        tv, tg = _conv_taps(uv_ref[0], FFN_CONV), _conv_taps(ug_ref[0], FFN_CONV)
        hv = _conv_fwd(tv, wv_ref, bv_ref)
        hg = _conv_fwd(tg, wg_ref, bg_ref)
        sg = _sig(hg)
        dav = da_ref[0].astype(F32)
        dhv = dav * hg * sg
        dhg = dav * hv * (sg * (1.0 + hg * (1.0 - sg)))
        duv_ref[0] = _conv_bwd_input(dhv, wv_ref, FFN_CONV).astype(duv_ref.dtype)
        dug_ref[0] = _conv_bwd_input(dhg, wg_ref, FFN_CONV).astype(dug_ref.dtype)

        @pl.when(pl.program_id(1) == 0)
        def _():
            for r in (dwv_ref, dwg_ref, dbv_ref, dbg_ref):
                r[...] = jnp.zeros_like(r)

        for k, row in enumerate(_conv_bwd_weight(dhv, tv)):
            dwv_ref[k:k + 1, :] += row
        for k, row in enumerate(_conv_bwd_weight(dhg, tg)):
            dwg_ref[k:k + 1, :] += row
        dbv_ref[...] += jnp.sum(dhv, axis=0, keepdims=True)
        dbg_ref[...] += jnp.sum(dhg, axis=0, keepdims=True)

    uv = pl.BlockSpec((1, s, tc), lambda j, b: (b, 0, j))
    ug = pl.BlockSpec((1, s, tc), lambda j, b: (b, 0, j + nj))
    wv = pl.BlockSpec((FFN_CONV, tc), lambda j, b: (0, j))
    wg = pl.BlockSpec((FFN_CONV, tc), lambda j, b: (0, j + nj))
    bv = pl.BlockSpec((1, tc), lambda j, b: (0, j))
    bg = pl.BlockSpec((1, tc), lambda j, b: (0, j + nj))
    act = jax.ShapeDtypeStruct((bsz, s, f), BF16)
    return pl.pallas_call(
        body, grid=(nj, bsz), in_specs=[uv, ug, wv, wg, bv, bg, uv],
        out_specs=[uv, uv, wv, wv, bv, bv],
        out_shape=[act, act, jax.ShapeDtypeStruct((FFN_CONV, f), F32), jax.ShapeDtypeStruct((FFN_CONV, f), F32),
                   jax.ShapeDtypeStruct((1, f), F32), jax.ShapeDtypeStruct((1, f), F32)],
        compiler_params=_cp("parallel", "arbitrary"), name=name)(u, u, cw, cw, cb, cb, da)


def ffn_fwd(x, g, w_in, cw, cb, w_out, bsz):
    m, d = x.shape
    h = rms_fwd(x, g, BF16, "ffn_norm")
    u = matmul(h, w_in, name="ffn_in")
    a = ffn_act_fwd(u.reshape(bsz, m // bsz, -1), cw, cb, "ffn_act")
    a2 = a.reshape(m, -1)
    out = matmul(a2, w_out, res=x, name="ffn_out")
    return out, (x, h, u, a2)


def ffn_bwd(saved, dout, g, w_in, cw, cb, w_out, bsz):
    x, h, u, a2 = saved
    m, d = x.shape
    da = matmul(dout, w_out, tb=True, out_dtype=BF16, name="ffn_out_dx")
    dw_out = matmul(a2, dout, ta=True, name="ffn_out_dw")
    u3 = u.reshape(bsz, m // bsz, -1)
    duv, dug, dwv, dwg, dbv, dbg = ffn_act_bwd(u3, cw, cb, da.reshape(bsz, m // bsz, -1), "ffn_act_bwd")
    du = jnp.concatenate([duv, dug], axis=-1).reshape(m, -1)
    dh = matmul(du, w_in, tb=True, name="ffn_in_dx")
    dw_in = matmul(h, du, ta=True, name="ffn_in_dw")
    dx, dg = rms_bwd(x, g, dh, dout, "ffn_norm_bwd")
    grads = dict(norm_ffn_g=dg, ffn_w_in=dw_in, ffn_conv_w=jnp.concatenate([dwv, dwg], axis=-1),
                 ffn_conv_b=jnp.concatenate([dbv, dbg], axis=-1), ffn_w_out=dw_out)
    return dx, grads


def loss_head(y, target, name="loss_head"):
    m, d = y.shape
    tr = _pick(m, (512, 256, 128))

    def body(y_ref, t_ref, dy_ref, l_ref):
        e = y_ref[...] - t_ref[...]
        dy_ref[...] = e * (1.0 / d)

        @pl.when(pl.program_id(0) == 0)
        def _():
            l_ref[...] = jnp.zeros_like(l_ref)

        l_ref[...] += jnp.sum(e * e, axis=0, keepdims=True) * (0.5 / d)

    row = pl.BlockSpec((tr, d), lambda i: (i, 0))
    vec = pl.BlockSpec((1, d), lambda i: (0, 0))
    dy, part = pl.pallas_call(
        body, grid=(m // tr,), in_specs=[row, row], out_specs=[row, vec],
        out_shape=[jax.ShapeDtypeStruct((m, d), F32), jax.ShapeDtypeStruct((1, d), F32)],
        compiler_params=_cp("arbitrary"), name=name)(y, target)
    return dy, part


def _pool_windows(h, gi):
    sums, s, width = [], h, 1
    for _ in POOL_WINDOWS:
        s = s + _shift_down(s, width)
        width *= 2
        sums.append(s)
    pos = _rows(h.shape).astype(F32) + 1.0
    wsum, inv = sums[-1], 1.0 / jnp.minimum(pos, float(POOL_WINDOWS[-1]))
    for k in range(len(POOL_WINDOWS) - 2, -1, -1):
        wsum = jnp.where(gi == k, sums[k], wsum)
        inv = jnp.where(gi == k, 1.0 / jnp.minimum(pos, float(POOL_WINDOWS[k])), inv)
    return wsum * inv - h, inv


def _pool_windows_transpose(e, gi):
    sums, s, width = [], e, 1
    for _ in POOL_WINDOWS:
        s = s + _shift_up(s, width)
        width *= 2
        sums.append(s)
    out = sums[-1]
    for k in range(len(POOL_WINDOWS) - 2, -1, -1):
        out = jnp.where(gi == k, sums[k], out)
    return out


def pool_fwd(h, w, b, scale, x, name="pool_fwd"):
    bsz, s, d = h.shape
    ng = d // POOL_GROUP

    def body(h_ref, w_ref, b_ref, s_ref, x_ref, o_ref):
        dd, _ = _pool_windows(h_ref[0], pl.program_id(1))
        y = _dot(dd, w_ref[0]) + b_ref[...]
        o_ref[0] = x_ref[0] + s_ref[...] * y

    act = pl.BlockSpec((1, s, POOL_GROUP), lambda bb, gi: (bb, 0, gi))
    vec = pl.BlockSpec((1, POOL_GROUP), lambda bb, gi: (0, gi))
    return pl.pallas_call(
        body, grid=(bsz, ng),
        in_specs=[act, pl.BlockSpec((1, POOL_GROUP, POOL_GROUP), lambda bb, gi: (gi, 0, 0)), vec, vec, act],
        out_specs=act, out_shape=jax.ShapeDtypeStruct((bsz, s, d), F32),
        compiler_params=_cp("parallel", "parallel"), name=name)(h, w, b, scale, x)


def pool_bwd(h, w, b, scale, dy, name="pool_bwd"):
    bsz, s, d = h.shape
    ng = d // POOL_GROUP

    def body(h_ref, w_ref, b_ref, s_ref, dy_ref, dh_ref, dw_ref, db_ref, ds_ref):
        gi = pl.program_id(0)
        dd, inv = _pool_windows(h_ref[0], gi)
        ypre = _dot(dd, w_ref[0]) + b_ref[...]
        dyv = dy_ref[0]
        dyb = dyv * s_ref[...]

        @pl.when(pl.program_id(1) == 0)
        def _():
            for r in (dw_ref, db_ref, ds_ref):
                r[...] = jnp.zeros_like(r)

        ds_ref[...] += jnp.sum(dyv * ypre, axis=0, keepdims=True)
        db_ref[...] += jnp.sum(dyb, axis=0, keepdims=True)
        dw_ref[0] += _dot(dd, dyb, TN)
        ddd = _dot(dyb, w_ref[0], NT)
        dh_ref[0] = _pool_windows_transpose(ddd * inv, gi) - ddd

    act = pl.BlockSpec((1, s, POOL_GROUP), lambda gi, bb: (bb, 0, gi))
    vec = pl.BlockSpec((1, POOL_GROUP), lambda gi, bb: (0, gi))
    wsp = pl.BlockSpec((1, POOL_GROUP, POOL_GROUP), lambda gi, bb: (gi, 0, 0))
    return pl.pallas_call(
        body, grid=(ng, bsz), in_specs=[act, wsp, vec, vec, act], out_specs=[act, wsp, vec, vec],
        out_shape=[jax.ShapeDtypeStruct((bsz, s, d), F32), jax.ShapeDtypeStruct((ng, POOL_GROUP, POOL_GROUP), F32),
                   jax.ShapeDtypeStruct((1, d), F32), jax.ShapeDtypeStruct((1, d), F32)],
        compiler_params=_cp("parallel", "arbitrary"), name=name)(h, w, b, scale, dy)


def pool_layer_fwd(x, g, w, b, scale, bsz):
    m, d = x.shape
    h = rms_fwd(x, g, F32, "pool_norm")
    out = pool_fwd(h.reshape(bsz, m // bsz, d), w, b, scale, x.reshape(bsz, m // bsz, d))
    return out.reshape(m, d), (x, h)


def pool_layer_bwd(saved, dout, g, w, b, scale, bsz):
    x, h = saved
    m, d = x.shape
    dh, dw, db, ds = pool_bwd(h.reshape(bsz, m // bsz, d), w, b, scale, dout.reshape(bsz, m // bsz, d))
    dx, dg = rms_bwd(x, g, dh.reshape(m, d), dout, "pool_norm_bwd")
    return dx, dict(norm_mix_g=dg, pool_w=dw[None], pool_b=db, pool_scale=ds)


def _scan_fwd(a, b):
    return _scan(a, b, False)


def _scan(a, b, up):
    k = 1
    while k < a.shape[0]:
        if 2 * k < a.shape[0]:
            sa, sb = _shift_all([a, b], k, up)
            a, b = a * sa, b + a * sb
        else:
            b = b + a * _shift_all([b], k, up)[0]
        k *= 2
    return b


def _scan_bwd(a, b):
    return _scan(a, b, True)


def _neg_expm1(x):
    series = -x * (1.0 + x * (0.5 + x * (1.0 / 6.0 + x * (1.0 / 24.0 + x * (1.0 / 120.0)))))
    return jnp.where(x > -0.03, series, 1.0 - jnp.exp(x))


def _lru_gates(rec, wa_ref, ba_ref, wx_ref, bx_ref, lam_ref):
    r = _sig(_dot(rec, wa_ref[0]) + ba_ref[...])
    i = _sig(_dot(rec, wx_ref[0]) + bx_ref[...])
    sp = _softplus(-lam_ref[...])
    log_a = -LRU_C * r * sp
    a = jnp.exp(log_a)
    mult = jnp.sqrt(_neg_expm1(2.0 * log_a))
    return r, i, sp, a, mult


def lru_fwd(zz, cw, cb, wa, ba, wx, bx, lam, name="lru_fwd"):
    bsz, s, r2 = zz.shape
    rw = r2 // 2
    nb = rw // LRU_BLOCK

    def body(g_ref, p_ref, cw_ref, cb_ref, wa_ref, ba_ref, wx_ref, bx_ref, lam_ref, h_ref, y_ref):
        rec = _conv_fwd(_conv_taps(p_ref[0], LRU_CONV), cw_ref, cb_ref)
        _, i, _, a, mult = _lru_gates(rec, wa_ref, ba_ref, wx_ref, bx_ref, lam_ref)
        hst = _scan_fwd(a, mult * (i * rec))
        h_ref[0] = hst
        y_ref[0] = (_gelu(g_ref[0]) * hst).astype(y_ref.dtype)

    gsp = pl.BlockSpec((1, s, LRU_BLOCK), lambda bb, n: (bb, 0, n))
    psp = pl.BlockSpec((1, s, LRU_BLOCK), lambda bb, n: (bb, 0, n + nb))
    cws = pl.BlockSpec((LRU_CONV, LRU_BLOCK), lambda bb, n: (0, n))
    vec = pl.BlockSpec((1, LRU_BLOCK), lambda bb, n: (0, n))
    wsp = pl.BlockSpec((1, LRU_BLOCK, LRU_BLOCK), lambda bb, n: (n, 0, 0))
    return pl.pallas_call(
        body, grid=(bsz, nb), in_specs=[gsp, psp, cws, vec, wsp, vec, wsp, vec, vec], out_specs=[gsp, gsp],
        out_shape=[jax.ShapeDtypeStruct((bsz, s, rw), F32), jax.ShapeDtypeStruct((bsz, s, rw), BF16)],
        compiler_params=_cp("parallel", "parallel"), name=name)(zz, zz, cw, cb, wa, ba, wx, bx, lam)


def lru_bwd(zz, hst, dy, cw, cb, wa, ba, wx, bx, lam, name="lru_bwd"):
    bsz, s, r2 = zz.shape
    rw = r2 // 2
    nb = rw // LRU_BLOCK

    def body(g_ref, p_ref, h_ref, dy_ref, cw_ref, cb_ref, wa_ref, ba_ref, wx_ref, bx_ref, lam_ref,
             dg_ref, dp_ref, dcw_ref, dcb_ref, dwa_ref, dba_ref, dwx_ref, dbx_ref, dlam_ref):
        pre = p_ref[0]
        taps = _conv_taps(pre, LRU_CONV)
        rec = _conv_fwd(taps, cw_ref, cb_ref)
        r, i, sp, a, mult = _lru_gates(rec, wa_ref, ba_ref, wx_ref, bx_ref, lam_ref)
        hst_v, gate, dyv = h_ref[0], g_ref[0], dy_ref[0]
        dg_ref[0] = (dyv * hst_v * _gelu_grad(gate)).astype(dg_ref.dtype)
        lmb = _scan_bwd(_shift_up(a, 1), dyv * _gelu(gate))
        da = lmb * _shift_down(hst_v, 1)
        dmult = lmb * (i * rec)
        dlog_a = da * a - dmult * (a * a) / mult
        dr = dlog_a * (-LRU_C) * sp
        dra = dr * r * (1.0 - r)
        dxa = lmb * mult * rec * i * (1.0 - i)
        drec = lmb * mult * i + _dot(dra, wa_ref[0], NT) + _dot(dxa, wx_ref[0], NT)
        dp_ref[0] = _conv_bwd_input(drec, cw_ref, LRU_CONV).astype(dp_ref.dtype)

        @pl.when(pl.program_id(1) == 0)
        def _():
            for ref in (dcw_ref, dcb_ref, dwa_ref, dba_ref, dwx_ref, dbx_ref, dlam_ref):
                ref[...] = jnp.zeros_like(ref)

        for k, row in enumerate(_conv_bwd_weight(drec, taps)):
            dcw_ref[k:k + 1, :] += row
        dcb_ref[...] += jnp.sum(drec, axis=0, keepdims=True)
        dwa_ref[0] += _dot(rec, dra, TN)
        dwx_ref[0] += _dot(rec, dxa, TN)
        dba_ref[...] += jnp.sum(dra, axis=0, keepdims=True)
        dbx_ref[...] += jnp.sum(dxa, axis=0, keepdims=True)
        dsp = jnp.sum(dlog_a * (-LRU_C) * r, axis=0, keepdims=True)
        dlam_ref[...] += dsp * (-_sig(-lam_ref[...]))

    gsp = pl.BlockSpec((1, s, LRU_BLOCK), lambda n, bb: (bb, 0, n))
    psp = pl.BlockSpec((1, s, LRU_BLOCK), lambda n, bb: (bb, 0, n + nb))
    cws = pl.BlockSpec((LRU_CONV, LRU_BLOCK), lambda n, bb: (0, n))
    vec = pl.BlockSpec((1, LRU_BLOCK), lambda n, bb: (0, n))
    wsp = pl.BlockSpec((1, LRU_BLOCK, LRU_BLOCK), lambda n, bb: (n, 0, 0))
    act = jax.ShapeDtypeStruct((bsz, s, rw), BF16)
    vsh = jax.ShapeDtypeStruct((1, rw), F32)
    wsh = jax.ShapeDtypeStruct((nb, LRU_BLOCK, LRU_BLOCK), F32)
    return pl.pallas_call(
        body, grid=(nb, bsz), in_specs=[gsp, psp, gsp, gsp, cws, vec, wsp, vec, wsp, vec, vec],
        out_specs=[gsp, gsp, cws, vec, wsp, vec, wsp, vec, vec],
        out_shape=[act, act, jax.ShapeDtypeStruct((LRU_CONV, rw), F32), vsh, wsh, vsh, wsh, vsh, vsh],
        compiler_params=_cp("parallel", "arbitrary"), name=name)(zz, zz, hst, dy, cw, cb, wa, ba, wx, bx, lam)


def lru_layer_fwd(x, g, p, bsz):
    m, d = x.shape
    h = rms_fwd(x, g, BF16, "lru_norm")
    zz = matmul(h, p["lru_w_in"], name="lru_in")
    hst, y = lru_fwd(zz.reshape(bsz, m // bsz, -1), p["lru_conv_w"], p["lru_conv_b"], p["lru_w_a"], p["lru_b_a"],
                     p["lru_w_x"], p["lru_b_x"], p["lru_lam"])
    y2 = y.reshape(m, -1)
    out = matmul(y2, p["lru_w_out"], res=x, name="lru_out")
    return out, (x, h, zz, hst, y2)


def lru_layer_bwd(saved, dout, g, p, bsz):
    x, h, zz, hst, y2 = saved
    m, d = x.shape
    dy = matmul(dout, p["lru_w_out"], tb=True, name="lru_out_dx")
    dw_out = matmul(y2, dout, ta=True, name="lru_out_dw")
    dgate, dpre, dcw, dcb, dwa, dba, dwx, dbx, dlam = lru_bwd(
        zz.reshape(bsz, m // bsz, -1), hst, dy.reshape(bsz, m // bsz, -1), p["lru_conv_w"], p["lru_conv_b"],
        p["lru_w_a"], p["lru_b_a"], p["lru_w_x"], p["lru_b_x"], p["lru_lam"])
    dzz = jnp.concatenate([dgate, dpre], axis=-1).reshape(m, -1)
    dh = matmul(dzz, p["lru_w_in"], tb=True, name="lru_in_dx")
    dw_in = matmul(h, dzz, ta=True, name="lru_in_dw")
    dx, dg = rms_bwd(x, g, dh, dout, "lru_norm_bwd")
    return dx, dict(norm_mix_g=dg, lru_w_in=dw_in, lru_conv_w=dcw[None], lru_conv_b=dcb, lru_w_a=dwa[None],
                    lru_b_a=dba, lru_w_x=dwx[None], lru_b_x=dbx, lru_lam=dlam, lru_w_out=dw_out)


def _s5_discretise(lam_re, lam_im, log_dt, b_re, b_im):
    lr = jnp.minimum(lam_re, -1e-4)
    dt = jnp.exp(log_dt)[:, None]
    mag = jnp.exp(lr * dt)
    ar, ai = mag * jnp.cos(lam_im * dt), mag * jnp.sin(lam_im * dt)
    den = lr * lr + lam_im * lam_im
    cr = ((ar - 1.0) * lr + ai * lam_im) / den
    ci = (ai * lr - (ar - 1.0) * lam_im) / den
    bbr = cr[..., None] * b_re - ci[..., None] * b_im
    bbi = cr[..., None] * b_im + ci[..., None] * b_re
    return ar, ai, bbr, bbi


def _s5_powers(lam_re, lam_im, log_dt, ns):
    lr = jnp.minimum(lam_re, -1e-4)
    dt = jnp.exp(log_dt)[:, None]
    n = jnp.asarray(ns, F32)[:, None, None]
    mag = jnp.exp(n * (lr * dt))
    ang = n * (lam_im * dt)
    to_chunks = lambda t: t.reshape(len(ns), S5_CHUNKS, S5_LANES).transpose(1, 0, 2)
    return jnp.concatenate([to_chunks(mag * jnp.cos(ang)), to_chunks(mag * jnp.sin(ang))], axis=-1)


def _s5_in_matrix(bbr, bbi):
    eye = jnp.eye(8, dtype=F32)
    blk = lambda t: jnp.einsum("qgph,gk->qghkp", t.reshape(S5_CHUNKS, 8, S5_STATE, S5_GROUP), eye).reshape(
        S5_CHUNKS, 128, S5_LANES)
    return jnp.concatenate([blk(bbr), blk(bbi)], axis=-1)


def _s5_in_matrix_diag(dmat):
    eye = jnp.eye(8, dtype=F32)[None, :, None, :, None]
    pick = lambda t: (t.reshape(S5_CHUNKS, 8, S5_GROUP, 8, S5_STATE) * eye).sum(3).transpose(0, 1, 3, 2).reshape(
        S5_CHUNKS * 8, S5_STATE, S5_GROUP)
    return pick(dmat[..., :S5_LANES]), pick(dmat[..., S5_LANES:])


def _s5_out_matrix(c_re, c_im):
    eye = jnp.eye(8, dtype=F32)
    blk = lambda t: jnp.einsum("qghp,gk->qgpkh", t.reshape(S5_CHUNKS, 8, S5_GROUP, S5_STATE), eye).reshape(
        S5_CHUNKS, S5_LANES, 128)
    return jnp.concatenate([blk(c_re), -blk(c_im)], axis=1)


def _s5_out_matrix_diag(dmat):
    eye = jnp.eye(8, dtype=F32)[None, :, None, :, None]
    pick = lambda t: (t.reshape(S5_CHUNKS, 8, S5_STATE, 8, S5_GROUP) * eye).sum(3).transpose(0, 1, 3, 2).reshape(
        S5_CHUNKS * 8, S5_GROUP, S5_STATE)
    return pick(dmat[:, :S5_LANES]), -pick(dmat[:, S5_LANES:])


def s5_fwd(h, bmat, cmat, atab, pw, dskip, later_shards=(), name="s5_fwd"):
    bsz, s, d = h.shape
    t = min(S5_T, s)
    nt, nlev, ln = s // t, atab.shape[1], S5_LANES
    ng = len(later_shards)

    def body(*refs):
        h_ref, b_ref, c_ref, a_ref, pw_ref, d_ref = refs[:6]
        xs_ref, yp_ref, yg_ref = refs[6 + ng:9 + ng]
        carry = refs[9 + 2 * ng]
        if ng:
            start, finish = _gather_steps([g.shape for g in later_shards], refs[6:6 + ng], refs[9 + ng:9 + 2 * ng],
                                          refs[10 + 2 * ng:])
            step = (pl.program_id(0) * S5_CHUNKS + pl.program_id(1)) * nt + pl.program_id(2)
            pl.when(step == 0)(start)

        @pl.when(pl.program_id(2) == 0)
        def _():
            carry[...] = jnp.zeros_like(carry)

        u = h_ref[0]
        bu = _dot3(u, b_ref[0])
        xr, xi = bu[:, :ln], bu[:, ln:]
        for k in range(nlev):
            ar, ai = a_ref[0, k:k + 1, :ln], a_ref[0, k:k + 1, ln:]
            sr, si = _shift_all([xr, xi], 1 << k, False)
            xr, xi = xr + ar * sr - ai * si, xi + ar * si + ai * sr
        cr, ci = carry[0:1, :ln], carry[0:1, ln:]
        pr, pi = pw_ref[0, :, :ln], pw_ref[0, :, ln:]
        xr, xi = xr + pr * cr - pi * ci, xi + pr * ci + pi * cr
        carry[0:1, :ln] = xr[t - 1:t, :]
        carry[0:1, ln:] = xi[t - 1:t, :]
        xs_ref[0, :, :ln] = xr
        xs_ref[0, :, ln:] = xi
        y = _dot3(xr, c_ref[0, :ln, :]) + _dot3(xi, c_ref[0, ln:, :]) + d_ref[...] * u
        yp_ref[0] = y
        yg_ref[0] = _gelu(y).astype(yg_ref.dtype)
        if ng:
            pl.when(step == bsz * S5_CHUNKS * nt - 1)(finish)

    act = pl.BlockSpec((1, t, 128), lambda b, q, i: (b, i, q))
    par = lambda r, c: pl.BlockSpec((1, r, c), lambda b, q, i: (q, 0, 0))
    out = pl.pallas_call(
        body, grid=(bsz, S5_CHUNKS, nt),
        in_specs=[act, par(128, 2 * ln), par(2 * ln, 128), par(nlev, 2 * ln), par(t, 2 * ln),
                  pl.BlockSpec((1, 128), lambda b, q, i: (0, q))] + [_ANY] * ng,
        out_specs=[pl.BlockSpec((1, t, 2 * ln), lambda b, q, i: (b, i, q)), act, act] + [_ANY] * ng,
        out_shape=[jax.ShapeDtypeStruct((bsz, s, S5_CHUNKS * 2 * ln), F32), jax.ShapeDtypeStruct((bsz, s, d), F32),
                   jax.ShapeDtypeStruct((bsz, s, d), BF16)] + _gather_out_shapes(later_shards),
        scratch_shapes=[pltpu.VMEM((8, 2 * ln), F32)] + (_gather_semaphores(ng) if ng else []),
        compiler_params=_cp("arbitrary", "arbitrary", "arbitrary"), name=name)(
            h, bmat, cmat, atab, pw, dskip, *later_shards)
    return out[0], out[1], out[2], out[3:]


def s5_bwd(h, ypre, xs, dyg, bmat_t, cmat_t, atab, pw_rev, dskip, name="s5_bwd"):
    bsz, s, d = h.shape
    t = min(S5_T, s)
    nt, nlev, ln = s // t, atab.shape[1], S5_LANES

    def body(h_ref, yp_ref, xs_ref, xp_ref, dy_ref, bt_ref, ct_ref, a_ref, pw_ref, d_ref,
             dh_ref, db_ref, dc_ref, da_ref, dd_ref, carry):
        b, i = pl.program_id(1), pl.program_id(2)

        @pl.when((b == 0) & (i == 0))
        def _():
            for r in (db_ref, dc_ref, da_ref, dd_ref):
                r[...] = jnp.zeros_like(r)

        @pl.when(i == 0)
        def _():
            carry[...] = jnp.zeros_like(carry)

        u = h_ref[0]
        dyp = dy_ref[0] * _gelu_grad(yp_ref[0])
        dd_ref[...] += jnp.sum(dyp * u, axis=0, keepdims=True)
        xr, xi = xs_ref[0, :, :ln], xs_ref[0, :, ln:]
        dc_ref[0, :ln, :] += _dot3(xr, dyp, TN)
        dc_ref[0, ln:, :] += _dot3(xi, dyp, TN)
        lr, li = _dot3(dyp, ct_ref[0, :, :ln]), _dot3(dyp, ct_ref[0, :, ln:])
        for k in range(nlev):
            ar, ai = a_ref[0, k:k + 1, :ln], a_ref[0, k:k + 1, ln:]
            sr, si = _shift_all([lr, li], 1 << k, True)
            lr, li = lr + ar * sr + ai * si, li + ar * si - ai * sr
        cr, ci = carry[0:1, :ln], carry[0:1, ln:]
        pr, pi = pw_ref[0, :, :ln], pw_ref[0, :, ln:]
        lr, li = lr + pr * cr + pi * ci, li + pr * ci - pi * cr
        carry[0:1, :ln] = lr[0:1, :]
        carry[0:1, ln:] = li[0:1, :]
        dh_ref[0] = _dot3(lr, bt_ref[0, :ln, :]) + _dot3(li, bt_ref[0, ln:, :]) + dyp * d_ref[...]
        db_ref[0, :, :ln] += _dot3(u, lr, TN)
        db_ref[0, :, ln:] += _dot3(u, li, TN)
        first = _rows(xr.shape) == 0
        keep = jnp.where(i == nt - 1, 0.0, 1.0)
        xpr = jnp.where(first, xp_ref[0, 7:8, :ln] * keep, pltpu.roll(xr, 1, 0))
        xpi = jnp.where(first, xp_ref[0, 7:8, ln:] * keep, pltpu.roll(xi, 1, 0))
        da_ref[0, 0:1, :ln] += jnp.sum(lr * xpr + li * xpi, axis=0, keepdims=True)
        da_ref[0, 0:1, ln:] += jnp.sum(li * xpr - lr * xpi, axis=0, keepdims=True)

    rev = lambda i: nt - 1 - i
    act = pl.BlockSpec((1, t, 128), lambda q, b, i: (b, rev(i), q))
    xsp = pl.BlockSpec((1, t, 2 * ln), lambda q, b, i: (b, rev(i), q))
    xpp = pl.BlockSpec((1, 8, 2 * ln), lambda q, b, i: (b, jnp.maximum(rev(i) * (t // 8) - 1, 0), q))
    par = lambda r, c: pl.BlockSpec((1, r, c), lambda q, b, i: (q, 0, 0))
    dsp = pl.BlockSpec((1, 128), lambda q, b, i: (0, q))
    return pl.pallas_call(
        body, grid=(S5_CHUNKS, bsz, nt),
        in_specs=[act, act, xsp, xpp, act, par(2 * ln, 128), par(128, 2 * ln), par(nlev, 2 * ln), par(t, 2 * ln), dsp],
        out_specs=[act, par(128, 2 * ln), par(2 * ln, 128), par(8, 2 * ln), dsp],
        out_shape=[jax.ShapeDtypeStruct((bsz, s, d), F32), jax.ShapeDtypeStruct((S5_CHUNKS, 128, 2 * ln), F32),
                   jax.ShapeDtypeStruct((S5_CHUNKS, 2 * ln, 128), F32), jax.ShapeDtypeStruct((S5_CHUNKS, 8, 2 * ln), F32),
                   jax.ShapeDtypeStruct((1, d), F32)],
        scratch_shapes=[pltpu.VMEM((8, 2 * ln), F32)],
        compiler_params=_cp("parallel", "arbitrary", "arbitrary"), name=name)(
            h, ypre, xs, xs, dyg, bmat_t, cmat_t, atab, pw_rev, dskip)


def glu_fwd(z, x, name="s5_glu"):
    m, d = x.shape
    tr = _pick(m, (512, 256, 128))

    def body(z_ref, x_ref, o_ref):
        o_ref[...] = x_ref[...] + z_ref[:, :d] * _sig(z_ref[:, d:])

    return pl.pallas_call(
        body, grid=(m // tr,),
        in_specs=[pl.BlockSpec((tr, 2 * d), lambda i: (i, 0)), pl.BlockSpec((tr, d), lambda i: (i, 0))],
        out_specs=pl.BlockSpec((tr, d), lambda i: (i, 0)), out_shape=jax.ShapeDtypeStruct((m, d), F32),
        compiler_params=_cp("parallel"), name=name)(z, x)


def glu_bwd(z, dout, name="s5_glu_bwd"):
    m, d = dout.shape
    tr = _pick(m, (512, 256, 128))

    def body(z_ref, do_ref, dz_ref, db_ref):
        sg = _sig(z_ref[:, d:])
        dv = do_ref[...] * sg
        dgt = do_ref[...] * z_ref[:, :d] * sg * (1.0 - sg)
        dz_ref[:, :d] = dv.astype(dz_ref.dtype)
        dz_ref[:, d:] = dgt.astype(dz_ref.dtype)

        @pl.when(pl.program_id(0) == 0)
        def _():
            db_ref[...] = jnp.zeros_like(db_ref)

        db_ref[:, :d] += jnp.sum(dv, axis=0, keepdims=True)
        db_ref[:, d:] += jnp.sum(dgt, axis=0, keepdims=True)

    wide = pl.BlockSpec((tr, 2 * d), lambda i: (i, 0))
    return pl.pallas_call(
        body, grid=(m // tr,), in_specs=[wide, pl.BlockSpec((tr, d), lambda i: (i, 0))],
        out_specs=[wide, pl.BlockSpec((1, 2 * d), lambda i: (0, 0))],
        out_shape=[jax.ShapeDtypeStruct((m, 2 * d), BF16), jax.ShapeDtypeStruct((1, 2 * d), F32)],
        compiler_params=_cp("arbitrary"), name=name)(z, dout)


def _s5_tables(p, t):
    nlev = max(1, (t - 1).bit_length())
    lam = (p["s5_lam_re"], p["s5_lam_im"], p["s5_log_dt"])
    atab = _s5_powers(*lam, [1 << k for k in range(nlev)])
    if nlev < 8:
        atab = jnp.pad(atab, ((0, 0), (0, 8 - nlev), (0, 0)))
    pw = _s5_powers(*lam, list(range(1, t + 1)))
    return nlev, atab, pw


def s5_layer_fwd(x, g, p, bsz, later_shards=()):
    m, d = x.shape
    s = m // bsz
    t = min(S5_T, s)
    h = rms_fwd(x, g, F32, "s5_norm")
    _, _, bbr, bbi = _s5_discretise(p["s5_lam_re"], p["s5_lam_im"], p["s5_log_dt"], p["s5_b_re"], p["s5_b_im"])
    nlev, atab, pw = _s5_tables(p, t)
    bmat, cmat = _s5_in_matrix(bbr, bbi), _s5_out_matrix(p["s5_c_re"], p["s5_c_im"])
    xs, ypre, yg, gathered = s5_fwd(h.reshape(bsz, s, d), bmat, cmat, atab[:, :max(nlev, 8)], pw, p["s5_d"],
                                    later_shards)
    z = matmul(yg.reshape(m, d), p["s5_w_out"], bias=p["s5_b_out"], name="s5_out")
    out = glu_fwd(z, x)
    return out, (x, h, xs, ypre, yg, z, bmat, cmat, atab, pw), gathered


def s5_layer_bwd(saved, dout, g, p, bsz):
    x, h, xs, ypre, yg, z, bmat, cmat, atab, pw = saved
    m, d = x.shape
    s = m // bsz
    pw_rev = _s5_powers(p["s5_lam_re"], p["s5_lam_im"], p["s5_log_dt"], list(range(pw.shape[1], 0, -1)))
    dz, db_out = glu_bwd(z, dout)
    dyg = matmul(dz, p["s5_w_out"], tb=True, name="s5_out_dx")
    dw_out = matmul(yg.reshape(m, d), dz, ta=True, name="s5_out_dw")
    dh, dbm, dcm, dlam, dd = s5_bwd(h.reshape(bsz, s, d), ypre, xs, dyg.reshape(bsz, s, d),
                                    bmat.transpose(0, 2, 1), cmat.transpose(0, 2, 1), atab, pw_rev, p["s5_d"])
    dx, dg = rms_bwd(x, g, dh.reshape(m, d), dout, "s5_norm_bwd")
    dbbr, dbbi = _s5_in_matrix_diag(dbm)
    dc_re, dc_im = _s5_out_matrix_diag(dcm)
    dar = dlam[:, 0, :S5_LANES].reshape(S5_CHUNKS * 8, S5_STATE)
    dai = dlam[:, 0, S5_LANES:].reshape(S5_CHUNKS * 8, S5_STATE)
    _, vjp = jax.vjp(_s5_discretise, p["s5_lam_re"], p["s5_lam_im"], p["s5_log_dt"], p["s5_b_re"], p["s5_b_im"])
    dl_re, dl_im, dldt, db_re, db_im = vjp((dar, dai, dbbr, dbbi))
    return dx, dict(norm_mix_g=dg, s5_lam_re=dl_re[None], s5_lam_im=dl_im[None], s5_log_dt=dldt[None],
                    s5_b_re=db_re[None], s5_b_im=db_im[None], s5_c_re=dc_re[None], s5_c_im=dc_im[None],
                    s5_d=dd, s5_w_out=dw_out, s5_b_out=db_out)


def _log_sigmoid(z):
    return jnp.minimum(z, 0.0) - jnp.log(1.0 + jnp.exp(-jnp.abs(z)))


def _head_norm(t, g_ref):
    r = lax.rsqrt(jnp.mean(t * t, axis=-1, keepdims=True) + EPS)
    th = t * r
    return th * g_ref[...], th, r


def _tri(shape, fn):
    row = lax.broadcasted_iota(jnp.int32, shape, 0)
    col = lax.broadcasted_iota(jnp.int32, shape, 1)
    return fn(row, col)


_SB_SCALE = 1.0 / math.sqrt(SB_DIM)


def _suffix_sums(t, later):
    n = t.shape[1] // SB_CHUNK
    outs, carry = [None] * n, jnp.zeros((t.shape[0], 1), F32)
    for ci in range(n - 1, -1, -1):
        ch = t[:, ci * SB_CHUNK:(ci + 1) * SB_CHUNK]
        outs[ci] = _dot_exact_rhs(ch, later) + carry
        carry = carry + jnp.sum(ch, axis=1, keepdims=True)
    return (outs[0] if n == 1 else jnp.concatenate(outs, axis=1)), carry


def _prefix_sums(t, tri):
    n = t.shape[1] // SB_CHUNK
    outs, carry = [None] * n, jnp.zeros((t.shape[0], 1), F32)
    for ci in range(n):
        ch = t[:, ci * SB_CHUNK:(ci + 1) * SB_CHUNK]
        outs[ci] = _dot_exact_rhs(ch, tri) + carry
        carry = carry + jnp.sum(ch, axis=1, keepdims=True)
    return (outs[0] if n == 1 else jnp.concatenate(outs, axis=1)), carry


def sb_fwd(q, k, v, qg, kg, name="sb_fwd"):
    bsz, nh, s, dh = q.shape
    tb = min(SB_BLOCK, s)
    nq = s // tb

    def body(q_ref, k_ref, v_ref, qg_ref, kg_ref, o_ref, rt_ref):
        qi = pl.program_id(2)
        qn, _, _ = _head_norm(q_ref[0, 0], qg_ref)
        later = _tri((SB_CHUNK, SB_CHUNK), lambda r, c: r > c).astype(BF16)
        causal = _tri((tb, tb), lambda r, c: c < r)

        def block(kb, run, acc, diag):
            ks = pl.ds(pl.multiple_of(kb * tb, tb), tb)
            kn, _, _ = _head_norm(k_ref[0, 0, ks, :], kg_ref)
            z = _dot(qn, kn, NT) * _SB_SCALE
            ls = _log_sigmoid(z)
            lm = ls - z
            if diag:
                lm = jnp.where(causal, lm, 0.0)
            rest, total = _suffix_sums(lm, later)
            att = jnp.exp(ls + run + rest)
            if diag:
                att = jnp.where(causal, att, 0.0)
            return run + total, acc + _dot(att, v_ref[0, 0, ks, :])

        run, acc = block(qi, jnp.zeros((tb, 1), F32), jnp.zeros((tb, dh), F32), True)
        run, acc = lax.fori_loop(0, qi, lambda j, c: block(qi - 1 - j, c[0], c[1], False), (run, acc))
        o_ref[0, 0] = acc
        rt_ref[0, 0] = run

    qsp = pl.BlockSpec((1, 1, tb, dh), lambda b, h, i: (b, h, i, 0))
    rsp = pl.BlockSpec((1, 1, tb, 1), lambda b, h, i: (b, h, i, 0))
    ksp = pl.BlockSpec((1, 1, s, dh), lambda b, h, i: (b, h, 0, 0))
    gsp = pl.BlockSpec((1, dh), lambda b, h, i: (0, 0))
    return pl.pallas_call(
        body, grid=(bsz, nh, nq), in_specs=[qsp, ksp, ksp, gsp, gsp], out_specs=[qsp, rsp],
        out_shape=[jax.ShapeDtypeStruct((bsz, nh, s, dh), F32), jax.ShapeDtypeStruct((bsz, nh, s, 1), F32)],
        compiler_params=_cp("parallel", "parallel", "arbitrary"), name=name)(q, k, v, qg, kg)


def sb_bwd(q, k, v, rtot, do, qg, kg, name="sb_bwd"):
    bsz, nh, s, dh = q.shape
    tb = min(SB_BLOCK, s)
    nq = s // tb

    def body(q_ref, k_ref, v_ref, rt_ref, do_ref, qg_ref, kg_ref, dq_ref, dk_ref, dv_ref, dqg_ref, dkg_ref,
             qn_s, kn_s, dqn_s, dkn_s, dv_s):
        qn, qh, rq = _head_norm(q_ref[0, 0], qg_ref)
        kn, kh, rk = _head_norm(k_ref[0, 0], kg_ref)
        qn_s[...] = qn
        kn_s[...] = kn
        dkn_s[...] = jnp.zeros_like(dkn_s)
        dv_s[...] = jnp.zeros_like(dv_s)
        chunk = (SB_CHUNK, SB_CHUNK)
        upto = _tri(chunk, lambda r, c: r <= c).astype(BF16)
        earlier = _tri(chunk, lambda r, c: r < c).astype(BF16)
        causal = _tri((tb, tb), lambda r, c: c < r)

        def q_block(qi, _):
            qs = pl.ds(pl.multiple_of(qi * tb, tb), tb)
            qnb, dob, rtb = qn_s[qs, :], do_ref[0, 0, qs, :], rt_ref[0, 0, qs, :]

            def block(kb, left, seen, dqn, diag):
                ks = pl.ds(pl.multiple_of(kb * tb, tb), tb)
                knb, vb = kn_s[ks, :], v_ref[0, 0, ks, :]
                z = _dot(qnb, knb, NT) * _SB_SCALE
                ls = _log_sigmoid(z)
                lm = ls - z
                if diag:
                    lm = jnp.where(causal, lm, 0.0)
                through, lm_total = _prefix_sums(lm, upto)
                att = jnp.exp(ls + (rtb - left - through))
                if diag:
                    att = jnp.where(causal, att, 0.0)
                gg = att * _dot(dob, vb, NT)
                before, gg_total = _prefix_sums(gg, earlier)
                sg = jnp.exp(ls)
                dz = gg * (1.0 - sg) - sg * (seen + before)
                if diag:
                    dz = jnp.where(causal, dz, 0.0)
                dz = dz * _SB_SCALE
                dkn_s[ks, :] += _dot(dz, qnb, TN)
                dv_s[ks, :] += _dot(att, dob, TN)
                return left + lm_total, seen + gg_total, dqn + _dot(dz, knb)

            zero = jnp.zeros((tb, 1), F32)
            c = lax.fori_loop(0, qi, lambda kb, c: block(kb, c[0], c[1], c[2], False),
                              (zero, zero, jnp.zeros((tb, dh), F32)))
            c = block(qi, c[0], c[1], c[2], True)
            dqn_s[qs, :] = c[2]
            return 0

        lax.fori_loop(0, nq, q_block, 0)

        @pl.when((pl.program_id(0) == 0) & (pl.program_id(1) == 0))
        def _():
            dqg_ref[...] = jnp.zeros_like(dqg_ref)
            dkg_ref[...] = jnp.zeros_like(dkg_ref)

        def norm_bwd(dn, th, r, g_ref, dt_ref, dg_ref):
            dg_ref[...] += jnp.sum(dn * th, axis=0, keepdims=True)
            dth = dn * g_ref[...]
            dt_ref[0, 0] = r * (dth - th * jnp.mean(dth * th, axis=-1, keepdims=True))

        norm_bwd(dqn_s[...], qh, rq, qg_ref, dq_ref, dqg_ref)
        norm_bwd(dkn_s[...], kh, rk, kg_ref, dk_ref, dkg_ref)
        dv_ref[0, 0] = dv_s[...]

    hsp = pl.BlockSpec((1, 1, s, dh), lambda b, h: (b, h, 0, 0))
    gsp = pl.BlockSpec((1, dh), lambda b, h: (0, 0))
    act = jax.ShapeDtypeStruct((bsz, nh, s, dh), F32)
    gsh = jax.ShapeDtypeStruct((1, dh), F32)
    rsp = pl.BlockSpec((1, 1, s, 1), lambda b, h: (b, h, 0, 0))
    return pl.pallas_call(
        body, grid=(bsz, nh), in_specs=[hsp, hsp, hsp, rsp, hsp, gsp, gsp], out_specs=[hsp, hsp, hsp, gsp, gsp],
        out_shape=[act, act, act, gsh, gsh], scratch_shapes=[pltpu.VMEM((s, dh), F32)] * 5,
        compiler_params=_cp("arbitrary", "arbitrary"), name=name)(q, k, v, rtot, do, qg, kg)


def _to_heads(t, bsz):
    m, w = t.shape
    n = w // (SB_HEADS * SB_DIM)
    t = t.reshape(bsz, m // bsz, n, SB_HEADS, SB_DIM).transpose(2, 0, 3, 1, 4)
    return [t[i] for i in range(n)]


def _from_heads(ts):
    t = jnp.stack(ts, axis=0)
    n, bsz, nh, s, dh = t.shape
    return t.transpose(1, 3, 0, 2, 4).reshape(bsz * s, n * nh * dh)


def sb_layer_fwd(x, g, p, bsz):
    m, d = x.shape
    h = rms_fwd(x, g, BF16, "sb_norm")
    qkv = matmul(h, p["sb_w_qkv"], name="sb_qkv")
    q, k, v = _to_heads(qkv, bsz)
    o, rtot = sb_fwd(q, k, v, p["sb_q_g"], p["sb_k_g"])
    o2 = _from_heads([o])
    out = matmul(o2, p["sb_w_o"], res=x, name="sb_out")
    return out, (x, h, q, k, v, rtot, o2)


def sb_layer_bwd(saved, dout, g, p, bsz):
    x, h, q, k, v, rtot, o2 = saved
    do2 = matmul(dout, p["sb_w_o"], tb=True, name="sb_out_dx")
    dw_o = matmul(o2, dout, ta=True, name="sb_out_dw")
    dq, dk, dv, dqg, dkg = sb_bwd(q, k, v, rtot, _to_heads(do2, bsz)[0], p["sb_q_g"], p["sb_k_g"])
    dqkv = _from_heads([dq, dk, dv])
    dh = matmul(dqkv, p["sb_w_qkv"], tb=True, name="sb_qkv_dx")
    dw_qkv = matmul(h, dqkv, ta=True, name="sb_qkv_dw")
    dx, dg = rms_bwd(x, g, dh, dout, "sb_norm_bwd")
    return dx, dict(norm_mix_g=dg, sb_w_qkv=dw_qkv, sb_q_g=dqg, sb_k_g=dkg, sb_w_o=dw_o)


_CHIP_FLIPS = ((1, 0), (0, 1), (1, 1))
_MESH = pl.DeviceIdType.MESH
_ANY = pl.BlockSpec(memory_space=pl.ANY)


def _flip(v, f):
    return 1 - v if f else v


def _splits(shape):
    return shape[-2] % 32 == 0


def _half(ref, c, rows):
    idx = (slice(None),) * (len(ref.shape) - 2) + (pl.ds(pl.multiple_of(c * (rows // 2), 16), rows // 2),)
    return ref.at[idx]


def gather_weights(shards, name="gather_weights"):
    n = len(shards)

    def body(*refs):
        start, finish = _gather_steps([s.shape for s in shards], refs[:n], refs[n:2 * n], refs[2 * n:])
        start()
        finish()

    return pl.pallas_call(
        body, in_specs=[_ANY] * n, out_specs=[_ANY] * n, out_shape=_gather_out_shapes(shards),
        scratch_shapes=_gather_semaphores(n), name=name)(*shards)


def _gather_out_shapes(shards):
    return [jax.ShapeDtypeStruct((N_CHIPS,) + s.shape, s.dtype) for s in shards]


def _gather_semaphores(n):
    return [pltpu.SemaphoreType.DMA((3 * n,))] * 4


def _gather_steps(shapes, ins, outs, sems):
    n = len(shapes)
    split = [_splits(s) for s in shapes]
    send, recv, fsend, frecv = sems
    x, y, c = lax.axis_index("x"), lax.axis_index("y"), lax.axis_index("c")
    me = 2 * x + y
    sibling = (x, y, 1 - c)

    def remote(i, j, block):
        px, py = _flip(x, _CHIP_FLIPS[j][0]), _flip(y, _CHIP_FLIPS[j][1])
        rows = shapes[i][0]
        src = _half(ins[i], c, rows) if split[i] else ins[i]
        dst = _half(outs[i].at[block], c, rows) if split[i] else outs[i].at[block]
        return pltpu.make_async_remote_copy(
            src_ref=src, dst_ref=dst, send_sem=send.at[3 * i + j], recv_sem=recv.at[3 * i + j],
            device_id=(px, py, c), device_id_type=_MESH)

    def forward(i, j, half):
        rows = _half(outs[i].at[2 * _flip(x, _CHIP_FLIPS[j][0]) + _flip(y, _CHIP_FLIPS[j][1])], half, shapes[i][0])
        return pltpu.make_async_remote_copy(
            src_ref=rows, dst_ref=rows, send_sem=fsend.at[3 * i + j], recv_sem=frecv.at[3 * i + j],
            device_id=sibling, device_id_type=_MESH)

    def start():
        for i in range(n):
            for j in range(3):
                remote(i, j, me).start()

    def finish():
        for i in range(n):
            for j, (fx, fy) in enumerate(_CHIP_FLIPS):
                remote(i, j, 2 * _flip(x, fx) + _flip(y, fy)).wait_recv()
                if split[i]:
                    forward(i, j, c).start()
        for i in range(n):
            for j in range(3):
                if split[i]:
                    forward(i, j, 1 - c).wait_recv()
                    forward(i, j, c).wait_send()
                remote(i, j, me).wait_send()

    return start, finish


def grad_halves_exchange(parts, name="grad_halves_exchange"):
    n = len(parts)

    def body(*refs):
        ins, got = refs[:n], refs[n:2 * n]
        send, recv = refs[2 * n:]
        x, y, c = lax.axis_index("x"), lax.axis_index("y"), lax.axis_index("c")
        swap = [pltpu.make_async_remote_copy(
            src_ref=_half(ins[i], 1 - c, parts[i].shape[1]), dst_ref=got[i], send_sem=send.at[i], recv_sem=recv.at[i],
            device_id=(x, y, 1 - c), device_id_type=_MESH) for i in range(n)]
        for cp in swap:
            cp.start()
        for cp in swap:
            cp.wait()

    half = [jax.ShapeDtypeStruct((N_CHIPS, p.shape[1] // 2, p.shape[2]), p.dtype) for p in parts]
    return pl.pallas_call(
        body, in_specs=[_ANY] * n, out_specs=[_ANY] * n, out_shape=half,
        scratch_shapes=[pltpu.SemaphoreType.DMA((n,))] * 2, name=name)(*parts)


def pair_sum(full, got, core, out_dtype, name):
    k, r, c = full.shape
    rh = r // 2
    tr = _pick(rh, tuple(t for t in (512, 256, 128, 64, 32, 16) if t * c * 4 <= 1024 * 1024))
    nb = rh // tr

    def body(core_ref, a_ref, b_ref, o_ref):
        o_ref[...] = (a_ref[...] + b_ref[...]).astype(o_ref.dtype)

    blk = pl.BlockSpec((1, tr, c), lambda kk, i, core_ref: (kk, i, 0))
    mine = pl.BlockSpec((1, tr, c), lambda kk, i, core_ref: (kk, core_ref[0] * nb + i, 0))
    return pl.pallas_call(
        body, out_shape=jax.ShapeDtypeStruct((k, rh, c), out_dtype),
        grid_spec=pltpu.PrefetchScalarGridSpec(num_scalar_prefetch=1, grid=(k, nb), in_specs=[mine, blk],
                                               out_specs=blk),
        compiler_params=_cp("parallel", "parallel"), name=name)(core, full, got)


def grad_sync(halves, parts, packed, name="grad_sync"):
    nh, n = len(halves), len(parts)
    nt = nh + n

    def body(*refs):
        hin, ins, pk = refs[:nh], refs[nh:nt], refs[nt]
        outs = refs[nt + 1:]
        landed, mine, theirs, pk_all = outs[:nh], outs[nh:nt], outs[nt:nt + n], outs[nt + n]
        send, recv, loc, fsend, frecv, psend, precv, ploc = outs[nt + n + 1:]
        x, y, c = lax.axis_index("x"), lax.axis_index("y"), lax.axis_index("c")
        me = 2 * x + y
        dev = 4 * x + 2 * y + c
        sibling = (x, y, 1 - c)
        rows = [h.shape[1] * 2 for h in halves]

        def remote(i, j, slot):
            px, py = _flip(x, _CHIP_FLIPS[j][0]), _flip(y, _CHIP_FLIPS[j][1])
            if i < nh:
                src, dst = hin[i].at[2 * px + py], _half(landed[i].at[slot], c, rows[i])
            else:
                src, dst = ins[i - nh].at[2 * px + py], mine[i - nh].at[slot]
            return pltpu.make_async_remote_copy(
                src_ref=src, dst_ref=dst, send_sem=send.at[3 * i + j], recv_sem=recv.at[3 * i + j],
                device_id=(px, py, c), device_id_type=_MESH)

        def packed_to(r, slot):
            px, py, pc = _flip(x, r & 4), _flip(y, r & 2), _flip(c, r & 1)
            return pltpu.make_async_remote_copy(
                src_ref=pk, dst_ref=pk_all.at[slot], send_sem=psend.at[r - 1], recv_sem=precv.at[r - 1],
                device_id=(px, py, pc), device_id_type=_MESH)

        def forward(i, half):
            if i < nh:
                src = dst = _half(landed[i], half, rows[i])
            else:
                src, dst = mine[i - nh], theirs[i - nh]
            return pltpu.make_async_remote_copy(
                src_ref=src, dst_ref=dst, send_sem=fsend.at[i], recv_sem=frecv.at[i],
                device_id=sibling, device_id_type=_MESH)

        local = [pltpu.make_async_copy(hin[i].at[me], _half(landed[i].at[me], c, rows[i]), loc.at[i])
                 for i in range(nh)]
        local += [pltpu.make_async_copy(ins[i].at[me], mine[i].at[me], loc.at[nh + i]) for i in range(n)]
        plocal = pltpu.make_async_copy(pk, pk_all.at[dev], ploc.at[0])
        sends = [remote(i, j, me) for i in range(nt) for j in range(3)]
        psends = [packed_to(r, dev) for r in range(1, N_DEV)]
        for cp in local + [plocal] + sends + psends:
            cp.start()
        fwd = [forward(i, c) for i in range(nt)]
        for i in range(nt):
            for j, (fx, fy) in enumerate(_CHIP_FLIPS):
                remote(i, j, 2 * _flip(x, fx) + _flip(y, fy)).wait_recv()
            local[i].wait()
            fwd[i].start()
        for i in range(nt):
            forward(i, 1 - c).wait_recv()
        for r in range(1, N_DEV):
            packed_to(r, 4 * _flip(x, r & 4) + 2 * _flip(y, r & 2) + _flip(c, r & 1)).wait_recv()
        for cp in sends + psends + fwd:
            cp.wait_send()
        plocal.wait()

    full = [jax.ShapeDtypeStruct((N_CHIPS, 2 * h.shape[1], h.shape[2]), h.dtype) for h in halves]
    land = [jax.ShapeDtypeStruct(p.shape, p.dtype) for p in parts]
    out = pl.pallas_call(
        body, in_specs=[_ANY] * (nt + 1), out_specs=[_ANY] * (nt + n + 1),
        out_shape=full + land + land + [jax.ShapeDtypeStruct((N_DEV,) + packed.shape, packed.dtype)],
        scratch_shapes=[pltpu.SemaphoreType.DMA((3 * nt,)), pltpu.SemaphoreType.DMA((3 * nt,)),
                        pltpu.SemaphoreType.DMA((nt,)), pltpu.SemaphoreType.DMA((nt,)), pltpu.SemaphoreType.DMA((nt,)),
                        pltpu.SemaphoreType.DMA((N_DEV - 1,)), pltpu.SemaphoreType.DMA((N_DEV - 1,)),
                        pltpu.SemaphoreType.DMA((1,))],
        name=name)(*halves, *parts, packed)
    return out[:nh], out[nh:nt], out[nt:nt + n], out[nt + n]


def adamw(w, m, v, parts, name):
    r, c = w.shape
    tr = r
    for cand in (512, 256, 128, 64, 32, 16, 8):
        if r % cand == 0 and cand * c * 4 <= 512 * 1024:
            tr = cand
            break
    np_ = len(parts)
    nslot = parts[0].shape[0]
    bc1 = 1.0 - ADAM_B1 ** ADAM_STEP
    bc2 = 1.0 - ADAM_B2 ** ADAM_STEP

    def body(*refs):
        w_ref, m_ref, v_ref = refs[:3]
        p_refs = refs[3:3 + np_]
        g_ref, d_ref, nm_ref, nv_ref = refs[3 + np_:]
        g = None
        for k in range(nslot):
            t = p_refs[0][k].astype(F32)
            for p_ref in p_refs[1:]:
                t = t + p_ref[k].astype(F32)
            g = t if g is None else g + t
        wv = w_ref[...]
        nm = ADAM_B1 * m_ref[...] + (1.0 - ADAM_B1) * g
        nv = ADAM_B2 * v_ref[...] + (1.0 - ADAM_B2) * (g * g)
        g_ref[...] = g
        nm_ref[...] = nm
        nv_ref[...] = nv
        d_ref[...] = -ADAM_LR * ((nm / bc1) / (jnp.sqrt(nv / bc2) + ADAM_EPS) + ADAM_WD * wv)

    row = pl.BlockSpec((tr, c), lambda i: (i, 0))
    slab = pl.BlockSpec((nslot, tr, c), lambda i: (0, i, 0))
    sh = jax.ShapeDtypeStruct((r, c), F32)
    return pl.pallas_call(
        body, grid=(r // tr,), in_specs=[row, row, row] + [slab] * np_, out_specs=[row] * 4,
        out_shape=[sh] * 4, compiler_params=_cp("parallel"), name=name)(w, m, v, *parts)


WEIGHTS = ["norm_mix_g", "norm_ffn_g", "pool_w", "pool_b", "pool_scale", "s5_lam_re", "s5_lam_im", "s5_log_dt",
           "s5_b_re", "s5_b_im", "s5_c_re", "s5_c_im", "s5_d", "s5_w_out", "s5_b_out", "lru_w_in", "lru_conv_w",
           "lru_conv_b", "lru_w_a", "lru_b_a", "lru_w_x", "lru_b_x", "lru_lam", "lru_w_out", "sb_w_qkv", "sb_q_g",
           "sb_k_g", "sb_w_o", "ffn_w_in", "ffn_conv_w", "ffn_conv_b", "ffn_w_out"]
SHARD_AXIS = dict(pool_w=2, s5_d=1, s5_w_out=2, s5_b_out=1, lru_w_in=2, lru_conv_w=2, lru_conv_b=1, lru_w_a=2,
                  lru_b_a=1, lru_w_x=2, lru_b_x=1, lru_lam=1, lru_w_out=1, sb_w_qkv=2, sb_w_o=1, ffn_w_in=2,
                  ffn_conv_w=2, ffn_w_out=1)
MXU_WEIGHTS = ("pool_w", "s5_w_out", "lru_w_in", "lru_w_a", "lru_w_x", "lru_w_out", "sb_w_qkv", "sb_w_o",
               "ffn_w_in", "ffn_w_out")
SHARDED = [n for n in WEIGHTS if n in SHARD_AXIS]
REPLICATED = [n for n in WEIGHTS if n not in SHARD_AXIS]
PACK_WIDTH = 1024


def _as_rows(a):
    return a.reshape(-1, a.shape[-1])


def _pack(arrays):
    rows = []
    for a in arrays:
        flat = a.reshape(-1)
        pad = (-flat.shape[0]) % PACK_WIDTH
        rows.append(jnp.pad(flat, (0, pad)).reshape(-1, PACK_WIDTH))
    out = jnp.concatenate(rows, axis=0)
    return jnp.pad(out, ((0, (-out.shape[0]) % 8), (0, 0)))


def _unpack(packed, like):
    out, r = [], 0
    for a in like:
        size = math.prod(a.shape)
        nrow = -(-size // PACK_WIDTH)
        out.append(packed[r:r + nrow].reshape(-1)[:size].reshape(a.shape))
        r += nrow
    return out


def kernel(*args):
    names = ["x"] + WEIGHTS + ["loss_target"] + ["m_" + n for n in WEIGHTS] + ["v_" + n for n in WEIGHTS]
    assert len(args) == len(names)
    given = dict(zip(names, args))
    x, target = given["x"], given["loss_target"]
    bsz, seq, d = x.shape
    m_tok = bsz * seq

    pieces = {}
    for n in SHARDED:
        a = given[n].astype(BF16) if n in MXU_WEIGHTS else given[n]
        if n in ("ffn_w_in", "ffn_w_out"):
            for l in range(4):
                pieces[(n, l)] = (a[l], a[l].shape, SHARD_AXIS[n] - 1)
        else:
            pieces[(n, None)] = (_as_rows(a), a.shape, SHARD_AXIS[n])
    early = [k for k in pieces if k[0] in ("pool_w", "s5_w_out") or k[1] == 0 or not _splits(pieces[k][0].shape)]
    late = [k for k in pieces if k not in early]
    p = {n: given[n] for n in REPLICATED}
    p.update(ffn_w_in=[None] * 4, ffn_w_out=[None] * 4)
    my_chip = 2 * lax.axis_index("x") + lax.axis_index("y")

    def assemble(keys, gathered):
        for key, g4 in zip(keys, gathered):
            own, shape, axis = pieces[key]
            blocks = g4.reshape((N_CHIPS,) + shape)
            own = own.reshape(shape)
            full = jnp.concatenate([jnp.where(my_chip == k, own, blocks[k]) for k in range(N_CHIPS)], axis=axis)
            if key[1] is None:
                p[key[0]] = full
            else:
                p[key[0]][key[1]] = full

    assemble(early, gather_weights([pieces[k][0] for k in early]))

    def mixer_params():
        return {k: (v[0] if v.ndim > 2 or k == "s5_log_dt" else v) for k, v in p.items()
                if not k.startswith(("norm_", "ffn_"))}

    mixers = ((pool_layer_fwd, pool_layer_bwd), (s5_layer_fwd, s5_layer_bwd), (lru_layer_fwd, lru_layer_bwd),
              (sb_layer_fwd, sb_layer_bwd))
    ffn_p = lambda l: (p["norm_ffn_g"][l:l + 1], p["ffn_w_in"][l], p["ffn_conv_w"][l], p["ffn_conv_b"][l:l + 1],
                       p["ffn_w_out"][l])

    def mixer_args(l):
        g = p["norm_mix_g"][l:l + 1]
        if l == 0:
            return (g, mix["pool_w"], mix["pool_b"], mix["pool_scale"], bsz)
        return (g, mix, bsz)

    h = x.reshape(m_tok, d)
    saved = []
    mix = mixer_params()
    for l in range(4):
        if l == 1:
            h, s_mix, gathered = s5_layer_fwd(h, *mixer_args(l), later_shards=[pieces[k][0] for k in late])
            assemble(late, gathered)
            mix = mixer_params()
        else:
            h, s_mix = mixers[l][0](h, *mixer_args(l))
        gl, w_in, cw, cb, w_out = ffn_p(l)
        h, s_ffn = ffn_fwd(h, gl, w_in, cw, cb, w_out, bsz)
        saved.append((s_mix, s_ffn))
    dh, loss_part = loss_head(h, target.reshape(m_tok, d))
    loss = lax.psum(jnp.sum(loss_part), ("x", "y", "c"))

    grads = {}
    ffn_g = [None] * 4
    mix_g = [None] * 4
    for l in range(3, -1, -1):
        gl, w_in, cw, cb, w_out = ffn_p(l)
        dh, ffn_g[l] = ffn_bwd(saved[l][1], dh, gl, w_in, cw, cb, w_out, bsz)
        dh, mix_g[l] = mixers[l][1](saved[l][0], dh, *mixer_args(l))
    def full_shape(n):
        shape = list(given[n].shape)
        if n in SHARD_AXIS:
            shape[SHARD_AXIS[n]] *= N_CHIPS
        return tuple(shape)

    for k in ("norm_ffn_g", "ffn_w_in", "ffn_conv_w", "ffn_conv_b", "ffn_w_out"):
        grads[k] = jnp.stack([ffn_g[l][k] for l in range(4)]).reshape(full_shape(k))
    grads["norm_mix_g"] = jnp.concatenate([mix_g[l]["norm_mix_g"] for l in range(4)], axis=0)
    for l in range(4):
        for k, v in mix_g[l].items():
            if k != "norm_mix_g":
                grads[k] = v.reshape(full_shape(k))
    grad_x = dh.reshape(bsz, seq, d)

    parts = {}
    for n in SHARDED:
        blocks = jnp.stack(jnp.split(grads[n], N_CHIPS, axis=SHARD_AXIS[n]))
        parts[n] = blocks.reshape(N_CHIPS, -1, blocks.shape[-1])
    big = [n for n in SHARDED if _splits(parts[n].shape)]
    small = [n for n in SHARDED if n not in big]
    got = grad_halves_exchange([parts[n] for n in big])
    core = lax.axis_index("c").astype(jnp.int32).reshape(1)
    halves = [pair_sum(parts[n], b, core, BF16, "pair_sum_" + n) for n, b in zip(big, got)]
    landed, mine, theirs, packed_all = grad_sync(halves, [parts[n] for n in small],
                                                 _pack([grads[n] for n in REPLICATED]))
    summed = {n: [landed[i]] for i, n in enumerate(big)}
    summed.update({n: [mine[i], theirs[i]] for i, n in enumerate(small)})

    out = {}
    for n in SHARDED:
        res = adamw(_as_rows(given[n]), _as_rows(given["m_" + n]), _as_rows(given["v_" + n]), summed[n], "adamw_" + n)
        out[n] = [r.reshape(given[n].shape) for r in res]
    res = adamw(_pack([given[n] for n in REPLICATED]), _pack([given["m_" + n] for n in REPLICATED]),
                _pack([given["v_" + n] for n in REPLICATED]), [packed_all], "adamw_replicated")
    like = [given[n] for n in REPLICATED]
    for n, *vals in zip(REPLICATED, *[_unpack(r, like) for r in res]):
        out[n] = list(vals)
    return (loss, grad_x, *[out[n][0] for n in WEIGHTS], *[out[n][1] for n in WEIGHTS],
            *[out[n][2] for n in WEIGHTS], *[out[n][3] for n in WEIGHTS])
```

```python
import functools
import math

import jax
import jax.numpy as jnp
from jax import lax
from jax.experimental import pallas as pl
from jax.experimental.pallas import tpu as pltpu

F32 = jnp.float32
BF16 = jnp.bfloat16

EPS = 1e-6
N_CHIPS = 4
N_DEV = 8
POOL_WINDOWS = (2, 4, 8, 16)
POOL_GROUP = 256
S5_GROUP = 16
S5_STATE = 64
S5_CHUNKS = 8
S5_LANES = 512
S5_T = 256
LRU_BLOCK = 256
LRU_CONV = 4
LRU_C = 8.0
SB_HEADS = 16
SB_DIM = 64
SB_BLOCK = 512
SB_CHUNK = 128
FFN_CONV = 3
ADAM_LR, ADAM_B1, ADAM_B2, ADAM_EPS, ADAM_WD, ADAM_STEP = 0.001, 0.9, 0.999, 1e-08, 0.01, 10
VMEM_LIMIT_BYTES = 56 * 1024 * 1024
MATMUL_VMEM_BUDGET = 30 * 1024 * 1024
MATMUL_WHOLE_K = 2816

NN = (((1,), (0,)), ((), ()))
NT = (((1,), (1,)), ((), ()))
TN = (((0,), (0,)), ((), ()))


def _cp(*sem):
    return pltpu.CompilerParams(dimension_semantics=sem, vmem_limit_bytes=VMEM_LIMIT_BYTES)


def _pick(n, prefs):
    for p in prefs:
        if n % p == 0:
            return p
    return n


def _dot(a, b, dims=NN):
    return lax.dot_general(a.astype(BF16), b.astype(BF16), dims, preferred_element_type=F32)


def _split(x):
    hi = x.astype(BF16)
    return hi, (x - hi.astype(F32)).astype(BF16)


def _dot3(a, b, dims=NN):
    ah, al = _split(a)
    bh, bl = _split(b)
    d = lambda p, q: lax.dot_general(p, q, dims, preferred_element_type=F32)
    return d(ah, bh) + (d(ah, bl) + d(al, bh))


def _dot_exact_rhs(a, b01):
    ah, al = _split(a)
    d = lambda p: lax.dot_general(p, b01, NN, preferred_element_type=F32)
    return d(ah) + d(al)


def _sig(x):
    return 1.0 / (1.0 + jnp.exp(-x))


def _softplus(x):
    return jnp.maximum(x, 0.0) + jnp.log(1.0 + jnp.exp(-jnp.abs(x)))


_GELU_C = math.sqrt(2.0 / math.pi)


def _gelu(x):
    return 0.5 * x * (1.0 + jnp.tanh(_GELU_C * (x + 0.044715 * x * x * x)))


def _gelu_grad(x):
    th = jnp.tanh(_GELU_C * (x + 0.044715 * x * x * x))
    return 0.5 * (1.0 + th) + 0.5 * x * (1.0 - th * th) * _GELU_C * (1.0 + 3.0 * 0.044715 * x * x)


def _rows(shape):
    return lax.broadcasted_iota(jnp.int32, shape, 0)


SUBLANES = 8


def _shift_all(xs, k, up):
    t = xs[0].shape[0]
    if k >= t:
        return [jnp.zeros_like(x) for x in xs]
    if k % SUBLANES == 0:
        pad = jnp.zeros((k,) + xs[0].shape[1:], xs[0].dtype)
        return [jnp.concatenate([x[k:], pad] if up else [pad, x[:t - k]], axis=0) for x in xs]
    rows = _rows(xs[0].shape)
    keep = rows < t - k if up else rows >= k
    return [jnp.where(keep, pltpu.roll(x, t - k if up else k, 0), 0.0) for x in xs]


def _shift_down(x, k):
    return _shift_all([x], k, False)[0]


def _shift_up(x, k):
    return _shift_all([x], k, True)[0]


def matmul(a, b, *, ta=False, tb=False, bias=None, res=None, out_dtype=F32, name):
    m, k = (a.shape[1], a.shape[0]) if ta else a.shape
    n = b.shape[0] if tb else b.shape[1]
    assert (b.shape[1] if tb else b.shape[0]) == k
    has_bias, has_res = bias is not None, res is not None
    tk = k if k <= MATMUL_WHOLE_K else _pick(k, (1408, 1024, 512, 256, 128))
    nk = k // tk
    sa, sb, so = a.dtype.itemsize, b.dtype.itemsize, jnp.dtype(out_dtype).itemsize
    tm = tn = None
    for cm in (2048, 1024, 512, 1408, 256, 128):
        for cn in (1024, 512, 1408, 256, 128):
            if m % cm or n % cn:
                continue
            need = 2 * cm * tk * sa + 2 * tk * cn * sb + cm * cn * (2 * so + 4 + (4 if nk > 1 else 0)
                                                                  + (8 if has_res else 0))
            if need <= MATMUL_VMEM_BUDGET and (tm is None or cm * cn > tm * tn):
                tm, tn = cm, cn
    assert tm is not None, (m, n, k)
    dims = (((0 if ta else 1,), (1 if tb else 0,)), ((), ()))

    def body(*refs):
        a_ref, b_ref = refs[:2]
        rest = list(refs[2:])
        bias_ref = rest.pop(0) if has_bias else None
        res_ref = rest.pop(0) if has_res else None
        o_ref = rest[0]

        def finish(r):
            if has_bias:
                r = r + bias_ref[...]
            if has_res:
                r = r + res_ref[...]
            o_ref[...] = r.astype(o_ref.dtype)

        part = lax.dot_general(a_ref[...].astype(BF16), b_ref[...].astype(BF16), dims, preferred_element_type=F32)
        if nk == 1:
            finish(part)
            return
        acc_ref = rest[1]
        kk = pl.program_id(2)

        @pl.when(kk == 0)
        def _():
            acc_ref[...] = part

        @pl.when(kk > 0)
        def _():
            acc_ref[...] += part

        @pl.when(kk == nk - 1)
        def _():
            finish(acc_ref[...])

    in_specs = [
        pl.BlockSpec((tk, tm), lambda i, j, kk: (kk, i)) if ta else pl.BlockSpec((tm, tk), lambda i, j, kk: (i, kk)),
        pl.BlockSpec((tn, tk), lambda i, j, kk: (j, kk)) if tb else pl.BlockSpec((tk, tn), lambda i, j, kk: (kk, j)),
    ]
    args = [a, b]
    if has_bias:
        in_specs.append(pl.BlockSpec((1, tn), lambda i, j, kk: (0, j)))
        args.append(bias)
    if has_res:
        in_specs.append(pl.BlockSpec((tm, tn), lambda i, j, kk: (i, j)))
        args.append(res)
    return pl.pallas_call(
        body, grid=(m // tm, n // tn, nk), in_specs=in_specs,
        out_specs=pl.BlockSpec((tm, tn), lambda i, j, kk: (i, j)),
        out_shape=jax.ShapeDtypeStruct((m, n), out_dtype),
        scratch_shapes=[pltpu.VMEM((tm, tn), F32)] if nk > 1 else [],
        compiler_params=_cp("parallel", "parallel", "arbitrary"), name=name)(*args)


def rms_fwd(x, g, out_dtype, name):
    m, d = x.shape
    tr = _pick(m, (512, 256, 128))

    def body(x_ref, g_ref, o_ref):
        xv = x_ref[...]
        r = lax.rsqrt(jnp.mean(xv * xv, axis=-1, keepdims=True) + EPS)
        o_ref[...] = (xv * r * g_ref[...]).astype(o_ref.dtype)

    return pl.pallas_call(
        body, grid=(m // tr,),
        in_specs=[pl.BlockSpec((tr, d), lambda i: (i, 0)), pl.BlockSpec((1, d), lambda i: (0, 0))],
        out_specs=pl.BlockSpec((tr, d), lambda i: (i, 0)),
        out_shape=jax.ShapeDtypeStruct((m, d), out_dtype),
        compiler_params=_cp("parallel"), name=name)(x, g)


def rms_bwd(x, g, dh, dres, name):
    m, d = x.shape
    tr = _pick(m, (512, 256, 128))

    def body(x_ref, g_ref, dh_ref, dres_ref, dx_ref, dg_ref):
        xv = x_ref[...]
        r = lax.rsqrt(jnp.mean(xv * xv, axis=-1, keepdims=True) + EPS)
        xh = xv * r
        dhv = dh_ref[...].astype(F32)
        dxh = dhv * g_ref[...]
        dx_ref[...] = dres_ref[...] + r * (dxh - xh * jnp.mean(dxh * xh, axis=-1, keepdims=True))

        @pl.when(pl.program_id(0) == 0)
        def _():
            dg_ref[...] = jnp.zeros_like(dg_ref)

        dg_ref[...] += jnp.sum(dhv * xh, axis=0, keepdims=True)

    row = pl.BlockSpec((tr, d), lambda i: (i, 0))
    vec = pl.BlockSpec((1, d), lambda i: (0, 0))
    return pl.pallas_call(
        body, grid=(m // tr,), in_specs=[row, vec, row, row], out_specs=[row, vec],
        out_shape=[jax.ShapeDtypeStruct((m, d), F32), jax.ShapeDtypeStruct((1, d), F32)],
        compiler_params=_cp("arbitrary"), name=name)(x, g, dh, dres)


def _conv_taps(u, kw):
    return [_shift_down(u, kw - 1 - k) for k in range(kw - 1)] + [u]


def _conv_fwd(taps, w_ref, b_ref):
    y = b_ref[...] + w_ref[len(taps) - 1:len(taps), :] * taps[-1]
    for k in range(len(taps) - 1):
        y = y + w_ref[k:k + 1, :] * taps[k]
    return y


def _conv_bwd_input(dy, w_ref, kw):
    du = w_ref[kw - 1:kw, :] * dy
    for k in range(kw - 1):
        du = du + w_ref[k:k + 1, :] * _shift_up(dy, kw - 1 - k)
    return du


def _conv_bwd_weight(dy, taps):
    return [jnp.sum(dy * tap, axis=0, keepdims=True) for tap in taps]


def ffn_act_fwd(u, cw, cb, name):
    bsz, s, f2 = u.shape
    f = f2 // 2
    tc = _pick(f, (256, 128))
    nj = f // tc

    def body(uv_ref, ug_ref, wv_ref, wg_ref, bv_ref, bg_ref, a_ref):
        hv = _conv_fwd(_conv_taps(uv_ref[0], FFN_CONV), wv_ref, bv_ref)
        hg = _conv_fwd(_conv_taps(ug_ref[0], FFN_CONV), wg_ref, bg_ref)
        a_ref[0] = (hg * _sig(hg) * hv).astype(a_ref.dtype)

    uv = pl.BlockSpec((1, s, tc), lambda b, j: (b, 0, j))
    ug = pl.BlockSpec((1, s, tc), lambda b, j: (b, 0, j + nj))
    wv = pl.BlockSpec((FFN_CONV, tc), lambda b, j: (0, j))
    wg = pl.BlockSpec((FFN_CONV, tc), lambda b, j: (0, j + nj))
    bv = pl.BlockSpec((1, tc), lambda b, j: (0, j))
    bg = pl.BlockSpec((1, tc), lambda b, j: (0, j + nj))
    return pl.pallas_call(
        body, grid=(bsz, nj), in_specs=[uv, ug, wv, wg, bv, bg], out_specs=uv,
        out_shape=jax.ShapeDtypeStruct((bsz, s, f), BF16),
        compiler_params=_cp("parallel", "parallel"), name=name)(u, u, cw, cw, cb, cb)


def ffn_act_bwd(u, cw, cb, da, name):
    bsz, s, f2 = u.shape
    f = f2 // 2
    tc = _pick(f, (256, 128))
    nj = f // tc

    def body(uv_ref, ug_ref, wv_ref, wg_ref, bv_ref, bg_ref, da_ref,
             duv_ref, dug_ref, dwv_ref, dwg_ref, dbv_ref, dbg_ref):
        tv, tg = _conv_taps(uv_ref[0], FFN_CONV), _conv_taps(ug_ref[0], FFN_CONV)
        hv = _conv_fwd(tv, wv_ref, bv_ref)
        hg = _conv_fwd(tg, wg_ref, bg_ref)
        sg = _sig(hg)
        dav = da_ref[0].astype(F32)
        dhv = dav * hg * sg
        dhg = dav * hv * (sg * (1.0 + hg * (1.0 - sg)))
        duv_ref[0] = _conv_bwd_input(dhv, wv_ref, FFN_CONV).astype(duv_ref.dtype)
        dug_ref[0] = _conv_bwd_input(dhg, wg_ref, FFN_CONV).astype(dug_ref.dtype)

        @pl.when(pl.program_id(1) == 0)
        def _():
            for r in (dwv_ref, dwg_ref, dbv_ref, dbg_ref):
                r[...] = jnp.zeros_like(r)

        for k, row in enumerate(_conv_bwd_weight(dhv, tv)):
            dwv_ref[k:k + 1, :] += row
        for k, row in enumerate(_conv_bwd_weight(dhg, tg)):
            dwg_ref[k:k + 1, :] += row
        dbv_ref[...] += jnp.sum(dhv, axis=0, keepdims=True)
        dbg_ref[...] += jnp.sum(dhg, axis=0, keepdims=True)

    uv = pl.BlockSpec((1, s, tc), lambda j, b: (b, 0, j))
    ug = pl.BlockSpec((1, s, tc), lambda j, b: (b, 0, j + nj))
    wv = pl.BlockSpec((FFN_CONV, tc), lambda j, b: (0, j))
    wg = pl.BlockSpec((FFN_CONV, tc), lambda j, b: (0, j + nj))
    bv = pl.BlockSpec((1, tc), lambda j, b: (0, j))
    bg = pl.BlockSpec((1, tc), lambda j, b: (0, j + nj))
    act = jax.ShapeDtypeStruct((bsz, s, f), BF16)
    return pl.pallas_call(
        body, grid=(nj, bsz), in_specs=[uv, ug, wv, wg, bv, bg, uv],
        out_specs=[uv, uv, wv, wv, bv, bv],
        out_shape=[act, act, jax.ShapeDtypeStruct((FFN_CONV, f), F32), jax.ShapeDtypeStruct((FFN_CONV, f), F32),
                   jax.ShapeDtypeStruct((1, f), F32), jax.ShapeDtypeStruct((1, f), F32)],
        compiler_params=_cp("parallel", "arbitrary"), name=name)(u, u, cw, cw, cb, cb, da)


def ffn_fwd(x, g, w_in, cw, cb, w_out, bsz):
    m, d = x.shape
    h = rms_fwd(x, g, BF16, "ffn_norm")
    u = matmul(h, w_in, name="ffn_in")
    a = ffn_act_fwd(u.reshape(bsz, m // bsz, -1), cw, cb, "ffn_act")
    a2 = a.reshape(m, -1)
    out = matmul(a2, w_out, res=x, name="ffn_out")
    return out, (x, h, u, a2)


def ffn_bwd(saved, dout, g, w_in, cw, cb, w_out, bsz):
    x, h, u, a2 = saved
    m, d = x.shape
    da = matmul(dout, w_out, tb=True, out_dtype=BF16, name="ffn_out_dx")
    dw_out = matmul(a2, dout, ta=True, name="ffn_out_dw")
    u3 = u.reshape(bsz, m // bsz, -1)
    duv, dug, dwv, dwg, dbv, dbg = ffn_act_bwd(u3, cw, cb, da.reshape(bsz, m // bsz, -1), "ffn_act_bwd")
    du = jnp.concatenate([duv, dug], axis=-1).reshape(m, -1)
    dh = matmul(du, w_in, tb=True, name="ffn_in_dx")
    dw_in = matmul(h, du, ta=True, name="ffn_in_dw")
    dx, dg = rms_bwd(x, g, dh, dout, "ffn_norm_bwd")
    grads = dict(norm_ffn_g=dg, ffn_w_in=dw_in, ffn_conv_w=jnp.concatenate([dwv, dwg], axis=-1),
                 ffn_conv_b=jnp.concatenate([dbv, dbg], axis=-1), ffn_w_out=dw_out)
    return dx, grads


def loss_head(y, target, name="loss_head"):
    m, d = y.shape
    tr = _pick(m, (512, 256, 128))

    def body(y_ref, t_ref, dy_ref, l_ref):
        e = y_ref[...] - t_ref[...]
        dy_ref[...] = e * (1.0 / d)

        @pl.when(pl.program_id(0) == 0)
        def _():
            l_ref[...] = jnp.zeros_like(l_ref)

        l_ref[...] += jnp.sum(e * e, axis=0, keepdims=True) * (0.5 / d)

    row = pl.BlockSpec((tr, d), lambda i: (i, 0))
    vec = pl.BlockSpec((1, d), lambda i: (0, 0))
    dy, part = pl.pallas_call(
        body, grid=(m // tr,), in_specs=[row, row], out_specs=[row, vec],
        out_shape=[jax.ShapeDtypeStruct((m, d), F32), jax.ShapeDtypeStruct((1, d), F32)],
        compiler_params=_cp("arbitrary"), name=name)(y, target)
    return dy, part


def _pool_windows(h, gi):
    sums, s, width = [], h, 1
    for _ in POOL_WINDOWS:
        s = s + _shift_down(s, width)
        width *= 2
        sums.append(s)
    pos = _rows(h.shape).astype(F32) + 1.0
    wsum, inv = sums[-1], 1.0 / jnp.minimum(pos, float(POOL_WINDOWS[-1]))
    for k in range(len(POOL_WINDOWS) - 2, -1, -1):
        wsum = jnp.where(gi == k, sums[k], wsum)
        inv = jnp.where(gi == k, 1.0 / jnp.minimum(pos, float(POOL_WINDOWS[k])), inv)
    return wsum * inv - h, inv


def _pool_windows_transpose(e, gi):
    sums, s, width = [], e, 1
    for _ in POOL_WINDOWS:
        s = s + _shift_up(s, width)
        width *= 2
        sums.append(s)
    out = sums[-1]
    for k in range(len(POOL_WINDOWS) - 2, -1, -1):
        out = jnp.where(gi == k, sums[k], out)
    return out


def pool_fwd(h, w, b, scale, x, name="pool_fwd"):
    bsz, s, d = h.shape
    ng = d // POOL_GROUP

    def body(h_ref, w_ref, b_ref, s_ref, x_ref, o_ref):
        dd, _ = _pool_windows(h_ref[0], pl.program_id(1))
        y = _dot(dd, w_ref[0]) + b_ref[...]
        o_ref[0] = x_ref[0] + s_ref[...] * y

    act = pl.BlockSpec((1, s, POOL_GROUP), lambda bb, gi: (bb, 0, gi))
    vec = pl.BlockSpec((1, POOL_GROUP), lambda bb, gi: (0, gi))
    return pl.pallas_call(
        body, grid=(bsz, ng),
        in_specs=[act, pl.BlockSpec((1, POOL_GROUP, POOL_GROUP), lambda bb, gi: (gi, 0, 0)), vec, vec, act],
        out_specs=act, out_shape=jax.ShapeDtypeStruct((bsz, s, d), F32),
        compiler_params=_cp("parallel", "parallel"), name=name)(h, w, b, scale, x)


def pool_bwd(h, w, b, scale, dy, name="pool_bwd"):
    bsz, s, d = h.shape
    ng = d // POOL_GROUP

    def body(h_ref, w_ref, b_ref, s_ref, dy_ref, dh_ref, dw_ref, db_ref, ds_ref):
        gi = pl.program_id(0)
        dd, inv = _pool_windows(h_ref[0], gi)
        ypre = _dot(dd, w_ref[0]) + b_ref[...]
        dyv = dy_ref[0]
        dyb = dyv * s_ref[...]

        @pl.when(pl.program_id(1) == 0)
        def _():
            for r in (dw_ref, db_ref, ds_ref):
                r[...] = jnp.zeros_like(r)

        ds_ref[...] += jnp.sum(dyv * ypre, axis=0, keepdims=True)
        db_ref[...] += jnp.sum(dyb, axis=0, keepdims=True)
        dw_ref[0] += _dot(dd, dyb, TN)
        ddd = _dot(dyb, w_ref[0], NT)
        dh_ref[0] = _pool_windows_transpose(ddd * inv, gi) - ddd

    act = pl.BlockSpec((1, s, POOL_GROUP), lambda gi, bb: (bb, 0, gi))
    vec = pl.BlockSpec((1, POOL_GROUP), lambda gi, bb: (0, gi))
    wsp = pl.BlockSpec((1, POOL_GROUP, POOL_GROUP), lambda gi, bb: (gi, 0, 0))
    return pl.pallas_call(
        body, grid=(ng, bsz), in_specs=[act, wsp, vec, vec, act], out_specs=[act, wsp, vec, vec],
        out_shape=[jax.ShapeDtypeStruct((bsz, s, d), F32), jax.ShapeDtypeStruct((ng, POOL_GROUP, POOL_GROUP), F32),
                   jax.ShapeDtypeStruct((1, d), F32), jax.ShapeDtypeStruct((1, d), F32)],
        compiler_params=_cp("parallel", "arbitrary"), name=name)(h, w, b, scale, dy)


def pool_layer_fwd(x, g, w, b, scale, bsz):
    m, d = x.shape
    h = rms_fwd(x, g, F32, "pool_norm")
    out = pool_fwd(h.reshape(bsz, m // bsz, d), w, b, scale, x.reshape(bsz, m // bsz, d))
    return out.reshape(m, d), (x, h)


def pool_layer_bwd(saved, dout, g, w, b, scale, bsz):
    x, h = saved
    m, d = x.shape
    dh, dw, db, ds = pool_bwd(h.reshape(bsz, m // bsz, d), w, b, scale, dout.reshape(bsz, m // bsz, d))
    dx, dg = rms_bwd(x, g, dh.reshape(m, d), dout, "pool_norm_bwd")
    return dx, dict(norm_mix_g=dg, pool_w=dw[None], pool_b=db, pool_scale=ds)


def _scan_fwd(a, b):
    return _scan(a, b, False)


def _scan(a, b, up):
    k = 1
    while k < a.shape[0]:
        if 2 * k < a.shape[0]:
            sa, sb = _shift_all([a, b], k, up)
            a, b = a * sa, b + a * sb
        else:
            b = b + a * _shift_all([b], k, up)[0]
        k *= 2
    return b


def _scan_bwd(a, b):
    return _scan(a, b, True)


def _neg_expm1(x):
    series = -x * (1.0 + x * (0.5 + x * (1.0 / 6.0 + x * (1.0 / 24.0 + x * (1.0 / 120.0)))))
    return jnp.where(x > -0.03, series, 1.0 - jnp.exp(x))


def _lru_gates(rec, wa_ref, ba_ref, wx_ref, bx_ref, lam_ref):
    r = _sig(_dot(rec, wa_ref[0]) + ba_ref[...])
    i = _sig(_dot(rec, wx_ref[0]) + bx_ref[...])
    sp = _softplus(-lam_ref[...])
    log_a = -LRU_C * r * sp
    a = jnp.exp(log_a)
    mult = jnp.sqrt(_neg_expm1(2.0 * log_a))
    return r, i, sp, a, mult


def lru_fwd(zz, cw, cb, wa, ba, wx, bx, lam, name="lru_fwd"):
    bsz, s, r2 = zz.shape
    rw = r2 // 2
    nb = rw // LRU_BLOCK

    def body(g_ref, p_ref, cw_ref, cb_ref, wa_ref, ba_ref, wx_ref, bx_ref, lam_ref, h_ref, y_ref):
        rec = _conv_fwd(_conv_taps(p_ref[0], LRU_CONV), cw_ref, cb_ref)
        _, i, _, a, mult = _lru_gates(rec, wa_ref, ba_ref, wx_ref, bx_ref, lam_ref)
        hst = _scan_fwd(a, mult * (i * rec))
        h_ref[0] = hst
        y_ref[0] = (_gelu(g_ref[0]) * hst).astype(y_ref.dtype)

    gsp = pl.BlockSpec((1, s, LRU_BLOCK), lambda bb, n: (bb, 0, n))
    psp = pl.BlockSpec((1, s, LRU_BLOCK), lambda bb, n: (bb, 0, n + nb))
    cws = pl.BlockSpec((LRU_CONV, LRU_BLOCK), lambda bb, n: (0, n))
    vec = pl.BlockSpec((1, LRU_BLOCK), lambda bb, n: (0, n))
    wsp = pl.BlockSpec((1, LRU_BLOCK, LRU_BLOCK), lambda bb, n: (n, 0, 0))
    return pl.pallas_call(
        body, grid=(bsz, nb), in_specs=[gsp, psp, cws, vec, wsp, vec, wsp, vec, vec], out_specs=[gsp, gsp],
        out_shape=[jax.ShapeDtypeStruct((bsz, s, rw), F32), jax.ShapeDtypeStruct((bsz, s, rw), BF16)],
        compiler_params=_cp("parallel", "parallel"), name=name)(zz, zz, cw, cb, wa, ba, wx, bx, lam)


def lru_bwd(zz, hst, dy, cw, cb, wa, ba, wx, bx, lam, name="lru_bwd"):
    bsz, s, r2 = zz.shape
    rw = r2 // 2
    nb = rw // LRU_BLOCK

    def body(g_ref, p_ref, h_ref, dy_ref, cw_ref, cb_ref, wa_ref, ba_ref, wx_ref, bx_ref, lam_ref,
             dg_ref, dp_ref, dcw_ref, dcb_ref, dwa_ref, dba_ref, dwx_ref, dbx_ref, dlam_ref):
        pre = p_ref[0]
        taps = _conv_taps(pre, LRU_CONV)
        rec = _conv_fwd(taps, cw_ref, cb_ref)
        r, i, sp, a, mult = _lru_gates(rec, wa_ref, ba_ref, wx_ref, bx_ref, lam_ref)
        hst_v, gate, dyv = h_ref[0], g_ref[0], dy_ref[0]
        dg_ref[0] = (dyv * hst_v * _gelu_grad(gate)).astype(dg_ref.dtype)
        lmb = _scan_bwd(_shift_up(a, 1), dyv * _gelu(gate))
        da = lmb * _shift_down(hst_v, 1)
        dmult = lmb * (i * rec)
        dlog_a = da * a - dmult * (a * a) / mult
        dr = dlog_a * (-LRU_C) * sp
        dra = dr * r * (1.0 - r)
        dxa = lmb * mult * rec * i * (1.0 - i)
        drec = lmb * mult * i + _dot(dra, wa_ref[0], NT) + _dot(dxa, wx_ref[0], NT)
        dp_ref[0] = _conv_bwd_input(drec, cw_ref, LRU_CONV).astype(dp_ref.dtype)

        @pl.when(pl.program_id(1) == 0)
        def _():
            for ref in (dcw_ref, dcb_ref, dwa_ref, dba_ref, dwx_ref, dbx_ref, dlam_ref):
                ref[...] = jnp.zeros_like(ref)

        for k, row in enumerate(_conv_bwd_weight(drec, taps)):
            dcw_ref[k:k + 1, :] += row
        dcb_ref[...] += jnp.sum(drec, axis=0, keepdims=True)
        dwa_ref[0] += _dot(rec, dra, TN)
        dwx_ref[0] += _dot(rec, dxa, TN)
        dba_ref[...] += jnp.sum(dra, axis=0, keepdims=True)
        dbx_ref[...] += jnp.sum(dxa, axis=0, keepdims=True)
        dsp = jnp.sum(dlog_a * (-LRU_C) * r, axis=0, keepdims=True)
        dlam_ref[...] += dsp * (-_sig(-lam_ref[...]))

    gsp = pl.BlockSpec((1, s, LRU_BLOCK), lambda n, bb: (bb, 0, n))
    psp = pl.BlockSpec((1, s, LRU_BLOCK), lambda n, bb: (bb, 0, n + nb))
    cws = pl.BlockSpec((LRU_CONV, LRU_BLOCK), lambda n, bb: (0, n))
    vec = pl.BlockSpec((1, LRU_BLOCK), lambda n, bb: (0, n))
    wsp = pl.BlockSpec((1, LRU_BLOCK, LRU_BLOCK), lambda n, bb: (n, 0, 0))
    act = jax.ShapeDtypeStruct((bsz, s, rw), BF16)
    vsh = jax.ShapeDtypeStruct((1, rw), F32)
    wsh = jax.ShapeDtypeStruct((nb, LRU_BLOCK, LRU_BLOCK), F32)
    return pl.pallas_call(
        body, grid=(nb, bsz), in_specs=[gsp, psp, gsp, gsp, cws, vec, wsp, vec, wsp, vec, vec],
        out_specs=[gsp, gsp, cws, vec, wsp, vec, wsp, vec, vec],
        out_shape=[act, act, jax.ShapeDtypeStruct((LRU_CONV, rw), F32), vsh, wsh, vsh, wsh, vsh, vsh],
        compiler_params=_cp("parallel", "arbitrary"), name=name)(zz, zz, hst, dy, cw, cb, wa, ba, wx, bx, lam)


def lru_layer_fwd(x, g, p, bsz):
    m, d = x.shape
    h = rms_fwd(x, g, BF16, "lru_norm")
    zz = matmul(h, p["lru_w_in"], name="lru_in")
    hst, y = lru_fwd(zz.reshape(bsz, m // bsz, -1), p["lru_conv_w"], p["lru_conv_b"], p["lru_w_a"], p["lru_b_a"],
                     p["lru_w_x"], p["lru_b_x"], p["lru_lam"])
    y2 = y.reshape(m, -1)
    out = matmul(y2, p["lru_w_out"], res=x, name="lru_out")
    return out, (x, h, zz, hst, y2)


def lru_layer_bwd(saved, dout, g, p, bsz):
    x, h, zz, hst, y2 = saved
    m, d = x.shape
    dy = matmul(dout, p["lru_w_out"], tb=True, name="lru_out_dx")
    dw_out = matmul(y2, dout, ta=True, name="lru_out_dw")
    dgate, dpre, dcw, dcb, dwa, dba, dwx, dbx, dlam = lru_bwd(
        zz.reshape(bsz, m // bsz, -1), hst, dy.reshape(bsz, m // bsz, -1), p["lru_conv_w"], p["lru_conv_b"],
        p["lru_w_a"], p["lru_b_a"], p["lru_w_x"], p["lru_b_x"], p["lru_lam"])
    dzz = jnp.concatenate([dgate, dpre], axis=-1).reshape(m, -1)
    dh = matmul(dzz, p["lru_w_in"], tb=True, name="lru_in_dx")
    dw_in = matmul(h, dzz, ta=True, name="lru_in_dw")
    dx, dg = rms_bwd(x, g, dh, dout, "lru_norm_bwd")
    return dx, dict(norm_mix_g=dg, lru_w_in=dw_in, lru_conv_w=dcw[None], lru_conv_b=dcb, lru_w_a=dwa[None],
                    lru_b_a=dba, lru_w_x=dwx[None], lru_b_x=dbx, lru_lam=dlam, lru_w_out=dw_out)


def _s5_discretise(lam_re, lam_im, log_dt, b_re, b_im):
    lr = jnp.minimum(lam_re, -1e-4)
    dt = jnp.exp(log_dt)[:, None]
    mag = jnp.exp(lr * dt)
    ar, ai = mag * jnp.cos(lam_im * dt), mag * jnp.sin(lam_im * dt)
    den = lr * lr + lam_im * lam_im
    cr = ((ar - 1.0) * lr + ai * lam_im) / den
    ci = (ai * lr - (ar - 1.0) * lam_im) / den
    bbr = cr[..., None] * b_re - ci[..., None] * b_im
    bbi = cr[..., None] * b_im + ci[..., None] * b_re
    return ar, ai, bbr, bbi


def _s5_powers(lam_re, lam_im, log_dt, ns):
    lr = jnp.minimum(lam_re, -1e-4)
    dt = jnp.exp(log_dt)[:, None]
    n = jnp.asarray(ns, F32)[:, None, None]
    mag = jnp.exp(n * (lr * dt))
    ang = n * (lam_im * dt)
    to_chunks = lambda t: t.reshape(len(ns), S5_CHUNKS, S5_LANES).transpose(1, 0, 2)
    return jnp.concatenate([to_chunks(mag * jnp.cos(ang)), to_chunks(mag * jnp.sin(ang))], axis=-1)


def _s5_in_matrix(bbr, bbi):
    eye = jnp.eye(8, dtype=F32)
    blk = lambda t: jnp.einsum("qgph,gk->qghkp", t.reshape(S5_CHUNKS, 8, S5_STATE, S5_GROUP), eye).reshape(
        S5_CHUNKS, 128, S5_LANES)
    return jnp.concatenate([blk(bbr), blk(bbi)], axis=-1)


def _s5_in_matrix_diag(dmat):
    eye = jnp.eye(8, dtype=F32)[None, :, None, :, None]
    pick = lambda t: (t.reshape(S5_CHUNKS, 8, S5_GROUP, 8, S5_STATE) * eye).sum(3).transpose(0, 1, 3, 2).reshape(
        S5_CHUNKS * 8, S5_STATE, S5_GROUP)
    return pick(dmat[..., :S5_LANES]), pick(dmat[..., S5_LANES:])


def _s5_out_matrix(c_re, c_im):
    eye = jnp.eye(8, dtype=F32)
    blk = lambda t: jnp.einsum("qghp,gk->qgpkh", t.reshape(S5_CHUNKS, 8, S5_GROUP, S5_STATE), eye).reshape(
        S5_CHUNKS, S5_LANES, 128)
    return jnp.concatenate([blk(c_re), -blk(c_im)], axis=1)


def _s5_out_matrix_diag(dmat):
    eye = jnp.eye(8, dtype=F32)[None, :, None, :, None]
    pick = lambda t: (t.reshape(S5_CHUNKS, 8, S5_STATE, 8, S5_GROUP) * eye).sum(3).transpose(0, 1, 3, 2).reshape(
        S5_CHUNKS * 8, S5_GROUP, S5_STATE)
    return pick(dmat[:, :S5_LANES]), -pick(dmat[:, S5_LANES:])


def s5_fwd(h, bmat, cmat, atab, pw, dskip, later_shards=(), name="s5_fwd"):
    bsz, s, d = h.shape
    t = min(S5_T, s)
    nt, nlev, ln = s // t, atab.shape[1], S5_LANES
    ng = len(later_shards)

    def body(*refs):
        h_ref, b_ref, c_ref, a_ref, pw_ref, d_ref = refs[:6]
        xs_ref, yp_ref, yg_ref = refs[6 + ng:9 + ng]
        carry = refs[9 + 2 * ng]
        if ng:
            start, finish = _gather_steps([g.shape for g in later_shards], refs[6:6 + ng], refs[9 + ng:9 + 2 * ng],
                                          refs[10 + 2 * ng:])
            step = (pl.program_id(0) * S5_CHUNKS + pl.program_id(1)) * nt + pl.program_id(2)
            pl.when(step == 0)(start)

        @pl.when(pl.program_id(2) == 0)
        def _():
            carry[...] = jnp.zeros_like(carry)

        u = h_ref[0]
        bu = _dot3(u, b_ref[0])
        xr, xi = bu[:, :ln], bu[:, ln:]
        for k in range(nlev):
            ar, ai = a_ref[0, k:k + 1, :ln], a_ref[0, k:k + 1, ln:]
            sr, si = _shift_all([xr, xi], 1 << k, False)
            xr, xi = xr + ar * sr - ai * si, xi + ar * si + ai * sr
        cr, ci = carry[0:1, :ln], carry[0:1, ln:]
        pr, pi = pw_ref[0, :, :ln], pw_ref[0, :, ln:]
        xr, xi = xr + pr * cr - pi * ci, xi + pr * ci + pi * cr
        carry[0:1, :ln] = xr[t - 1:t, :]
        carry[0:1, ln:] = xi[t - 1:t, :]
        xs_ref[0, :, :ln] = xr
        xs_ref[0, :, ln:] = xi
        y = _dot3(xr, c_ref[0, :ln, :]) + _dot3(xi, c_ref[0, ln:, :]) + d_ref[...] * u
        yp_ref[0] = y
        yg_ref[0] = _gelu(y).astype(yg_ref.dtype)
        if ng:
            pl.when(step == bsz * S5_CHUNKS * nt - 1)(finish)

    act = pl.BlockSpec((1, t, 128), lambda b, q, i: (b, i, q))
    par = lambda r, c: pl.BlockSpec((1, r, c), lambda b, q, i: (q, 0, 0))
    out = pl.pallas_call(
        body, grid=(bsz, S5_CHUNKS, nt),
        in_specs=[act, par(128, 2 * ln), par(2 * ln, 128), par(nlev, 2 * ln), par(t, 2 * ln),
                  pl.BlockSpec((1, 128), lambda b, q, i: (0, q))] + [_ANY] * ng,
        out_specs=[pl.BlockSpec((1, t, 2 * ln), lambda b, q, i: (b, i, q)), act, act] + [_ANY] * ng,
        out_shape=[jax.ShapeDtypeStruct((bsz, s, S5_CHUNKS * 2 * ln), F32), jax.ShapeDtypeStruct((bsz, s, d), F32),
                   jax.ShapeDtypeStruct((bsz, s, d), BF16)] + _gather_out_shapes(later_shards),
        scratch_shapes=[pltpu.VMEM((8, 2 * ln), F32)] + (_gather_semaphores(ng) if ng else []),
        compiler_params=_cp("arbitrary", "arbitrary", "arbitrary"), name=name)(
            h, bmat, cmat, atab, pw, dskip, *later_shards)
    return out[0], out[1], out[2], out[3:]


def s5_bwd(h, ypre, xs, dyg, bmat_t, cmat_t, atab, pw_rev, dskip, scatter=None, name="s5_bwd"):
    bsz, s, d = h.shape
    t = min(S5_T, s)
    nt, nlev, ln = s // t, atab.shape[1], S5_LANES
    halves, targets, buf_shapes = scatter if scatter else ((), (), ())
    nh, nb = len(halves), len(buf_shapes)

    def body(*refs):
        h_ref, yp_ref, xs_ref, xp_ref, dy_ref, bt_ref, ct_ref, a_ref, pw_ref, d_ref = refs[:10]
        dh_ref, db_ref, dc_ref, da_ref, dd_ref = refs[10 + nh:15 + nh]
        carry = refs[15 + nh + nb]
        b, i = pl.program_id(1), pl.program_id(2)
        if nh:
            start, finish = _scatter_steps(targets, refs[10:10 + nh], refs[15 + nh:15 + nh + nb], refs[16 + nh + nb:])
            step = (pl.program_id(0) * bsz + b) * nt + i
            pl.when(step == 0)(start)

        @pl.when((b == 0) & (i == 0))
        def _():
            for r in (db_ref, dc_ref, da_ref, dd_ref):
                r[...] = jnp.zeros_like(r)

        @pl.when(i == 0)
        def _():
            carry[...] = jnp.zeros_like(carry)

        u = h_ref[0]
        dyp = dy_ref[0] * _gelu_grad(yp_ref[0])
        dd_ref[...] += jnp.sum(dyp * u, axis=0, keepdims=True)
        xr, xi = xs_ref[0, :, :ln], xs_ref[0, :, ln:]
        dc_ref[0, :ln, :] += _dot3(xr, dyp, TN)
        dc_ref[0, ln:, :] += _dot3(xi, dyp, TN)
        lr, li = _dot3(dyp, ct_ref[0, :, :ln]), _dot3(dyp, ct_ref[0, :, ln:])
        for k in range(nlev):
            ar, ai = a_ref[0, k:k + 1, :ln], a_ref[0, k:k + 1, ln:]
            sr, si = _shift_all([lr, li], 1 << k, True)
            lr, li = lr + ar * sr + ai * si, li + ar * si - ai * sr
        cr, ci = carry[0:1, :ln], carry[0:1, ln:]
        pr, pi = pw_ref[0, :, :ln], pw_ref[0, :, ln:]
        lr, li = lr + pr * cr + pi * ci, li + pr * ci - pi * cr
        carry[0:1, :ln] = lr[0:1, :]
        carry[0:1, ln:] = li[0:1, :]
        dh_ref[0] = _dot3(lr, bt_ref[0, :ln, :]) + _dot3(li, bt_ref[0, ln:, :]) + dyp * d_ref[...]
        db_ref[0, :, :ln] += _dot3(u, lr, TN)
        db_ref[0, :, ln:] += _dot3(u, li, TN)
        first = _rows(xr.shape) == 0
        keep = jnp.where(i == nt - 1, 0.0, 1.0)
        xpr = jnp.where(first, xp_ref[0, 7:8, :ln] * keep, pltpu.roll(xr, 1, 0))
        xpi = jnp.where(first, xp_ref[0, 7:8, ln:] * keep, pltpu.roll(xi, 1, 0))
        da_ref[0, 0:1, :ln] += jnp.sum(lr * xpr + li * xpi, axis=0, keepdims=True)
        da_ref[0, 0:1, ln:] += jnp.sum(li * xpr - lr * xpi, axis=0, keepdims=True)
        if nh:
            pl.when(step == S5_CHUNKS * bsz * nt - 1)(finish)

    rev = lambda i: nt - 1 - i
    act = pl.BlockSpec((1, t, 128), lambda q, b, i: (b, rev(i), q))
    xsp = pl.BlockSpec((1, t, 2 * ln), lambda q, b, i: (b, rev(i), q))
    xpp = pl.BlockSpec((1, 8, 2 * ln), lambda q, b, i: (b, jnp.maximum(rev(i) * (t // 8) - 1, 0), q))
    par = lambda r, c: pl.BlockSpec((1, r, c), lambda q, b, i: (q, 0, 0))
    dsp = pl.BlockSpec((1, 128), lambda q, b, i: (0, q))
    out = pl.pallas_call(
        body, grid=(S5_CHUNKS, bsz, nt),
        in_specs=[act, act, xsp, xpp, act, par(2 * ln, 128), par(128, 2 * ln), par(nlev, 2 * ln), par(t, 2 * ln), dsp]
        + [_ANY] * nh,
        out_specs=[act, par(128, 2 * ln), par(2 * ln, 128), par(8, 2 * ln), dsp] + [_ANY] * nb,
        out_shape=[jax.ShapeDtypeStruct((bsz, s, d), F32), jax.ShapeDtypeStruct((S5_CHUNKS, 128, 2 * ln), F32),
                   jax.ShapeDtypeStruct((S5_CHUNKS, 2 * ln, 128), F32), jax.ShapeDtypeStruct((S5_CHUNKS, 8, 2 * ln), F32),
                   jax.ShapeDtypeStruct((1, d), F32)] + list(buf_shapes),
        scratch_shapes=[pltpu.VMEM((8, 2 * ln), F32)] + (_scatter_semaphores(nh) if nh else []),
        compiler_params=_cp("arbitrary", "arbitrary", "arbitrary"), name=name)(
            h, ypre, xs, xs, dyg, bmat_t, cmat_t, atab, pw_rev, dskip, *halves)
    return out[0], out[1], out[2], out[3], out[4], out[5:]


def glu_fwd(z, x, name="s5_glu"):
    m, d = x.shape
    tr = _pick(m, (512, 256, 128))

    def body(z_ref, x_ref, o_ref):
        o_ref[...] = x_ref[...] + z_ref[:, :d] * _sig(z_ref[:, d:])

    return pl.pallas_call(
        body, grid=(m // tr,),
        in_specs=[pl.BlockSpec((tr, 2 * d), lambda i: (i, 0)), pl.BlockSpec((tr, d), lambda i: (i, 0))],
        out_specs=pl.BlockSpec((tr, d), lambda i: (i, 0)), out_shape=jax.ShapeDtypeStruct((m, d), F32),
        compiler_params=_cp("parallel"), name=name)(z, x)


def glu_bwd(z, dout, name="s5_glu_bwd"):
    m, d = dout.shape
    tr = _pick(m, (512, 256, 128))

    def body(z_ref, do_ref, dz_ref, db_ref):
        sg = _sig(z_ref[:, d:])
        dv = do_ref[...] * sg
        dgt = do_ref[...] * z_ref[:, :d] * sg * (1.0 - sg)
        dz_ref[:, :d] = dv.astype(dz_ref.dtype)
        dz_ref[:, d:] = dgt.astype(dz_ref.dtype)

        @pl.when(pl.program_id(0) == 0)
        def _():
            db_ref[...] = jnp.zeros_like(db_ref)

        db_ref[:, :d] += jnp.sum(dv, axis=0, keepdims=True)
        db_ref[:, d:] += jnp.sum(dgt, axis=0, keepdims=True)

    wide = pl.BlockSpec((tr, 2 * d), lambda i: (i, 0))
    return pl.pallas_call(
        body, grid=(m // tr,), in_specs=[wide, pl.BlockSpec((tr, d), lambda i: (i, 0))],
        out_specs=[wide, pl.BlockSpec((1, 2 * d), lambda i: (0, 0))],
        out_shape=[jax.ShapeDtypeStruct((m, 2 * d), BF16), jax.ShapeDtypeStruct((1, 2 * d), F32)],
        compiler_params=_cp("arbitrary"), name=name)(z, dout)


def _s5_tables(p, t):
    nlev = max(1, (t - 1).bit_length())
    lam = (p["s5_lam_re"], p["s5_lam_im"], p["s5_log_dt"])
    atab = _s5_powers(*lam, [1 << k for k in range(nlev)])
    if nlev < 8:
        atab = jnp.pad(atab, ((0, 0), (0, 8 - nlev), (0, 0)))
    pw = _s5_powers(*lam, list(range(1, t + 1)))
    return nlev, atab, pw


def s5_layer_fwd(x, g, p, bsz, later_shards=()):
    m, d = x.shape
    s = m // bsz
    t = min(S5_T, s)
    h = rms_fwd(x, g, F32, "s5_norm")
    _, _, bbr, bbi = _s5_discretise(p["s5_lam_re"], p["s5_lam_im"], p["s5_log_dt"], p["s5_b_re"], p["s5_b_im"])
    nlev, atab, pw = _s5_tables(p, t)
    bmat, cmat = _s5_in_matrix(bbr, bbi), _s5_out_matrix(p["s5_c_re"], p["s5_c_im"])
    xs, ypre, yg, gathered = s5_fwd(h.reshape(bsz, s, d), bmat, cmat, atab[:, :max(nlev, 8)], pw, p["s5_d"],
                                    later_shards)
    z = matmul(yg.reshape(m, d), p["s5_w_out"], bias=p["s5_b_out"], name="s5_out")
    out = glu_fwd(z, x)
    return out, (x, h, xs, ypre, yg, z, bmat, cmat, atab, pw), gathered


def s5_layer_bwd(saved, dout, g, p, bsz, scatter=None):
    x, h, xs, ypre, yg, z, bmat, cmat, atab, pw = saved
    m, d = x.shape
    s = m // bsz
    pw_rev = _s5_powers(p["s5_lam_re"], p["s5_lam_im"], p["s5_log_dt"], list(range(pw.shape[1], 0, -1)))
    dz, db_out = glu_bwd(z, dout)
    dyg = matmul(dz, p["s5_w_out"], tb=True, name="s5_out_dx")
    dw_out = matmul(yg.reshape(m, d), dz, ta=True, name="s5_out_dw")
    dh, dbm, dcm, dlam, dd, landed = s5_bwd(h.reshape(bsz, s, d), ypre, xs, dyg.reshape(bsz, s, d),
                                            bmat.transpose(0, 2, 1), cmat.transpose(0, 2, 1), atab, pw_rev, p["s5_d"],
                                            scatter)
    dx, dg = rms_bwd(x, g, dh.reshape(m, d), dout, "s5_norm_bwd")
    dbbr, dbbi = _s5_in_matrix_diag(dbm)
    dc_re, dc_im = _s5_out_matrix_diag(dcm)
    dar = dlam[:, 0, :S5_LANES].reshape(S5_CHUNKS * 8, S5_STATE)
    dai = dlam[:, 0, S5_LANES:].reshape(S5_CHUNKS * 8, S5_STATE)
    _, vjp = jax.vjp(_s5_discretise, p["s5_lam_re"], p["s5_lam_im"], p["s5_log_dt"], p["s5_b_re"], p["s5_b_im"])
    dl_re, dl_im, dldt, db_re, db_im = vjp((dar, dai, dbbr, dbbi))
    grads = dict(norm_mix_g=dg, s5_lam_re=dl_re[None], s5_lam_im=dl_im[None], s5_log_dt=dldt[None],
                 s5_b_re=db_re[None], s5_b_im=db_im[None], s5_c_re=dc_re[None], s5_c_im=dc_im[None],
                 s5_d=dd, s5_w_out=dw_out, s5_b_out=db_out)
    return dx, grads, landed


def _log_sigmoid(z):
    return jnp.minimum(z, 0.0) - jnp.log(1.0 + jnp.exp(-jnp.abs(z)))


def _head_norm(t, g_ref):
    r = lax.rsqrt(jnp.mean(t * t, axis=-1, keepdims=True) + EPS)
    th = t * r
    return th * g_ref[...], th, r


def _tri(shape, fn):
    row = lax.broadcasted_iota(jnp.int32, shape, 0)
    col = lax.broadcasted_iota(jnp.int32, shape, 1)
    return fn(row, col)


_SB_SCALE = 1.0 / math.sqrt(SB_DIM)


def _suffix_sums(t, later):
    n = t.shape[1] // SB_CHUNK
    outs, carry = [None] * n, jnp.zeros((t.shape[0], 1), F32)
    for ci in range(n - 1, -1, -1):
        ch = t[:, ci * SB_CHUNK:(ci + 1) * SB_CHUNK]
        outs[ci] = _dot_exact_rhs(ch, later) + carry
        carry = carry + jnp.sum(ch, axis=1, keepdims=True)
    return (outs[0] if n == 1 else jnp.concatenate(outs, axis=1)), carry


def _prefix_sums(t, tri):
    n = t.shape[1] // SB_CHUNK
    outs, carry = [None] * n, jnp.zeros((t.shape[0], 1), F32)
    for ci in range(n):
        ch = t[:, ci * SB_CHUNK:(ci + 1) * SB_CHUNK]
        outs[ci] = _dot_exact_rhs(ch, tri) + carry
        carry = carry + jnp.sum(ch, axis=1, keepdims=True)
    return (outs[0] if n == 1 else jnp.concatenate(outs, axis=1)), carry


def sb_fwd(q, k, v, qg, kg, name="sb_fwd"):
    bsz, nh, s, dh = q.shape
    tb = min(SB_BLOCK, s)
    nq = s // tb

    def body(q_ref, k_ref, v_ref, qg_ref, kg_ref, o_ref, rt_ref):
        qi = pl.program_id(2)
        qn, _, _ = _head_norm(q_ref[0, 0], qg_ref)
        later = _tri((SB_CHUNK, SB_CHUNK), lambda r, c: r > c).astype(BF16)
        causal = _tri((tb, tb), lambda r, c: c < r)

        def block(kb, run, acc, diag):
            ks = pl.ds(pl.multiple_of(kb * tb, tb), tb)
            kn, _, _ = _head_norm(k_ref[0, 0, ks, :], kg_ref)
            z = _dot(qn, kn, NT) * _SB_SCALE
            ls = _log_sigmoid(z)
            lm = ls - z
            if diag:
                lm = jnp.where(causal, lm, 0.0)
            rest, total = _suffix_sums(lm, later)
            att = jnp.exp(ls + run + rest)
            if diag:
                att = jnp.where(causal, att, 0.0)
            return run + total, acc + _dot(att, v_ref[0, 0, ks, :])

        run, acc = block(qi, jnp.zeros((tb, 1), F32), jnp.zeros((tb, dh), F32), True)
        run, acc = lax.fori_loop(0, qi, lambda j, c: block(qi - 1 - j, c[0], c[1], False), (run, acc))
        o_ref[0, 0] = acc
        rt_ref[0, 0] = run

    qsp = pl.BlockSpec((1, 1, tb, dh), lambda b, h, i: (b, h, i, 0))
    rsp = pl.BlockSpec((1, 1, tb, 1), lambda b, h, i: (b, h, i, 0))
    ksp = pl.BlockSpec((1, 1, s, dh), lambda b, h, i: (b, h, 0, 0))
    gsp = pl.BlockSpec((1, dh), lambda b, h, i: (0, 0))
    return pl.pallas_call(
        body, grid=(bsz, nh, nq), in_specs=[qsp, ksp, ksp, gsp, gsp], out_specs=[qsp, rsp],
        out_shape=[jax.ShapeDtypeStruct((bsz, nh, s, dh), F32), jax.ShapeDtypeStruct((bsz, nh, s, 1), F32)],
        compiler_params=_cp("parallel", "parallel", "arbitrary"), name=name)(q, k, v, qg, kg)


def sb_bwd(q, k, v, rtot, do, qg, kg, name="sb_bwd"):
    bsz, nh, s, dh = q.shape
    tb = min(SB_BLOCK, s)
    nq = s // tb

    def body(q_ref, k_ref, v_ref, rt_ref, do_ref, qg_ref, kg_ref, dq_ref, dk_ref, dv_ref, dqg_ref, dkg_ref,
             qn_s, kn_s, dqn_s, dkn_s, dv_s):
        qn, qh, rq = _head_norm(q_ref[0, 0], qg_ref)
        kn, kh, rk = _head_norm(k_ref[0, 0], kg_ref)
        qn_s[...] = qn
        kn_s[...] = kn
        dkn_s[...] = jnp.zeros_like(dkn_s)
        dv_s[...] = jnp.zeros_like(dv_s)
        chunk = (SB_CHUNK, SB_CHUNK)
        upto = _tri(chunk, lambda r, c: r <= c).astype(BF16)
        earlier = _tri(chunk, lambda r, c: r < c).astype(BF16)
        causal = _tri((tb, tb), lambda r, c: c < r)

        def q_block(qi, _):
            qs = pl.ds(pl.multiple_of(qi * tb, tb), tb)
            qnb, dob, rtb = qn_s[qs, :], do_ref[0, 0, qs, :], rt_ref[0, 0, qs, :]

            def block(kb, left, seen, dqn, diag):
                ks = pl.ds(pl.multiple_of(kb * tb, tb), tb)
                knb, vb = kn_s[ks, :], v_ref[0, 0, ks, :]
                z = _dot(qnb, knb, NT) * _SB_SCALE
                ls = _log_sigmoid(z)
                lm = ls - z
                if diag:
                    lm = jnp.where(causal, lm, 0.0)
                through, lm_total = _prefix_sums(lm, upto)
                att = jnp.exp(ls + (rtb - left - through))
                if diag:
                    att = jnp.where(causal, att, 0.0)
                gg = att * _dot(dob, vb, NT)
                before, gg_total = _prefix_sums(gg, earlier)
                sg = jnp.exp(ls)
                dz = gg * (1.0 - sg) - sg * (seen + before)
                if diag:
                    dz = jnp.where(causal, dz, 0.0)
                dz = dz * _SB_SCALE
                dkn_s[ks, :] += _dot(dz, qnb, TN)
                dv_s[ks, :] += _dot(att, dob, TN)
                return left + lm_total, seen + gg_total, dqn + _dot(dz, knb)

            zero = jnp.zeros((tb, 1), F32)
            c = lax.fori_loop(0, qi, lambda kb, c: block(kb, c[0], c[1], c[2], False),
                              (zero, zero, jnp.zeros((tb, dh), F32)))
            c = block(qi, c[0], c[1], c[2], True)
            dqn_s[qs, :] = c[2]
            return 0

        lax.fori_loop(0, nq, q_block, 0)

        @pl.when((pl.program_id(0) == 0) & (pl.program_id(1) == 0))
        def _():
            dqg_ref[...] = jnp.zeros_like(dqg_ref)
            dkg_ref[...] = jnp.zeros_like(dkg_ref)

        def norm_bwd(dn, th, r, g_ref, dt_ref, dg_ref):
            dg_ref[...] += jnp.sum(dn * th, axis=0, keepdims=True)
            dth = dn * g_ref[...]
            dt_ref[0, 0] = r * (dth - th * jnp.mean(dth * th, axis=-1, keepdims=True))

        norm_bwd(dqn_s[...], qh, rq, qg_ref, dq_ref, dqg_ref)
        norm_bwd(dkn_s[...], kh, rk, kg_ref, dk_ref, dkg_ref)
        dv_ref[0, 0] = dv_s[...]

    hsp = pl.BlockSpec((1, 1, s, dh), lambda b, h: (b, h, 0, 0))
    gsp = pl.BlockSpec((1, dh), lambda b, h: (0, 0))
    act = jax.ShapeDtypeStruct((bsz, nh, s, dh), F32)
    gsh = jax.ShapeDtypeStruct((1, dh), F32)
    rsp = pl.BlockSpec((1, 1, s, 1), lambda b, h: (b, h, 0, 0))
    return pl.pallas_call(
        body, grid=(bsz, nh), in_specs=[hsp, hsp, hsp, rsp, hsp, gsp, gsp], out_specs=[hsp, hsp, hsp, gsp, gsp],
        out_shape=[act, act, act, gsh, gsh], scratch_shapes=[pltpu.VMEM((s, dh), F32)] * 5,
        compiler_params=_cp("arbitrary", "arbitrary"), name=name)(q, k, v, rtot, do, qg, kg)


def _to_heads(t, bsz):
    m, w = t.shape
    n = w // (SB_HEADS * SB_DIM)
    t = t.reshape(bsz, m // bsz, n, SB_HEADS, SB_DIM).transpose(2, 0, 3, 1, 4)
    return [t[i] for i in range(n)]


def _from_heads(ts):
    t = jnp.stack(ts, axis=0)
    n, bsz, nh, s, dh = t.shape
    return t.transpose(1, 3, 0, 2, 4).reshape(bsz * s, n * nh * dh)


def sb_layer_fwd(x, g, p, bsz):
    m, d = x.shape
    h = rms_fwd(x, g, BF16, "sb_norm")
    qkv = matmul(h, p["sb_w_qkv"], name="sb_qkv")
    q, k, v = _to_heads(qkv, bsz)
    o, rtot = sb_fwd(q, k, v, p["sb_q_g"], p["sb_k_g"])
    o2 = _from_heads([o])
    out = matmul(o2, p["sb_w_o"], res=x, name="sb_out")
    return out, (x, h, q, k, v, rtot, o2)


def sb_layer_bwd(saved, dout, g, p, bsz):
    x, h, q, k, v, rtot, o2 = saved
    do2 = matmul(dout, p["sb_w_o"], tb=True, name="sb_out_dx")
    dw_o = matmul(o2, dout, ta=True, name="sb_out_dw")
    dq, dk, dv, dqg, dkg = sb_bwd(q, k, v, rtot, _to_heads(do2, bsz)[0], p["sb_q_g"], p["sb_k_g"])
    dqkv = _from_heads([dq, dk, dv])
    dh = matmul(dqkv, p["sb_w_qkv"], tb=True, name="sb_qkv_dx")
    dw_qkv = matmul(h, dqkv, ta=True, name="sb_qkv_dw")
    dx, dg = rms_bwd(x, g, dh, dout, "sb_norm_bwd")
    return dx, dict(norm_mix_g=dg, sb_w_qkv=dw_qkv, sb_q_g=dqg, sb_k_g=dkg, sb_w_o=dw_o)


_CHIP_FLIPS = ((1, 0), (0, 1), (1, 1))
_MESH = pl.DeviceIdType.MESH
_ANY = pl.BlockSpec(memory_space=pl.ANY)


def _flip(v, f):
    return 1 - v if f else v


def _splits(shape):
    return shape[-2] % 32 == 0


def _half(ref, c, rows):
    idx = (slice(None),) * (len(ref.shape) - 2) + (pl.ds(pl.multiple_of(c * (rows // 2), 16), rows // 2),)
    return ref.at[idx]


def gather_weights(shards, name="gather_weights"):
    n = len(shards)

    def body(*refs):
        start, finish = _gather_steps([s.shape for s in shards], refs[:n], refs[n:2 * n], refs[2 * n:])
        start()
        finish()

    return pl.pallas_call(
        body, in_specs=[_ANY] * n, out_specs=[_ANY] * n, out_shape=_gather_out_shapes(shards),
        scratch_shapes=_gather_semaphores(n), name=name)(*shards)


def _gather_out_shapes(shards):
    return [jax.ShapeDtypeStruct((N_CHIPS,) + s.shape, s.dtype) for s in shards]


def _gather_semaphores(n):
    return [pltpu.SemaphoreType.DMA((3 * n,))] * 4


def _gather_steps(shapes, ins, outs, sems):
    n = len(shapes)
    split = [_splits(s) for s in shapes]
    send, recv, fsend, frecv = sems
    x, y, c = lax.axis_index("x"), lax.axis_index("y"), lax.axis_index("c")
    me = 2 * x + y
    sibling = (x, y, 1 - c)

    def remote(i, j, block):
        px, py = _flip(x, _CHIP_FLIPS[j][0]), _flip(y, _CHIP_FLIPS[j][1])
        rows = shapes[i][0]
        src = _half(ins[i], c, rows) if split[i] else ins[i]
        dst = _half(outs[i].at[block], c, rows) if split[i] else outs[i].at[block]
        return pltpu.make_async_remote_copy(
            src_ref=src, dst_ref=dst, send_sem=send.at[3 * i + j], recv_sem=recv.at[3 * i + j],
            device_id=(px, py, c), device_id_type=_MESH)

    def forward(i, j, half):
        rows = _half(outs[i].at[2 * _flip(x, _CHIP_FLIPS[j][0]) + _flip(y, _CHIP_FLIPS[j][1])], half, shapes[i][0])
        return pltpu.make_async_remote_copy(
            src_ref=rows, dst_ref=rows, send_sem=fsend.at[3 * i + j], recv_sem=frecv.at[3 * i + j],
            device_id=sibling, device_id_type=_MESH)

    def start():
        for i in range(n):
            for j in range(3):
                remote(i, j, me).start()

    def finish():
        for i in range(n):
            for j, (fx, fy) in enumerate(_CHIP_FLIPS):
                remote(i, j, 2 * _flip(x, fx) + _flip(y, fy)).wait_recv()
                if split[i]:
                    forward(i, j, c).start()
        for i in range(n):
            for j in range(3):
                if split[i]:
                    forward(i, j, 1 - c).wait_recv()
                    forward(i, j, c).wait_send()
                remote(i, j, me).wait_send()

    return start, finish


def grad_halves_exchange(parts, name="grad_halves_exchange"):
    n = len(parts)

    def body(*refs):
        ins, got = refs[:n], refs[n:2 * n]
        send, recv = refs[2 * n:]
        x, y, c = lax.axis_index("x"), lax.axis_index("y"), lax.axis_index("c")
        swap = [pltpu.make_async_remote_copy(
            src_ref=_half(ins[i], 1 - c, parts[i].shape[1]), dst_ref=got[i], send_sem=send.at[i], recv_sem=recv.at[i],
            device_id=(x, y, 1 - c), device_id_type=_MESH) for i in range(n)]
        for cp in swap:
            cp.start()
        for cp in swap:
            cp.wait()

    half = [jax.ShapeDtypeStruct((N_CHIPS, p.shape[1] // 2, p.shape[2]), p.dtype) for p in parts]
    return pl.pallas_call(
        body, in_specs=[_ANY] * n, out_specs=[_ANY] * n, out_shape=half,
        scratch_shapes=[pltpu.SemaphoreType.DMA((n,))] * 2, name=name)(*parts)


def pair_sum(full, got, core, out_dtype, name):
    k, r, c = full.shape
    rh = r // 2
    tr = _pick(rh, tuple(t for t in (512, 256, 128, 64, 32, 16) if t * c * 4 <= 1024 * 1024))
    nb = rh // tr

    def body(core_ref, a_ref, b_ref, o_ref):
        o_ref[...] = (a_ref[...] + b_ref[...]).astype(o_ref.dtype)

    blk = pl.BlockSpec((1, tr, c), lambda kk, i, core_ref: (kk, i, 0))
    mine = pl.BlockSpec((1, tr, c), lambda kk, i, core_ref: (kk, core_ref[0] * nb + i, 0))
    return pl.pallas_call(
        body, out_shape=jax.ShapeDtypeStruct((k, rh, c), out_dtype),
        grid_spec=pltpu.PrefetchScalarGridSpec(num_scalar_prefetch=1, grid=(k, nb), in_specs=[mine, blk],
                                               out_specs=blk),
        compiler_params=_cp("parallel", "parallel"), name=name)(core, full, got)


def _scatter_semaphores(n):
    return [pltpu.SemaphoreType.DMA((3 * n,)), pltpu.SemaphoreType.DMA((3 * n,)), pltpu.SemaphoreType.DMA((n,)),
            pltpu.SemaphoreType.DMA((n,)), pltpu.SemaphoreType.DMA((n,))]


def _scatter_steps(targets, hin, bufs, sems):
    n = len(targets)
    send, recv, loc, fsend, frecv = sems
    x, y, c = lax.axis_index("x"), lax.axis_index("y"), lax.axis_index("c")
    me = 2 * x + y
    sibling = (x, y, 1 - c)

    def rows_of(ref, i, half):
        _, base, rows = targets[i]
        at = pl.ds(pl.multiple_of(base + half * (rows // 2), 16), rows // 2)
        return ref.at[(slice(None),) * (len(ref.shape) - 2) + (at,)]

    def remote(i, j, slot):
        px, py = _flip(x, _CHIP_FLIPS[j][0]), _flip(y, _CHIP_FLIPS[j][1])
        return pltpu.make_async_remote_copy(
            src_ref=hin[i].at[2 * px + py], dst_ref=rows_of(bufs[targets[i][0]].at[slot], i, c),
            send_sem=send.at[3 * i + j], recv_sem=recv.at[3 * i + j], device_id=(px, py, c), device_id_type=_MESH)

    def local(i):
        return pltpu.make_async_copy(hin[i].at[me], rows_of(bufs[targets[i][0]].at[me], i, c), loc.at[i])

    def forward(i, half):
        rows = rows_of(bufs[targets[i][0]], i, half)
        return pltpu.make_async_remote_copy(src_ref=rows, dst_ref=rows, send_sem=fsend.at[i], recv_sem=frecv.at[i],
                                            device_id=sibling, device_id_type=_MESH)

    def start():
        for i in range(n):
            local(i).start()
            for j in range(3):
                remote(i, j, me).start()

    def finish():
        for i in range(n):
            for j, (fx, fy) in enumerate(_CHIP_FLIPS):
                remote(i, j, 2 * _flip(x, fx) + _flip(y, fy)).wait_recv()
            local(i).wait()
            forward(i, c).start()
        for i in range(n):
            forward(i, 1 - c).wait_recv()
            forward(i, c).wait_send()
            for j in range(3):
                remote(i, j, me).wait_send()

    return start, finish


def grad_sync(halves, targets, bufs, parts, packed, name="grad_sync"):
    nh, n, nb = len(halves), len(parts), len(bufs)
    passed = [i for i, b in enumerate(bufs) if not isinstance(b, jax.ShapeDtypeStruct)]
    n_in = nh + n + 1 + len(passed)

    def body(*refs):
        hin, ins, pk = refs[:nh], refs[nh:nh + n], refs[nh + n]
        outs = refs[n_in:]
        landed, mine, theirs, pk_all = outs[:nb], outs[nb:nb + n], outs[nb + n:nb + 2 * n], outs[nb + 2 * n]
        sems = outs[nb + 2 * n + 1:]
        send, recv, loc, fsend, frecv, psend, precv, ploc = sems[5:]
        start, finish = _scatter_steps(targets, hin, landed, sems[:5])
        x, y, c = lax.axis_index("x"), lax.axis_index("y"), lax.axis_index("c")
        me = 2 * x + y
        dev = 4 * x + 2 * y + c
        sibling = (x, y, 1 - c)

        def remote(i, j, slot):
            px, py = _flip(x, _CHIP_FLIPS[j][0]), _flip(y, _CHIP_FLIPS[j][1])
            return pltpu.make_async_remote_copy(
                src_ref=ins[i].at[2 * px + py], dst_ref=mine[i].at[slot], send_sem=send.at[3 * i + j],
                recv_sem=recv.at[3 * i + j], device_id=(px, py, c), device_id_type=_MESH)

        def packed_to(r, slot):
            px, py, pc = _flip(x, r & 4), _flip(y, r & 2), _flip(c, r & 1)
            return pltpu.make_async_remote_copy(
                src_ref=pk, dst_ref=pk_all.at[slot], send_sem=psend.at[r - 1], recv_sem=precv.at[r - 1],
                device_id=(px, py, pc), device_id_type=_MESH)

        def forward(i):
            return pltpu.make_async_remote_copy(
                src_ref=mine[i], dst_ref=theirs[i], send_sem=fsend.at[i], recv_sem=frecv.at[i],
                device_id=sibling, device_id_type=_MESH)

        start()
        local = [pltpu.make_async_copy(ins[i].at[me], mine[i].at[me], loc.at[i]) for i in range(n)]
        plocal = pltpu.make_async_copy(pk, pk_all.at[dev], ploc.at[0])
        sends = [remote(i, j, me) for i in range(n) for j in range(3)]
        psends = [packed_to(r, dev) for r in range(1, N_DEV)]
        for cp in local + [plocal] + sends + psends:
            cp.start()
        fwd = [forward(i) for i in range(n)]
        for i in range(n):
            for j, (fx, fy) in enumerate(_CHIP_FLIPS):
                remote(i, j, 2 * _flip(x, fx) + _flip(y, fy)).wait_recv()
            local[i].wait()
            fwd[i].start()
        finish()
        for i in range(n):
            fwd[i].wait_recv()
        for r in range(1, N_DEV):
            packed_to(r, 4 * _flip(x, r & 4) + 2 * _flip(y, r & 2) + _flip(c, r & 1)).wait_recv()
        for cp in sends + psends + fwd:
            cp.wait_send()
        plocal.wait()

    land = [jax.ShapeDtypeStruct(p.shape, p.dtype) for p in parts]
    out = pl.pallas_call(
        body, in_specs=[_ANY] * n_in, out_specs=[_ANY] * (nb + 2 * n + 1),
        out_shape=[jax.ShapeDtypeStruct(b.shape, b.dtype) for b in bufs] + land + land
        + [jax.ShapeDtypeStruct((N_DEV,) + packed.shape, packed.dtype)],
        input_output_aliases={nh + n + 1 + k: i for k, i in enumerate(passed)},
        scratch_shapes=_scatter_semaphores(nh) + [
            pltpu.SemaphoreType.DMA((3 * n,)), pltpu.SemaphoreType.DMA((3 * n,)), pltpu.SemaphoreType.DMA((n,)),
            pltpu.SemaphoreType.DMA((n,)), pltpu.SemaphoreType.DMA((n,)), pltpu.SemaphoreType.DMA((N_DEV - 1,)),
            pltpu.SemaphoreType.DMA((N_DEV - 1,)), pltpu.SemaphoreType.DMA((1,))],
        name=name)(*halves, *parts, packed, *[bufs[i] for i in passed])
    return out[:nb], out[nb:nb + n], out[nb + n:nb + 2 * n], out[nb + 2 * n]


def adamw(w, m, v, parts, name):
    r, c = w.shape
    tr = r
    for cand in (512, 256, 128, 64, 32, 16, 8):
        if r % cand == 0 and cand * c * 4 <= 512 * 1024:
            tr = cand
            break
    np_ = len(parts)
    nslot = parts[0].shape[0]
    bc1 = 1.0 - ADAM_B1 ** ADAM_STEP
    bc2 = 1.0 - ADAM_B2 ** ADAM_STEP

    def body(*refs):
        w_ref, m_ref, v_ref = refs[:3]
        p_refs = refs[3:3 + np_]
        g_ref, d_ref, nm_ref, nv_ref = refs[3 + np_:]
        g = None
        for k in range(nslot):
            t = p_refs[0][k].astype(F32)
            for p_ref in p_refs[1:]:
                t = t + p_ref[k].astype(F32)
            g = t if g is None else g + t
        wv = w_ref[...]
        nm = ADAM_B1 * m_ref[...] + (1.0 - ADAM_B1) * g
        nv = ADAM_B2 * v_ref[...] + (1.0 - ADAM_B2) * (g * g)
        g_ref[...] = g
        nm_ref[...] = nm
        nv_ref[...] = nv
        d_ref[...] = -ADAM_LR * ((nm / bc1) / (jnp.sqrt(nv / bc2) + ADAM_EPS) + ADAM_WD * wv)

    row = pl.BlockSpec((tr, c), lambda i: (i, 0))
    slab = pl.BlockSpec((nslot, tr, c), lambda i: (0, i, 0))
    sh = jax.ShapeDtypeStruct((r, c), F32)
    return pl.pallas_call(
        body, grid=(r // tr,), in_specs=[row, row, row] + [slab] * np_, out_specs=[row] * 4,
        out_shape=[sh] * 4, compiler_params=_cp("parallel"), name=name)(w, m, v, *parts)


WEIGHTS = ["norm_mix_g", "norm_ffn_g", "pool_w", "pool_b", "pool_scale", "s5_lam_re", "s5_lam_im", "s5_log_dt",
           "s5_b_re", "s5_b_im", "s5_c_re", "s5_c_im", "s5_d", "s5_w_out", "s5_b_out", "lru_w_in", "lru_conv_w",
           "lru_conv_b", "lru_w_a", "lru_b_a", "lru_w_x", "lru_b_x", "lru_lam", "lru_w_out", "sb_w_qkv", "sb_q_g",
           "sb_k_g", "sb_w_o", "ffn_w_in", "ffn_conv_w", "ffn_conv_b", "ffn_w_out"]
SHARD_AXIS = dict(pool_w=2, s5_d=1, s5_w_out=2, s5_b_out=1, lru_w_in=2, lru_conv_w=2, lru_conv_b=1, lru_w_a=2,
                  lru_b_a=1, lru_w_x=2, lru_b_x=1, lru_lam=1, lru_w_out=1, sb_w_qkv=2, sb_w_o=1, ffn_w_in=2,
                  ffn_conv_w=2, ffn_w_out=1)
MXU_WEIGHTS = ("pool_w", "s5_w_out", "lru_w_in", "lru_w_a", "lru_w_x", "lru_w_out", "sb_w_qkv", "sb_w_o",
               "ffn_w_in", "ffn_w_out")
SHARDED = [n for n in WEIGHTS if n in SHARD_AXIS]
REPLICATED = [n for n in WEIGHTS if n not in SHARD_AXIS]
PACK_WIDTH = 1024


def _as_rows(a):
    return a.reshape(-1, a.shape[-1])


def _pack(arrays):
    rows = []
    for a in arrays:
        flat = a.reshape(-1)
        pad = (-flat.shape[0]) % PACK_WIDTH
        rows.append(jnp.pad(flat, (0, pad)).reshape(-1, PACK_WIDTH))
    out = jnp.concatenate(rows, axis=0)
    return jnp.pad(out, ((0, (-out.shape[0]) % 8), (0, 0)))


def _unpack(packed, like):
    out, r = [], 0
    for a in like:
        size = math.prod(a.shape)
        nrow = -(-size // PACK_WIDTH)
        out.append(packed[r:r + nrow].reshape(-1)[:size].reshape(a.shape))
        r += nrow
    return out


def kernel(*args):
    names = ["x"] + WEIGHTS + ["loss_target"] + ["m_" + n for n in WEIGHTS] + ["v_" + n for n in WEIGHTS]
    assert len(args) == len(names)
    given = dict(zip(names, args))
    x, target = given["x"], given["loss_target"]
    bsz, seq, d = x.shape
    m_tok = bsz * seq

    pieces = {}
    for n in SHARDED:
        a = given[n].astype(BF16) if n in MXU_WEIGHTS else given[n]
        if n in ("ffn_w_in", "ffn_w_out"):
            for l in range(4):
                pieces[(n, l)] = (a[l], a[l].shape, SHARD_AXIS[n] - 1)
        else:
            pieces[(n, None)] = (_as_rows(a), a.shape, SHARD_AXIS[n])
    early = [k for k in pieces if k[0] in ("pool_w", "s5_w_out") or k[1] == 0 or not _splits(pieces[k][0].shape)]
    late = [k for k in pieces if k not in early]
    p = {n: given[n] for n in REPLICATED}
    p.update(ffn_w_in=[None] * 4, ffn_w_out=[None] * 4)
    my_chip = 2 * lax.axis_index("x") + lax.axis_index("y")

    def assemble(keys, gathered):
        for key, g4 in zip(keys, gathered):
            own, shape, axis = pieces[key]
            blocks = g4.reshape((N_CHIPS,) + shape)
            own = own.reshape(shape)
            full = jnp.concatenate([jnp.where(my_chip == k, own, blocks[k]) for k in range(N_CHIPS)], axis=axis)
            if key[1] is None:
                p[key[0]] = full
            else:
                p[key[0]][key[1]] = full

    assemble(early, gather_weights([pieces[k][0] for k in early]))

    def mixer_params():
        return {k: (v[0] if v.ndim > 2 or k == "s5_log_dt" else v) for k, v in p.items()
                if not k.startswith(("norm_", "ffn_"))}

    mixers = ((pool_layer_fwd, pool_layer_bwd), (s5_layer_fwd, s5_layer_bwd), (lru_layer_fwd, lru_layer_bwd),
              (sb_layer_fwd, sb_layer_bwd))
    ffn_p = lambda l: (p["norm_ffn_g"][l:l + 1], p["ffn_w_in"][l], p["ffn_conv_w"][l], p["ffn_conv_b"][l:l + 1],
                       p["ffn_w_out"][l])

    def mixer_args(l):
        g = p["norm_mix_g"][l:l + 1]
        if l == 0:
            return (g, mix["pool_w"], mix["pool_b"], mix["pool_scale"], bsz)
        return (g, mix, bsz)

    h = x.reshape(m_tok, d)
    saved = []
    mix = mixer_params()
    for l in range(4):
        if l == 1:
            h, s_mix, gathered = s5_layer_fwd(h, *mixer_args(l), later_shards=[pieces[k][0] for k in late])
            assemble(late, gathered)
            mix = mixer_params()
        else:
            h, s_mix = mixers[l][0](h, *mixer_args(l))
        gl, w_in, cw, cb, w_out = ffn_p(l)
        h, s_ffn = ffn_fwd(h, gl, w_in, cw, cb, w_out, bsz)
        saved.append((s_mix, s_ffn))
    dh, loss_part = loss_head(h, target.reshape(m_tok, d))
    loss = lax.psum(jnp.sum(loss_part), ("x", "y", "c"))

    def full_shape(n):
        shape = list(given[n].shape)
        if n in SHARD_AXIS:
            shape[SHARD_AXIS[n]] *= N_CHIPS
        return tuple(shape)

    grads = {}
    ffn_g = [None] * 4
    mix_g = [None] * 4
    core = lax.axis_index("c").astype(jnp.int32).reshape(1)
    big = [k for k in pieces if _splits(pieces[k][0].shape)]
    small = [k[0] for k in pieces if k not in big]
    behind_s5 = [k for k in big if k[0].startswith(("sb_", "lru_")) or (k[1] is not None and k[1] >= 1)]
    at_end = [k for k in big if k not in behind_s5]

    def part_of(key):
        n, l = key
        g, axis = (grads[n], SHARD_AXIS[n]) if l is None else (ffn_g[l][n], SHARD_AXIS[n] - 1)
        blocks = jnp.stack(jnp.split(g, N_CHIPS, axis=axis))
        return blocks.reshape(N_CHIPS, -1, blocks.shape[-1])

    def reduce_pairs(keys, tag):
        mine = [part_of(k) for k in keys]
        got = grad_halves_exchange(mine, name="grad_halves_exchange_" + tag)
        return [pair_sum(a, b, core, BF16, "pair_sum_%s_%s" % k) for k, a, b in zip(keys, mine, got)]

    def landing(keys, made):
        names = list(dict.fromkeys(k[0] for k in keys))
        bufs = [made.get(n, jax.ShapeDtypeStruct((N_CHIPS,) + _as_rows(given[n]).shape, BF16)) for n in names]
        targets = []
        for n, l in keys:
            rows = pieces[(n, l)][0].shape[0]
            targets.append((names.index(n), (l or 0) * rows, rows))
        return names, bufs, targets

    landed = {}
    for l in range(3, -1, -1):
        gl, w_in, cw, cb, w_out = ffn_p(l)
        dh, ffn_g[l] = ffn_bwd(saved[l][1], dh, gl, w_in, cw, cb, w_out, bsz)
        if l == 1:
            names, bufs, targets = landing(behind_s5, landed)
            dh, mix_g[l], filled = s5_layer_bwd(saved[l][0], dh, *mixer_args(l),
                                                scatter=(reduce_pairs(behind_s5, "behind_s5"), targets, bufs))
            landed.update(zip(names, filled))
        else:
            dh, mix_g[l] = mixers[l][1](saved[l][0], dh, *mixer_args(l))
        for k, v in mix_g[l].items():
            if k != "norm_mix_g":
                grads[k] = v.reshape(full_shape(k))
    for k in ("norm_ffn_g", "ffn_conv_w", "ffn_conv_b"):
        grads[k] = jnp.stack([ffn_g[l][k] for l in range(4)]).reshape(full_shape(k))
    grads["norm_mix_g"] = jnp.concatenate([mix_g[l]["norm_mix_g"] for l in range(4)], axis=0)
    grad_x = dh.reshape(bsz, seq, d)

    names, bufs, targets = landing(at_end, landed)
    filled, mine, theirs, packed_all = grad_sync(reduce_pairs(at_end, "at_end"), targets, bufs,
                                                 [part_of((n, None)) for n in small],
                                                 _pack([grads[n] for n in REPLICATED]))
    landed.update(zip(names, filled))

    summed = {n: [landed[n]] for n in landed}
    summed.update({n: [mine[i], theirs[i]] for i, n in enumerate(small)})

    out = {}
    for n in SHARDED:
        res = adamw(_as_rows(given[n]), _as_rows(given["m_" + n]), _as_rows(given["v_" + n]), summed[n], "adamw_" + n)
        out[n] = [r.reshape(given[n].shape) for r in res]
    res = adamw(_pack([given[n] for n in REPLICATED]), _pack([given["m_" + n] for n in REPLICATED]),
                _pack([given["v_" + n] for n in REPLICATED]), [packed_all], "adamw_replicated")
    like = [given[n] for n in REPLICATED]
    for n, *vals in zip(REPLICATED, *[_unpack(r, like) for r in res]):
        out[n] = list(vals)
    return (loss, grad_x, *[out[n][0] for n in WEIGHTS], *[out[n][1] for n in WEIGHTS],
            *[out[n][2] for n in WEIGHTS], *[out[n][3] for n in WEIGHTS])
```

```python
import functools
import math

import jax
import jax.numpy as jnp
from jax import lax
from jax.experimental import pallas as pl
from jax.experimental.pallas import tpu as pltpu

F32 = jnp.float32
BF16 = jnp.bfloat16

EPS = 1e-6
N_CHIPS = 4
N_DEV = 8
POOL_WINDOWS = (2, 4, 8, 16)
POOL_GROUP = 256
S5_GROUP = 16
S5_STATE = 64
S5_CHUNKS = 8
S5_LANES = 512
S5_T = 256
LRU_BLOCK = 256
LRU_CONV = 4
LRU_C = 8.0
SB_HEADS = 16
SB_DIM = 64
SB_BLOCK = 512
SB_CHUNK = 128
FFN_CONV = 3
ADAM_LR, ADAM_B1, ADAM_B2, ADAM_EPS, ADAM_WD, ADAM_STEP = 0.001, 0.9, 0.999, 1e-08, 0.01, 10
VMEM_LIMIT_BYTES = 56 * 1024 * 1024
MATMUL_VMEM_BUDGET = 30 * 1024 * 1024
MATMUL_WHOLE_K = 2816

NN = (((1,), (0,)), ((), ()))
NT = (((1,), (1,)), ((), ()))
TN = (((0,), (0,)), ((), ()))


def _cp(*sem):
    return pltpu.CompilerParams(dimension_semantics=sem, vmem_limit_bytes=VMEM_LIMIT_BYTES)


def _pick(n, prefs):
    for p in prefs:
        if n % p == 0:
            return p
    return n


def _dot(a, b, dims=NN):
    return lax.dot_general(a.astype(BF16), b.astype(BF16), dims, preferred_element_type=F32)


def _split(x):
    hi = x.astype(BF16)
    return hi, (x - hi.astype(F32)).astype(BF16)


def _dot3(a, b, dims=NN):
    ah, al = _split(a)
    bh, bl = _split(b)
    d = lambda p, q: lax.dot_general(p, q, dims, preferred_element_type=F32)
    return d(ah, bh) + (d(ah, bl) + d(al, bh))


def _dot_exact_rhs(a, b01):
    ah, al = _split(a)
    d = lambda p: lax.dot_general(p, b01, NN, preferred_element_type=F32)
    return d(ah) + d(al)


def _sig(x):
    return 1.0 / (1.0 + jnp.exp(-x))


def _softplus(x):
    return jnp.maximum(x, 0.0) + jnp.log(1.0 + jnp.exp(-jnp.abs(x)))


_GELU_C = math.sqrt(2.0 / math.pi)


def _gelu(x):
    return 0.5 * x * (1.0 + jnp.tanh(_GELU_C * (x + 0.044715 * x * x * x)))


def _gelu_grad(x):
    th = jnp.tanh(_GELU_C * (x + 0.044715 * x * x * x))
    return 0.5 * (1.0 + th) + 0.5 * x * (1.0 - th * th) * _GELU_C * (1.0 + 3.0 * 0.044715 * x * x)


def _rows(shape):
    return lax.broadcasted_iota(jnp.int32, shape, 0)


SUBLANES = 8


def _shift_all(xs, k, up):
    t = xs[0].shape[0]
    if k >= t:
        return [jnp.zeros_like(x) for x in xs]
    if k % SUBLANES == 0:
        pad = jnp.zeros((k,) + xs[0].shape[1:], xs[0].dtype)
        return [jnp.concatenate([x[k:], pad] if up else [pad, x[:t - k]], axis=0) for x in xs]
    rows = _rows(xs[0].shape)
    keep = rows < t - k if up else rows >= k
    return [jnp.where(keep, pltpu.roll(x, t - k if up else k, 0), 0.0) for x in xs]


def _shift_down(x, k):
    return _shift_all([x], k, False)[0]


def _shift_up(x, k):
    return _shift_all([x], k, True)[0]


def matmul(a, b, *, ta=False, tb=False, bias=None, res=None, out_dtype=F32, name):
    m, k = (a.shape[1], a.shape[0]) if ta else a.shape
    n = b.shape[0] if tb else b.shape[1]
    assert (b.shape[1] if tb else b.shape[0]) == k
    has_bias, has_res = bias is not None, res is not None
    tk = k if k <= MATMUL_WHOLE_K else _pick(k, (1408, 1024, 512, 256, 128))
    nk = k // tk
    sa, sb, so = a.dtype.itemsize, b.dtype.itemsize, jnp.dtype(out_dtype).itemsize
    tm = tn = None
    for cm in (2048, 1024, 512, 1408, 256, 128):
        for cn in (1024, 512, 1408, 256, 128):
            if m % cm or n % cn:
                continue
            need = 2 * cm * tk * sa + 2 * tk * cn * sb + cm * cn * (2 * so + 4 + (4 if nk > 1 else 0)
                                                                  + (8 if has_res else 0))
            if need <= MATMUL_VMEM_BUDGET and (tm is None or cm * cn > tm * tn):
                tm, tn = cm, cn
    assert tm is not None, (m, n, k)
    dims = (((0 if ta else 1,), (1 if tb else 0,)), ((), ()))

    def body(*refs):
        a_ref, b_ref = refs[:2]
        rest = list(refs[2:])
        bias_ref = rest.pop(0) if has_bias else None
        res_ref = rest.pop(0) if has_res else None
        o_ref = rest[0]

        def finish(r):
            if has_bias:
                r = r + bias_ref[...]
            if has_res:
                r = r + res_ref[...]
            o_ref[...] = r.astype(o_ref.dtype)

        part = lax.dot_general(a_ref[...].astype(BF16), b_ref[...].astype(BF16), dims, preferred_element_type=F32)
        if nk == 1:
            finish(part)
            return
        acc_ref = rest[1]
        kk = pl.program_id(2)

        @pl.when(kk == 0)
        def _():
            acc_ref[...] = part

        @pl.when(kk > 0)
        def _():
            acc_ref[...] += part

        @pl.when(kk == nk - 1)
        def _():
            finish(acc_ref[...])

    in_specs = [
        pl.BlockSpec((tk, tm), lambda i, j, kk: (kk, i)) if ta else pl.BlockSpec((tm, tk), lambda i, j, kk: (i, kk)),
        pl.BlockSpec((tn, tk), lambda i, j, kk: (j, kk)) if tb else pl.BlockSpec((tk, tn), lambda i, j, kk: (kk, j)),
    ]
    args = [a, b]
    if has_bias:
        in_specs.append(pl.BlockSpec((1, tn), lambda i, j, kk: (0, j)))
        args.append(bias)
    if has_res:
        in_specs.append(pl.BlockSpec((tm, tn), lambda i, j, kk: (i, j)))
        args.append(res)
    return pl.pallas_call(
        body, grid=(m // tm, n // tn, nk), in_specs=in_specs,
        out_specs=pl.BlockSpec((tm, tn), lambda i, j, kk: (i, j)),
        out_shape=jax.ShapeDtypeStruct((m, n), out_dtype),
        scratch_shapes=[pltpu.VMEM((tm, tn), F32)] if nk > 1 else [],
        compiler_params=_cp("parallel", "parallel", "arbitrary"), name=name)(*args)


def rms_fwd(x, g, out_dtype, name):
    m, d = x.shape
    tr = _pick(m, (512, 256, 128))

    def body(x_ref, g_ref, o_ref):
        xv = x_ref[...]
        r = lax.rsqrt(jnp.mean(xv * xv, axis=-1, keepdims=True) + EPS)
        o_ref[...] = (xv * r * g_ref[...]).astype(o_ref.dtype)

    return pl.pallas_call(
        body, grid=(m // tr,),
        in_specs=[pl.BlockSpec((tr, d), lambda i: (i, 0)), pl.BlockSpec((1, d), lambda i: (0, 0))],
        out_specs=pl.BlockSpec((tr, d), lambda i: (i, 0)),
        out_shape=jax.ShapeDtypeStruct((m, d), out_dtype),
        compiler_params=_cp("parallel"), name=name)(x, g)


def rms_bwd(x, g, dh, dres, name):
    m, d = x.shape
    tr = _pick(m, (512, 256, 128))

    def body(x_ref, g_ref, dh_ref, dres_ref, dx_ref, dg_ref):
        xv = x_ref[...]
        r = lax.rsqrt(jnp.mean(xv * xv, axis=-1, keepdims=True) + EPS)
        xh = xv * r
        dhv = dh_ref[...].astype(F32)
        dxh = dhv * g_ref[...]
        dx_ref[...] = dres_ref[...] + r * (dxh - xh * jnp.mean(dxh * xh, axis=-1, keepdims=True))

        @pl.when(pl.program_id(0) == 0)
        def _():
            dg_ref[...] = jnp.zeros_like(dg_ref)

        dg_ref[...] += jnp.sum(dhv * xh, axis=0, keepdims=True)

    row = pl.BlockSpec((tr, d), lambda i: (i, 0))
    vec = pl.BlockSpec((1, d), lambda i: (0, 0))
    return pl.pallas_call(
        body, grid=(m // tr,), in_specs=[row, vec, row, row], out_specs=[row, vec],
        out_shape=[jax.ShapeDtypeStruct((m, d), F32), jax.ShapeDtypeStruct((1, d), F32)],
        compiler_params=_cp("arbitrary"), name=name)(x, g, dh, dres)


def _conv_taps(u, kw):
    return [_shift_down(u, kw - 1 - k) for k in range(kw - 1)] + [u]


def _conv_fwd(taps, w_ref, b_ref):
    y = b_ref[...] + w_ref[len(taps) - 1:len(taps), :] * taps[-1]
    for k in range(len(taps) - 1):
        y = y + w_ref[k:k + 1, :] * taps[k]
    return y


def _conv_bwd_input(dy, w_ref, kw):
    du = w_ref[kw - 1:kw, :] * dy
    for k in range(kw - 1):
        du = du + w_ref[k:k + 1, :] * _shift_up(dy, kw - 1 - k)
    return du


def _conv_bwd_weight(dy, taps):
    return [jnp.sum(dy * tap, axis=0, keepdims=True) for tap in taps]


def ffn_act_fwd(u, cw, cb, name):
    bsz, s, f2 = u.shape
    f = f2 // 2
    tc = _pick(f, (256, 128))
    nj = f // tc

    def body(uv_ref, ug_ref, wv_ref, wg_ref, bv_ref, bg_ref, a_ref):
        hv = _conv_fwd(_conv_taps(uv_ref[0], FFN_CONV), wv_ref, bv_ref)
        hg = _conv_fwd(_conv_taps(ug_ref[0], FFN_CONV), wg_ref, bg_ref)
        a_ref[0] = (hg * _sig(hg) * hv).astype(a_ref.dtype)

    uv = pl.BlockSpec((1, s, tc), lambda b, j: (b, 0, j))
    ug = pl.BlockSpec((1, s, tc), lambda b, j: (b, 0, j + nj))
    wv = pl.BlockSpec((FFN_CONV, tc), lambda b, j: (0, j))
    wg = pl.BlockSpec((FFN_CONV, tc), lambda b, j: (0, j + nj))
    bv = pl.BlockSpec((1, tc), lambda b, j: (0, j))
    bg = pl.BlockSpec((1, tc), lambda b, j: (0, j + nj))
    return pl.pallas_call(
        body, grid=(bsz, nj), in_specs=[uv, ug, wv, wg, bv, bg], out_specs=uv,
        out_shape=jax.ShapeDtypeStruct((bsz, s, f), BF16),
        compiler_params=_cp("parallel", "parallel"), name=name)(u, u, cw, cw, cb, cb)


def ffn_act_bwd(u, cw, cb, da, name):
    bsz, s, f2 = u.shape
    f = f2 // 2
    tc = _pick(f, (256, 128))
    nj = f // tc

    def body(uv_ref, ug_ref, wv_ref, wg_ref, bv_ref, bg_ref, da_ref,
             duv_ref, dug_ref, dwv_ref, dwg_ref, dbv_ref, dbg_ref):
        tv, tg = _conv_taps(uv_ref[0], FFN_CONV), _conv_taps(ug_ref[0], FFN_CONV)
        hv = _conv_fwd(tv, wv_ref, bv_ref)
        hg = _conv_fwd(tg, wg_ref, bg_ref)
        sg = _sig(hg)
        dav = da_ref[0].astype(F32)
        dhv = dav * hg * sg
        dhg = dav * hv * (sg * (1.0 + hg * (1.0 - sg)))
        duv_ref[0] = _conv_bwd_input(dhv, wv_ref, FFN_CONV).astype(duv_ref.dtype)
        dug_ref[0] = _conv_bwd_input(dhg, wg_ref, FFN_CONV).astype(dug_ref.dtype)

        @pl.when(pl.program_id(1) == 0)
        def _():
            for r in (dwv_ref, dwg_ref, dbv_ref, dbg_ref):
                r[...] = jnp.zeros_like(r)

        for k, row in enumerate(_conv_bwd_weight(dhv, tv)):
            dwv_ref[k:k + 1, :] += row
        for k, row in enumerate(_conv_bwd_weight(dhg, tg)):
            dwg_ref[k:k + 1, :] += row
        dbv_ref[...] += jnp.sum(dhv, axis=0, keepdims=True)
        dbg_ref[...] += jnp.sum(dhg, axis=0, keepdims=True)

    uv = pl.BlockSpec((1, s, tc), lambda j, b: (b, 0, j))
    ug = pl.BlockSpec((1, s, tc), lambda j, b: (b, 0, j + nj))
    wv = pl.BlockSpec((FFN_CONV, tc), lambda j, b: (0, j))
    wg = pl.BlockSpec((FFN_CONV, tc), lambda j, b: (0, j + nj))
    bv = pl.BlockSpec((1, tc), lambda j, b: (0, j))
    bg = pl.BlockSpec((1, tc), lambda j, b: (0, j + nj))
    act = jax.ShapeDtypeStruct((bsz, s, f), BF16)
    return pl.pallas_call(
        body, grid=(nj, bsz), in_specs=[uv, ug, wv, wg, bv, bg, uv],
        out_specs=[uv, uv, wv, wv, bv, bv],
        out_shape=[act, act, jax.ShapeDtypeStruct((FFN_CONV, f), F32), jax.ShapeDtypeStruct((FFN_CONV, f), F32),
                   jax.ShapeDtypeStruct((1, f), F32), jax.ShapeDtypeStruct((1, f), F32)],
        compiler_params=_cp("parallel", "arbitrary"), name=name)(u, u, cw, cw, cb, cb, da)


def ffn_fwd(x, g, w_in, cw, cb, w_out, bsz):
    m, d = x.shape
    h = rms_fwd(x, g, BF16, "ffn_norm")
    u = matmul(h, w_in, name="ffn_in")
    a = ffn_act_fwd(u.reshape(bsz, m // bsz, -1), cw, cb, "ffn_act")
    a2 = a.reshape(m, -1)
    out = matmul(a2, w_out, res=x, name="ffn_out")
    return out, (x, h, u, a2)


def ffn_bwd(saved, dout, g, w_in, cw, cb, w_out, bsz):
    x, h, u, a2 = saved
    m, d = x.shape
    da = matmul(dout, w_out, tb=True, out_dtype=BF16, name="ffn_out_dx")
    dw_out = matmul(a2, dout, ta=True, name="ffn_out_dw")
    u3 = u.reshape(bsz, m // bsz, -1)
    duv, dug, dwv, dwg, dbv, dbg = ffn_act_bwd(u3, cw, cb, da.reshape(bsz, m // bsz, -1), "ffn_act_bwd")
    du = jnp.concatenate([duv, dug], axis=-1).reshape(m, -1)
    dh = matmul(du, w_in, tb=True, name="ffn_in_dx")
    dw_in = matmul(h, du, ta=True, name="ffn_in_dw")
    dx, dg = rms_bwd(x, g, dh, dout, "ffn_norm_bwd")
    grads = dict(norm_ffn_g=dg, ffn_w_in=dw_in, ffn_conv_w=jnp.concatenate([dwv, dwg], axis=-1),
                 ffn_conv_b=jnp.concatenate([dbv, dbg], axis=-1), ffn_w_out=dw_out)
    return dx, grads


def loss_head(y, target, name="loss_head"):
    m, d = y.shape
    tr = _pick(m, (512, 256, 128))

    def body(y_ref, t_ref, dy_ref, l_ref):
        e = y_ref[...] - t_ref[...]
        dy_ref[...] = e * (1.0 / d)

        @pl.when(pl.program_id(0) == 0)
        def _():
            l_ref[...] = jnp.zeros_like(l_ref)

        l_ref[...] += jnp.sum(e * e, axis=0, keepdims=True) * (0.5 / d)

    row = pl.BlockSpec((tr, d), lambda i: (i, 0))
    vec = pl.BlockSpec((1, d), lambda i: (0, 0))
    dy, part = pl.pallas_call(
        body, grid=(m // tr,), in_specs=[row, row], out_specs=[row, vec],
        out_shape=[jax.ShapeDtypeStruct((m, d), F32), jax.ShapeDtypeStruct((1, d), F32)],
        compiler_params=_cp("arbitrary"), name=name)(y, target)
    return dy, part


def _pool_windows(h, gi):
    sums, s, width = [], h, 1
    for _ in POOL_WINDOWS:
        s = s + _shift_down(s, width)
        width *= 2
        sums.append(s)
    pos = _rows(h.shape).astype(F32) + 1.0
    wsum, inv = sums[-1], 1.0 / jnp.minimum(pos, float(POOL_WINDOWS[-1]))
    for k in range(len(POOL_WINDOWS) - 2, -1, -1):
        wsum = jnp.where(gi == k, sums[k], wsum)
        inv = jnp.where(gi == k, 1.0 / jnp.minimum(pos, float(POOL_WINDOWS[k])), inv)
    return wsum * inv - h, inv


def _pool_windows_transpose(e, gi):
    sums, s, width = [], e, 1
    for _ in POOL_WINDOWS:
        s = s + _shift_up(s, width)
        width *= 2
        sums.append(s)
    out = sums[-1]
    for k in range(len(POOL_WINDOWS) - 2, -1, -1):
        out = jnp.where(gi == k, sums[k], out)
    return out


def pool_fwd(h, w, b, scale, x, name="pool_fwd"):
    bsz, s, d = h.shape
    ng = d // POOL_GROUP

    def body(h_ref, w_ref, b_ref, s_ref, x_ref, o_ref):
        dd, _ = _pool_windows(h_ref[0], pl.program_id(1))
        y = _dot(dd, w_ref[0]) + b_ref[...]
        o_ref[0] = x_ref[0] + s_ref[...] * y

    act = pl.BlockSpec((1, s, POOL_GROUP), lambda bb, gi: (bb, 0, gi))
    vec = pl.BlockSpec((1, POOL_GROUP), lambda bb, gi: (0, gi))
    return pl.pallas_call(
        body, grid=(bsz, ng),
        in_specs=[act, pl.BlockSpec((1, POOL_GROUP, POOL_GROUP), lambda bb, gi: (gi, 0, 0)), vec, vec, act],
        out_specs=act, out_shape=jax.ShapeDtypeStruct((bsz, s, d), F32),
        compiler_params=_cp("parallel", "parallel"), name=name)(h, w, b, scale, x)


def pool_bwd(h, w, b, scale, dy, name="pool_bwd"):
    bsz, s, d = h.shape
    ng = d // POOL_GROUP

    def body(h_ref, w_ref, b_ref, s_ref, dy_ref, dh_ref, dw_ref, db_ref, ds_ref):
        gi = pl.program_id(0)
        dd, inv = _pool_windows(h_ref[0], gi)
        ypre = _dot(dd, w_ref[0]) + b_ref[...]
        dyv = dy_ref[0]
        dyb = dyv * s_ref[...]

        @pl.when(pl.program_id(1) == 0)
        def _():
            for r in (dw_ref, db_ref, ds_ref):
                r[...] = jnp.zeros_like(r)

        ds_ref[...] += jnp.sum(dyv * ypre, axis=0, keepdims=True)
        db_ref[...] += jnp.sum(dyb, axis=0, keepdims=True)
        dw_ref[0] += _dot(dd, dyb, TN)
        ddd = _dot(dyb, w_ref[0], NT)
        dh_ref[0] = _pool_windows_transpose(ddd * inv, gi) - ddd

    act = pl.BlockSpec((1, s, POOL_GROUP), lambda gi, bb: (bb, 0, gi))
    vec = pl.BlockSpec((1, POOL_GROUP), lambda gi, bb: (0, gi))
    wsp = pl.BlockSpec((1, POOL_GROUP, POOL_GROUP), lambda gi, bb: (gi, 0, 0))
    return pl.pallas_call(
        body, grid=(ng, bsz), in_specs=[act, wsp, vec, vec, act], out_specs=[act, wsp, vec, vec],
        out_shape=[jax.ShapeDtypeStruct((bsz, s, d), F32), jax.ShapeDtypeStruct((ng, POOL_GROUP, POOL_GROUP), F32),
                   jax.ShapeDtypeStruct((1, d), F32), jax.ShapeDtypeStruct((1, d), F32)],
        compiler_params=_cp("parallel", "arbitrary"), name=name)(h, w, b, scale, dy)


def pool_layer_fwd(x, g, w, b, scale, bsz):
    m, d = x.shape
    h = rms_fwd(x, g, F32, "pool_norm")
    out = pool_fwd(h.reshape(bsz, m // bsz, d), w, b, scale, x.reshape(bsz, m // bsz, d))
    return out.reshape(m, d), (x, h)


def pool_layer_bwd(saved, dout, g, w, b, scale, bsz):
    x, h = saved
    m, d = x.shape
    dh, dw, db, ds = pool_bwd(h.reshape(bsz, m // bsz, d), w, b, scale, dout.reshape(bsz, m // bsz, d))
    dx, dg = rms_bwd(x, g, dh.reshape(m, d), dout, "pool_norm_bwd")
    return dx, dict(norm_mix_g=dg, pool_w=dw[None], pool_b=db, pool_scale=ds)


def _scan_fwd(a, b):
    return _scan(a, b, False)


def _scan(a, b, up):
    k = 1
    while k < a.shape[0]:
        if 2 * k < a.shape[0]:
            sa, sb = _shift_all([a, b], k, up)
            a, b = a * sa, b + a * sb
        else:
            b = b + a * _shift_all([b], k, up)[0]
        k *= 2
    return b


def _scan_bwd(a, b):
    return _scan(a, b, True)


def _neg_expm1(x):
    series = -x * (1.0 + x * (0.5 + x * (1.0 / 6.0 + x * (1.0 / 24.0 + x * (1.0 / 120.0)))))
    return jnp.where(x > -0.03, series, 1.0 - jnp.exp(x))


def _lru_gates(rec, wa_ref, ba_ref, wx_ref, bx_ref, lam_ref):
    r = _sig(_dot(rec, wa_ref[0]) + ba_ref[...])
    i = _sig(_dot(rec, wx_ref[0]) + bx_ref[...])
    sp = _softplus(-lam_ref[...])
    log_a = -LRU_C * r * sp
    a = jnp.exp(log_a)
    mult = jnp.sqrt(_neg_expm1(2.0 * log_a))
    return r, i, sp, a, mult


def lru_fwd(zz, cw, cb, wa, ba, wx, bx, lam, name="lru_fwd"):
    bsz, s, r2 = zz.shape
    rw = r2 // 2
    nb = rw // LRU_BLOCK

    def body(g_ref, p_ref, cw_ref, cb_ref, wa_ref, ba_ref, wx_ref, bx_ref, lam_ref, h_ref, y_ref):
        rec = _conv_fwd(_conv_taps(p_ref[0], LRU_CONV), cw_ref, cb_ref)
        _, i, _, a, mult = _lru_gates(rec, wa_ref, ba_ref, wx_ref, bx_ref, lam_ref)
        hst = _scan_fwd(a, mult * (i * rec))
        h_ref[0] = hst
        y_ref[0] = (_gelu(g_ref[0]) * hst).astype(y_ref.dtype)

    gsp = pl.BlockSpec((1, s, LRU_BLOCK), lambda bb, n: (bb, 0, n))
    psp = pl.BlockSpec((1, s, LRU_BLOCK), lambda bb, n: (bb, 0, n + nb))
    cws = pl.BlockSpec((LRU_CONV, LRU_BLOCK), lambda bb, n: (0, n))
    vec = pl.BlockSpec((1, LRU_BLOCK), lambda bb, n: (0, n))
    wsp = pl.BlockSpec((1, LRU_BLOCK, LRU_BLOCK), lambda bb, n: (n, 0, 0))
    return pl.pallas_call(
        body, grid=(bsz, nb), in_specs=[gsp, psp, cws, vec, wsp, vec, wsp, vec, vec], out_specs=[gsp, gsp],
        out_shape=[jax.ShapeDtypeStruct((bsz, s, rw), F32), jax.ShapeDtypeStruct((bsz, s, rw), BF16)],
        compiler_params=_cp("parallel", "parallel"), name=name)(zz, zz, cw, cb, wa, ba, wx, bx, lam)


def lru_bwd(zz, hst, dy, cw, cb, wa, ba, wx, bx, lam, name="lru_bwd"):
    bsz, s, r2 = zz.shape
    rw = r2 // 2
    nb = rw // LRU_BLOCK

    def body(g_ref, p_ref, h_ref, dy_ref, cw_ref, cb_ref, wa_ref, ba_ref, wx_ref, bx_ref, lam_ref,
             dg_ref, dp_ref, dcw_ref, dcb_ref, dwa_ref, dba_ref, dwx_ref, dbx_ref, dlam_ref):
        pre = p_ref[0]
        taps = _conv_taps(pre, LRU_CONV)
        rec = _conv_fwd(taps, cw_ref, cb_ref)
        r, i, sp, a, mult = _lru_gates(rec, wa_ref, ba_ref, wx_ref, bx_ref, lam_ref)
        hst_v, gate, dyv = h_ref[0], g_ref[0], dy_ref[0]
        dg_ref[0] = (dyv * hst_v * _gelu_grad(gate)).astype(dg_ref.dtype)
        lmb = _scan_bwd(_shift_up(a, 1), dyv * _gelu(gate))
        da = lmb * _shift_down(hst_v, 1)
        dmult = lmb * (i * rec)
        dlog_a = da * a - dmult * (a * a) / mult
        dr = dlog_a * (-LRU_C) * sp
        dra = dr * r * (1.0 - r)
        dxa = lmb * mult * rec * i * (1.0 - i)
        drec = lmb * mult * i + _dot(dra, wa_ref[0], NT) + _dot(dxa, wx_ref[0], NT)
        dp_ref[0] = _conv_bwd_input(drec, cw_ref, LRU_CONV).astype(dp_ref.dtype)

        @pl.when(pl.program_id(1) == 0)
        def _():
            for ref in (dcw_ref, dcb_ref, dwa_ref, dba_ref, dwx_ref, dbx_ref, dlam_ref):
                ref[...] = jnp.zeros_like(ref)

        for k, row in enumerate(_conv_bwd_weight(drec, taps)):
            dcw_ref[k:k + 1, :] += row
        dcb_ref[...] += jnp.sum(drec, axis=0, keepdims=True)
        dwa_ref[0] += _dot(rec, dra, TN)
        dwx_ref[0] += _dot(rec, dxa, TN)
        dba_ref[...] += jnp.sum(dra, axis=0, keepdims=True)
        dbx_ref[...] += jnp.sum(dxa, axis=0, keepdims=True)
        dsp = jnp.sum(dlog_a * (-LRU_C) * r, axis=0, keepdims=True)
        dlam_ref[...] += dsp * (-_sig(-lam_ref[...]))

    gsp = pl.BlockSpec((1, s, LRU_BLOCK), lambda n, bb: (bb, 0, n))
    psp = pl.BlockSpec((1, s, LRU_BLOCK), lambda n, bb: (bb, 0, n + nb))
    cws = pl.BlockSpec((LRU_CONV, LRU_BLOCK), lambda n, bb: (0, n))
    vec = pl.BlockSpec((1, LRU_BLOCK), lambda n, bb: (0, n))
    wsp = pl.BlockSpec((1, LRU_BLOCK, LRU_BLOCK), lambda n, bb: (n, 0, 0))
    act = jax.ShapeDtypeStruct((bsz, s, rw), BF16)
    vsh = jax.ShapeDtypeStruct((1, rw), F32)
    wsh = jax.ShapeDtypeStruct((nb, LRU_BLOCK, LRU_BLOCK), F32)
    return pl.pallas_call(
        body, grid=(nb, bsz), in_specs=[gsp, psp, gsp, gsp, cws, vec, wsp, vec, wsp, vec, vec],
        out_specs=[gsp, gsp, cws, vec, wsp, vec, wsp, vec, vec],
        out_shape=[act, act, jax.ShapeDtypeStruct((LRU_CONV, rw), F32), vsh, wsh, vsh, wsh, vsh, vsh],
        compiler_params=_cp("parallel", "arbitrary"), name=name)(zz, zz, hst, dy, cw, cb, wa, ba, wx, bx, lam)


def lru_layer_fwd(x, g, p, bsz):
    m, d = x.shape
    h = rms_fwd(x, g, BF16, "lru_norm")
    zz = matmul(h, p["lru_w_in"], name="lru_in")
    hst, y = lru_fwd(zz.reshape(bsz, m // bsz, -1), p["lru_conv_w"], p["lru_conv_b"], p["lru_w_a"], p["lru_b_a"],
                     p["lru_w_x"], p["lru_b_x"], p["lru_lam"])
    y2 = y.reshape(m, -1)
    out = matmul(y2, p["lru_w_out"], res=x, name="lru_out")
    return out, (x, h, zz, hst, y2)


def lru_layer_bwd(saved, dout, g, p, bsz):
    x, h, zz, hst, y2 = saved
    m, d = x.shape
    dy = matmul(dout, p["lru_w_out"], tb=True, name="lru_out_dx")
    dw_out = matmul(y2, dout, ta=True, name="lru_out_dw")
    dgate, dpre, dcw, dcb, dwa, dba, dwx, dbx, dlam = lru_bwd(
        zz.reshape(bsz, m // bsz, -1), hst, dy.reshape(bsz, m // bsz, -1), p["lru_conv_w"], p["lru_conv_b"],
        p["lru_w_a"], p["lru_b_a"], p["lru_w_x"], p["lru_b_x"], p["lru_lam"])
    dzz = jnp.concatenate([dgate, dpre], axis=-1).reshape(m, -1)
    dh = matmul(dzz, p["lru_w_in"], tb=True, name="lru_in_dx")
    dw_in = matmul(h, dzz, ta=True, name="lru_in_dw")
    dx, dg = rms_bwd(x, g, dh, dout, "lru_norm_bwd")
    return dx, dict(norm_mix_g=dg, lru_w_in=dw_in, lru_conv_w=dcw[None], lru_conv_b=dcb, lru_w_a=dwa[None],
                    lru_b_a=dba, lru_w_x=dwx[None], lru_b_x=dbx, lru_lam=dlam, lru_w_out=dw_out)


def _s5_discretise(lam_re, lam_im, log_dt, b_re, b_im):
    lr = jnp.minimum(lam_re, -1e-4)
    dt = jnp.exp(log_dt)[:, None]
    mag = jnp.exp(lr * dt)
    ar, ai = mag * jnp.cos(lam_im * dt), mag * jnp.sin(lam_im * dt)
    den = lr * lr + lam_im * lam_im
    cr = ((ar - 1.0) * lr + ai * lam_im) / den
    ci = (ai * lr - (ar - 1.0) * lam_im) / den
    bbr = cr[..., None] * b_re - ci[..., None] * b_im
    bbi = cr[..., None] * b_im + ci[..., None] * b_re
    return ar, ai, bbr, bbi


def _s5_powers(lam_re, lam_im, log_dt, ns):
    lr = jnp.minimum(lam_re, -1e-4)
    dt = jnp.exp(log_dt)[:, None]
    n = jnp.asarray(ns, F32)[:, None, None]
    mag = jnp.exp(n * (lr * dt))
    ang = n * (lam_im * dt)
    to_chunks = lambda t: t.reshape(len(ns), S5_CHUNKS, S5_LANES).transpose(1, 0, 2)
    return jnp.concatenate([to_chunks(mag * jnp.cos(ang)), to_chunks(mag * jnp.sin(ang))], axis=-1)


def _s5_in_matrix(bbr, bbi):
    eye = jnp.eye(8, dtype=F32)
    blk = lambda t: jnp.einsum("qgph,gk->qghkp", t.reshape(S5_CHUNKS, 8, S5_STATE, S5_GROUP), eye).reshape(
        S5_CHUNKS, 128, S5_LANES)
    return jnp.concatenate([blk(bbr), blk(bbi)], axis=-1)


def _s5_in_matrix_diag(dmat):
    eye = jnp.eye(8, dtype=F32)[None, :, None, :, None]
    pick = lambda t: (t.reshape(S5_CHUNKS, 8, S5_GROUP, 8, S5_STATE) * eye).sum(3).transpose(0, 1, 3, 2).reshape(
        S5_CHUNKS * 8, S5_STATE, S5_GROUP)
    return pick(dmat[..., :S5_LANES]), pick(dmat[..., S5_LANES:])


def _s5_out_matrix(c_re, c_im):
    eye = jnp.eye(8, dtype=F32)
    blk = lambda t: jnp.einsum("qghp,gk->qgpkh", t.reshape(S5_CHUNKS, 8, S5_GROUP, S5_STATE), eye).reshape(
        S5_CHUNKS, S5_LANES, 128)
    return jnp.concatenate([blk(c_re), -blk(c_im)], axis=1)


def _s5_out_matrix_diag(dmat):
    eye = jnp.eye(8, dtype=F32)[None, :, None, :, None]
    pick = lambda t: (t.reshape(S5_CHUNKS, 8, S5_STATE, 8, S5_GROUP) * eye).sum(3).transpose(0, 1, 3, 2).reshape(
        S5_CHUNKS * 8, S5_GROUP, S5_STATE)
    return pick(dmat[:, :S5_LANES]), -pick(dmat[:, S5_LANES:])


def s5_fwd(h, bmat, cmat, atab, pw, dskip, later_shards=(), name="s5_fwd"):
    bsz, s, d = h.shape
    t = min(S5_T, s)
    nt, nlev, ln = s // t, atab.shape[1], S5_LANES
    ng = len(later_shards)

    def body(*refs):
        h_ref, b_ref, c_ref, a_ref, pw_ref, d_ref = refs[:6]
        xs_ref, yp_ref, yg_ref = refs[6 + ng:9 + ng]
        carry = refs[9 + 2 * ng]
        if ng:
            start, finish = _gather_steps([g.shape for g in later_shards], refs[6:6 + ng], refs[9 + ng:9 + 2 * ng],
                                          refs[10 + 2 * ng:])
            step = (pl.program_id(0) * S5_CHUNKS + pl.program_id(1)) * nt + pl.program_id(2)
            pl.when(step == 0)(start)

        @pl.when(pl.program_id(2) == 0)
        def _():
            carry[...] = jnp.zeros_like(carry)

        u = h_ref[0]
        bu = _dot3(u, b_ref[0])
        xr, xi = bu[:, :ln], bu[:, ln:]
        for k in range(nlev):
            ar, ai = a_ref[0, k:k + 1, :ln], a_ref[0, k:k + 1, ln:]
            sr, si = _shift_all([xr, xi], 1 << k, False)
            xr, xi = xr + ar * sr - ai * si, xi + ar * si + ai * sr
        cr, ci = carry[0:1, :ln], carry[0:1, ln:]
        pr, pi = pw_ref[0, :, :ln], pw_ref[0, :, ln:]
        xr, xi = xr + pr * cr - pi * ci, xi + pr * ci + pi * cr
        carry[0:1, :ln] = xr[t - 1:t, :]
        carry[0:1, ln:] = xi[t - 1:t, :]
        xs_ref[0, :, :ln] = xr
        xs_ref[0, :, ln:] = xi
        y = _dot3(xr, c_ref[0, :ln, :]) + _dot3(xi, c_ref[0, ln:, :]) + d_ref[...] * u
        yp_ref[0] = y
        yg_ref[0] = _gelu(y).astype(yg_ref.dtype)
        if ng:
            pl.when(step == bsz * S5_CHUNKS * nt - 1)(finish)

    act = pl.BlockSpec((1, t, 128), lambda b, q, i: (b, i, q))
    par = lambda r, c: pl.BlockSpec((1, r, c), lambda b, q, i: (q, 0, 0))
    out = pl.pallas_call(
        body, grid=(bsz, S5_CHUNKS, nt),
        in_specs=[act, par(128, 2 * ln), par(2 * ln, 128), par(nlev, 2 * ln), par(t, 2 * ln),
                  pl.BlockSpec((1, 128), lambda b, q, i: (0, q))] + [_ANY] * ng,
        out_specs=[pl.BlockSpec((1, t, 2 * ln), lambda b, q, i: (b, i, q)), act, act] + [_ANY] * ng,
        out_shape=[jax.ShapeDtypeStruct((bsz, s, S5_CHUNKS * 2 * ln), F32), jax.ShapeDtypeStruct((bsz, s, d), F32),
                   jax.ShapeDtypeStruct((bsz, s, d), BF16)] + _gather_out_shapes(later_shards),
        scratch_shapes=[pltpu.VMEM((8, 2 * ln), F32)] + (_gather_semaphores(ng) if ng else []),
        compiler_params=_cp("arbitrary", "arbitrary", "arbitrary"), name=name)(
            h, bmat, cmat, atab, pw, dskip, *later_shards)
    return out[0], out[1], out[2], out[3:]


def s5_bwd(h, ypre, xs, dyg, bmat_t, cmat_t, atab, pw_rev, dskip, scatter=None, name="s5_bwd"):
    bsz, s, d = h.shape
    t = min(S5_T, s)
    nt, nlev, ln = s // t, atab.shape[1], S5_LANES
    halves, targets, buf_shapes = scatter if scatter else ((), (), ())
    nh, nb = len(halves), len(buf_shapes)

    def body(*refs):
        h_ref, yp_ref, xs_ref, xp_ref, dy_ref, bt_ref, ct_ref, a_ref, pw_ref, d_ref = refs[:10]
        dh_ref, db_ref, dc_ref, da_ref, dd_ref = refs[10 + nh:15 + nh]
        carry = refs[15 + nh + nb]
        b, i = pl.program_id(1), pl.program_id(2)
        if nh:
            start, collect, finish = _scatter_steps(targets, refs[10:10 + nh], refs[15 + nh:15 + nh + nb],
                                                    refs[16 + nh + nb:])
            step = (pl.program_id(0) * bsz + b) * nt + i
            n_steps = S5_CHUNKS * bsz * nt
            pl.when(step == 0)(start)

        @pl.when((b == 0) & (i == 0))
        def _():
            for r in (db_ref, dc_ref, da_ref, dd_ref):
                r[...] = jnp.zeros_like(r)

        @pl.when(i == 0)
        def _():
            carry[...] = jnp.zeros_like(carry)

        u = h_ref[0]
        dyp = dy_ref[0] * _gelu_grad(yp_ref[0])
        dd_ref[...] += jnp.sum(dyp * u, axis=0, keepdims=True)
        xr, xi = xs_ref[0, :, :ln], xs_ref[0, :, ln:]
        dc_ref[0, :ln, :] += _dot3(xr, dyp, TN)
        dc_ref[0, ln:, :] += _dot3(xi, dyp, TN)
        lr, li = _dot3(dyp, ct_ref[0, :, :ln]), _dot3(dyp, ct_ref[0, :, ln:])
        for k in range(nlev):
            ar, ai = a_ref[0, k:k + 1, :ln], a_ref[0, k:k + 1, ln:]
            sr, si = _shift_all([lr, li], 1 << k, True)
            lr, li = lr + ar * sr + ai * si, li + ar * si - ai * sr
        cr, ci = carry[0:1, :ln], carry[0:1, ln:]
        pr, pi = pw_ref[0, :, :ln], pw_ref[0, :, ln:]
        lr, li = lr + pr * cr + pi * ci, li + pr * ci - pi * cr
        carry[0:1, :ln] = lr[0:1, :]
        carry[0:1, ln:] = li[0:1, :]
        dh_ref[0] = _dot3(lr, bt_ref[0, :ln, :]) + _dot3(li, bt_ref[0, ln:, :]) + dyp * d_ref[...]
        db_ref[0, :, :ln] += _dot3(u, lr, TN)
        db_ref[0, :, ln:] += _dot3(u, li, TN)
        first = _rows(xr.shape) == 0
        keep = jnp.where(i == nt - 1, 0.0, 1.0)
        xpr = jnp.where(first, xp_ref[0, 7:8, :ln] * keep, pltpu.roll(xr, 1, 0))
        xpi = jnp.where(first, xp_ref[0, 7:8, ln:] * keep, pltpu.roll(xi, 1, 0))
        da_ref[0, 0:1, :ln] += jnp.sum(lr * xpr + li * xpi, axis=0, keepdims=True)
        da_ref[0, 0:1, ln:] += jnp.sum(li * xpr - lr * xpi, axis=0, keepdims=True)
        if nh:
            pl.when(step == (7 * n_steps) // 8 - 1)(collect)
            pl.when(step == n_steps - 1)(finish)

    rev = lambda i: nt - 1 - i
    act = pl.BlockSpec((1, t, 128), lambda q, b, i: (b, rev(i), q))
    xsp = pl.BlockSpec((1, t, 2 * ln), lambda q, b, i: (b, rev(i), q))
    xpp = pl.BlockSpec((1, 8, 2 * ln), lambda q, b, i: (b, jnp.maximum(rev(i) * (t // 8) - 1, 0), q))
    par = lambda r, c: pl.BlockSpec((1, r, c), lambda q, b, i: (q, 0, 0))
    dsp = pl.BlockSpec((1, 128), lambda q, b, i: (0, q))
    out = pl.pallas_call(
        body, grid=(S5_CHUNKS, bsz, nt),
        in_specs=[act, act, xsp, xpp, act, par(2 * ln, 128), par(128, 2 * ln), par(nlev, 2 * ln), par(t, 2 * ln), dsp]
        + [_ANY] * nh,
        out_specs=[act, par(128, 2 * ln), par(2 * ln, 128), par(8, 2 * ln), dsp] + [_ANY] * nb,
        out_shape=[jax.ShapeDtypeStruct((bsz, s, d), F32), jax.ShapeDtypeStruct((S5_CHUNKS, 128, 2 * ln), F32),
                   jax.ShapeDtypeStruct((S5_CHUNKS, 2 * ln, 128), F32), jax.ShapeDtypeStruct((S5_CHUNKS, 8, 2 * ln), F32),
                   jax.ShapeDtypeStruct((1, d), F32)] + list(buf_shapes),
        scratch_shapes=[pltpu.VMEM((8, 2 * ln), F32)] + (_scatter_semaphores(nh) if nh else []),
        compiler_params=_cp("arbitrary", "arbitrary", "arbitrary"), name=name)(
            h, ypre, xs, xs, dyg, bmat_t, cmat_t, atab, pw_rev, dskip, *halves)
    return out[0], out[1], out[2], out[3], out[4], out[5:]


def glu_fwd(z, x, name="s5_glu"):
    m, d = x.shape
    tr = _pick(m, (512, 256, 128))

    def body(z_ref, x_ref, o_ref):
        o_ref[...] = x_ref[...] + z_ref[:, :d] * _sig(z_ref[:, d:])

    return pl.pallas_call(
        body, grid=(m // tr,),
        in_specs=[pl.BlockSpec((tr, 2 * d), lambda i: (i, 0)), pl.BlockSpec((tr, d), lambda i: (i, 0))],
        out_specs=pl.BlockSpec((tr, d), lambda i: (i, 0)), out_shape=jax.ShapeDtypeStruct((m, d), F32),
        compiler_params=_cp("parallel"), name=name)(z, x)


def glu_bwd(z, dout, name="s5_glu_bwd"):
    m, d = dout.shape
    tr = _pick(m, (512, 256, 128))

    def body(z_ref, do_ref, dz_ref, db_ref):
        sg = _sig(z_ref[:, d:])
        dv = do_ref[...] * sg
        dgt = do_ref[...] * z_ref[:, :d] * sg * (1.0 - sg)
        dz_ref[:, :d] = dv.astype(dz_ref.dtype)
        dz_ref[:, d:] = dgt.astype(dz_ref.dtype)

        @pl.when(pl.program_id(0) == 0)
        def _():
            db_ref[...] = jnp.zeros_like(db_ref)

        db_ref[:, :d] += jnp.sum(dv, axis=0, keepdims=True)
        db_ref[:, d:] += jnp.sum(dgt, axis=0, keepdims=True)

    wide = pl.BlockSpec((tr, 2 * d), lambda i: (i, 0))
    return pl.pallas_call(
        body, grid=(m // tr,), in_specs=[wide, pl.BlockSpec((tr, d), lambda i: (i, 0))],
        out_specs=[wide, pl.BlockSpec((1, 2 * d), lambda i: (0, 0))],
        out_shape=[jax.ShapeDtypeStruct((m, 2 * d), BF16), jax.ShapeDtypeStruct((1, 2 * d), F32)],
        compiler_params=_cp("arbitrary"), name=name)(z, dout)


def _s5_tables(p, t):
    nlev = max(1, (t - 1).bit_length())
    lam = (p["s5_lam_re"], p["s5_lam_im"], p["s5_log_dt"])
    atab = _s5_powers(*lam, [1 << k for k in range(nlev)])
    if nlev < 8:
        atab = jnp.pad(atab, ((0, 0), (0, 8 - nlev), (0, 0)))
    pw = _s5_powers(*lam, list(range(1, t + 1)))
    return nlev, atab, pw


def s5_layer_fwd(x, g, p, bsz, later_shards=()):
    m, d = x.shape
    s = m // bsz
    t = min(S5_T, s)
    h = rms_fwd(x, g, F32, "s5_norm")
    _, _, bbr, bbi = _s5_discretise(p["s5_lam_re"], p["s5_lam_im"], p["s5_log_dt"], p["s5_b_re"], p["s5_b_im"])
    nlev, atab, pw = _s5_tables(p, t)
    bmat, cmat = _s5_in_matrix(bbr, bbi), _s5_out_matrix(p["s5_c_re"], p["s5_c_im"])
    xs, ypre, yg, gathered = s5_fwd(h.reshape(bsz, s, d), bmat, cmat, atab[:, :max(nlev, 8)], pw, p["s5_d"],
                                    later_shards)
    z = matmul(yg.reshape(m, d), p["s5_w_out"], bias=p["s5_b_out"], name="s5_out")
    out = glu_fwd(z, x)
    return out, (x, h, xs, ypre, yg, z, bmat, cmat, atab, pw), gathered


def s5_layer_bwd(saved, dout, g, p, bsz, scatter=None):
    x, h, xs, ypre, yg, z, bmat, cmat, atab, pw = saved
    m, d = x.shape
    s = m // bsz
    pw_rev = _s5_powers(p["s5_lam_re"], p["s5_lam_im"], p["s5_log_dt"], list(range(pw.shape[1], 0, -1)))
    dz, db_out = glu_bwd(z, dout)
    dyg = matmul(dz, p["s5_w_out"], tb=True, name="s5_out_dx")
    dw_out = matmul(yg.reshape(m, d), dz, ta=True, name="s5_out_dw")
    dh, dbm, dcm, dlam, dd, landed = s5_bwd(h.reshape(bsz, s, d), ypre, xs, dyg.reshape(bsz, s, d),
                                            bmat.transpose(0, 2, 1), cmat.transpose(0, 2, 1), atab, pw_rev, p["s5_d"],
                                            scatter)
    dx, dg = rms_bwd(x, g, dh.reshape(m, d), dout, "s5_norm_bwd")
    dbbr, dbbi = _s5_in_matrix_diag(dbm)
    dc_re, dc_im = _s5_out_matrix_diag(dcm)
    dar = dlam[:, 0, :S5_LANES].reshape(S5_CHUNKS * 8, S5_STATE)
    dai = dlam[:, 0, S5_LANES:].reshape(S5_CHUNKS * 8, S5_STATE)
    _, vjp = jax.vjp(_s5_discretise, p["s5_lam_re"], p["s5_lam_im"], p["s5_log_dt"], p["s5_b_re"], p["s5_b_im"])
    dl_re, dl_im, dldt, db_re, db_im = vjp((dar, dai, dbbr, dbbi))
    grads = dict(norm_mix_g=dg, s5_lam_re=dl_re[None], s5_lam_im=dl_im[None], s5_log_dt=dldt[None],
                 s5_b_re=db_re[None], s5_b_im=db_im[None], s5_c_re=dc_re[None], s5_c_im=dc_im[None],
                 s5_d=dd, s5_w_out=dw_out, s5_b_out=db_out)
    return dx, grads, landed


def _log_sigmoid(z):
    return jnp.minimum(z, 0.0) - jnp.log(1.0 + jnp.exp(-jnp.abs(z)))


def _head_norm(t, g_ref):
    r = lax.rsqrt(jnp.mean(t * t, axis=-1, keepdims=True) + EPS)
    th = t * r
    return th * g_ref[...], th, r


def _tri(shape, fn):
    row = lax.broadcasted_iota(jnp.int32, shape, 0)
    col = lax.broadcasted_iota(jnp.int32, shape, 1)
    return fn(row, col)


_SB_SCALE = 1.0 / math.sqrt(SB_DIM)


def _suffix_sums(t, later):
    n = t.shape[1] // SB_CHUNK
    outs, carry = [None] * n, jnp.zeros((t.shape[0], 1), F32)
    for ci in range(n - 1, -1, -1):
        ch = t[:, ci * SB_CHUNK:(ci + 1) * SB_CHUNK]
        outs[ci] = _dot_exact_rhs(ch, later) + carry
        carry = carry + jnp.sum(ch, axis=1, keepdims=True)
    return (outs[0] if n == 1 else jnp.concatenate(outs, axis=1)), carry


def _prefix_sums(t, tri):
    n = t.shape[1] // SB_CHUNK
    outs, carry = [None] * n, jnp.zeros((t.shape[0], 1), F32)
    for ci in range(n):
        ch = t[:, ci * SB_CHUNK:(ci + 1) * SB_CHUNK]
        outs[ci] = _dot_exact_rhs(ch, tri) + carry
        carry = carry + jnp.sum(ch, axis=1, keepdims=True)
    return (outs[0] if n == 1 else jnp.concatenate(outs, axis=1)), carry


def sb_fwd(q, k, v, qg, kg, name="sb_fwd"):
    bsz, nh, s, dh = q.shape
    tb = min(SB_BLOCK, s)
    nq = s // tb

    def body(q_ref, k_ref, v_ref, qg_ref, kg_ref, o_ref, rt_ref):
        qi = pl.program_id(2)
        qn, _, _ = _head_norm(q_ref[0, 0], qg_ref)
        later = _tri((SB_CHUNK, SB_CHUNK), lambda r, c: r > c).astype(BF16)
        causal = _tri((tb, tb), lambda r, c: c < r)

        def block(kb, run, acc, diag):
            ks = pl.ds(pl.multiple_of(kb * tb, tb), tb)
            kn, _, _ = _head_norm(k_ref[0, 0, ks, :], kg_ref)
            z = _dot(qn, kn, NT) * _SB_SCALE
            ls = _log_sigmoid(z)
            lm = ls - z
            if diag:
                lm = jnp.where(causal, lm, 0.0)
            rest, total = _suffix_sums(lm, later)
            att = jnp.exp(ls + run + rest)
            if diag:
                att = jnp.where(causal, att, 0.0)
            return run + total, acc + _dot(att, v_ref[0, 0, ks, :])

        run, acc = block(qi, jnp.zeros((tb, 1), F32), jnp.zeros((tb, dh), F32), True)
        run, acc = lax.fori_loop(0, qi, lambda j, c: block(qi - 1 - j, c[0], c[1], False), (run, acc))
        o_ref[0, 0] = acc
        rt_ref[0, 0] = run

    qsp = pl.BlockSpec((1, 1, tb, dh), lambda b, h, i: (b, h, i, 0))
    rsp = pl.BlockSpec((1, 1, tb, 1), lambda b, h, i: (b, h, i, 0))
    ksp = pl.BlockSpec((1, 1, s, dh), lambda b, h, i: (b, h, 0, 0))
    gsp = pl.BlockSpec((1, dh), lambda b, h, i: (0, 0))
    return pl.pallas_call(
        body, grid=(bsz, nh, nq), in_specs=[qsp, ksp, ksp, gsp, gsp], out_specs=[qsp, rsp],
        out_shape=[jax.ShapeDtypeStruct((bsz, nh, s, dh), F32), jax.ShapeDtypeStruct((bsz, nh, s, 1), F32)],
        compiler_params=_cp("parallel", "parallel", "arbitrary"), name=name)(q, k, v, qg, kg)


def sb_bwd(q, k, v, rtot, do, qg, kg, name="sb_bwd"):
    bsz, nh, s, dh = q.shape
    tb = min(SB_BLOCK, s)
    nq = s // tb

    def body(q_ref, k_ref, v_ref, rt_ref, do_ref, qg_ref, kg_ref, dq_ref, dk_ref, dv_ref, dqg_ref, dkg_ref,
             qn_s, kn_s, dqn_s, dkn_s, dv_s):
        qn, qh, rq = _head_norm(q_ref[0, 0], qg_ref)
        kn, kh, rk = _head_norm(k_ref[0, 0], kg_ref)
        qn_s[...] = qn
        kn_s[...] = kn
        dkn_s[...] = jnp.zeros_like(dkn_s)
        dv_s[...] = jnp.zeros_like(dv_s)
        chunk = (SB_CHUNK, SB_CHUNK)
        upto = _tri(chunk, lambda r, c: r <= c).astype(BF16)
        earlier = _tri(chunk, lambda r, c: r < c).astype(BF16)
        causal = _tri((tb, tb), lambda r, c: c < r)

        def q_block(qi, _):
            qs = pl.ds(pl.multiple_of(qi * tb, tb), tb)
            qnb, dob, rtb = qn_s[qs, :], do_ref[0, 0, qs, :], rt_ref[0, 0, qs, :]

            def block(kb, left, seen, dqn, diag):
                ks = pl.ds(pl.multiple_of(kb * tb, tb), tb)
                knb, vb = kn_s[ks, :], v_ref[0, 0, ks, :]
                z = _dot(qnb, knb, NT) * _SB_SCALE
                ls = _log_sigmoid(z)
                lm = ls - z
                if diag:
                    lm = jnp.where(causal, lm, 0.0)
                through, lm_total = _prefix_sums(lm, upto)
                att = jnp.exp(ls + (rtb - left - through))
                if diag:
                    att = jnp.where(causal, att, 0.0)
                gg = att * _dot(dob, vb, NT)
                before, gg_total = _prefix_sums(gg, earlier)
                sg = jnp.exp(ls)
                dz = gg * (1.0 - sg) - sg * (seen + before)
                if diag:
                    dz = jnp.where(causal, dz, 0.0)
                dz = dz * _SB_SCALE
                dkn_s[ks, :] += _dot(dz, qnb, TN)
                dv_s[ks, :] += _dot(att, dob, TN)
                return left + lm_total, seen + gg_total, dqn + _dot(dz, knb)

            zero = jnp.zeros((tb, 1), F32)
            c = lax.fori_loop(0, qi, lambda kb, c: block(kb, c[0], c[1], c[2], False),
                              (zero, zero, jnp.zeros((tb, dh), F32)))
            c = block(qi, c[0], c[1], c[2], True)
            dqn_s[qs, :] = c[2]
            return 0

        lax.fori_loop(0, nq, q_block, 0)

        @pl.when((pl.program_id(0) == 0) & (pl.program_id(1) == 0))
        def _():
            dqg_ref[...] = jnp.zeros_like(dqg_ref)
            dkg_ref[...] = jnp.zeros_like(dkg_ref)

        def norm_bwd(dn, th, r, g_ref, dt_ref, dg_ref):
            dg_ref[...] += jnp.sum(dn * th, axis=0, keepdims=True)
            dth = dn * g_ref[...]
            dt_ref[0, 0] = r * (dth - th * jnp.mean(dth * th, axis=-1, keepdims=True))

        norm_bwd(dqn_s[...], qh, rq, qg_ref, dq_ref, dqg_ref)
        norm_bwd(dkn_s[...], kh, rk, kg_ref, dk_ref, dkg_ref)
        dv_ref[0, 0] = dv_s[...]

    hsp = pl.BlockSpec((1, 1, s, dh), lambda b, h: (b, h, 0, 0))
    gsp = pl.BlockSpec((1, dh), lambda b, h: (0, 0))
    act = jax.ShapeDtypeStruct((bsz, nh, s, dh), F32)
    gsh = jax.ShapeDtypeStruct((1, dh), F32)
    rsp = pl.BlockSpec((1, 1, s, 1), lambda b, h: (b, h, 0, 0))
    return pl.pallas_call(
        body, grid=(bsz, nh), in_specs=[hsp, hsp, hsp, rsp, hsp, gsp, gsp], out_specs=[hsp, hsp, hsp, gsp, gsp],
        out_shape=[act, act, act, gsh, gsh], scratch_shapes=[pltpu.VMEM((s, dh), F32)] * 5,
        compiler_params=_cp("arbitrary", "arbitrary"), name=name)(q, k, v, rtot, do, qg, kg)


def _to_heads(t, bsz):
    m, w = t.shape
    n = w // (SB_HEADS * SB_DIM)
    t = t.reshape(bsz, m // bsz, n, SB_HEADS, SB_DIM).transpose(2, 0, 3, 1, 4)
    return [t[i] for i in range(n)]


def _from_heads(ts):
    t = jnp.stack(ts, axis=0)
    n, bsz, nh, s, dh = t.shape
    return t.transpose(1, 3, 0, 2, 4).reshape(bsz * s, n * nh * dh)


def sb_layer_fwd(x, g, p, bsz):
    m, d = x.shape
    h = rms_fwd(x, g, BF16, "sb_norm")
    qkv = matmul(h, p["sb_w_qkv"], name="sb_qkv")
    q, k, v = _to_heads(qkv, bsz)
    o, rtot = sb_fwd(q, k, v, p["sb_q_g"], p["sb_k_g"])
    o2 = _from_heads([o])
    out = matmul(o2, p["sb_w_o"], res=x, name="sb_out")
    return out, (x, h, q, k, v, rtot, o2)


def sb_layer_bwd(saved, dout, g, p, bsz):
    x, h, q, k, v, rtot, o2 = saved
    do2 = matmul(dout, p["sb_w_o"], tb=True, name="sb_out_dx")
    dw_o = matmul(o2, dout, ta=True, name="sb_out_dw")
    dq, dk, dv, dqg, dkg = sb_bwd(q, k, v, rtot, _to_heads(do2, bsz)[0], p["sb_q_g"], p["sb_k_g"])
    dqkv = _from_heads([dq, dk, dv])
    dh = matmul(dqkv, p["sb_w_qkv"], tb=True, name="sb_qkv_dx")
    dw_qkv = matmul(h, dqkv, ta=True, name="sb_qkv_dw")
    dx, dg = rms_bwd(x, g, dh, dout, "sb_norm_bwd")
    return dx, dict(norm_mix_g=dg, sb_w_qkv=dw_qkv, sb_q_g=dqg, sb_k_g=dkg, sb_w_o=dw_o)


_CHIP_FLIPS = ((1, 0), (0, 1), (1, 1))
_MESH = pl.DeviceIdType.MESH
_ANY = pl.BlockSpec(memory_space=pl.ANY)


def _flip(v, f):
    return 1 - v if f else v


def _splits(shape):
    return shape[-2] % 32 == 0


def _half(ref, c, rows):
    idx = (slice(None),) * (len(ref.shape) - 2) + (pl.ds(pl.multiple_of(c * (rows // 2), 16), rows // 2),)
    return ref.at[idx]


def gather_weights(shards, name="gather_weights"):
    n = len(shards)

    def body(*refs):
        start, finish = _gather_steps([s.shape for s in shards], refs[:n], refs[n:2 * n], refs[2 * n:])
        start()
        finish()

    return pl.pallas_call(
        body, in_specs=[_ANY] * n, out_specs=[_ANY] * n, out_shape=_gather_out_shapes(shards),
        scratch_shapes=_gather_semaphores(n), name=name)(*shards)


def _gather_out_shapes(shards):
    return [jax.ShapeDtypeStruct((N_CHIPS,) + s.shape, s.dtype) for s in shards]


def _gather_semaphores(n):
    return [pltpu.SemaphoreType.DMA((3 * n,))] * 4


def _gather_steps(shapes, ins, outs, sems):
    n = len(shapes)
    split = [_splits(s) for s in shapes]
    send, recv, fsend, frecv = sems
    x, y, c = lax.axis_index("x"), lax.axis_index("y"), lax.axis_index("c")
    me = 2 * x + y
    sibling = (x, y, 1 - c)

    def remote(i, j, block):
        px, py = _flip(x, _CHIP_FLIPS[j][0]), _flip(y, _CHIP_FLIPS[j][1])
        rows = shapes[i][0]
        src = _half(ins[i], c, rows) if split[i] else ins[i]
        dst = _half(outs[i].at[block], c, rows) if split[i] else outs[i].at[block]
        return pltpu.make_async_remote_copy(
            src_ref=src, dst_ref=dst, send_sem=send.at[3 * i + j], recv_sem=recv.at[3 * i + j],
            device_id=(px, py, c), device_id_type=_MESH)

    def forward(i, j, half):
        rows = _half(outs[i].at[2 * _flip(x, _CHIP_FLIPS[j][0]) + _flip(y, _CHIP_FLIPS[j][1])], half, shapes[i][0])
        return pltpu.make_async_remote_copy(
            src_ref=rows, dst_ref=rows, send_sem=fsend.at[3 * i + j], recv_sem=frecv.at[3 * i + j],
            device_id=sibling, device_id_type=_MESH)

    def start():
        for i in range(n):
            for j in range(3):
                remote(i, j, me).start()

    def finish():
        for i in range(n):
            for j, (fx, fy) in enumerate(_CHIP_FLIPS):
                remote(i, j, 2 * _flip(x, fx) + _flip(y, fy)).wait_recv()
                if split[i]:
                    forward(i, j, c).start()
        for i in range(n):
            for j in range(3):
                if split[i]:
                    forward(i, j, 1 - c).wait_recv()
                    forward(i, j, c).wait_send()
                remote(i, j, me).wait_send()

    return start, finish


def grad_halves_exchange(parts, name="grad_halves_exchange"):
    n = len(parts)

    def body(*refs):
        ins, got = refs[:n], refs[n:2 * n]
        send, recv = refs[2 * n:]
        x, y, c = lax.axis_index("x"), lax.axis_index("y"), lax.axis_index("c")
        swap = [pltpu.make_async_remote_copy(
            src_ref=_half(ins[i], 1 - c, parts[i].shape[1]), dst_ref=got[i], send_sem=send.at[i], recv_sem=recv.at[i],
            device_id=(x, y, 1 - c), device_id_type=_MESH) for i in range(n)]
        for cp in swap:
            cp.start()
        for cp in swap:
            cp.wait()

    half = [jax.ShapeDtypeStruct((N_CHIPS, p.shape[1] // 2, p.shape[2]), p.dtype) for p in parts]
    return pl.pallas_call(
        body, in_specs=[_ANY] * n, out_specs=[_ANY] * n, out_shape=half,
        scratch_shapes=[pltpu.SemaphoreType.DMA((n,))] * 2, name=name)(*parts)


def pair_sum(full, got, core, out_dtype, name):
    k, r, c = full.shape
    rh = r // 2
    tr = _pick(rh, tuple(t for t in (rh, 512, 256, 128, 64, 32, 16) if t * c * 4 <= 2 * 1024 * 1024))
    nb = rh // tr

    def body(core_ref, a_ref, b_ref, o_ref):
        o_ref[...] = (a_ref[...] + b_ref[...]).astype(o_ref.dtype)

    blk = pl.BlockSpec((1, tr, c), lambda kk, i, core_ref: (kk, i, 0))
    mine = pl.BlockSpec((1, tr, c), lambda kk, i, core_ref: (kk, core_ref[0] * nb + i, 0))
    return pl.pallas_call(
        body, out_shape=jax.ShapeDtypeStruct((k, rh, c), out_dtype),
        grid_spec=pltpu.PrefetchScalarGridSpec(num_scalar_prefetch=1, grid=(k, nb), in_specs=[mine, blk],
                                               out_specs=blk),
        compiler_params=_cp("parallel", "parallel"), name=name)(core, full, got)


def _scatter_semaphores(n):
    return [pltpu.SemaphoreType.DMA((3 * n,)), pltpu.SemaphoreType.DMA((3 * n,)), pltpu.SemaphoreType.DMA((n,)),
            pltpu.SemaphoreType.DMA((n,)), pltpu.SemaphoreType.DMA((n,))]


def _scatter_steps(targets, hin, bufs, sems):
    n = len(targets)
    send, recv, loc, fsend, frecv = sems
    x, y, c = lax.axis_index("x"), lax.axis_index("y"), lax.axis_index("c")
    me = 2 * x + y
    sibling = (x, y, 1 - c)

    def rows_of(ref, i, half):
        _, base, rows = targets[i]
        at = pl.ds(pl.multiple_of(base + half * (rows // 2), 16), rows // 2)
        return ref.at[(slice(None),) * (len(ref.shape) - 2) + (at,)]

    def remote(i, j, slot):
        px, py = _flip(x, _CHIP_FLIPS[j][0]), _flip(y, _CHIP_FLIPS[j][1])
        return pltpu.make_async_remote_copy(
            src_ref=hin[i].at[2 * px + py], dst_ref=rows_of(bufs[targets[i][0]].at[slot], i, c),
            send_sem=send.at[3 * i + j], recv_sem=recv.at[3 * i + j], device_id=(px, py, c), device_id_type=_MESH)

    def local(i):
        return pltpu.make_async_copy(hin[i].at[me], rows_of(bufs[targets[i][0]].at[me], i, c), loc.at[i])

    def forward(i, half):
        rows = rows_of(bufs[targets[i][0]], i, half)
        return pltpu.make_async_remote_copy(src_ref=rows, dst_ref=rows, send_sem=fsend.at[i], recv_sem=frecv.at[i],
                                            device_id=sibling, device_id_type=_MESH)

    def start():
        for i in range(n):
            local(i).start()
            for j in range(3):
                remote(i, j, me).start()

    def collect():
        for i in range(n):
            for j, (fx, fy) in enumerate(_CHIP_FLIPS):
                remote(i, j, 2 * _flip(x, fx) + _flip(y, fy)).wait_recv()
            local(i).wait()
            forward(i, c).start()

    def finish():
        for i in range(n):
            forward(i, 1 - c).wait_recv()
            forward(i, c).wait_send()
            for j in range(3):
                remote(i, j, me).wait_send()

    return start, collect, finish


def grad_sync(halves, targets, bufs, parts, packed, name="grad_sync"):
    nh, n, nb = len(halves), len(parts), len(bufs)
    passed = [i for i, b in enumerate(bufs) if not isinstance(b, jax.ShapeDtypeStruct)]
    n_in = nh + n + 1 + len(passed)

    def body(*refs):
        hin, ins, pk = refs[:nh], refs[nh:nh + n], refs[nh + n]
        outs = refs[n_in:]
        landed, mine, theirs, pk_all = outs[:nb], outs[nb:nb + n], outs[nb + n:nb + 2 * n], outs[nb + 2 * n]
        sems = outs[nb + 2 * n + 1:]
        send, recv, loc, fsend, frecv, psend, precv, ploc = sems[5:]
        start, collect, finish = _scatter_steps(targets, hin, landed, sems[:5])
        x, y, c = lax.axis_index("x"), lax.axis_index("y"), lax.axis_index("c")
        me = 2 * x + y
        dev = 4 * x + 2 * y + c
        sibling = (x, y, 1 - c)

        def remote(i, j, slot):
            px, py = _flip(x, _CHIP_FLIPS[j][0]), _flip(y, _CHIP_FLIPS[j][1])
            return pltpu.make_async_remote_copy(
                src_ref=ins[i].at[2 * px + py], dst_ref=mine[i].at[slot], send_sem=send.at[3 * i + j],
                recv_sem=recv.at[3 * i + j], device_id=(px, py, c), device_id_type=_MESH)

        def packed_to(r, slot):
            px, py, pc = _flip(x, r & 4), _flip(y, r & 2), _flip(c, r & 1)
            return pltpu.make_async_remote_copy(
                src_ref=pk, dst_ref=pk_all.at[slot], send_sem=psend.at[r - 1], recv_sem=precv.at[r - 1],
                device_id=(px, py, pc), device_id_type=_MESH)

        def forward(i):
            return pltpu.make_async_remote_copy(
                src_ref=mine[i], dst_ref=theirs[i], send_sem=fsend.at[i], recv_sem=frecv.at[i],
                device_id=sibling, device_id_type=_MESH)

        start()
        local = [pltpu.make_async_copy(ins[i].at[me], mine[i].at[me], loc.at[i]) for i in range(n)]
        plocal = pltpu.make_async_copy(pk, pk_all.at[dev], ploc.at[0])
        sends = [remote(i, j, me) for i in range(n) for j in range(3)]
        psends = [packed_to(r, dev) for r in range(1, N_DEV)]
        for cp in local + [plocal] + sends + psends:
            cp.start()
        fwd = [forward(i) for i in range(n)]
        for i in range(n):
            for j, (fx, fy) in enumerate(_CHIP_FLIPS):
                remote(i, j, 2 * _flip(x, fx) + _flip(y, fy)).wait_recv()
            local[i].wait()
            fwd[i].start()
        collect()
        finish()
        for i in range(n):
            fwd[i].wait_recv()
        for r in range(1, N_DEV):
            packed_to(r, 4 * _flip(x, r & 4) + 2 * _flip(y, r & 2) + _flip(c, r & 1)).wait_recv()
        for cp in sends + psends + fwd:
            cp.wait_send()
        plocal.wait()

    land = [jax.ShapeDtypeStruct(p.shape, p.dtype) for p in parts]
    out = pl.pallas_call(
        body, in_specs=[_ANY] * n_in, out_specs=[_ANY] * (nb + 2 * n + 1),
        out_shape=[jax.ShapeDtypeStruct(b.shape, b.dtype) for b in bufs] + land + land
        + [jax.ShapeDtypeStruct((N_DEV,) + packed.shape, packed.dtype)],
        input_output_aliases={nh + n + 1 + k: i for k, i in enumerate(passed)},
        scratch_shapes=_scatter_semaphores(nh) + [
            pltpu.SemaphoreType.DMA((3 * n,)), pltpu.SemaphoreType.DMA((3 * n,)), pltpu.SemaphoreType.DMA((n,)),
            pltpu.SemaphoreType.DMA((n,)), pltpu.SemaphoreType.DMA((n,)), pltpu.SemaphoreType.DMA((N_DEV - 1,)),
            pltpu.SemaphoreType.DMA((N_DEV - 1,)), pltpu.SemaphoreType.DMA((1,))],
        name=name)(*halves, *parts, packed, *[bufs[i] for i in passed])
    return out[:nb], out[nb:nb + n], out[nb + n:nb + 2 * n], out[nb + 2 * n]


def adamw(w, m, v, parts, name):
    r, c = w.shape
    tr = r
    for cand in (512, 256, 128, 64, 32, 16, 8):
        if r % cand == 0 and cand * c * 4 <= 512 * 1024:
            tr = cand
            break
    np_ = len(parts)
    nslot = parts[0].shape[0]
    bc1 = 1.0 - ADAM_B1 ** ADAM_STEP
    bc2 = 1.0 - ADAM_B2 ** ADAM_STEP

    def body(*refs):
        w_ref, m_ref, v_ref = refs[:3]
        p_refs = refs[3:3 + np_]
        g_ref, d_ref, nm_ref, nv_ref = refs[3 + np_:]
        g = None
        for k in range(nslot):
            t = p_refs[0][k].astype(F32)
            for p_ref in p_refs[1:]:
                t = t + p_ref[k].astype(F32)
            g = t if g is None else g + t
        wv = w_ref[...]
        nm = ADAM_B1 * m_ref[...] + (1.0 - ADAM_B1) * g
        nv = ADAM_B2 * v_ref[...] + (1.0 - ADAM_B2) * (g * g)
        g_ref[...] = g
        nm_ref[...] = nm
        nv_ref[...] = nv
        d_ref[...] = -ADAM_LR * ((nm / bc1) / (jnp.sqrt(nv / bc2) + ADAM_EPS) + ADAM_WD * wv)

    row = pl.BlockSpec((tr, c), lambda i: (i, 0))
    slab = pl.BlockSpec((nslot, tr, c), lambda i: (0, i, 0))
    sh = jax.ShapeDtypeStruct((r, c), F32)
    return pl.pallas_call(
        body, grid=(r // tr,), in_specs=[row, row, row] + [slab] * np_, out_specs=[row] * 4,
        out_shape=[sh] * 4, compiler_params=_cp("parallel"), name=name)(w, m, v, *parts)


WEIGHTS = ["norm_mix_g", "norm_ffn_g", "pool_w", "pool_b", "pool_scale", "s5_lam_re", "s5_lam_im", "s5_log_dt",
           "s5_b_re", "s5_b_im", "s5_c_re", "s5_c_im", "s5_d", "s5_w_out", "s5_b_out", "lru_w_in", "lru_conv_w",
           "lru_conv_b", "lru_w_a", "lru_b_a", "lru_w_x", "lru_b_x", "lru_lam", "lru_w_out", "sb_w_qkv", "sb_q_g",
           "sb_k_g", "sb_w_o", "ffn_w_in", "ffn_conv_w", "ffn_conv_b", "ffn_w_out"]
SHARD_AXIS = dict(pool_w=2, s5_d=1, s5_w_out=2, s5_b_out=1, lru_w_in=2, lru_conv_w=2, lru_conv_b=1, lru_w_a=2,
                  lru_b_a=1, lru_w_x=2, lru_b_x=1, lru_lam=1, lru_w_out=1, sb_w_qkv=2, sb_w_o=1, ffn_w_in=2,
                  ffn_conv_w=2, ffn_w_out=1)
MXU_WEIGHTS = ("pool_w", "s5_w_out", "lru_w_in", "lru_w_a", "lru_w_x", "lru_w_out", "sb_w_qkv", "sb_w_o",
               "ffn_w_in", "ffn_w_out")
SHARDED = [n for n in WEIGHTS if n in SHARD_AXIS]
REPLICATED = [n for n in WEIGHTS if n not in SHARD_AXIS]
PACK_WIDTH = 1024


def _as_rows(a):
    return a.reshape(-1, a.shape[-1])


def _pack(arrays):
    rows = []
    for a in arrays:
        flat = a.reshape(-1)
        pad = (-flat.shape[0]) % PACK_WIDTH
        rows.append(jnp.pad(flat, (0, pad)).reshape(-1, PACK_WIDTH))
    out = jnp.concatenate(rows, axis=0)
    return jnp.pad(out, ((0, (-out.shape[0]) % 8), (0, 0)))


def _unpack(packed, like):
    out, r = [], 0
    for a in like:
        size = math.prod(a.shape)
        nrow = -(-size // PACK_WIDTH)
        out.append(packed[r:r + nrow].reshape(-1)[:size].reshape(a.shape))
        r += nrow
    return out


def kernel(*args):
    names = ["x"] + WEIGHTS + ["loss_target"] + ["m_" + n for n in WEIGHTS] + ["v_" + n for n in WEIGHTS]
    assert len(args) == len(names)
    given = dict(zip(names, args))
    x, target = given["x"], given["loss_target"]
    bsz, seq, d = x.shape
    m_tok = bsz * seq

    pieces = {}
    for n in SHARDED:
        a = given[n].astype(BF16) if n in MXU_WEIGHTS else given[n]
        if n in ("ffn_w_in", "ffn_w_out"):
            for l in range(4):
                pieces[(n, l)] = (a[l], a[l].shape, SHARD_AXIS[n] - 1)
        else:
            pieces[(n, None)] = (_as_rows(a), a.shape, SHARD_AXIS[n])
    early = [k for k in pieces if k[0] in ("pool_w", "s5_w_out") or k[1] == 0 or not _splits(pieces[k][0].shape)]
    late = [k for k in pieces if k not in early]
    p = {n: given[n] for n in REPLICATED}
    p.update(ffn_w_in=[None] * 4, ffn_w_out=[None] * 4)
    my_chip = 2 * lax.axis_index("x") + lax.axis_index("y")

    def assemble(keys, gathered):
        for key, g4 in zip(keys, gathered):
            own, shape, axis = pieces[key]
            blocks = g4.reshape((N_CHIPS,) + shape)
            own = own.reshape(shape)
            full = jnp.concatenate([jnp.where(my_chip == k, own, blocks[k]) for k in range(N_CHIPS)], axis=axis)
            if key[1] is None:
                p[key[0]] = full
            else:
                p[key[0]][key[1]] = full

    assemble(early, gather_weights([pieces[k][0] for k in early]))

    def mixer_params():
        return {k: (v[0] if v.ndim > 2 or k == "s5_log_dt" else v) for k, v in p.items()
                if not k.startswith(("norm_", "ffn_"))}

    mixers = ((pool_layer_fwd, pool_layer_bwd), (s5_layer_fwd, s5_layer_bwd), (lru_layer_fwd, lru_layer_bwd),
              (sb_layer_fwd, sb_layer_bwd))
    ffn_p = lambda l: (p["norm_ffn_g"][l:l + 1], p["ffn_w_in"][l], p["ffn_conv_w"][l], p["ffn_conv_b"][l:l + 1],
                       p["ffn_w_out"][l])

    def mixer_args(l):
        g = p["norm_mix_g"][l:l + 1]
        if l == 0:
            return (g, mix["pool_w"], mix["pool_b"], mix["pool_scale"], bsz)
        return (g, mix, bsz)

    h = x.reshape(m_tok, d)
    saved = []
    mix = mixer_params()
    for l in range(4):
        if l == 1:
            h, s_mix, gathered = s5_layer_fwd(h, *mixer_args(l), later_shards=[pieces[k][0] for k in late])
            assemble(late, gathered)
            mix = mixer_params()
        else:
            h, s_mix = mixers[l][0](h, *mixer_args(l))
        gl, w_in, cw, cb, w_out = ffn_p(l)
        h, s_ffn = ffn_fwd(h, gl, w_in, cw, cb, w_out, bsz)
        saved.append((s_mix, s_ffn))
    dh, loss_part = loss_head(h, target.reshape(m_tok, d))
    loss = lax.psum(jnp.sum(loss_part), ("x", "y", "c"))

    def full_shape(n):
        shape = list(given[n].shape)
        if n in SHARD_AXIS:
            shape[SHARD_AXIS[n]] *= N_CHIPS
        return tuple(shape)

    grads = {}
    ffn_g = [None] * 4
    mix_g = [None] * 4
    core = lax.axis_index("c").astype(jnp.int32).reshape(1)
    big = [k for k in pieces if _splits(pieces[k][0].shape)]
    small = [k[0] for k in pieces if k not in big]
    behind_s5 = [k for k in big if k[0].startswith(("sb_", "lru_")) or (k[1] is not None and k[1] >= 1)]
    at_end = [k for k in big if k not in behind_s5]

    def part_of(key):
        n, l = key
        g, axis = (grads[n], SHARD_AXIS[n]) if l is None else (ffn_g[l][n], SHARD_AXIS[n] - 1)
        blocks = jnp.stack(jnp.split(g, N_CHIPS, axis=axis))
        return blocks.reshape(N_CHIPS, -1, blocks.shape[-1])

    def reduce_pairs(keys, tag):
        mine = [part_of(k) for k in keys]
        got = grad_halves_exchange(mine, name="grad_halves_exchange_" + tag)
        return [pair_sum(a, b, core, BF16, "pair_sum_%s_%s" % k) for k, a, b in zip(keys, mine, got)]

    def landing(keys, made):
        names = list(dict.fromkeys(k[0] for k in keys))
        bufs = [made.get(n, jax.ShapeDtypeStruct((N_CHIPS,) + _as_rows(given[n]).shape, BF16)) for n in names]
        targets = []
        for n, l in keys:
            rows = pieces[(n, l)][0].shape[0]
            targets.append((names.index(n), (l or 0) * rows, rows))
        return names, bufs, targets

    landed = {}
    for l in range(3, -1, -1):
        gl, w_in, cw, cb, w_out = ffn_p(l)
        dh, ffn_g[l] = ffn_bwd(saved[l][1], dh, gl, w_in, cw, cb, w_out, bsz)
        if l == 1:
            names, bufs, targets = landing(behind_s5, landed)
            dh, mix_g[l], filled = s5_layer_bwd(saved[l][0], dh, *mixer_args(l),
                                                scatter=(reduce_pairs(behind_s5, "behind_s5"), targets, bufs))
            landed.update(zip(names, filled))
        else:
            dh, mix_g[l] = mixers[l][1](saved[l][0], dh, *mixer_args(l))
        for k, v in mix_g[l].items():
            if k != "norm_mix_g":
                grads[k] = v.reshape(full_shape(k))
    for k in ("norm_ffn_g", "ffn_conv_w", "ffn_conv_b"):
        grads[k] = jnp.stack([ffn_g[l][k] for l in range(4)]).reshape(full_shape(k))
    grads["norm_mix_g"] = jnp.concatenate([mix_g[l]["norm_mix_g"] for l in range(4)], axis=0)
    grad_x = dh.reshape(bsz, seq, d)

    names, bufs, targets = landing(at_end, landed)
    filled, mine, theirs, packed_all = grad_sync(reduce_pairs(at_end, "at_end"), targets, bufs,
                                                 [part_of((n, None)) for n in small],
                                                 _pack([grads[n] for n in REPLICATED]))
    landed.update(zip(names, filled))

    summed = {n: [landed[n]] for n in landed}
    summed.update({n: [mine[i], theirs[i]] for i, n in enumerate(small)})

    out = {}
    for n in SHARDED:
        res = adamw(_as_rows(given[n]), _as_rows(given["m_" + n]), _as_rows(given["v_" + n]), summed[n], "adamw_" + n)
        out[n] = [r.reshape(given[n].shape) for r in res]
    res = adamw(_pack([given[n] for n in REPLICATED]), _pack([given["m_" + n] for n in REPLICATED]),
                _pack([given["v_" + n] for n in REPLICATED]), [packed_all], "adamw_replicated")
    like = [given[n] for n in REPLICATED]
    for n, *vals in zip(REPLICATED, *[_unpack(r, like) for r in res]):
        out[n] = list(vals)
    return (loss, grad_x, *[out[n][0] for n in WEIGHTS], *[out[n][1] for n in WEIGHTS],
            *[out[n][2] for n in WEIGHTS], *[out[n][3] for n in WEIGHTS])
```

```python
import functools
import math

import jax
import jax.numpy as jnp
from jax import lax
from jax.experimental import pallas as pl
from jax.experimental.pallas import tpu as pltpu

F32 = jnp.float32
BF16 = jnp.bfloat16

EPS = 1e-6
N_CHIPS = 4
N_DEV = 8
POOL_WINDOWS = (2, 4, 8, 16)
POOL_GROUP = 256
S5_GROUP = 16
S5_STATE = 64
S5_CHUNKS = 8
S5_LANES = 512
S5_T = 256
LRU_BLOCK = 256
LRU_CONV = 4
LRU_C = 8.0
SB_HEADS = 16
SB_DIM = 64
SB_BLOCK = 512
SB_CHUNK = 128
FFN_CONV = 3
ADAM_LR, ADAM_B1, ADAM_B2, ADAM_EPS, ADAM_WD, ADAM_STEP = 0.001, 0.9, 0.999, 1e-08, 0.01, 10
VMEM_LIMIT_BYTES = 56 * 1024 * 1024
MATMUL_VMEM_BUDGET = 30 * 1024 * 1024
MATMUL_WHOLE_K = 2816

NN = (((1,), (0,)), ((), ()))
NT = (((1,), (1,)), ((), ()))
TN = (((0,), (0,)), ((), ()))


def _cp(*sem):
    return pltpu.CompilerParams(dimension_semantics=sem, vmem_limit_bytes=VMEM_LIMIT_BYTES)


def _pick(n, prefs):
    for p in prefs:
        if n % p == 0:
            return p
    return n


def _dot(a, b, dims=NN):
    return lax.dot_general(a.astype(BF16), b.astype(BF16), dims, preferred_element_type=F32)


def _split(x):
    hi = x.astype(BF16)
    return hi, (x - hi.astype(F32)).astype(BF16)


def _dot3(a, b, dims=NN):
    ah, al = _split(a)
    bh, bl = _split(b)
    d = lambda p, q: lax.dot_general(p, q, dims, preferred_element_type=F32)
    return d(ah, bh) + (d(ah, bl) + d(al, bh))


def _dot_exact_rhs(a, b01):
    ah, al = _split(a)
    d = lambda p: lax.dot_general(p, b01, NN, preferred_element_type=F32)
    return d(ah) + d(al)


def _sig(x):
    return 1.0 / (1.0 + jnp.exp(-x))


def _softplus(x):
    return jnp.maximum(x, 0.0) + jnp.log(1.0 + jnp.exp(-jnp.abs(x)))


_GELU_C = math.sqrt(2.0 / math.pi)


def _gelu(x):
    return 0.5 * x * (1.0 + jnp.tanh(_GELU_C * (x + 0.044715 * x * x * x)))


def _gelu_grad(x):
    th = jnp.tanh(_GELU_C * (x + 0.044715 * x * x * x))
    return 0.5 * (1.0 + th) + 0.5 * x * (1.0 - th * th) * _GELU_C * (1.0 + 3.0 * 0.044715 * x * x)


def _rows(shape):
    return lax.broadcasted_iota(jnp.int32, shape, 0)


SUBLANES = 8


def _shift_all(xs, k, up):
    t = xs[0].shape[0]
    if k >= t:
        return [jnp.zeros_like(x) for x in xs]
    if k % SUBLANES == 0:
        pad = jnp.zeros((k,) + xs[0].shape[1:], xs[0].dtype)
        return [jnp.concatenate([x[k:], pad] if up else [pad, x[:t - k]], axis=0) for x in xs]
    rows = _rows(xs[0].shape)
    keep = rows < t - k if up else rows >= k
    return [jnp.where(keep, pltpu.roll(x, t - k if up else k, 0), 0.0) for x in xs]


def _shift_down(x, k):
    return _shift_all([x], k, False)[0]


def _shift_up(x, k):
    return _shift_all([x], k, True)[0]


def matmul(a, b, *, ta=False, tb=False, bias=None, res=None, out_dtype=F32, name):
    m, k = (a.shape[1], a.shape[0]) if ta else a.shape
    n = b.shape[0] if tb else b.shape[1]
    assert (b.shape[1] if tb else b.shape[0]) == k
    has_bias, has_res = bias is not None, res is not None
    tk = k if k <= MATMUL_WHOLE_K else _pick(k, (1408, 1024, 512, 256, 128))
    nk = k // tk
    sa, sb, so = a.dtype.itemsize, b.dtype.itemsize, jnp.dtype(out_dtype).itemsize
    tm = tn = None
    for cm in (2048, 1024, 512, 1408, 256, 128):
        for cn in (1024, 512, 1408, 256, 128):
            if m % cm or n % cn:
                continue
            need = 2 * cm * tk * sa + 2 * tk * cn * sb + cm * cn * (2 * so + 4 + (4 if nk > 1 else 0)
                                                                  + (8 if has_res else 0))
            if need <= MATMUL_VMEM_BUDGET and (tm is None or cm * cn > tm * tn):
                tm, tn = cm, cn
    assert tm is not None, (m, n, k)
    dims = (((0 if ta else 1,), (1 if tb else 0,)), ((), ()))

    def body(*refs):
        a_ref, b_ref = refs[:2]
        rest = list(refs[2:])
        bias_ref = rest.pop(0) if has_bias else None
        res_ref = rest.pop(0) if has_res else None
        o_ref = rest[0]

        def finish(r):
            if has_bias:
                r = r + bias_ref[...]
            if has_res:
                r = r + res_ref[...]
            o_ref[...] = r.astype(o_ref.dtype)

        part = lax.dot_general(a_ref[...].astype(BF16), b_ref[...].astype(BF16), dims, preferred_element_type=F32)
        if nk == 1:
            finish(part)
            return
        acc_ref = rest[1]
        kk = pl.program_id(2)

        @pl.when(kk == 0)
        def _():
            acc_ref[...] = part

        @pl.when(kk > 0)
        def _():
            acc_ref[...] += part

        @pl.when(kk == nk - 1)
        def _():
            finish(acc_ref[...])

    in_specs = [
        pl.BlockSpec((tk, tm), lambda i, j, kk: (kk, i)) if ta else pl.BlockSpec((tm, tk), lambda i, j, kk: (i, kk)),
        pl.BlockSpec((tn, tk), lambda i, j, kk: (j, kk)) if tb else pl.BlockSpec((tk, tn), lambda i, j, kk: (kk, j)),
    ]
    args = [a, b]
    if has_bias:
        in_specs.append(pl.BlockSpec((1, tn), lambda i, j, kk: (0, j)))
        args.append(bias)
    if has_res:
        in_specs.append(pl.BlockSpec((tm, tn), lambda i, j, kk: (i, j)))
        args.append(res)
    return pl.pallas_call(
        body, grid=(m // tm, n // tn, nk), in_specs=in_specs,
        out_specs=pl.BlockSpec((tm, tn), lambda i, j, kk: (i, j)),
        out_shape=jax.ShapeDtypeStruct((m, n), out_dtype),
        scratch_shapes=[pltpu.VMEM((tm, tn), F32)] if nk > 1 else [],
        compiler_params=_cp("parallel", "parallel", "arbitrary"), name=name)(*args)


def rms_fwd(x, g, out_dtype, name):
    m, d = x.shape
    tr = _pick(m, (512, 256, 128))

    def body(x_ref, g_ref, o_ref):
        xv = x_ref[...]
        r = lax.rsqrt(jnp.mean(xv * xv, axis=-1, keepdims=True) + EPS)
        o_ref[...] = (xv * r * g_ref[...]).astype(o_ref.dtype)

    return pl.pallas_call(
        body, grid=(m // tr,),
        in_specs=[pl.BlockSpec((tr, d), lambda i: (i, 0)), pl.BlockSpec((1, d), lambda i: (0, 0))],
        out_specs=pl.BlockSpec((tr, d), lambda i: (i, 0)),
        out_shape=jax.ShapeDtypeStruct((m, d), out_dtype),
        compiler_params=_cp("parallel"), name=name)(x, g)


def rms_bwd(x, g, dh, dres, name):
    m, d = x.shape
    tr = _pick(m, (512, 256, 128))

    def body(x_ref, g_ref, dh_ref, dres_ref, dx_ref, dg_ref):
        xv = x_ref[...]
        r = lax.rsqrt(jnp.mean(xv * xv, axis=-1, keepdims=True) + EPS)
        xh = xv * r
        dhv = dh_ref[...].astype(F32)
        dxh = dhv * g_ref[...]
        dx_ref[...] = dres_ref[...] + r * (dxh - xh * jnp.mean(dxh * xh, axis=-1, keepdims=True))

        @pl.when(pl.program_id(0) == 0)
        def _():
            dg_ref[...] = jnp.zeros_like(dg_ref)

        dg_ref[...] += jnp.sum(dhv * xh, axis=0, keepdims=True)

    row = pl.BlockSpec((tr, d), lambda i: (i, 0))
    vec = pl.BlockSpec((1, d), lambda i: (0, 0))
    return pl.pallas_call(
        body, grid=(m // tr,), in_specs=[row, vec, row, row], out_specs=[row, vec],
        out_shape=[jax.ShapeDtypeStruct((m, d), F32), jax.ShapeDtypeStruct((1, d), F32)],
        compiler_params=_cp("arbitrary"), name=name)(x, g, dh, dres)


def _conv_taps(u, kw):
    return [_shift_down(u, kw - 1 - k) for k in range(kw - 1)] + [u]


def _conv_fwd(taps, w_ref, b_ref):
    y = b_ref[...] + w_ref[len(taps) - 1:len(taps), :] * taps[-1]
    for k in range(len(taps) - 1):
        y = y + w_ref[k:k + 1, :] * taps[k]
    return y


def _conv_bwd_input(dy, w_ref, kw):
    du = w_ref[kw - 1:kw, :] * dy
    for k in range(kw - 1):
        du = du + w_ref[k:k + 1, :] * _shift_up(dy, kw - 1 - k)
    return du


def _conv_bwd_weight(dy, taps):
    return [jnp.sum(dy * tap, axis=0, keepdims=True) for tap in taps]


def ffn_act_fwd(u, cw, cb, name):
    bsz, s, f2 = u.shape
    f = f2 // 2
    tc = _pick(f, (256, 128))
    nj = f // tc

    def body(uv_ref, ug_ref, wv_ref, wg_ref, bv_ref, bg_ref, a_ref):
        hv = _conv_fwd(_conv_taps(uv_ref[0], FFN_CONV), wv_ref, bv_ref)
        hg = _conv_fwd(_conv_taps(ug_ref[0], FFN_CONV), wg_ref, bg_ref)
        a_ref[0] = (hg * _sig(hg) * hv).astype(a_ref.dtype)

    uv = pl.BlockSpec((1, s, tc), lambda b, j: (b, 0, j))
    ug = pl.BlockSpec((1, s, tc), lambda b, j: (b, 0, j + nj))
    wv = pl.BlockSpec((FFN_CONV, tc), lambda b, j: (0, j))
    wg = pl.BlockSpec((FFN_CONV, tc), lambda b, j: (0, j + nj))
    bv = pl.BlockSpec((1, tc), lambda b, j: (0, j))
    bg = pl.BlockSpec((1, tc), lambda b, j: (0, j + nj))
    return pl.pallas_call(
        body, grid=(bsz, nj), in_specs=[uv, ug, wv, wg, bv, bg], out_specs=uv,
        out_shape=jax.ShapeDtypeStruct((bsz, s, f), BF16),
        compiler_params=_cp("parallel", "parallel"), name=name)(u, u, cw, cw, cb, cb)


def ffn_act_bwd(u, cw, cb, da, name):
    bsz, s, f2 = u.shape
    f = f2 // 2
    tc = _pick(f, (256, 128))
    nj = f // tc

    def body(uv_ref, ug_ref, wv_ref, wg_ref, bv_ref, bg_ref, da_ref,
             duv_ref, dug_ref, dwv_ref, dwg_ref, dbv_ref, dbg_ref):
        tv, tg = _conv_taps(uv_ref[0], FFN_CONV), _conv_taps(ug_ref[0], FFN_CONV)
        hv = _conv_fwd(tv, wv_ref, bv_ref)
        hg = _conv_fwd(tg, wg_ref, bg_ref)
        sg = _sig(hg)
        dav = da_ref[0].astype(F32)
        dhv = dav * hg * sg
        dhg = dav * hv * (sg * (1.0 + hg * (1.0 - sg)))
        duv_ref[0] = _conv_bwd_input(dhv, wv_ref, FFN_CONV).astype(duv_ref.dtype)
        dug_ref[0] = _conv_bwd_input(dhg, wg_ref, FFN_CONV).astype(dug_ref.dtype)

        @pl.when(pl.program_id(1) == 0)
        def _():
            for r in (dwv_ref, dwg_ref, dbv_ref, dbg_ref):
                r[...] = jnp.zeros_like(r)

        for k, row in enumerate(_conv_bwd_weight(dhv, tv)):
            dwv_ref[k:k + 1, :] += row
        for k, row in enumerate(_conv_bwd_weight(dhg, tg)):
            dwg_ref[k:k + 1, :] += row
        dbv_ref[...] += jnp.sum(dhv, axis=0, keepdims=True)
        dbg_ref[...] += jnp.sum(dhg, axis=0, keepdims=True)

    uv = pl.BlockSpec((1, s, tc), lambda j, b: (b, 0, j))
    ug = pl.BlockSpec((1, s, tc), lambda j, b: (b, 0, j + nj))
    wv = pl.BlockSpec((FFN_CONV, tc), lambda j, b: (0, j))
    wg = pl.BlockSpec((FFN_CONV, tc), lambda j, b: (0, j + nj))
    bv = pl.BlockSpec((1, tc), lambda j, b: (0, j))
    bg = pl.BlockSpec((1, tc), lambda j, b: (0, j + nj))
    act = jax.ShapeDtypeStruct((bsz, s, f), BF16)
    return pl.pallas_call(
        body, grid=(nj, bsz), in_specs=[uv, ug, wv, wg, bv, bg, uv],
        out_specs=[uv, uv, wv, wv, bv, bv],
        out_shape=[act, act, jax.ShapeDtypeStruct((FFN_CONV, f), F32), jax.ShapeDtypeStruct((FFN_CONV, f), F32),
                   jax.ShapeDtypeStruct((1, f), F32), jax.ShapeDtypeStruct((1, f), F32)],
        compiler_params=_cp("parallel", "arbitrary"), name=name)(u, u, cw, cw, cb, cb, da)


def ffn_fwd(x, g, w_in, cw, cb, w_out, bsz):
    m, d = x.shape
    h = rms_fwd(x, g, BF16, "ffn_norm")
    u = matmul(h, w_in, name="ffn_in")
    a = ffn_act_fwd(u.reshape(bsz, m // bsz, -1), cw, cb, "ffn_act")
    a2 = a.reshape(m, -1)
    out = matmul(a2, w_out, res=x, name="ffn_out")
    return out, (x, h, u, a2)


def ffn_bwd(saved, dout, g, w_in, cw, cb, w_out, bsz):
    x, h, u, a2 = saved
    m, d = x.shape
    da = matmul(dout, w_out, tb=True, out_dtype=BF16, name="ffn_out_dx")
    dw_out = matmul(a2, dout, ta=True, name="ffn_out_dw")
    u3 = u.reshape(bsz, m // bsz, -1)
    duv, dug, dwv, dwg, dbv, dbg = ffn_act_bwd(u3, cw, cb, da.reshape(bsz, m // bsz, -1), "ffn_act_bwd")
    du = jnp.concatenate([duv, dug], axis=-1).reshape(m, -1)
    dh = matmul(du, w_in, tb=True, name="ffn_in_dx")
    dw_in = matmul(h, du, ta=True, name="ffn_in_dw")
    dx, dg = rms_bwd(x, g, dh, dout, "ffn_norm_bwd")
    grads = dict(norm_ffn_g=dg, ffn_w_in=dw_in, ffn_conv_w=jnp.concatenate([dwv, dwg], axis=-1),
                 ffn_conv_b=jnp.concatenate([dbv, dbg], axis=-1), ffn_w_out=dw_out)
    return dx, grads


def loss_head(y, target, name="loss_head"):
    m, d = y.shape
    tr = _pick(m, (512, 256, 128))

    def body(y_ref, t_ref, dy_ref, l_ref):
        e = y_ref[...] - t_ref[...]
        dy_ref[...] = e * (1.0 / d)

        @pl.when(pl.program_id(0) == 0)
        def _():
            l_ref[...] = jnp.zeros_like(l_ref)

        l_ref[...] += jnp.sum(e * e, axis=0, keepdims=True) * (0.5 / d)

    row = pl.BlockSpec((tr, d), lambda i: (i, 0))
    vec = pl.BlockSpec((1, d), lambda i: (0, 0))
    dy, part = pl.pallas_call(
        body, grid=(m // tr,), in_specs=[row, row], out_specs=[row, vec],
        out_shape=[jax.ShapeDtypeStruct((m, d), F32), jax.ShapeDtypeStruct((1, d), F32)],
        compiler_params=_cp("arbitrary"), name=name)(y, target)
    return dy, part


def _pool_windows(h, gi):
    sums, s, width = [], h, 1
    for _ in POOL_WINDOWS:
        s = s + _shift_down(s, width)
        width *= 2
        sums.append(s)
    pos = _rows(h.shape).astype(F32) + 1.0
    wsum, inv = sums[-1], 1.0 / jnp.minimum(pos, float(POOL_WINDOWS[-1]))
    for k in range(len(POOL_WINDOWS) - 2, -1, -1):
        wsum = jnp.where(gi == k, sums[k], wsum)
        inv = jnp.where(gi == k, 1.0 / jnp.minimum(pos, float(POOL_WINDOWS[k])), inv)
    return wsum * inv - h, inv


def _pool_windows_transpose(e, gi):
    sums, s, width = [], e, 1
    for _ in POOL_WINDOWS:
        s = s + _shift_up(s, width)
        width *= 2
        sums.append(s)
    out = sums[-1]
    for k in range(len(POOL_WINDOWS) - 2, -1, -1):
        out = jnp.where(gi == k, sums[k], out)
    return out


def pool_fwd(h, w, b, scale, x, name="pool_fwd"):
    bsz, s, d = h.shape
    ng = d // POOL_GROUP

    def body(h_ref, w_ref, b_ref, s_ref, x_ref, o_ref):
        dd, _ = _pool_windows(h_ref[0], pl.program_id(1))
        y = _dot(dd, w_ref[0]) + b_ref[...]
        o_ref[0] = x_ref[0] + s_ref[...] * y

    act = pl.BlockSpec((1, s, POOL_GROUP), lambda bb, gi: (bb, 0, gi))
    vec = pl.BlockSpec((1, POOL_GROUP), lambda bb, gi: (0, gi))
    return pl.pallas_call(
        body, grid=(bsz, ng),
        in_specs=[act, pl.BlockSpec((1, POOL_GROUP, POOL_GROUP), lambda bb, gi: (gi, 0, 0)), vec, vec, act],
        out_specs=act, out_shape=jax.ShapeDtypeStruct((bsz, s, d), F32),
        compiler_params=_cp("parallel", "parallel"), name=name)(h, w, b, scale, x)


def pool_bwd(h, w, b, scale, dy, name="pool_bwd"):
    bsz, s, d = h.shape
    ng = d // POOL_GROUP

    def body(h_ref, w_ref, b_ref, s_ref, dy_ref, dh_ref, dw_ref, db_ref, ds_ref):
        gi = pl.program_id(0)
        dd, inv = _pool_windows(h_ref[0], gi)
        ypre = _dot(dd, w_ref[0]) + b_ref[...]
        dyv = dy_ref[0]
        dyb = dyv * s_ref[...]

        @pl.when(pl.program_id(1) == 0)
        def _():
            for r in (dw_ref, db_ref, ds_ref):
                r[...] = jnp.zeros_like(r)

        ds_ref[...] += jnp.sum(dyv * ypre, axis=0, keepdims=True)
        db_ref[...] += jnp.sum(dyb, axis=0, keepdims=True)
        dw_ref[0] += _dot(dd, dyb, TN)
        ddd = _dot(dyb, w_ref[0], NT)
        dh_ref[0] = _pool_windows_transpose(ddd * inv, gi) - ddd

    act = pl.BlockSpec((1, s, POOL_GROUP), lambda gi, bb: (bb, 0, gi))
    vec = pl.BlockSpec((1, POOL_GROUP), lambda gi, bb: (0, gi))
    wsp = pl.BlockSpec((1, POOL_GROUP, POOL_GROUP), lambda gi, bb: (gi, 0, 0))
    return pl.pallas_call(
        body, grid=(ng, bsz), in_specs=[act, wsp, vec, vec, act], out_specs=[act, wsp, vec, vec],
        out_shape=[jax.ShapeDtypeStruct((bsz, s, d), F32), jax.ShapeDtypeStruct((ng, POOL_GROUP, POOL_GROUP), F32),
                   jax.ShapeDtypeStruct((1, d), F32), jax.ShapeDtypeStruct((1, d), F32)],
        compiler_params=_cp("parallel", "arbitrary"), name=name)(h, w, b, scale, dy)


def pool_layer_fwd(x, g, w, b, scale, bsz):
    m, d = x.shape
    h = rms_fwd(x, g, F32, "pool_norm")
    out = pool_fwd(h.reshape(bsz, m // bsz, d), w, b, scale, x.reshape(bsz, m // bsz, d))
    return out.reshape(m, d), (x, h)


def pool_layer_bwd(saved, dout, g, w, b, scale, bsz):
    x, h = saved
    m, d = x.shape
    dh, dw, db, ds = pool_bwd(h.reshape(bsz, m // bsz, d), w, b, scale, dout.reshape(bsz, m // bsz, d))
    dx, dg = rms_bwd(x, g, dh.reshape(m, d), dout, "pool_norm_bwd")
    return dx, dict(norm_mix_g=dg, pool_w=dw[None], pool_b=db, pool_scale=ds)


def _scan_fwd(a, b):
    return _scan(a, b, False)


def _scan(a, b, up):
    k = 1
    while k < a.shape[0]:
        if 2 * k < a.shape[0]:
            sa, sb = _shift_all([a, b], k, up)
            a, b = a * sa, b + a * sb
        else:
            b = b + a * _shift_all([b], k, up)[0]
        k *= 2
    return b


def _scan_bwd(a, b):
    return _scan(a, b, True)


def _neg_expm1(x):
    series = -x * (1.0 + x * (0.5 + x * (1.0 / 6.0 + x * (1.0 / 24.0 + x * (1.0 / 120.0)))))
    return jnp.where(x > -0.03, series, 1.0 - jnp.exp(x))


def _lru_gates(rec, wa_ref, ba_ref, wx_ref, bx_ref, lam_ref):
    r = _sig(_dot(rec, wa_ref[0]) + ba_ref[...])
    i = _sig(_dot(rec, wx_ref[0]) + bx_ref[...])
    sp = _softplus(-lam_ref[...])
    log_a = -LRU_C * r * sp
    a = jnp.exp(log_a)
    mult = jnp.sqrt(_neg_expm1(2.0 * log_a))
    return r, i, sp, a, mult


def lru_fwd(zz, cw, cb, wa, ba, wx, bx, lam, name="lru_fwd"):
    bsz, s, r2 = zz.shape
    rw = r2 // 2
    nb = rw // LRU_BLOCK

    def body(g_ref, p_ref, cw_ref, cb_ref, wa_ref, ba_ref, wx_ref, bx_ref, lam_ref, h_ref, y_ref):
        rec = _conv_fwd(_conv_taps(p_ref[0], LRU_CONV), cw_ref, cb_ref)
        _, i, _, a, mult = _lru_gates(rec, wa_ref, ba_ref, wx_ref, bx_ref, lam_ref)
        hst = _scan_fwd(a, mult * (i * rec))
        h_ref[0] = hst
        y_ref[0] = (_gelu(g_ref[0]) * hst).astype(y_ref.dtype)

    gsp = pl.BlockSpec((1, s, LRU_BLOCK), lambda bb, n: (bb, 0, n))
    psp = pl.BlockSpec((1, s, LRU_BLOCK), lambda bb, n: (bb, 0, n + nb))
    cws = pl.BlockSpec((LRU_CONV, LRU_BLOCK), lambda bb, n: (0, n))
    vec = pl.BlockSpec((1, LRU_BLOCK), lambda bb, n: (0, n))
    wsp = pl.BlockSpec((1, LRU_BLOCK, LRU_BLOCK), lambda bb, n: (n, 0, 0))
    return pl.pallas_call(
        body, grid=(bsz, nb), in_specs=[gsp, psp, cws, vec, wsp, vec, wsp, vec, vec], out_specs=[gsp, gsp],
        out_shape=[jax.ShapeDtypeStruct((bsz, s, rw), F32), jax.ShapeDtypeStruct((bsz, s, rw), BF16)],
        compiler_params=_cp("parallel", "parallel"), name=name)(zz, zz, cw, cb, wa, ba, wx, bx, lam)


def lru_bwd(zz, hst, dy, cw, cb, wa, ba, wx, bx, lam, name="lru_bwd"):
    bsz, s, r2 = zz.shape
    rw = r2 // 2
    nb = rw // LRU_BLOCK

    def body(g_ref, p_ref, h_ref, dy_ref, cw_ref, cb_ref, wa_ref, ba_ref, wx_ref, bx_ref, lam_ref,
             dg_ref, dp_ref, dcw_ref, dcb_ref, dwa_ref, dba_ref, dwx_ref, dbx_ref, dlam_ref):
        pre = p_ref[0]
        taps = _conv_taps(pre, LRU_CONV)
        rec = _conv_fwd(taps, cw_ref, cb_ref)
        r, i, sp, a, mult = _lru_gates(rec, wa_ref, ba_ref, wx_ref, bx_ref, lam_ref)
        hst_v, gate, dyv = h_ref[0], g_ref[0], dy_ref[0]
        dg_ref[0] = (dyv * hst_v * _gelu_grad(gate)).astype(dg_ref.dtype)
        lmb = _scan_bwd(_shift_up(a, 1), dyv * _gelu(gate))
        da = lmb * _shift_down(hst_v, 1)
        dmult = lmb * (i * rec)
        dlog_a = da * a - dmult * (a * a) / mult
        dr = dlog_a * (-LRU_C) * sp
        dra = dr * r * (1.0 - r)
        dxa = lmb * mult * rec * i * (1.0 - i)
        drec = lmb * mult * i + _dot(dra, wa_ref[0], NT) + _dot(dxa, wx_ref[0], NT)
        dp_ref[0] = _conv_bwd_input(drec, cw_ref, LRU_CONV).astype(dp_ref.dtype)

        @pl.when(pl.program_id(1) == 0)
        def _():
            for ref in (dcw_ref, dcb_ref, dwa_ref, dba_ref, dwx_ref, dbx_ref, dlam_ref):
                ref[...] = jnp.zeros_like(ref)

        for k, row in enumerate(_conv_bwd_weight(drec, taps)):
            dcw_ref[k:k + 1, :] += row
        dcb_ref[...] += jnp.sum(drec, axis=0, keepdims=True)
        dwa_ref[0] += _dot(rec, dra, TN)
        dwx_ref[0] += _dot(rec, dxa, TN)
        dba_ref[...] += jnp.sum(dra, axis=0, keepdims=True)
        dbx_ref[...] += jnp.sum(dxa, axis=0, keepdims=True)
        dsp = jnp.sum(dlog_a * (-LRU_C) * r, axis=0, keepdims=True)
        dlam_ref[...] += dsp * (-_sig(-lam_ref[...]))

    gsp = pl.BlockSpec((1, s, LRU_BLOCK), lambda n, bb: (bb, 0, n))
    psp = pl.BlockSpec((1, s, LRU_BLOCK), lambda n, bb: (bb, 0, n + nb))
    cws = pl.BlockSpec((LRU_CONV, LRU_BLOCK), lambda n, bb: (0, n))
    vec = pl.BlockSpec((1, LRU_BLOCK), lambda n, bb: (0, n))
    wsp = pl.BlockSpec((1, LRU_BLOCK, LRU_BLOCK), lambda n, bb: (n, 0, 0))
    act = jax.ShapeDtypeStruct((bsz, s, rw), BF16)
    vsh = jax.ShapeDtypeStruct((1, rw), F32)
    wsh = jax.ShapeDtypeStruct((nb, LRU_BLOCK, LRU_BLOCK), F32)
    return pl.pallas_call(
        body, grid=(nb, bsz), in_specs=[gsp, psp, gsp, gsp, cws, vec, wsp, vec, wsp, vec, vec],
        out_specs=[gsp, gsp, cws, vec, wsp, vec, wsp, vec, vec],
        out_shape=[act, act, jax.ShapeDtypeStruct((LRU_CONV, rw), F32), vsh, wsh, vsh, wsh, vsh, vsh],
        compiler_params=_cp("parallel", "arbitrary"), name=name)(zz, zz, hst, dy, cw, cb, wa, ba, wx, bx, lam)


def lru_layer_fwd(x, g, p, bsz):
    m, d = x.shape
    h = rms_fwd(x, g, BF16, "lru_norm")
    zz = matmul(h, p["lru_w_in"], name="lru_in")
    hst, y = lru_fwd(zz.reshape(bsz, m // bsz, -1), p["lru_conv_w"], p["lru_conv_b"], p["lru_w_a"], p["lru_b_a"],
                     p["lru_w_x"], p["lru_b_x"], p["lru_lam"])
    y2 = y.reshape(m, -1)
    out = matmul(y2, p["lru_w_out"], res=x, name="lru_out")
    return out, (x, h, zz, hst, y2)


def lru_layer_bwd(saved, dout, g, p, bsz):
    x, h, zz, hst, y2 = saved
    m, d = x.shape
    dy = matmul(dout, p["lru_w_out"], tb=True, name="lru_out_dx")
    dw_out = matmul(y2, dout, ta=True, name="lru_out_dw")
    dgate, dpre, dcw, dcb, dwa, dba, dwx, dbx, dlam = lru_bwd(
        zz.reshape(bsz, m // bsz, -1), hst, dy.reshape(bsz, m // bsz, -1), p["lru_conv_w"], p["lru_conv_b"],
        p["lru_w_a"], p["lru_b_a"], p["lru_w_x"], p["lru_b_x"], p["lru_lam"])
    dzz = jnp.concatenate([dgate, dpre], axis=-1).reshape(m, -1)
    dh = matmul(dzz, p["lru_w_in"], tb=True, name="lru_in_dx")
    dw_in = matmul(h, dzz, ta=True, name="lru_in_dw")
    dx, dg = rms_bwd(x, g, dh, dout, "lru_norm_bwd")
    return dx, dict(norm_mix_g=dg, lru_w_in=dw_in, lru_conv_w=dcw[None], lru_conv_b=dcb, lru_w_a=dwa[None],
                    lru_b_a=dba, lru_w_x=dwx[None], lru_b_x=dbx, lru_lam=dlam, lru_w_out=dw_out)


def _s5_discretise(lam_re, lam_im, log_dt, b_re, b_im):
    lr = jnp.minimum(lam_re, -1e-4)
    dt = jnp.exp(log_dt)[:, None]
    mag = jnp.exp(lr * dt)
    ar, ai = mag * jnp.cos(lam_im * dt), mag * jnp.sin(lam_im * dt)
    den = lr * lr + lam_im * lam_im
    cr = ((ar - 1.0) * lr + ai * lam_im) / den
    ci = (ai * lr - (ar - 1.0) * lam_im) / den
    bbr = cr[..., None] * b_re - ci[..., None] * b_im
    bbi = cr[..., None] * b_im + ci[..., None] * b_re
    return ar, ai, bbr, bbi


def _s5_powers(lam_re, lam_im, log_dt, ns):
    lr = jnp.minimum(lam_re, -1e-4)
    dt = jnp.exp(log_dt)[:, None]
    n = jnp.asarray(ns, F32)[:, None, None]
    mag = jnp.exp(n * (lr * dt))
    ang = n * (lam_im * dt)
    to_chunks = lambda t: t.reshape(len(ns), S5_CHUNKS, S5_LANES).transpose(1, 0, 2)
    return jnp.concatenate([to_chunks(mag * jnp.cos(ang)), to_chunks(mag * jnp.sin(ang))], axis=-1)


def _s5_in_matrix(bbr, bbi):
    eye = jnp.eye(8, dtype=F32)
    blk = lambda t: jnp.einsum("qgph,gk->qghkp", t.reshape(S5_CHUNKS, 8, S5_STATE, S5_GROUP), eye).reshape(
        S5_CHUNKS, 128, S5_LANES)
    return jnp.concatenate([blk(bbr), blk(bbi)], axis=-1)


def _s5_in_matrix_diag(dmat):
    eye = jnp.eye(8, dtype=F32)[None, :, None, :, None]
    pick = lambda t: (t.reshape(S5_CHUNKS, 8, S5_GROUP, 8, S5_STATE) * eye).sum(3).transpose(0, 1, 3, 2).reshape(
        S5_CHUNKS * 8, S5_STATE, S5_GROUP)
    return pick(dmat[..., :S5_LANES]), pick(dmat[..., S5_LANES:])


def _s5_out_matrix(c_re, c_im):
    eye = jnp.eye(8, dtype=F32)
    blk = lambda t: jnp.einsum("qghp,gk->qgpkh", t.reshape(S5_CHUNKS, 8, S5_GROUP, S5_STATE), eye).reshape(
        S5_CHUNKS, S5_LANES, 128)
    return jnp.concatenate([blk(c_re), -blk(c_im)], axis=1)


def _s5_out_matrix_diag(dmat):
    eye = jnp.eye(8, dtype=F32)[None, :, None, :, None]
    pick = lambda t: (t.reshape(S5_CHUNKS, 8, S5_STATE, 8, S5_GROUP) * eye).sum(3).transpose(0, 1, 3, 2).reshape(
        S5_CHUNKS * 8, S5_GROUP, S5_STATE)
    return pick(dmat[:, :S5_LANES]), -pick(dmat[:, S5_LANES:])


def s5_fwd(h, bmat, cmat, atab, pw, dskip, later_shards=(), name="s5_fwd"):
    bsz, s, d = h.shape
    t = min(S5_T, s)
    nt, nlev, ln = s // t, atab.shape[1], S5_LANES
    ng = len(later_shards)

    def body(*refs):
        h_ref, b_ref, c_ref, a_ref, pw_ref, d_ref = refs[:6]
        xs_ref, yp_ref, yg_ref = refs[6 + ng:9 + ng]
        carry = refs[9 + 2 * ng]
        if ng:
            start, finish = _gather_steps([g.shape for g in later_shards], refs[6:6 + ng], refs[9 + ng:9 + 2 * ng],
                                          refs[10 + 2 * ng:])
            step = (pl.program_id(0) * S5_CHUNKS + pl.program_id(1)) * nt + pl.program_id(2)
            pl.when(step == 0)(start)

        @pl.when(pl.program_id(2) == 0)
        def _():
            carry[...] = jnp.zeros_like(carry)

        u = h_ref[0]
        bu = _dot3(u, b_ref[0])
        xr, xi = bu[:, :ln], bu[:, ln:]
        for k in range(nlev):
            ar, ai = a_ref[0, k:k + 1, :ln], a_ref[0, k:k + 1, ln:]
            sr, si = _shift_all([xr, xi], 1 << k, False)
            xr, xi = xr + ar * sr - ai * si, xi + ar * si + ai * sr
        cr, ci = carry[0:1, :ln], carry[0:1, ln:]
        pr, pi = pw_ref[0, :, :ln], pw_ref[0, :, ln:]
        xr, xi = xr + pr * cr - pi * ci, xi + pr * ci + pi * cr
        carry[0:1, :ln] = xr[t - 1:t, :]
        carry[0:1, ln:] = xi[t - 1:t, :]
        xs_ref[0, :, :ln] = xr
        xs_ref[0, :, ln:] = xi
        y = _dot3(xr, c_ref[0, :ln, :]) + _dot3(xi, c_ref[0, ln:, :]) + d_ref[...] * u
        yp_ref[0] = y
        yg_ref[0] = _gelu(y).astype(yg_ref.dtype)
        if ng:
            pl.when(step == bsz * S5_CHUNKS * nt - 1)(finish)

    act = pl.BlockSpec((1, t, 128), lambda b, q, i: (b, i, q))
    par = lambda r, c: pl.BlockSpec((1, r, c), lambda b, q, i: (q, 0, 0))
    out = pl.pallas_call(
        body, grid=(bsz, S5_CHUNKS, nt),
        in_specs=[act, par(128, 2 * ln), par(2 * ln, 128), par(nlev, 2 * ln), par(t, 2 * ln),
                  pl.BlockSpec((1, 128), lambda b, q, i: (0, q))] + [_ANY] * ng,
        out_specs=[pl.BlockSpec((1, t, 2 * ln), lambda b, q, i: (b, i, q)), act, act] + [_ANY] * ng,
        out_shape=[jax.ShapeDtypeStruct((bsz, s, S5_CHUNKS * 2 * ln), F32), jax.ShapeDtypeStruct((bsz, s, d), F32),
                   jax.ShapeDtypeStruct((bsz, s, d), BF16)] + _gather_out_shapes(later_shards),
        scratch_shapes=[pltpu.VMEM((8, 2 * ln), F32)] + (_gather_semaphores(ng) if ng else []),
        compiler_params=_cp("arbitrary", "arbitrary", "arbitrary"), name=name)(
            h, bmat, cmat, atab, pw, dskip, *later_shards)
    return out[0], out[1], out[2], out[3:]


def s5_bwd(h, ypre, xs, dyg, bmat_t, cmat_t, atab, pw_rev, dskip, scatter=None, name="s5_bwd"):
    bsz, s, d = h.shape
    t = min(S5_T, s)
    nt, nlev, ln = s // t, atab.shape[1], S5_LANES
    halves, targets, buf_shapes = scatter if scatter else ((), (), ())
    nh, nb = len(halves), len(buf_shapes)

    def body(*refs):
        h_ref, yp_ref, xs_ref, xp_ref, dy_ref, bt_ref, ct_ref, a_ref, pw_ref, d_ref = refs[:10]
        dh_ref, db_ref, dc_ref, da_ref, dd_ref = refs[10 + nh:15 + nh]
        carry = refs[15 + nh + nb]
        b, i = pl.program_id(1), pl.program_id(2)
        if nh:
            start, collect, finish = _scatter_steps(targets, refs[10:10 + nh], refs[15 + nh:15 + nh + nb],
                                                    refs[16 + nh + nb:])
            step = (pl.program_id(0) * bsz + b) * nt + i
            n_steps = S5_CHUNKS * bsz * nt
            pl.when(step == 0)(start)

        @pl.when((b == 0) & (i == 0))
        def _():
            for r in (db_ref, dc_ref, da_ref, dd_ref):
                r[...] = jnp.zeros_like(r)

        @pl.when(i == 0)
        def _():
            carry[...] = jnp.zeros_like(carry)

        u = h_ref[0]
        dyp = dy_ref[0] * _gelu_grad(yp_ref[0])
        dd_ref[...] += jnp.sum(dyp * u, axis=0, keepdims=True)
        xr, xi = xs_ref[0, :, :ln], xs_ref[0, :, ln:]
        dc_ref[0, :ln, :] += _dot3(xr, dyp, TN)
        dc_ref[0, ln:, :] += _dot3(xi, dyp, TN)
        lr, li = _dot3(dyp, ct_ref[0, :, :ln]), _dot3(dyp, ct_ref[0, :, ln:])
        for k in range(nlev):
            ar, ai = a_ref[0, k:k + 1, :ln], a_ref[0, k:k + 1, ln:]
            sr, si = _shift_all([lr, li], 1 << k, True)
            lr, li = lr + ar * sr + ai * si, li + ar * si - ai * sr
        cr, ci = carry[0:1, :ln], carry[0:1, ln:]
        pr, pi = pw_ref[0, :, :ln], pw_ref[0, :, ln:]
        lr, li = lr + pr * cr + pi * ci, li + pr * ci - pi * cr
        carry[0:1, :ln] = lr[0:1, :]
        carry[0:1, ln:] = li[0:1, :]
        dh_ref[0] = _dot3(lr, bt_ref[0, :ln, :]) + _dot3(li, bt_ref[0, ln:, :]) + dyp * d_ref[...]
        db_ref[0, :, :ln] += _dot3(u, lr, TN)
        db_ref[0, :, ln:] += _dot3(u, li, TN)
        first = _rows(xr.shape) == 0
        keep = jnp.where(i == nt - 1, 0.0, 1.0)
        xpr = jnp.where(first, xp_ref[0, 7:8, :ln] * keep, pltpu.roll(xr, 1, 0))
        xpi = jnp.where(first, xp_ref[0, 7:8, ln:] * keep, pltpu.roll(xi, 1, 0))
        da_ref[0, 0:1, :ln] += jnp.sum(lr * xpr + li * xpi, axis=0, keepdims=True)
        da_ref[0, 0:1, ln:] += jnp.sum(li * xpr - lr * xpi, axis=0, keepdims=True)
        if nh:
            pl.when(step == (7 * n_steps) // 8 - 1)(collect)
            pl.when(step == n_steps - 1)(finish)

    rev = lambda i: nt - 1 - i
    act = pl.BlockSpec((1, t, 128), lambda q, b, i: (b, rev(i), q))
    xsp = pl.BlockSpec((1, t, 2 * ln), lambda q, b, i: (b, rev(i), q))
    xpp = pl.BlockSpec((1, 8, 2 * ln), lambda q, b, i: (b, jnp.maximum(rev(i) * (t // 8) - 1, 0), q))
    par = lambda r, c: pl.BlockSpec((1, r, c), lambda q, b, i: (q, 0, 0))
    dsp = pl.BlockSpec((1, 128), lambda q, b, i: (0, q))
    out = pl.pallas_call(
        body, grid=(S5_CHUNKS, bsz, nt),
        in_specs=[act, act, xsp, xpp, act, par(2 * ln, 128), par(128, 2 * ln), par(nlev, 2 * ln), par(t, 2 * ln), dsp]
        + [_ANY] * nh,
        out_specs=[act, par(128, 2 * ln), par(2 * ln, 128), par(8, 2 * ln), dsp] + [_ANY] * nb,
        out_shape=[jax.ShapeDtypeStruct((bsz, s, d), F32), jax.ShapeDtypeStruct((S5_CHUNKS, 128, 2 * ln), F32),
                   jax.ShapeDtypeStruct((S5_CHUNKS, 2 * ln, 128), F32), jax.ShapeDtypeStruct((S5_CHUNKS, 8, 2 * ln), F32),
                   jax.ShapeDtypeStruct((1, d), F32)] + list(buf_shapes),
        scratch_shapes=[pltpu.VMEM((8, 2 * ln), F32)] + (_scatter_semaphores(nh) if nh else []),
        compiler_params=_cp("arbitrary", "arbitrary", "arbitrary"), name=name)(
            h, ypre, xs, xs, dyg, bmat_t, cmat_t, atab, pw_rev, dskip, *halves)
    return out[0], out[1], out[2], out[3], out[4], out[5:]


def glu_fwd(z, x, name="s5_glu"):
    m, d = x.shape
    tr = _pick(m, (512, 256, 128))

    def body(z_ref, x_ref, o_ref):
        o_ref[...] = x_ref[...] + z_ref[:, :d] * _sig(z_ref[:, d:])

    return pl.pallas_call(
        body, grid=(m // tr,),
        in_specs=[pl.BlockSpec((tr, 2 * d), lambda i: (i, 0)), pl.BlockSpec((tr, d), lambda i: (i, 0))],
        out_specs=pl.BlockSpec((tr, d), lambda i: (i, 0)), out_shape=jax.ShapeDtypeStruct((m, d), F32),
        compiler_params=_cp("parallel"), name=name)(z, x)


def glu_bwd(z, dout, name="s5_glu_bwd"):
    m, d = dout.shape
    tr = _pick(m, (512, 256, 128))

    def body(z_ref, do_ref, dz_ref, db_ref):
        sg = _sig(z_ref[:, d:])
        dv = do_ref[...] * sg
        dgt = do_ref[...] * z_ref[:, :d] * sg * (1.0 - sg)
        dz_ref[:, :d] = dv.astype(dz_ref.dtype)
        dz_ref[:, d:] = dgt.astype(dz_ref.dtype)

        @pl.when(pl.program_id(0) == 0)
        def _():
            db_ref[...] = jnp.zeros_like(db_ref)

        db_ref[:, :d] += jnp.sum(dv, axis=0, keepdims=True)
        db_ref[:, d:] += jnp.sum(dgt, axis=0, keepdims=True)

    wide = pl.BlockSpec((tr, 2 * d), lambda i: (i, 0))
    return pl.pallas_call(
        body, grid=(m // tr,), in_specs=[wide, pl.BlockSpec((tr, d), lambda i: (i, 0))],
        out_specs=[wide, pl.BlockSpec((1, 2 * d), lambda i: (0, 0))],
        out_shape=[jax.ShapeDtypeStruct((m, 2 * d), BF16), jax.ShapeDtypeStruct((1, 2 * d), F32)],
        compiler_params=_cp("arbitrary"), name=name)(z, dout)


def _s5_tables(p, t):
    nlev = max(1, (t - 1).bit_length())
    lam = (p["s5_lam_re"], p["s5_lam_im"], p["s5_log_dt"])
    atab = _s5_powers(*lam, [1 << k for k in range(nlev)])
    if nlev < 8:
        atab = jnp.pad(atab, ((0, 0), (0, 8 - nlev), (0, 0)))
    pw = _s5_powers(*lam, list(range(1, t + 1)))
    return nlev, atab, pw


def s5_layer_fwd(x, g, p, bsz, later_shards=()):
    m, d = x.shape
    s = m // bsz
    t = min(S5_T, s)
    h = rms_fwd(x, g, F32, "s5_norm")
    _, _, bbr, bbi = _s5_discretise(p["s5_lam_re"], p["s5_lam_im"], p["s5_log_dt"], p["s5_b_re"], p["s5_b_im"])
    nlev, atab, pw = _s5_tables(p, t)
    bmat, cmat = _s5_in_matrix(bbr, bbi), _s5_out_matrix(p["s5_c_re"], p["s5_c_im"])
    xs, ypre, yg, gathered = s5_fwd(h.reshape(bsz, s, d), bmat, cmat, atab[:, :max(nlev, 8)], pw, p["s5_d"],
                                    later_shards)
    z = matmul(yg.reshape(m, d), p["s5_w_out"], bias=p["s5_b_out"], name="s5_out")
    out = glu_fwd(z, x)
    return out, (x, h, xs, ypre, yg, z, bmat, cmat, atab, pw), gathered


def s5_layer_bwd(saved, dout, g, p, bsz, scatter=None):
    x, h, xs, ypre, yg, z, bmat, cmat, atab, pw = saved
    m, d = x.shape
    s = m // bsz
    pw_rev = _s5_powers(p["s5_lam_re"], p["s5_lam_im"], p["s5_log_dt"], list(range(pw.shape[1], 0, -1)))
    dz, db_out = glu_bwd(z, dout)
    dyg = matmul(dz, p["s5_w_out"], tb=True, name="s5_out_dx")
    dw_out = matmul(yg.reshape(m, d), dz, ta=True, name="s5_out_dw")
    dh, dbm, dcm, dlam, dd, landed = s5_bwd(h.reshape(bsz, s, d), ypre, xs, dyg.reshape(bsz, s, d),
                                            bmat.transpose(0, 2, 1), cmat.transpose(0, 2, 1), atab, pw_rev, p["s5_d"],
                                            scatter)
    dx, dg = rms_bwd(x, g, dh.reshape(m, d), dout, "s5_norm_bwd")
    dbbr, dbbi = _s5_in_matrix_diag(dbm)
    dc_re, dc_im = _s5_out_matrix_diag(dcm)
    dar = dlam[:, 0, :S5_LANES].reshape(S5_CHUNKS * 8, S5_STATE)
    dai = dlam[:, 0, S5_LANES:].reshape(S5_CHUNKS * 8, S5_STATE)
    _, vjp = jax.vjp(_s5_discretise, p["s5_lam_re"], p["s5_lam_im"], p["s5_log_dt"], p["s5_b_re"], p["s5_b_im"])
    dl_re, dl_im, dldt, db_re, db_im = vjp((dar, dai, dbbr, dbbi))
    grads = dict(norm_mix_g=dg, s5_lam_re=dl_re[None], s5_lam_im=dl_im[None], s5_log_dt=dldt[None],
                 s5_b_re=db_re[None], s5_b_im=db_im[None], s5_c_re=dc_re[None], s5_c_im=dc_im[None],
                 s5_d=dd, s5_w_out=dw_out, s5_b_out=db_out)
    return dx, grads, landed


def _log_sigmoid(z):
    return jnp.minimum(z, 0.0) - jnp.log(1.0 + jnp.exp(-jnp.abs(z)))


def _head_norm(t, g_ref):
    r = lax.rsqrt(jnp.mean(t * t, axis=-1, keepdims=True) + EPS)
    th = t * r
    return th * g_ref[...], th, r


def _tri(shape, fn):
    row = lax.broadcasted_iota(jnp.int32, shape, 0)
    col = lax.broadcasted_iota(jnp.int32, shape, 1)
    return fn(row, col)


_SB_SCALE = 1.0 / math.sqrt(SB_DIM)


def _suffix_sums(t, later):
    n = t.shape[1] // SB_CHUNK
    outs, carry = [None] * n, jnp.zeros((t.shape[0], 1), F32)
    for ci in range(n - 1, -1, -1):
        ch = t[:, ci * SB_CHUNK:(ci + 1) * SB_CHUNK]
        part = _dot_exact_rhs(ch, later)
        outs[ci] = part + carry
        carry = carry + (part[:, 0:1] + ch[:, 0:1])
    return (outs[0] if n == 1 else jnp.concatenate(outs, axis=1)), carry


def _prefix_sums(t, tri, inclusive):
    n = t.shape[1] // SB_CHUNK
    outs, carry = [None] * n, jnp.zeros((t.shape[0], 1), F32)
    for ci in range(n):
        ch = t[:, ci * SB_CHUNK:(ci + 1) * SB_CHUNK]
        part = _dot_exact_rhs(ch, tri)
        outs[ci] = part + carry
        last = part[:, SB_CHUNK - 1:SB_CHUNK]
        carry = carry + (last if inclusive else last + ch[:, SB_CHUNK - 1:SB_CHUNK])
    return (outs[0] if n == 1 else jnp.concatenate(outs, axis=1)), carry


def sb_fwd(q, k, v, qg, kg, name="sb_fwd"):
    bsz, nh, s, dh = q.shape
    tb = min(SB_BLOCK, s)
    nq = s // tb

    def body(q_ref, k_ref, v_ref, qg_ref, kg_ref, o_ref, rt_ref):
        qi = pl.program_id(2)
        qn, _, _ = _head_norm(q_ref[0, 0], qg_ref)
        later = _tri((SB_CHUNK, SB_CHUNK), lambda r, c: r > c).astype(BF16)
        causal = _tri((tb, tb), lambda r, c: c < r)

        def block(kb, run, acc, diag):
            ks = pl.ds(pl.multiple_of(kb * tb, tb), tb)
            kn, _, _ = _head_norm(k_ref[0, 0, ks, :], kg_ref)
            z = _dot(qn, kn, NT) * _SB_SCALE
            ls = _log_sigmoid(z)
            lm = ls - z
            if diag:
                lm = jnp.where(causal, lm, 0.0)
            rest, total = _suffix_sums(lm, later)
            att = jnp.exp(ls + run + rest)
            if diag:
                att = jnp.where(causal, att, 0.0)
            return run + total, acc + _dot(att, v_ref[0, 0, ks, :])

        run, acc = block(qi, jnp.zeros((tb, 1), F32), jnp.zeros((tb, dh), F32), True)
        run, acc = lax.fori_loop(0, qi, lambda j, c: block(qi - 1 - j, c[0], c[1], False), (run, acc))
        o_ref[0, 0] = acc
        rt_ref[0, 0] = run

    qsp = pl.BlockSpec((1, 1, tb, dh), lambda b, h, i: (b, h, i, 0))
    rsp = pl.BlockSpec((1, 1, tb, 1), lambda b, h, i: (b, h, i, 0))
    ksp = pl.BlockSpec((1, 1, s, dh), lambda b, h, i: (b, h, 0, 0))
    gsp = pl.BlockSpec((1, dh), lambda b, h, i: (0, 0))
    return pl.pallas_call(
        body, grid=(bsz, nh, nq), in_specs=[qsp, ksp, ksp, gsp, gsp], out_specs=[qsp, rsp],
        out_shape=[jax.ShapeDtypeStruct((bsz, nh, s, dh), F32), jax.ShapeDtypeStruct((bsz, nh, s, 1), F32)],
        compiler_params=_cp("parallel", "parallel", "arbitrary"), name=name)(q, k, v, qg, kg)


def sb_bwd(q, k, v, rtot, do, qg, kg, name="sb_bwd"):
    bsz, nh, s, dh = q.shape
    tb = min(SB_BLOCK, s)
    nq = s // tb

    def body(q_ref, k_ref, v_ref, rt_ref, do_ref, qg_ref, kg_ref, dq_ref, dk_ref, dv_ref, dqg_ref, dkg_ref,
             qn_s, kn_s, dqn_s, dkn_s, dv_s):
        qn, qh, rq = _head_norm(q_ref[0, 0], qg_ref)
        kn, kh, rk = _head_norm(k_ref[0, 0], kg_ref)
        qn_s[...] = qn
        kn_s[...] = kn
        dkn_s[...] = jnp.zeros_like(dkn_s)
        dv_s[...] = jnp.zeros_like(dv_s)
        chunk = (SB_CHUNK, SB_CHUNK)
        upto = _tri(chunk, lambda r, c: r <= c).astype(BF16)
        earlier = _tri(chunk, lambda r, c: r < c).astype(BF16)
        causal = _tri((tb, tb), lambda r, c: c < r)

        def q_block(qi, _):
            qs = pl.ds(pl.multiple_of(qi * tb, tb), tb)
            qnb, dob, rtb = qn_s[qs, :], do_ref[0, 0, qs, :], rt_ref[0, 0, qs, :]

            def block(kb, left, seen, dqn, diag):
                ks = pl.ds(pl.multiple_of(kb * tb, tb), tb)
                knb, vb = kn_s[ks, :], v_ref[0, 0, ks, :]
                z = _dot(qnb, knb, NT) * _SB_SCALE
                ls = _log_sigmoid(z)
                lm = ls - z
                if diag:
                    lm = jnp.where(causal, lm, 0.0)
                through, lm_total = _prefix_sums(lm, upto, True)
                att = jnp.exp(ls + (rtb - left - through))
                if diag:
                    att = jnp.where(causal, att, 0.0)
                gg = att * _dot(dob, vb, NT)
                before, gg_total = _prefix_sums(gg, earlier, False)
                sg = jnp.exp(ls)
                dz = gg * (1.0 - sg) - sg * (seen + before)
                if diag:
                    dz = jnp.where(causal, dz, 0.0)
                dz = dz * _SB_SCALE
                dkn_s[ks, :] += _dot(dz, qnb, TN)
                dv_s[ks, :] += _dot(att, dob, TN)
                return left + lm_total, seen + gg_total, dqn + _dot(dz, knb)

            zero = jnp.zeros((tb, 1), F32)
            c = lax.fori_loop(0, qi, lambda kb, c: block(kb, c[0], c[1], c[2], False),
                              (zero, zero, jnp.zeros((tb, dh), F32)))
            c = block(qi, c[0], c[1], c[2], True)
            dqn_s[qs, :] = c[2]
            return 0

        lax.fori_loop(0, nq, q_block, 0)

        @pl.when((pl.program_id(0) == 0) & (pl.program_id(1) == 0))
        def _():
            dqg_ref[...] = jnp.zeros_like(dqg_ref)
            dkg_ref[...] = jnp.zeros_like(dkg_ref)

        def norm_bwd(dn, th, r, g_ref, dt_ref, dg_ref):
            dg_ref[...] += jnp.sum(dn * th, axis=0, keepdims=True)
            dth = dn * g_ref[...]
            dt_ref[0, 0] = r * (dth - th * jnp.mean(dth * th, axis=-1, keepdims=True))

        norm_bwd(dqn_s[...], qh, rq, qg_ref, dq_ref, dqg_ref)
        norm_bwd(dkn_s[...], kh, rk, kg_ref, dk_ref, dkg_ref)
        dv_ref[0, 0] = dv_s[...]

    hsp = pl.BlockSpec((1, 1, s, dh), lambda b, h: (b, h, 0, 0))
    gsp = pl.BlockSpec((1, dh), lambda b, h: (0, 0))
    act = jax.ShapeDtypeStruct((bsz, nh, s, dh), F32)
    gsh = jax.ShapeDtypeStruct((1, dh), F32)
    rsp = pl.BlockSpec((1, 1, s, 1), lambda b, h: (b, h, 0, 0))
    return pl.pallas_call(
        body, grid=(bsz, nh), in_specs=[hsp, hsp, hsp, rsp, hsp, gsp, gsp], out_specs=[hsp, hsp, hsp, gsp, gsp],
        out_shape=[act, act, act, gsh, gsh], scratch_shapes=[pltpu.VMEM((s, dh), F32)] * 5,
        compiler_params=_cp("arbitrary", "arbitrary"), name=name)(q, k, v, rtot, do, qg, kg)


def _to_heads(t, bsz):
    m, w = t.shape
    n = w // (SB_HEADS * SB_DIM)
    t = t.reshape(bsz, m // bsz, n, SB_HEADS, SB_DIM).transpose(2, 0, 3, 1, 4)
    return [t[i] for i in range(n)]


def _from_heads(ts):
    t = jnp.stack(ts, axis=0)
    n, bsz, nh, s, dh = t.shape
    return t.transpose(1, 3, 0, 2, 4).reshape(bsz * s, n * nh * dh)


def sb_layer_fwd(x, g, p, bsz):
    m, d = x.shape
    h = rms_fwd(x, g, BF16, "sb_norm")
    qkv = matmul(h, p["sb_w_qkv"], name="sb_qkv")
    q, k, v = _to_heads(qkv, bsz)
    o, rtot = sb_fwd(q, k, v, p["sb_q_g"], p["sb_k_g"])
    o2 = _from_heads([o])
    out = matmul(o2, p["sb_w_o"], res=x, name="sb_out")
    return out, (x, h, q, k, v, rtot, o2)


def sb_layer_bwd(saved, dout, g, p, bsz):
    x, h, q, k, v, rtot, o2 = saved
    do2 = matmul(dout, p["sb_w_o"], tb=True, name="sb_out_dx")
    dw_o = matmul(o2, dout, ta=True, name="sb_out_dw")
    dq, dk, dv, dqg, dkg = sb_bwd(q, k, v, rtot, _to_heads(do2, bsz)[0], p["sb_q_g"], p["sb_k_g"])
    dqkv = _from_heads([dq, dk, dv])
    dh = matmul(dqkv, p["sb_w_qkv"], tb=True, name="sb_qkv_dx")
    dw_qkv = matmul(h, dqkv, ta=True, name="sb_qkv_dw")
    dx, dg = rms_bwd(x, g, dh, dout, "sb_norm_bwd")
    return dx, dict(norm_mix_g=dg, sb_w_qkv=dw_qkv, sb_q_g=dqg, sb_k_g=dkg, sb_w_o=dw_o)


_CHIP_FLIPS = ((1, 0), (0, 1), (1, 1))
_MESH = pl.DeviceIdType.MESH
_ANY = pl.BlockSpec(memory_space=pl.ANY)


def _flip(v, f):
    return 1 - v if f else v


def _splits(shape):
    return shape[-2] % 32 == 0


def _half(ref, c, rows):
    idx = (slice(None),) * (len(ref.shape) - 2) + (pl.ds(pl.multiple_of(c * (rows // 2), 16), rows // 2),)
    return ref.at[idx]


def gather_weights(shards, name="gather_weights"):
    n = len(shards)

    def body(*refs):
        start, finish = _gather_steps([s.shape for s in shards], refs[:n], refs[n:2 * n], refs[2 * n:])
        start()
        finish()

    return pl.pallas_call(
        body, in_specs=[_ANY] * n, out_specs=[_ANY] * n, out_shape=_gather_out_shapes(shards),
        scratch_shapes=_gather_semaphores(n), name=name)(*shards)


def _gather_out_shapes(shards):
    return [jax.ShapeDtypeStruct((N_CHIPS,) + s.shape, s.dtype) for s in shards]


def _gather_semaphores(n):
    return [pltpu.SemaphoreType.DMA((3 * n,))] * 4


def _gather_steps(shapes, ins, outs, sems):
    n = len(shapes)
    split = [_splits(s) for s in shapes]
    send, recv, fsend, frecv = sems
    x, y, c = lax.axis_index("x"), lax.axis_index("y"), lax.axis_index("c")
    me = 2 * x + y
    sibling = (x, y, 1 - c)

    def remote(i, j, block):
        px, py = _flip(x, _CHIP_FLIPS[j][0]), _flip(y, _CHIP_FLIPS[j][1])
        rows = shapes[i][0]
        src = _half(ins[i], c, rows) if split[i] else ins[i]
        dst = _half(outs[i].at[block], c, rows) if split[i] else outs[i].at[block]
        return pltpu.make_async_remote_copy(
            src_ref=src, dst_ref=dst, send_sem=send.at[3 * i + j], recv_sem=recv.at[3 * i + j],
            device_id=(px, py, c), device_id_type=_MESH)

    def forward(i, j, half):
        rows = _half(outs[i].at[2 * _flip(x, _CHIP_FLIPS[j][0]) + _flip(y, _CHIP_FLIPS[j][1])], half, shapes[i][0])
        return pltpu.make_async_remote_copy(
            src_ref=rows, dst_ref=rows, send_sem=fsend.at[3 * i + j], recv_sem=frecv.at[3 * i + j],
            device_id=sibling, device_id_type=_MESH)

    def start():
        for i in range(n):
            for j in range(3):
                remote(i, j, me).start()

    def finish():
        for i in range(n):
            for j, (fx, fy) in enumerate(_CHIP_FLIPS):
                remote(i, j, 2 * _flip(x, fx) + _flip(y, fy)).wait_recv()
                if split[i]:
                    forward(i, j, c).start()
        for i in range(n):
            for j in range(3):
                if split[i]:
                    forward(i, j, 1 - c).wait_recv()
                    forward(i, j, c).wait_send()
                remote(i, j, me).wait_send()

    return start, finish


def grad_halves_exchange(parts, name="grad_halves_exchange"):
    n = len(parts)

    def body(*refs):
        ins, got = refs[:n], refs[n:2 * n]
        send, recv = refs[2 * n:]
        x, y, c = lax.axis_index("x"), lax.axis_index("y"), lax.axis_index("c")
        swap = [pltpu.make_async_remote_copy(
            src_ref=_half(ins[i], 1 - c, parts[i].shape[1]), dst_ref=got[i], send_sem=send.at[i], recv_sem=recv.at[i],
            device_id=(x, y, 1 - c), device_id_type=_MESH) for i in range(n)]
        for cp in swap:
            cp.start()
        for cp in swap:
            cp.wait()

    half = [jax.ShapeDtypeStruct((N_CHIPS, p.shape[1] // 2, p.shape[2]), p.dtype) for p in parts]
    return pl.pallas_call(
        body, in_specs=[_ANY] * n, out_specs=[_ANY] * n, out_shape=half,
        scratch_shapes=[pltpu.SemaphoreType.DMA((n,))] * 2, name=name)(*parts)


def pair_sum(full, got, core, out_dtype, name):
    k, r, c = full.shape
    rh = r // 2
    tr = _pick(rh, tuple(t for t in (rh, 512, 256, 128, 64, 32, 16) if t * c * 4 <= 2 * 1024 * 1024))
    nb = rh // tr

    def body(core_ref, a_ref, b_ref, o_ref):
        o_ref[...] = (a_ref[...] + b_ref[...]).astype(o_ref.dtype)

    blk = pl.BlockSpec((1, tr, c), lambda kk, i, core_ref: (kk, i, 0))
    mine = pl.BlockSpec((1, tr, c), lambda kk, i, core_ref: (kk, core_ref[0] * nb + i, 0))
    return pl.pallas_call(
        body, out_shape=jax.ShapeDtypeStruct((k, rh, c), out_dtype),
        grid_spec=pltpu.PrefetchScalarGridSpec(num_scalar_prefetch=1, grid=(k, nb), in_specs=[mine, blk],
                                               out_specs=blk),
        compiler_params=_cp("parallel", "parallel"), name=name)(core, full, got)


def _scatter_semaphores(n):
    return [pltpu.SemaphoreType.DMA((3 * n,)), pltpu.SemaphoreType.DMA((3 * n,)), pltpu.SemaphoreType.DMA((n,)),
            pltpu.SemaphoreType.DMA((n,)), pltpu.SemaphoreType.DMA((n,))]


def _scatter_steps(targets, hin, bufs, sems):
    n = len(targets)
    send, recv, loc, fsend, frecv = sems
    x, y, c = lax.axis_index("x"), lax.axis_index("y"), lax.axis_index("c")
    me = 2 * x + y
    sibling = (x, y, 1 - c)

    def rows_of(ref, i, half):
        _, base, rows = targets[i]
        at = pl.ds(pl.multiple_of(base + half * (rows // 2), 16), rows // 2)
        return ref.at[(slice(None),) * (len(ref.shape) - 2) + (at,)]

    def remote(i, j, slot):
        px, py = _flip(x, _CHIP_FLIPS[j][0]), _flip(y, _CHIP_FLIPS[j][1])
        return pltpu.make_async_remote_copy(
            src_ref=hin[i].at[2 * px + py], dst_ref=rows_of(bufs[targets[i][0]].at[slot], i, c),
            send_sem=send.at[3 * i + j], recv_sem=recv.at[3 * i + j], device_id=(px, py, c), device_id_type=_MESH)

    def local(i):
        return pltpu.make_async_copy(hin[i].at[me], rows_of(bufs[targets[i][0]].at[me], i, c), loc.at[i])

    def forward(i, half):
        rows = rows_of(bufs[targets[i][0]], i, half)
        return pltpu.make_async_remote_copy(src_ref=rows, dst_ref=rows, send_sem=fsend.at[i], recv_sem=frecv.at[i],
                                            device_id=sibling, device_id_type=_MESH)

    def start():
        for i in range(n):
            local(i).start()
            for j in range(3):
                remote(i, j, me).start()

    def collect():
        for i in range(n):
            for j, (fx, fy) in enumerate(_CHIP_FLIPS):
                remote(i, j, 2 * _flip(x, fx) + _flip(y, fy)).wait_recv()
            local(i).wait()
            forward(i, c).start()

    def finish():
        for i in range(n):
            forward(i, 1 - c).wait_recv()
            forward(i, c).wait_send()
            for j in range(3):
                remote(i, j, me).wait_send()

    return start, collect, finish


def grad_sync(halves, targets, bufs, parts, packed, name="grad_sync"):
    nh, n, nb = len(halves), len(parts), len(bufs)
    passed = [i for i, b in enumerate(bufs) if not isinstance(b, jax.ShapeDtypeStruct)]
    n_in = nh + n + 1 + len(passed)

    def body(*refs):
        hin, ins, pk = refs[:nh], refs[nh:nh + n], refs[nh + n]
        outs = refs[n_in:]
        landed, mine, theirs, pk_all = outs[:nb], outs[nb:nb + n], outs[nb + n:nb + 2 * n], outs[nb + 2 * n]
        sems = outs[nb + 2 * n + 1:]
        send, recv, loc, fsend, frecv, psend, precv, ploc = sems[5:]
        start, collect, finish = _scatter_steps(targets, hin, landed, sems[:5])
        x, y, c = lax.axis_index("x"), lax.axis_index("y"), lax.axis_index("c")
        me = 2 * x + y
        dev = 4 * x + 2 * y + c
        sibling = (x, y, 1 - c)

        def remote(i, j, slot):
            px, py = _flip(x, _CHIP_FLIPS[j][0]), _flip(y, _CHIP_FLIPS[j][1])
            return pltpu.make_async_remote_copy(
                src_ref=ins[i].at[2 * px + py], dst_ref=mine[i].at[slot], send_sem=send.at[3 * i + j],
                recv_sem=recv.at[3 * i + j], device_id=(px, py, c), device_id_type=_MESH)

        def packed_to(r, slot):
            px, py, pc = _flip(x, r & 4), _flip(y, r & 2), _flip(c, r & 1)
            return pltpu.make_async_remote_copy(
                src_ref=pk, dst_ref=pk_all.at[slot], send_sem=psend.at[r - 1], recv_sem=precv.at[r - 1],
                device_id=(px, py, pc), device_id_type=_MESH)

        def forward(i):
            return pltpu.make_async_remote_copy(
                src_ref=mine[i], dst_ref=theirs[i], send_sem=fsend.at[i], recv_sem=frecv.at[i],
                device_id=sibling, device_id_type=_MESH)

        start()
        local = [pltpu.make_async_copy(ins[i].at[me], mine[i].at[me], loc.at[i]) for i in range(n)]
        plocal = pltpu.make_async_copy(pk, pk_all.at[dev], ploc.at[0])
        sends = [remote(i, j, me) for i in range(n) for j in range(3)]
        psends = [packed_to(r, dev) for r in range(1, N_DEV)]
        for cp in local + [plocal] + sends + psends:
            cp.start()
        fwd = [forward(i) for i in range(n)]
        for i in range(n):
            for j, (fx, fy) in enumerate(_CHIP_FLIPS):
                remote(i, j, 2 * _flip(x, fx) + _flip(y, fy)).wait_recv()
            local[i].wait()
            fwd[i].start()
        collect()
        finish()
        for i in range(n):
            fwd[i].wait_recv()
        for r in range(1, N_DEV):
            packed_to(r, 4 * _flip(x, r & 4) + 2 * _flip(y, r & 2) + _flip(c, r & 1)).wait_recv()
        for cp in sends + psends + fwd:
            cp.wait_send()
        plocal.wait()

    land = [jax.ShapeDtypeStruct(p.shape, p.dtype) for p in parts]
    out = pl.pallas_call(
        body, in_specs=[_ANY] * n_in, out_specs=[_ANY] * (nb + 2 * n + 1),
        out_shape=[jax.ShapeDtypeStruct(b.shape, b.dtype) for b in bufs] + land + land
        + [jax.ShapeDtypeStruct((N_DEV,) + packed.shape, packed.dtype)],
        input_output_aliases={nh + n + 1 + k: i for k, i in enumerate(passed)},
        scratch_shapes=_scatter_semaphores(nh) + [
            pltpu.SemaphoreType.DMA((3 * n,)), pltpu.SemaphoreType.DMA((3 * n,)), pltpu.SemaphoreType.DMA((n,)),
            pltpu.SemaphoreType.DMA((n,)), pltpu.SemaphoreType.DMA((n,)), pltpu.SemaphoreType.DMA((N_DEV - 1,)),
            pltpu.SemaphoreType.DMA((N_DEV - 1,)), pltpu.SemaphoreType.DMA((1,))],
        name=name)(*halves, *parts, packed, *[bufs[i] for i in passed])
    return out[:nb], out[nb:nb + n], out[nb + n:nb + 2 * n], out[nb + 2 * n]


def adamw(w, m, v, parts, name):
    r, c = w.shape
    tr = r
    for cand in (512, 256, 128, 64, 32, 16, 8):
        if r % cand == 0 and cand * c * 4 <= 512 * 1024:
            tr = cand
            break
    np_ = len(parts)
    nslot = parts[0].shape[0]
    bc1 = 1.0 - ADAM_B1 ** ADAM_STEP
    bc2 = 1.0 - ADAM_B2 ** ADAM_STEP

    def body(*refs):
        w_ref, m_ref, v_ref = refs[:3]
        p_refs = refs[3:3 + np_]
        g_ref, d_ref, nm_ref, nv_ref = refs[3 + np_:]
        g = None
        for k in range(nslot):
            t = p_refs[0][k].astype(F32)
            for p_ref in p_refs[1:]:
                t = t + p_ref[k].astype(F32)
            g = t if g is None else g + t
        wv = w_ref[...]
        nm = ADAM_B1 * m_ref[...] + (1.0 - ADAM_B1) * g
        nv = ADAM_B2 * v_ref[...] + (1.0 - ADAM_B2) * (g * g)
        g_ref[...] = g
        nm_ref[...] = nm
        nv_ref[...] = nv
        d_ref[...] = -ADAM_LR * ((nm / bc1) / (jnp.sqrt(nv / bc2) + ADAM_EPS) + ADAM_WD * wv)

    row = pl.BlockSpec((tr, c), lambda i: (i, 0))
    slab = pl.BlockSpec((nslot, tr, c), lambda i: (0, i, 0))
    sh = jax.ShapeDtypeStruct((r, c), F32)
    return pl.pallas_call(
        body, grid=(r // tr,), in_specs=[row, row, row] + [slab] * np_, out_specs=[row] * 4,
        out_shape=[sh] * 4, compiler_params=_cp("parallel"), name=name)(w, m, v, *parts)


WEIGHTS = ["norm_mix_g", "norm_ffn_g", "pool_w", "pool_b", "pool_scale", "s5_lam_re", "s5_lam_im", "s5_log_dt",
           "s5_b_re", "s5_b_im", "s5_c_re", "s5_c_im", "s5_d", "s5_w_out", "s5_b_out", "lru_w_in", "lru_conv_w",
           "lru_conv_b", "lru_w_a", "lru_b_a", "lru_w_x", "lru_b_x", "lru_lam", "lru_w_out", "sb_w_qkv", "sb_q_g",
           "sb_k_g", "sb_w_o", "ffn_w_in", "ffn_conv_w", "ffn_conv_b", "ffn_w_out"]
SHARD_AXIS = dict(pool_w=2, s5_d=1, s5_w_out=2, s5_b_out=1, lru_w_in=2, lru_conv_w=2, lru_conv_b=1, lru_w_a=2,
                  lru_b_a=1, lru_w_x=2, lru_b_x=1, lru_lam=1, lru_w_out=1, sb_w_qkv=2, sb_w_o=1, ffn_w_in=2,
                  ffn_conv_w=2, ffn_w_out=1)
MXU_WEIGHTS = ("pool_w", "s5_w_out", "lru_w_in", "lru_w_a", "lru_w_x", "lru_w_out", "sb_w_qkv", "sb_w_o",
               "ffn_w_in", "ffn_w_out")
SHARDED = [n for n in WEIGHTS if n in SHARD_AXIS]
REPLICATED = [n for n in WEIGHTS if n not in SHARD_AXIS]
PACK_WIDTH = 1024


def _as_rows(a):
    return a.reshape(-1, a.shape[-1])


def _pack(arrays):
    rows = []
    for a in arrays:
        flat = a.reshape(-1)
        pad = (-flat.shape[0]) % PACK_WIDTH
        rows.append(jnp.pad(flat, (0, pad)).reshape(-1, PACK_WIDTH))
    out = jnp.concatenate(rows, axis=0)
    return jnp.pad(out, ((0, (-out.shape[0]) % 8), (0, 0)))


def _unpack(packed, like):
    out, r = [], 0
    for a in like:
        size = math.prod(a.shape)
        nrow = -(-size // PACK_WIDTH)
        out.append(packed[r:r + nrow].reshape(-1)[:size].reshape(a.shape))
        r += nrow
    return out


def kernel(*args):
    names = ["x"] + WEIGHTS + ["loss_target"] + ["m_" + n for n in WEIGHTS] + ["v_" + n for n in WEIGHTS]
    assert len(args) == len(names)
    given = dict(zip(names, args))
    x, target = given["x"], given["loss_target"]
    bsz, seq, d = x.shape
    m_tok = bsz * seq

    pieces = {}
    for n in SHARDED:
        a = given[n].astype(BF16) if n in MXU_WEIGHTS else given[n]
        if n in ("ffn_w_in", "ffn_w_out"):
            for l in range(4):
                pieces[(n, l)] = (a[l], a[l].shape, SHARD_AXIS[n] - 1)
        else:
            pieces[(n, None)] = (_as_rows(a), a.shape, SHARD_AXIS[n])
    early = [k for k in pieces if k[0] in ("pool_w", "s5_w_out") or k[1] == 0 or not _splits(pieces[k][0].shape)]
    late = [k for k in pieces if k not in early]
    p = {n: given[n] for n in REPLICATED}
    p.update(ffn_w_in=[None] * 4, ffn_w_out=[None] * 4)
    my_chip = 2 * lax.axis_index("x") + lax.axis_index("y")

    def assemble(keys, gathered):
        for key, g4 in zip(keys, gathered):
            own, shape, axis = pieces[key]
            blocks = g4.reshape((N_CHIPS,) + shape)
            own = own.reshape(shape)
            full = jnp.concatenate([jnp.where(my_chip == k, own, blocks[k]) for k in range(N_CHIPS)], axis=axis)
            if key[1] is None:
                p[key[0]] = full
            else:
                p[key[0]][key[1]] = full

    assemble(early, gather_weights([pieces[k][0] for k in early]))

    def mixer_params():
        return {k: (v[0] if v.ndim > 2 or k == "s5_log_dt" else v) for k, v in p.items()
                if not k.startswith(("norm_", "ffn_"))}

    mixers = ((pool_layer_fwd, pool_layer_bwd), (s5_layer_fwd, s5_layer_bwd), (lru_layer_fwd, lru_layer_bwd),
              (sb_layer_fwd, sb_layer_bwd))
    ffn_p = lambda l: (p["norm_ffn_g"][l:l + 1], p["ffn_w_in"][l], p["ffn_conv_w"][l], p["ffn_conv_b"][l:l + 1],
                       p["ffn_w_out"][l])

    def mixer_args(l):
        g = p["norm_mix_g"][l:l + 1]
        if l == 0:
            return (g, mix["pool_w"], mix["pool_b"], mix["pool_scale"], bsz)
        return (g, mix, bsz)

    h = x.reshape(m_tok, d)
    saved = []
    mix = mixer_params()
    for l in range(4):
        if l == 1:
            h, s_mix, gathered = s5_layer_fwd(h, *mixer_args(l), later_shards=[pieces[k][0] for k in late])
            assemble(late, gathered)
            mix = mixer_params()
        else:
            h, s_mix = mixers[l][0](h, *mixer_args(l))
        gl, w_in, cw, cb, w_out = ffn_p(l)
        h, s_ffn = ffn_fwd(h, gl, w_in, cw, cb, w_out, bsz)
        saved.append((s_mix, s_ffn))
    dh, loss_part = loss_head(h, target.reshape(m_tok, d))
    loss = lax.psum(jnp.sum(loss_part), ("x", "y", "c"))

    def full_shape(n):
        shape = list(given[n].shape)
        if n in SHARD_AXIS:
            shape[SHARD_AXIS[n]] *= N_CHIPS
        return tuple(shape)

    grads = {}
    ffn_g = [None] * 4
    mix_g = [None] * 4
    core = lax.axis_index("c").astype(jnp.int32).reshape(1)
    big = [k for k in pieces if _splits(pieces[k][0].shape)]
    small = [k[0] for k in pieces if k not in big]
    behind_s5 = [k for k in big if k[0].startswith(("sb_", "lru_")) or (k[1] is not None and k[1] >= 1)]
    at_end = [k for k in big if k not in behind_s5]

    def part_of(key):
        n, l = key
        g, axis = (grads[n], SHARD_AXIS[n]) if l is None else (ffn_g[l][n], SHARD_AXIS[n] - 1)
        blocks = jnp.stack(jnp.split(g, N_CHIPS, axis=axis))
        return blocks.reshape(N_CHIPS, -1, blocks.shape[-1])

    def reduce_pairs(keys, tag):
        mine = [part_of(k) for k in keys]
        got = grad_halves_exchange(mine, name="grad_halves_exchange_" + tag)
        return [pair_sum(a, b, core, BF16, "pair_sum_%s_%s" % k) for k, a, b in zip(keys, mine, got)]

    def landing(keys, made):
        names = list(dict.fromkeys(k[0] for k in keys))
        bufs = [made.get(n, jax.ShapeDtypeStruct((N_CHIPS,) + _as_rows(given[n]).shape, BF16)) for n in names]
        targets = []
        for n, l in keys:
            rows = pieces[(n, l)][0].shape[0]
            targets.append((names.index(n), (l or 0) * rows, rows))
        return names, bufs, targets

    landed = {}
    for l in range(3, -1, -1):
        gl, w_in, cw, cb, w_out = ffn_p(l)
        dh, ffn_g[l] = ffn_bwd(saved[l][1], dh, gl, w_in, cw, cb, w_out, bsz)
        if l == 1:
            names, bufs, targets = landing(behind_s5, landed)
            dh, mix_g[l], filled = s5_layer_bwd(saved[l][0], dh, *mixer_args(l),
                                                scatter=(reduce_pairs(behind_s5, "behind_s5"), targets, bufs))
            landed.update(zip(names, filled))
        else:
            dh, mix_g[l] = mixers[l][1](saved[l][0], dh, *mixer_args(l))
        for k, v in mix_g[l].items():
            if k != "norm_mix_g":
                grads[k] = v.reshape(full_shape(k))
    for k in ("norm_ffn_g", "ffn_conv_w", "ffn_conv_b"):
        grads[k] = jnp.stack([ffn_g[l][k] for l in range(4)]).reshape(full_shape(k))
    grads["norm_mix_g"] = jnp.concatenate([mix_g[l]["norm_mix_g"] for l in range(4)], axis=0)
    grad_x = dh.reshape(bsz, seq, d)

    names, bufs, targets = landing(at_end, landed)
    filled, mine, theirs, packed_all = grad_sync(reduce_pairs(at_end, "at_end"), targets, bufs,
                                                 [part_of((n, None)) for n in small],
                                                 _pack([grads[n] for n in REPLICATED]))
    landed.update(zip(names, filled))

    summed = {n: [landed[n]] for n in landed}
    summed.update({n: [mine[i], theirs[i]] for i, n in enumerate(small)})

    out = {}
    for n in SHARDED:
        res = adamw(_as_rows(given[n]), _as_rows(given["m_" + n]), _as_rows(given["v_" + n]), summed[n], "adamw_" + n)
        out[n] = [r.reshape(given[n].shape) for r in res]
    res = adamw(_pack([given[n] for n in REPLICATED]), _pack([given["m_" + n] for n in REPLICATED]),
                _pack([given["v_" + n] for n in REPLICATED]), [packed_all], "adamw_replicated")
    like = [given[n] for n in REPLICATED]
    for n, *vals in zip(REPLICATED, *[_unpack(r, like) for r in res]):
        out[n] = list(vals)
    return (loss, grad_x, *[out[n][0] for n in WEIGHTS], *[out[n][1] for n in WEIGHTS],
            *[out[n][2] for n in WEIGHTS], *[out[n][3] for n in WEIGHTS])
```
